```python
import math
import jax, jax.numpy as jnp
from jax import lax
import numpy as np

D_MODEL = 1024
BATCH = 8
SEQ = 2048
DEPTH = 4

SG_WIDTH = 512
SG_GROUPS = 8
SG_GROUP_DIM = SG_WIDTH // SG_GROUPS
CHUNK = 128
CV_WIDTH = 512
CV_KERNEL = 31
SB_HEADS = 8
SB_HEAD_DIM = 64
SB_WIDTH = SB_HEADS * SB_HEAD_DIM
Q_BLOCK = 128
N_BRANCH = 3
D_FF = 2816
FFN_KERNEL = 3

IN_COLS = 2 * SG_WIDTH + 2 * CV_WIDTH + 3 * SB_WIDTH + N_BRANCH * D_MODEL
EPS = 1e-6

kernel_name = "hybrid_sgu_conformer_stickbreak_block"


def rms_norm(x, g):
    xf = x.astype(jnp.float32)
    y = xf * lax.rsqrt(jnp.mean(xf * xf, axis=-1, keepdims=True) + EPS)
    return (y * g.astype(jnp.float32)).astype(x.dtype)


def layer_norm(x, g, b):
    xf = x.astype(jnp.float32)
    mu = jnp.mean(xf, axis=-1, keepdims=True)
    xc = xf - mu
    y = xc * lax.rsqrt(jnp.mean(xc * xc, axis=-1, keepdims=True) + EPS)
    return (y * g.astype(jnp.float32) + b.astype(jnp.float32)).astype(x.dtype)


def causal_depthwise_conv(x, w, b):
    k_width, ch = w.shape
    y = lax.conv_general_dilated(
        x, w[:, None, :].astype(x.dtype), window_strides=(1,), padding=[(k_width - 1, 0)],
        dimension_numbers=("NWC", "WIO", "NWC"), feature_group_count=ch)
    return y + b.astype(x.dtype)


def chunked_spatial_gating(v, w_s, b_s):
    bsz, t, _ = v.shape
    vc = v.reshape(bsz, t // CHUNK, CHUNK, SG_GROUPS, SG_GROUP_DIM)
    tril = jnp.tril(jnp.ones((CHUNK, CHUNK), dtype=bool))
    w_m = jnp.where(tril[None], w_s, jnp.zeros_like(w_s)).astype(v.dtype)
    out = jnp.einsum("gts,bcsgd->bctgd", w_m, vc) + b_s.T.astype(v.dtype)[:, :, None]
    return out.reshape(bsz, t, SG_WIDTH)


def stick_breaking_attention(q, k, v):
    bsz, t, h, dh = q.shape
    q = q.transpose(0, 2, 1, 3)
    k = k.transpose(0, 2, 1, 3)
    v = v.transpose(0, 2, 1, 3)
    scale = 1.0 / math.sqrt(dh)
    outs = []
    for i in range(t // Q_BLOCK):
        ctx = (i + 1) * Q_BLOCK
        q_blk = q[:, :, i * Q_BLOCK:ctx]
        z = jnp.einsum("bhqd,bhkd->bhqk", q_blk, k[:, :, :ctx]).astype(jnp.float32) * scale
        q_pos = i * Q_BLOCK + jnp.arange(Q_BLOCK)
        k_pos = jnp.arange(ctx)
        mask = k_pos[None, :] < q_pos[:, None]
        log_1m_beta = jnp.where(mask, jax.nn.log_sigmoid(-z), 0.0)
        cs = lax.cumsum(log_1m_beta, axis=3)
        log_a = jax.nn.log_sigmoid(z) + (cs[..., -1:] - cs)
        a = jnp.where(mask, jnp.exp(log_a), 0.0)
        outs.append(jnp.einsum("bhqk,bhkd->bhqd", a.astype(v.dtype), v[:, :, :ctx]))
    o = jnp.concatenate(outs, axis=2)
    return o.transpose(0, 2, 1, 3).reshape(bsz, t, h * dh)


def _fwd_setup_inputs(seed: int = 0) -> dict:
    key = jax.random.key(seed)
    ks = jax.random.split(key, 24)
    f32 = jnp.float32
    L, D = DEPTH, D_MODEL

    def nrm(k, shape, scale):
        return jax.random.normal(k, shape, f32) * scale

    return {
        "x": jax.random.normal(ks[0], (BATCH, SEQ, D), f32),
        "ln1_g": 1.0 + nrm(ks[1], (L, D), 0.05),
        "w_in": nrm(ks[2], (L, D, IN_COLS), D ** -0.5),
        "b_gate": nrm(ks[3], (L, N_BRANCH, D), 0.1),
        "sg_ln_g": 1.0 + nrm(ks[4], (L, SG_WIDTH), 0.05),
        "sg_ln_b": nrm(ks[5], (L, SG_WIDTH), 0.05),
        "sg_w": nrm(ks[6], (L, SG_GROUPS, CHUNK, CHUNK), CHUNK ** -0.5),
        "sg_b": 1.0 + nrm(ks[7], (L, SG_GROUPS, CHUNK), 0.1),
        "w_a_out": nrm(ks[8], (L, SG_WIDTH, D), SG_WIDTH ** -0.5),
        "cv_w": nrm(ks[9], (L, CV_KERNEL, CV_WIDTH), CV_KERNEL ** -0.5),
        "cv_b": nrm(ks[10], (L, CV_WIDTH), 0.05),
        "cv_ln_g": 1.0 + nrm(ks[11], (L, CV_WIDTH), 0.05),
        "cv_ln_b": nrm(ks[12], (L, CV_WIDTH), 0.05),
        "w_b_out": nrm(ks[13], (L, CV_WIDTH, D), CV_WIDTH ** -0.5),
        "q_norm_g": 1.0 + nrm(ks[14], (L, SB_HEAD_DIM), 0.05),
        "k_norm_g": 1.0 + nrm(ks[15], (L, SB_HEAD_DIM), 0.05),
        "w_c_out": nrm(ks[16], (L, SB_WIDTH, D), SB_WIDTH ** -0.5),
        "w_out": nrm(ks[17], (L, D, D), D ** -0.5),
        "ln2_g": 1.0 + nrm(ks[18], (L, D), 0.05),
        "w_up": nrm(ks[19], (L, D, 2 * D_FF), D ** -0.5),
        "ffn_conv_w": nrm(ks[20], (L, FFN_KERNEL, 2 * D_FF), FFN_KERNEL ** -0.5),
        "ffn_conv_b": nrm(ks[21], (L, 2 * D_FF), 0.05),
        "w_down": nrm(ks[22], (L, D_FF, D), D_FF ** -0.5),
    }


def _fwd_reference(x, ln1_g, w_in, b_gate, sg_ln_g, sg_ln_b, sg_w, sg_b, w_a_out,
              cv_w, cv_b, cv_ln_g, cv_ln_b, w_b_out, q_norm_g, k_norm_g, w_c_out,
              w_out, ln2_g, w_up, ffn_conv_w, ffn_conv_b, w_down):
    bsz, t, d = x.shape
    split_at = np.cumsum([2 * SG_WIDTH, 2 * CV_WIDTH, 3 * SB_WIDTH]).tolist()
    for l in range(DEPTH):
        h = rms_norm(x, ln1_g[l])
        z = h @ w_in[l]
        z_a, z_b, z_c, z_g = jnp.split(z, split_at, axis=-1)

        a = jax.nn.gelu(z_a)
        u, v = jnp.split(a, 2, axis=-1)
        v = layer_norm(v, sg_ln_g[l], sg_ln_b[l])
        y_a = (u * chunked_spatial_gating(v, sg_w[l], sg_b[l])) @ w_a_out[l]

        p, g_lin = jnp.split(z_b, 2, axis=-1)
        c = p * jax.nn.sigmoid(g_lin)
        c = causal_depthwise_conv(c, cv_w[l], cv_b[l])
        c = jax.nn.silu(layer_norm(c, cv_ln_g[l], cv_ln_b[l]))
        y_b = c @ w_b_out[l]

        q, k, vv = jnp.split(z_c, 3, axis=-1)
        q = rms_norm(q.reshape(bsz, t, SB_HEADS, SB_HEAD_DIM), q_norm_g[l])
        k = rms_norm(k.reshape(bsz, t, SB_HEADS, SB_HEAD_DIM), k_norm_g[l])
        vv = vv.reshape(bsz, t, SB_HEADS, SB_HEAD_DIM)
        y_c = stick_breaking_attention(q, k, vv) @ w_c_out[l]

        gates = jax.nn.sigmoid(z_g.reshape(bsz, t, N_BRANCH, d) + b_gate[l].astype(z_g.dtype))
        merged = gates[:, :, 0] * y_a + gates[:, :, 1] * y_b + gates[:, :, 2] * y_c
        x = x + merged @ w_out[l]

        h2 = rms_norm(x, ln2_g[l])
        up = causal_depthwise_conv(h2 @ w_up[l], ffn_conv_w[l], ffn_conv_b[l])
        gate, val = jnp.split(up, 2, axis=-1)
        x = x + (jax.nn.silu(gate) * val) @ w_down[l]
    return x


import jax as _jax
import jax.numpy as _jnp

TWIN_FORMAT = 'train_step'
FWD_PARAMS = ['x', 'ln1_g', 'w_in', 'b_gate', 'sg_ln_g', 'sg_ln_b', 'sg_w', 'sg_b', 'w_a_out', 'cv_w', 'cv_b', 'cv_ln_g', 'cv_ln_b', 'w_b_out', 'q_norm_g', 'k_norm_g', 'w_c_out', 'w_out', 'ln2_g', 'w_up', 'ffn_conv_w', 'ffn_conv_b', 'w_down']
TWIN_WEIGHTS = ['ln1_g', 'w_in', 'b_gate', 'sg_ln_g', 'sg_ln_b', 'sg_w', 'sg_b', 'w_a_out', 'cv_w', 'cv_b', 'cv_ln_g', 'cv_ln_b', 'w_b_out', 'q_norm_g', 'k_norm_g', 'w_c_out', 'w_out', 'ln2_g', 'w_up', 'ffn_conv_w', 'ffn_conv_b', 'w_down']
TWIN_DIFF_INPUT = 'x'
TWIN_INPUTS = ['x', 'ln1_g', 'w_in', 'b_gate', 'sg_ln_g', 'sg_ln_b', 'sg_w', 'sg_b', 'w_a_out', 'cv_w', 'cv_b', 'cv_ln_g', 'cv_ln_b', 'w_b_out', 'q_norm_g', 'k_norm_g', 'w_c_out', 'w_out', 'ln2_g', 'w_up', 'ffn_conv_w', 'ffn_conv_b', 'w_down', 'loss_target', 'm_ln1_g', 'm_w_in', 'm_b_gate', 'm_sg_ln_g', 'm_sg_ln_b', 'm_sg_w', 'm_sg_b', 'm_w_a_out', 'm_cv_w', 'm_cv_b', 'm_cv_ln_g', 'm_cv_ln_b', 'm_w_b_out', 'm_q_norm_g', 'm_k_norm_g', 'm_w_c_out', 'm_w_out', 'm_ln2_g', 'm_w_up', 'm_ffn_conv_w', 'm_ffn_conv_b', 'm_w_down', 'v_ln1_g', 'v_w_in', 'v_b_gate', 'v_sg_ln_g', 'v_sg_ln_b', 'v_sg_w', 'v_sg_b', 'v_w_a_out', 'v_cv_w', 'v_cv_b', 'v_cv_ln_g', 'v_cv_ln_b', 'v_w_b_out', 'v_q_norm_g', 'v_k_norm_g', 'v_w_c_out', 'v_w_out', 'v_ln2_g', 'v_w_up', 'v_ffn_conv_w', 'v_ffn_conv_b', 'v_w_down']
TWIN_OUTPUTS = ['loss', 'grad_x', 'grad_ln1_g', 'grad_w_in', 'grad_b_gate', 'grad_sg_ln_g', 'grad_sg_ln_b', 'grad_sg_w', 'grad_sg_b', 'grad_w_a_out', 'grad_cv_w', 'grad_cv_b', 'grad_cv_ln_g', 'grad_cv_ln_b', 'grad_w_b_out', 'grad_q_norm_g', 'grad_k_norm_g', 'grad_w_c_out', 'grad_w_out', 'grad_ln2_g', 'grad_w_up', 'grad_ffn_conv_w', 'grad_ffn_conv_b', 'grad_w_down', 'delta_ln1_g', 'delta_w_in', 'delta_b_gate', 'delta_sg_ln_g', 'delta_sg_ln_b', 'delta_sg_w', 'delta_sg_b', 'delta_w_a_out', 'delta_cv_w', 'delta_cv_b', 'delta_cv_ln_g', 'delta_cv_ln_b', 'delta_w_b_out', 'delta_q_norm_g', 'delta_k_norm_g', 'delta_w_c_out', 'delta_w_out', 'delta_ln2_g', 'delta_w_up', 'delta_ffn_conv_w', 'delta_ffn_conv_b', 'delta_w_down', 'new_m_ln1_g', 'new_m_w_in', 'new_m_b_gate', 'new_m_sg_ln_g', 'new_m_sg_ln_b', 'new_m_sg_w', 'new_m_sg_b', 'new_m_w_a_out', 'new_m_cv_w', 'new_m_cv_b', 'new_m_cv_ln_g', 'new_m_cv_ln_b', 'new_m_w_b_out', 'new_m_q_norm_g', 'new_m_k_norm_g', 'new_m_w_c_out', 'new_m_w_out', 'new_m_ln2_g', 'new_m_w_up', 'new_m_ffn_conv_w', 'new_m_ffn_conv_b', 'new_m_w_down', 'new_v_ln1_g', 'new_v_w_in', 'new_v_b_gate', 'new_v_sg_ln_g', 'new_v_sg_ln_b', 'new_v_sg_w', 'new_v_sg_b', 'new_v_w_a_out', 'new_v_cv_w', 'new_v_cv_b', 'new_v_cv_ln_g', 'new_v_cv_ln_b', 'new_v_w_b_out', 'new_v_q_norm_g', 'new_v_k_norm_g', 'new_v_w_c_out', 'new_v_w_out', 'new_v_ln2_g', 'new_v_w_up', 'new_v_ffn_conv_w', 'new_v_ffn_conv_b', 'new_v_w_down']
TWIN_LEAF_KINDS = {'loss': 'loss', 'grad_x': 'grad_x', 'grad_ln1_g': 'grad_w', 'grad_w_in': 'grad_w', 'grad_b_gate': 'grad_w', 'grad_sg_ln_g': 'grad_w', 'grad_sg_ln_b': 'grad_w', 'grad_sg_w': 'grad_w', 'grad_sg_b': 'grad_w', 'grad_w_a_out': 'grad_w', 'grad_cv_w': 'grad_w', 'grad_cv_b': 'grad_w', 'grad_cv_ln_g': 'grad_w', 'grad_cv_ln_b': 'grad_w', 'grad_w_b_out': 'grad_w', 'grad_q_norm_g': 'grad_w', 'grad_k_norm_g': 'grad_w', 'grad_w_c_out': 'grad_w', 'grad_w_out': 'grad_w', 'grad_ln2_g': 'grad_w', 'grad_w_up': 'grad_w', 'grad_ffn_conv_w': 'grad_w', 'grad_ffn_conv_b': 'grad_w', 'grad_w_down': 'grad_w', 'delta_ln1_g': 'delta_w', 'delta_w_in': 'delta_w', 'delta_b_gate': 'delta_w', 'delta_sg_ln_g': 'delta_w', 'delta_sg_ln_b': 'delta_w', 'delta_sg_w': 'delta_w', 'delta_sg_b': 'delta_w', 'delta_w_a_out': 'delta_w', 'delta_cv_w': 'delta_w', 'delta_cv_b': 'delta_w', 'delta_cv_ln_g': 'delta_w', 'delta_cv_ln_b': 'delta_w', 'delta_w_b_out': 'delta_w', 'delta_q_norm_g': 'delta_w', 'delta_k_norm_g': 'delta_w', 'delta_w_c_out': 'delta_w', 'delta_w_out': 'delta_w', 'delta_ln2_g': 'delta_w', 'delta_w_up': 'delta_w', 'delta_ffn_conv_w': 'delta_w', 'delta_ffn_conv_b': 'delta_w', 'delta_w_down': 'delta_w', 'new_m_ln1_g': 'new_m', 'new_m_w_in': 'new_m', 'new_m_b_gate': 'new_m', 'new_m_sg_ln_g': 'new_m', 'new_m_sg_ln_b': 'new_m', 'new_m_sg_w': 'new_m', 'new_m_sg_b': 'new_m', 'new_m_w_a_out': 'new_m', 'new_m_cv_w': 'new_m', 'new_m_cv_b': 'new_m', 'new_m_cv_ln_g': 'new_m', 'new_m_cv_ln_b': 'new_m', 'new_m_w_b_out': 'new_m', 'new_m_q_norm_g': 'new_m', 'new_m_k_norm_g': 'new_m', 'new_m_w_c_out': 'new_m', 'new_m_w_out': 'new_m', 'new_m_ln2_g': 'new_m', 'new_m_w_up': 'new_m', 'new_m_ffn_conv_w': 'new_m', 'new_m_ffn_conv_b': 'new_m', 'new_m_w_down': 'new_m', 'new_v_ln1_g': 'new_v', 'new_v_w_in': 'new_v', 'new_v_b_gate': 'new_v', 'new_v_sg_ln_g': 'new_v', 'new_v_sg_ln_b': 'new_v', 'new_v_sg_w': 'new_v', 'new_v_sg_b': 'new_v', 'new_v_w_a_out': 'new_v', 'new_v_cv_w': 'new_v', 'new_v_cv_b': 'new_v', 'new_v_cv_ln_g': 'new_v', 'new_v_cv_ln_b': 'new_v', 'new_v_w_b_out': 'new_v', 'new_v_q_norm_g': 'new_v', 'new_v_k_norm_g': 'new_v', 'new_v_w_c_out': 'new_v', 'new_v_w_out': 'new_v', 'new_v_ln2_g': 'new_v', 'new_v_w_up': 'new_v', 'new_v_ffn_conv_w': 'new_v', 'new_v_ffn_conv_b': 'new_v', 'new_v_w_down': 'new_v'}


def _forward(args):
    return _fwd_reference(*[args[k] for k in FWD_PARAMS])


def _output_shape():
    out = _jax.eval_shape(lambda: _forward(_fwd_setup_inputs(0)))
    return out.shape, out.dtype

N_MICROBATCH = 1
ADAM_LR = 0.001
ADAM_B1 = 0.9
ADAM_B2 = 0.999
ADAM_EPS = 1e-08
ADAM_WD = 0.01
ADAM_STEP = 10
PER_EXAMPLE_BATCH_AXIS = {'x': 0, 'loss_target': 0}
SHARED_INPUTS = []
_WEIGHT_DTYPES = {'ln1_g': _jnp.float32, 'w_in': _jnp.float32, 'b_gate': _jnp.float32, 'sg_ln_g': _jnp.float32, 'sg_ln_b': _jnp.float32, 'sg_w': _jnp.float32, 'sg_b': _jnp.float32, 'w_a_out': _jnp.float32, 'cv_w': _jnp.float32, 'cv_b': _jnp.float32, 'cv_ln_g': _jnp.float32, 'cv_ln_b': _jnp.float32, 'w_b_out': _jnp.float32, 'q_norm_g': _jnp.float32, 'k_norm_g': _jnp.float32, 'w_c_out': _jnp.float32, 'w_out': _jnp.float32, 'ln2_g': _jnp.float32, 'w_up': _jnp.float32, 'ffn_conv_w': _jnp.float32, 'ffn_conv_b': _jnp.float32, 'w_down': _jnp.float32}
MOMENT_SCALE = {'ln1_g': 6.378163e+00, 'w_in': 3.923560e-01, 'b_gate': 9.689109e-01, 'sg_ln_g': 2.050134e+00, 'sg_ln_b': 3.606847e-01, 'sg_w': 1.866585e-01, 'sg_b': 1.990341e+00, 'w_a_out': 1.571729e+00, 'cv_w': 6.293175e-01, 'cv_b': 6.910417e+00, 'cv_ln_g': 5.279533e+00, 'cv_ln_b': 4.879399e+00, 'w_b_out': 1.203267e+00, 'q_norm_g': 4.445305e+00, 'k_norm_g': 4.474940e+00, 'w_c_out': 6.204104e-01, 'w_out': 1.945898e+00, 'ln2_g': 1.325089e+01, 'w_up': 4.761133e-01, 'ffn_conv_w': 1.939702e+00, 'ffn_conv_b': 2.046025e+00, 'w_down': 5.065996e-01}


def _to_microbatches(a, axis):
    t = _jnp.moveaxis(a, axis, 0)
    t = t.reshape((N_MICROBATCH, t.shape[0] // N_MICROBATCH) + t.shape[1:])
    return _jnp.moveaxis(t, 1, axis + 1)


def setup_inputs(seed: int = 0) -> dict:
    inp = _fwd_setup_inputs(seed)
    key = _jax.random.fold_in(_jax.random.key(seed), 7919)
    shape, _ = _output_shape()
    out = dict(inp)
    out["loss_target"] = _jax.random.normal(_jax.random.fold_in(key, 0), shape, _jnp.float32)
    for i, name in enumerate(TWIN_WEIGHTS):
        w = inp[name].astype(_jnp.float32)
        if MOMENT_SCALE is None:
            s = _jnp.sqrt(_jnp.mean(_jnp.square(w)) + 1e-30)
        else:
            s = MOMENT_SCALE[name]
        km, kv = _jax.random.split(_jax.random.fold_in(key, i + 1))
        out[name] = w
        out["m_" + name] = s * _jax.random.normal(km, w.shape, _jnp.float32)
        out["v_" + name] = (s * s) * _jax.random.uniform(kv, w.shape, _jnp.float32, 0.5, 1.5)
    if N_MICROBATCH > 1:
        for name, axis in PER_EXAMPLE_BATCH_AXIS.items():
            out[name] = _to_microbatches(out[name], axis)
    return {'x': out['x'], 'ln1_g': out['ln1_g'], 'w_in': out['w_in'], 'b_gate': out['b_gate'], 'sg_ln_g': out['sg_ln_g'], 'sg_ln_b': out['sg_ln_b'], 'sg_w': out['sg_w'], 'sg_b': out['sg_b'], 'w_a_out': out['w_a_out'], 'cv_w': out['cv_w'], 'cv_b': out['cv_b'], 'cv_ln_g': out['cv_ln_g'], 'cv_ln_b': out['cv_ln_b'], 'w_b_out': out['w_b_out'], 'q_norm_g': out['q_norm_g'], 'k_norm_g': out['k_norm_g'], 'w_c_out': out['w_c_out'], 'w_out': out['w_out'], 'ln2_g': out['ln2_g'], 'w_up': out['w_up'], 'ffn_conv_w': out['ffn_conv_w'], 'ffn_conv_b': out['ffn_conv_b'], 'w_down': out['w_down'], 'loss_target': out['loss_target'], 'm_ln1_g': out['m_ln1_g'], 'm_w_in': out['m_w_in'], 'm_b_gate': out['m_b_gate'], 'm_sg_ln_g': out['m_sg_ln_g'], 'm_sg_ln_b': out['m_sg_ln_b'], 'm_sg_w': out['m_sg_w'], 'm_sg_b': out['m_sg_b'], 'm_w_a_out': out['m_w_a_out'], 'm_cv_w': out['m_cv_w'], 'm_cv_b': out['m_cv_b'], 'm_cv_ln_g': out['m_cv_ln_g'], 'm_cv_ln_b': out['m_cv_ln_b'], 'm_w_b_out': out['m_w_b_out'], 'm_q_norm_g': out['m_q_norm_g'], 'm_k_norm_g': out['m_k_norm_g'], 'm_w_c_out': out['m_w_c_out'], 'm_w_out': out['m_w_out'], 'm_ln2_g': out['m_ln2_g'], 'm_w_up': out['m_w_up'], 'm_ffn_conv_w': out['m_ffn_conv_w'], 'm_ffn_conv_b': out['m_ffn_conv_b'], 'm_w_down': out['m_w_down'], 'v_ln1_g': out['v_ln1_g'], 'v_w_in': out['v_w_in'], 'v_b_gate': out['v_b_gate'], 'v_sg_ln_g': out['v_sg_ln_g'], 'v_sg_ln_b': out['v_sg_ln_b'], 'v_sg_w': out['v_sg_w'], 'v_sg_b': out['v_sg_b'], 'v_w_a_out': out['v_w_a_out'], 'v_cv_w': out['v_cv_w'], 'v_cv_b': out['v_cv_b'], 'v_cv_ln_g': out['v_cv_ln_g'], 'v_cv_ln_b': out['v_cv_ln_b'], 'v_w_b_out': out['v_w_b_out'], 'v_q_norm_g': out['v_q_norm_g'], 'v_k_norm_g': out['v_k_norm_g'], 'v_w_c_out': out['v_w_c_out'], 'v_w_out': out['v_w_out'], 'v_ln2_g': out['v_ln2_g'], 'v_w_up': out['v_w_up'], 'v_ffn_conv_w': out['v_ffn_conv_w'], 'v_ffn_conv_b': out['v_ffn_conv_b'], 'v_w_down': out['v_w_down']}


def _loss(weights, diff, rest, loss_target):
    with _jax.named_scope("forward"):
        args = {**rest, TWIN_DIFF_INPUT: diff, **{k: w.astype(_WEIGHT_DTYPES[k]) for k, w in weights.items()}}
        y = _forward(args)
    with _jax.named_scope("loss_head"):
        err = _jnp.square(y.astype(_jnp.float32) - loss_target)
        return 0.5 * _jnp.sum(_jnp.mean(err, axis=-1)) if err.ndim else 0.5 * err


def _adamw(w, g, m, v):
    m = ADAM_B1 * m + (1.0 - ADAM_B1) * g
    v = ADAM_B2 * v + (1.0 - ADAM_B2) * _jnp.square(g)
    m_hat = m / (1.0 - ADAM_B1 ** ADAM_STEP)
    v_hat = v / (1.0 - ADAM_B2 ** ADAM_STEP)
    delta = -ADAM_LR * (m_hat / (_jnp.sqrt(v_hat) + ADAM_EPS) + ADAM_WD * w)
    return delta, m, v


def reference(x, ln1_g, w_in, b_gate, sg_ln_g, sg_ln_b, sg_w, sg_b, w_a_out, cv_w, cv_b, cv_ln_g, cv_ln_b, w_b_out, q_norm_g, k_norm_g, w_c_out, w_out, ln2_g, w_up, ffn_conv_w, ffn_conv_b, w_down, loss_target, m_ln1_g, m_w_in, m_b_gate, m_sg_ln_g, m_sg_ln_b, m_sg_w, m_sg_b, m_w_a_out, m_cv_w, m_cv_b, m_cv_ln_g, m_cv_ln_b, m_w_b_out, m_q_norm_g, m_k_norm_g, m_w_c_out, m_w_out, m_ln2_g, m_w_up, m_ffn_conv_w, m_ffn_conv_b, m_w_down, v_ln1_g, v_w_in, v_b_gate, v_sg_ln_g, v_sg_ln_b, v_sg_w, v_sg_b, v_w_a_out, v_cv_w, v_cv_b, v_cv_ln_g, v_cv_ln_b, v_w_b_out, v_q_norm_g, v_k_norm_g, v_w_c_out, v_w_out, v_ln2_g, v_w_up, v_ffn_conv_w, v_ffn_conv_b, v_w_down):
    given = dict(x=x, ln1_g=ln1_g, w_in=w_in, b_gate=b_gate, sg_ln_g=sg_ln_g, sg_ln_b=sg_ln_b, sg_w=sg_w, sg_b=sg_b, w_a_out=w_a_out, cv_w=cv_w, cv_b=cv_b, cv_ln_g=cv_ln_g, cv_ln_b=cv_ln_b, w_b_out=w_b_out, q_norm_g=q_norm_g, k_norm_g=k_norm_g, w_c_out=w_c_out, w_out=w_out, ln2_g=ln2_g, w_up=w_up, ffn_conv_w=ffn_conv_w, ffn_conv_b=ffn_conv_b, w_down=w_down, loss_target=loss_target, m_ln1_g=m_ln1_g, m_w_in=m_w_in, m_b_gate=m_b_gate, m_sg_ln_g=m_sg_ln_g, m_sg_ln_b=m_sg_ln_b, m_sg_w=m_sg_w, m_sg_b=m_sg_b, m_w_a_out=m_w_a_out, m_cv_w=m_cv_w, m_cv_b=m_cv_b, m_cv_ln_g=m_cv_ln_g, m_cv_ln_b=m_cv_ln_b, m_w_b_out=m_w_b_out, m_q_norm_g=m_q_norm_g, m_k_norm_g=m_k_norm_g, m_w_c_out=m_w_c_out, m_w_out=m_w_out, m_ln2_g=m_ln2_g, m_w_up=m_w_up, m_ffn_conv_w=m_ffn_conv_w, m_ffn_conv_b=m_ffn_conv_b, m_w_down=m_w_down, v_ln1_g=v_ln1_g, v_w_in=v_w_in, v_b_gate=v_b_gate, v_sg_ln_g=v_sg_ln_g, v_sg_ln_b=v_sg_ln_b, v_sg_w=v_sg_w, v_sg_b=v_sg_b, v_w_a_out=v_w_a_out, v_cv_w=v_cv_w, v_cv_b=v_cv_b, v_cv_ln_g=v_cv_ln_g, v_cv_ln_b=v_cv_ln_b, v_w_b_out=v_w_b_out, v_q_norm_g=v_q_norm_g, v_k_norm_g=v_k_norm_g, v_w_c_out=v_w_c_out, v_w_out=v_w_out, v_ln2_g=v_ln2_g, v_w_up=v_w_up, v_ffn_conv_w=v_ffn_conv_w, v_ffn_conv_b=v_ffn_conv_b, v_w_down=v_w_down)
    weights = {n: given[n] for n in TWIN_WEIGHTS}
    shared = {n: given[n] for n in SHARED_INPUTS}
    per_example = {n: given[n] for n in ['x']}
    grad_fn = _jax.value_and_grad(_loss, argnums=(0, 1))

    def one_microbatch(ex, loss_target):
        ex = dict(ex)
        diff = ex.pop(TWIN_DIFF_INPUT)
        return grad_fn(weights, diff, {**shared, **ex}, loss_target)

    if N_MICROBATCH == 1:
        loss, (grad_w, grad_x) = one_microbatch(per_example, given["loss_target"])
    else:
        def body(carry, xs):
            loss_sum, grad_sum = carry
            l_k, (gw_k, gx_k) = one_microbatch(xs[0], xs[1])
            with _jax.named_scope("update"):
                return (loss_sum + l_k, _jax.tree.map(_jnp.add, grad_sum, gw_k)), gx_k

        init = (_jnp.zeros((), _jnp.float32), _jax.tree.map(_jnp.zeros_like, weights))
        (loss, grad_w), grad_x = _jax.lax.scan(body, init, (per_example, given["loss_target"]))
    with _jax.named_scope("update"):
        delta_w, new_m, new_v = {}, {}, {}
        for n in TWIN_WEIGHTS:
            delta_w[n], new_m[n], new_v[n] = _adamw(weights[n], grad_w[n], given["m_" + n], given["v_" + n])
    return (loss, grad_x, *[grad_w[n] for n in TWIN_WEIGHTS], *[delta_w[n] for n in TWIN_WEIGHTS],
            *[new_m[n] for n in TWIN_WEIGHTS], *[new_v[n] for n in TWIN_WEIGHTS])
```

```python
import functools
import math

import jax
import jax.numpy as jnp
from jax import lax
from jax.experimental import pallas as pl
from jax.experimental.pallas import tpu as pltpu

F32 = jnp.float32
BF16 = jnp.bfloat16
MESH = pl.DeviceIdType.MESH
ANY = pl.BlockSpec(memory_space=pl.ANY)

EPS = 1e-6
D_MODEL = 1024
DEPTH = 4
SG_WIDTH = 512
CHUNK = 128
CV_WIDTH = 512
CV_KERNEL = 31
SB_WIDTH = 512
SB_HEAD_DIM = 64
Q_BLOCK = 128
D_FF = 2816
FFN_KERNEL = 3
COL_B = 1024
COL_C = 2048
COL_G = 3584
IN_COLS = 6656
N_CHIPS = 4
N_DEV = 8
CV_HALO = 32
FFN_HALO = 16

ADAM_LR = 0.001
ADAM_B1 = 0.9
ADAM_B2 = 0.999
ADAM_EPS = 1e-08
ADAM_WD = 0.01
ADAM_STEP = 10

VMEM_LIMIT_BYTES = 56 * 1024 * 1024

NT_DIMS = (((1,), (1,)), ((), ()))
TN_DIMS = (((0,), (0,)), ((), ()))

WEIGHTS = ['ln1_g', 'w_in', 'b_gate', 'sg_ln_g', 'sg_ln_b', 'sg_w', 'sg_b', 'w_a_out', 'cv_w', 'cv_b',
           'cv_ln_g', 'cv_ln_b', 'w_b_out', 'q_norm_g', 'k_norm_g', 'w_c_out', 'w_out', 'ln2_g', 'w_up',
           'ffn_conv_w', 'ffn_conv_b', 'w_down']
BIG_COL = ['w_in', 'w_a_out', 'w_b_out', 'w_c_out', 'w_up']
BIG_ROW = ['w_out', 'w_down']
BIG = BIG_COL + BIG_ROW
SMALL_COL = ['b_gate', 'cv_w', 'ffn_conv_w']
SMALL = [n for n in WEIGHTS if n not in BIG]


def _pc(body, **kw):
    return pl.pallas_call(body, **kw)


def _cp(*sem):
    return pltpu.CompilerParams(dimension_semantics=sem, vmem_limit_bytes=VMEM_LIMIT_BYTES)


def _sds(shape, dtype):
    return jax.ShapeDtypeStruct(shape, dtype)


_GELU_C = math.sqrt(2.0 / math.pi)
_GELU_A = 0.044715


def _sigmoid(x):
    return jax.nn.sigmoid(x)


def _gelu(x):
    return 0.5 * x * (1.0 + jnp.tanh(_GELU_C * (x + _GELU_A * x * x * x)))


def _gelu_grad(x):
    t = jnp.tanh(_GELU_C * (x + _GELU_A * x * x * x))
    return 0.5 * (1.0 + t) + 0.5 * x * (1.0 - t * t) * _GELU_C * (1.0 + 3.0 * _GELU_A * x * x)


def _silu(x):
    return x * _sigmoid(x)


def _silu_grad(x):
    s = _sigmoid(x)
    return s * (1.0 + x * (1.0 - s))


def _ln_stats(x):
    mu = jnp.mean(x, axis=-1, keepdims=True)
    xc = x - mu
    r = lax.rsqrt(jnp.mean(xc * xc, axis=-1, keepdims=True) + EPS)
    return xc * r, r


def _ln_bwd(dy, xhat, r, g):
    dxh = dy * g
    return r * (dxh - jnp.mean(dxh, axis=-1, keepdims=True) - xhat * jnp.mean(dxh * xhat, axis=-1, keepdims=True))


def _split_dot(x, m):
    hi = x.astype(BF16)
    lo = (x - hi.astype(F32)).astype(BF16)
    return jnp.dot(hi, m, preferred_element_type=F32) + jnp.dot(lo, m, preferred_element_type=F32)


def _rowsum0(x):
    return jnp.sum(x, axis=0, keepdims=True)


def _pick(n, prefs):
    for p in prefs:
        if n % p == 0:
            return p
    return n


def mm_nn(a, w, *, name, res=None, out_dtype=BF16):
    t, k = a.shape
    n = w.shape[1]
    tm = _pick(t, (512, 256))
    tn = _pick(n, (1024, 512, 256))

    def body(*refs):
        if res is None:
            a_ref, w_ref, o_ref = refs
        else:
            a_ref, w_ref, r_ref, o_ref = refs
        acc = jnp.dot(a_ref[...], w_ref[...], preferred_element_type=F32)
        if res is not None:
            acc = acc + r_ref[...]
        o_ref[...] = acc.astype(o_ref.dtype)

    in_specs = [pl.BlockSpec((tm, k), lambda i, j: (i, 0)), pl.BlockSpec((k, tn), lambda i, j: (0, j))]
    args = [a, w]
    if res is not None:
        in_specs.append(pl.BlockSpec((tm, tn), lambda i, j: (i, j)))
        args.append(res)
    return _pc(body, name=name, grid=(t // tm, n // tn), in_specs=in_specs,
               out_specs=pl.BlockSpec((tm, tn), lambda i, j: (i, j)),
               out_shape=_sds((t, n), out_dtype), compiler_params=_cp("parallel", "parallel"))(*args)


def mm_norm_nn(x, g, w, *, name):
    t, k = x.shape
    n = w.shape[1]
    tm = _pick(t, (512, 256))
    tn = _pick(n, (1664, 1408, 512))

    def body(x_ref, g_ref, w_ref, z_ref, h_ref):
        @pl.when(pl.program_id(1) == 0)
        def _():
            xv = x_ref[...]
            r = lax.rsqrt(jnp.mean(xv * xv, axis=-1, keepdims=True) + EPS)
            h_ref[...] = (xv * r * g_ref[...]).astype(BF16)

        z_ref[...] = jnp.dot(h_ref[...], w_ref[...], preferred_element_type=F32).astype(z_ref.dtype)

    return _pc(body, name=name, grid=(t // tm, n // tn),
               in_specs=[pl.BlockSpec((tm, k), lambda i, j: (i, 0)), pl.BlockSpec((1, k), lambda i, j: (0, 0)),
                         pl.BlockSpec((k, tn), lambda i, j: (0, j))],
               out_specs=[pl.BlockSpec((tm, tn), lambda i, j: (i, j)), pl.BlockSpec((tm, k), lambda i, j: (i, 0))],
               out_shape=[_sds((t, n), BF16), _sds((t, k), BF16)],
               compiler_params=_cp("parallel", "arbitrary"))(x, g, w)


def mm_nt(dy, w, *, name, out_dtype):
    t, n = dy.shape
    k = w.shape[0]
    tm = _pick(t, (512, 256))
    tn = _pick(n, (1664, 1408, 1024, 512))
    nj = n // tn

    def body(dy_ref, w_ref, o_ref, acc_ref):
        j = pl.program_id(1)
        p = lax.dot_general(dy_ref[...].astype(BF16), w_ref[...], NT_DIMS, preferred_element_type=F32)

        @pl.when(j == 0)
        def _():
            acc_ref[...] = p

        @pl.when(j > 0)
        def _():
            acc_ref[...] += p

        @pl.when(j == nj - 1)
        def _():
            o_ref[...] = acc_ref[...].astype(o_ref.dtype)

    return _pc(body, name=name, grid=(t // tm, nj),
               in_specs=[pl.BlockSpec((tm, tn), lambda i, j: (i, j)), pl.BlockSpec((k, tn), lambda i, j: (0, j))],
               out_specs=pl.BlockSpec((tm, k), lambda i, j: (i, 0)),
               out_shape=_sds((t, k), out_dtype), scratch_shapes=[pltpu.VMEM((tm, k), F32)],
               compiler_params=_cp("parallel", "arbitrary"))(dy, w)


def mm_tn(a, dy, *, name, out_dtype=BF16):
    t, k = a.shape
    n = dy.shape[1]
    tk = _pick(k, (512, 1408))
    tn = _pick(n, (1664, 1408, 1024, 512))
    tt = _pick(t, (512, 256))
    nt = t // tt

    def body(a_ref, dy_ref, o_ref, acc_ref):
        s = pl.program_id(2)
        p = lax.dot_general(a_ref[...], dy_ref[...].astype(BF16), TN_DIMS, preferred_element_type=F32)

        @pl.when(s == 0)
        def _():
            acc_ref[...] = p

        @pl.when(s > 0)
        def _():
            acc_ref[...] += p

        @pl.when(s == nt - 1)
        def _():
            o_ref[...] = acc_ref[...].astype(o_ref.dtype)

    return _pc(body, name=name, grid=(k // tk, n // tn, nt),
               in_specs=[pl.BlockSpec((tt, tk), lambda i, j, s: (s, i)), pl.BlockSpec((tt, tn), lambda i, j, s: (s, j))],
               out_specs=pl.BlockSpec((tk, tn), lambda i, j, s: (i, j)),
               out_shape=_sds((k, n), out_dtype), scratch_shapes=[pltpu.VMEM((tk, tn), F32)],
               compiler_params=_cp("parallel", "parallel", "arbitrary"))(a, dy)


def rms_bwd(dh, x, g, dres, *, name):
    t, d = x.shape
    tm = _pick(t, (256,))

    def body(dh_ref, x_ref, g_ref, dres_ref, dx_ref, dg_ref):
        xv = x_ref[...]
        r = lax.rsqrt(jnp.mean(xv * xv, axis=-1, keepdims=True) + EPS)
        xh = xv * r
        dy = dh_ref[...].astype(F32)
        dxh = dy * g_ref[...]
        dx_ref[...] = dres_ref[...] + r * (dxh - xh * jnp.mean(dxh * xh, axis=-1, keepdims=True))

        @pl.when(pl.program_id(0) == 0)
        def _():
            dg_ref[...] = jnp.zeros_like(dg_ref)

        dg_ref[...] += _rowsum0(dy * xh)

    row = pl.BlockSpec((tm, d), lambda i: (i, 0))
    vec = pl.BlockSpec((1, d), lambda i: (0, 0))
    return _pc(body, name=name, grid=(t // tm,), in_specs=[row, row, vec, row], out_specs=[row, vec],
               out_shape=[_sds((t, d), F32), _sds((1, d), F32)], compiler_params=_cp("arbitrary"))(dh, x, g, dres)


def loss_head(y, target, *, name):
    t, d = y.shape
    tm = _pick(t, (256,))

    def body(y_ref, t_ref, loss_ref, dy_ref):
        e = y_ref[...] - t_ref[...]
        dy_ref[...] = e * (1.0 / d)

        @pl.when(pl.program_id(0) == 0)
        def _():
            loss_ref[...] = jnp.zeros_like(loss_ref)

        loss_ref[...] += _rowsum0(jnp.sum(e * e, axis=1, keepdims=True)) * (0.5 / d)

    row = pl.BlockSpec((tm, d), lambda i: (i, 0))
    return _pc(body, name=name, grid=(t // tm,), in_specs=[row, row],
               out_specs=[pl.BlockSpec((1, 1), lambda i: (0, 0)), row],
               out_shape=[_sds((1, 1), F32), _sds((t, d), F32)], compiler_params=_cp("arbitrary"))(y, target)


def _sg_masks():
    lane = lax.broadcasted_iota(jnp.int32, (CHUNK, CHUNK), 1)
    row = lax.broadcasted_iota(jnp.int32, (CHUNK, CHUNK), 0)
    return lane < 64, lane <= row, row <= lane


def _sg_gate(vn_chunk, w_ref, bias_ref, p, first_group, tril):
    wa = jnp.where(tril, w_ref[2 * p], 0.0).astype(BF16)
    wb = jnp.where(tril, w_ref[2 * p + 1], 0.0).astype(BF16)
    oa = jnp.dot(wa, vn_chunk, preferred_element_type=F32)
    ob = jnp.dot(wb, vn_chunk, preferred_element_type=F32)
    return jnp.where(first_group, oa, ob) + bias_ref[:, p * 128:(p + 1) * 128]


def mixa_fwd(z, ln_g, ln_b, sg_w, sg_bias, *, name):
    t = z.shape[0]
    tm = _pick(t, (256,))

    def body(z_ref, g_ref, b_ref, w_ref, bias_ref, o_ref):
        first_group, tril, _ = _sg_masks()
        zv = z_ref[...].astype(F32)
        u = _gelu(zv[:, :SG_WIDTH])
        v = _gelu(zv[:, SG_WIDTH:])
        vh, _ = _ln_stats(v)
        vn = (vh * g_ref[...] + b_ref[...]).astype(BF16)
        for c in range(tm // CHUNK):
            rows = slice(c * CHUNK, (c + 1) * CHUNK)
            for p in range(4):
                cols = slice(p * 128, (p + 1) * 128)
                o = _sg_gate(vn[rows, cols], w_ref, bias_ref, p, first_group, tril)
                o_ref[rows, cols] = (u[rows, cols] * o).astype(o_ref.dtype)

    vec = pl.BlockSpec((1, SG_WIDTH), lambda i: (0, 0))
    return _pc(body, name=name, grid=(t // tm,),
               in_specs=[pl.BlockSpec((tm, 2 * SG_WIDTH), lambda i: (i, 0)), vec, vec,
                         pl.BlockSpec((8, CHUNK, CHUNK), lambda i: (0, 0, 0)),
                         pl.BlockSpec((CHUNK, SG_WIDTH), lambda i: (0, 0))],
               out_specs=pl.BlockSpec((tm, SG_WIDTH), lambda i: (i, 0)),
               out_shape=_sds((t, SG_WIDTH), BF16), compiler_params=_cp("parallel"))(z, ln_g, ln_b, sg_w, sg_bias)


def mixa_bwd(z, dga, ln_g, ln_b, sg_w, sg_wt, sg_bias, *, name):
    t = z.shape[0]
    tm = _pick(t, (256,))
    nsteps = t // tm

    def body(z_ref, dga_ref, g_ref, b_ref, w_ref, wt_ref, bias_ref, dz_ref, dw_ref, dsgb_ref, dg_ref, db_ref, dvn_s,
             dbias_ref):
        i = pl.program_id(0)
        first_group, tril, triu = _sg_masks()

        @pl.when(i == 0)
        def _():
            dw_ref[...] = jnp.zeros_like(dw_ref)
            dbias_ref[...] = jnp.zeros_like(dbias_ref)
            dg_ref[...] = jnp.zeros_like(dg_ref)
            db_ref[...] = jnp.zeros_like(db_ref)

        zv = z_ref[...].astype(F32)
        zu = zv[:, :SG_WIDTH]
        zg = zv[:, SG_WIDTH:]
        u = _gelu(zu)
        v = _gelu(zg)
        vh, r = _ln_stats(v)
        vn = (vh * g_ref[...] + b_ref[...]).astype(BF16)
        dga_v = dga_ref[...].astype(F32)
        d_o = dga_v * u
        for c in range(tm // CHUNK):
            rows = slice(c * CHUNK, (c + 1) * CHUNK)
            dbias_ref[...] += d_o[rows, :]
            for p in range(4):
                cols = slice(p * 128, (p + 1) * 128)
                vp = vn[rows, cols]
                o = _sg_gate(vp, w_ref, bias_ref, p, first_group, tril)
                dz_ref[rows, cols] = (dga_v[rows, cols] * o * _gelu_grad(zu[rows, cols])).astype(dz_ref.dtype)
                dop = d_o[rows, cols]
                dop_a = jnp.where(first_group, dop, 0.0).astype(BF16)
                dop_b = jnp.where(first_group, 0.0, dop).astype(BF16)
                dw_ref[2 * p] += lax.dot_general(dop_a, vp, NT_DIMS, preferred_element_type=F32)
                dw_ref[2 * p + 1] += lax.dot_general(dop_b, vp, NT_DIMS, preferred_element_type=F32)
                wta = jnp.where(triu, wt_ref[2 * p], 0.0).astype(BF16)
                wtb = jnp.where(triu, wt_ref[2 * p + 1], 0.0).astype(BF16)
                dop16 = dop.astype(BF16)
                dvn_s[rows, cols] = jnp.where(first_group, jnp.dot(wta, dop16, preferred_element_type=F32),
                                              jnp.dot(wtb, dop16, preferred_element_type=F32))
        dvn = dvn_s[...]
        dg_ref[...] += _rowsum0(dvn * vh)
        db_ref[...] += _rowsum0(dvn)
        dv = _ln_bwd(dvn, vh, r, g_ref[...])
        dz_ref[:, SG_WIDTH:] = (dv * _gelu_grad(zg)).astype(dz_ref.dtype)

        @pl.when(i == nsteps - 1)
        def _():
            for gi in range(8):
                dw_ref[gi] = jnp.where(tril, dw_ref[gi], 0.0)
            r_id = lax.broadcasted_iota(jnp.int32, (SG_WIDTH, 128), 0) // SB_HEAD_DIM
            c_id = lax.broadcasted_iota(jnp.int32, (SG_WIDTH, 128), 1)
            dsgb_ref[...] = _split_dot(dbias_ref[...], (r_id == c_id).astype(BF16))

    vec = pl.BlockSpec((1, SG_WIDTH), lambda i: (0, 0))
    wspec = pl.BlockSpec((8, CHUNK, CHUNK), lambda i: (0, 0, 0))
    bspec = pl.BlockSpec((CHUNK, SG_WIDTH), lambda i: (0, 0))
    sgb = pl.BlockSpec((CHUNK, 128), lambda i: (0, 0))
    return _pc(body, name=name, grid=(nsteps,),
               in_specs=[pl.BlockSpec((tm, 2 * SG_WIDTH), lambda i: (i, 0)), pl.BlockSpec((tm, SG_WIDTH), lambda i: (i, 0)),
                         vec, vec, wspec, wspec, bspec],
               out_specs=[pl.BlockSpec((tm, 2 * SG_WIDTH), lambda i: (i, 0)), wspec, sgb, vec, vec],
               out_shape=[_sds((t, 2 * SG_WIDTH), BF16), _sds((8, CHUNK, CHUNK), F32), _sds((CHUNK, 128), F32),
                          _sds((1, SG_WIDTH), F32), _sds((1, SG_WIDTH), F32)],
               scratch_shapes=[pltpu.VMEM((tm, SG_WIDTH), F32), pltpu.VMEM((CHUNK, SG_WIDTH), F32)],
               compiler_params=_cp("arbitrary"))(z, dga, ln_g, ln_b, sg_w, sg_wt, sg_bias)


def _glu(zv):
    return zv[:, :CV_WIDTH] * _sigmoid(zv[:, CV_WIDTH:])


def _cv_fill(zm_ref, zh_ref, x_s, i, tm):
    x_s[0:CV_HALO, :] = jnp.where(i > 0, _glu(zh_ref[...].astype(F32)), 0.0)
    x_s[CV_HALO:CV_HALO + tm, :] = _glu(zm_ref[...].astype(F32))


def _cv_conv(x_s, w_ref, cb_ref, tm):
    acc = jnp.zeros((tm, CV_WIDTH), F32) + cb_ref[...]
    for k in range(CV_KERNEL):
        acc = acc + w_ref[k:k + 1, :] * x_s[pl.ds(CV_HALO - (CV_KERNEL - 1) + k, tm), :]
    return acc


def _cv_specs(tm):
    zm = pl.BlockSpec((tm, 2 * CV_WIDTH), lambda i: (i, 1))
    zh = pl.BlockSpec((CV_HALO, 2 * CV_WIDTH), lambda i: (jnp.maximum(i * (tm // CV_HALO) - 1, 0), 1))
    w = pl.BlockSpec((CV_KERNEL, CV_WIDTH), lambda i: (0, 0))
    vec = pl.BlockSpec((1, CV_WIDTH), lambda i: (0, 0))
    return zm, zh, w, vec


def mixb_fwd(z, cv_w, cv_b, ln_g, ln_b, *, name):
    t = z.shape[0]
    tm = _pick(t, (256,))

    def body(zm_ref, zh_ref, w_ref, cb_ref, g_ref, b_ref, o_ref, x_s):
        _cv_fill(zm_ref, zh_ref, x_s, pl.program_id(0), tm)
        c1 = _cv_conv(x_s, w_ref, cb_ref, tm)
        ch, _ = _ln_stats(c1)
        o_ref[...] = _silu(ch * g_ref[...] + b_ref[...]).astype(o_ref.dtype)

    zm, zh, w, vec = _cv_specs(tm)
    return _pc(body, name=name, grid=(t // tm,), in_specs=[zm, zh, w, vec, vec, vec],
               out_specs=pl.BlockSpec((tm, CV_WIDTH), lambda i: (i, 0)), out_shape=_sds((t, CV_WIDTH), BF16),
               scratch_shapes=[pltpu.VMEM((CV_HALO + tm, CV_WIDTH), F32)],
               compiler_params=_cp("parallel"))(z, z, cv_w, cv_b, ln_g, ln_b)


def mixb_bwd1(z, dc3, cv_w, cv_b, ln_g, ln_b, *, name):
    t = z.shape[0]
    tm = _pick(t, (256,))

    def body(zm_ref, zh_ref, dc3_ref, w_ref, cb_ref, g_ref, b_ref, dc1_ref, dw_ref, dcb_ref, dg_ref, db_ref, x_s):
        i = pl.program_id(0)

        @pl.when(i == 0)
        def _():
            dw_ref[...] = jnp.zeros_like(dw_ref)
            dcb_ref[...] = jnp.zeros_like(dcb_ref)
            dg_ref[...] = jnp.zeros_like(dg_ref)
            db_ref[...] = jnp.zeros_like(db_ref)

        _cv_fill(zm_ref, zh_ref, x_s, i, tm)
        c1 = _cv_conv(x_s, w_ref, cb_ref, tm)
        ch, r = _ln_stats(c1)
        c2 = ch * g_ref[...] + b_ref[...]
        dc2 = dc3_ref[...].astype(F32) * _silu_grad(c2)
        dg_ref[...] += _rowsum0(dc2 * ch)
        db_ref[...] += _rowsum0(dc2)
        dc1 = _ln_bwd(dc2, ch, r, g_ref[...])
        dc1_ref[...] = dc1
        dcb_ref[...] += _rowsum0(dc1)
        for k in range(CV_KERNEL):
            dw_ref[k:k + 1, :] += _rowsum0(dc1 * x_s[pl.ds(CV_HALO - (CV_KERNEL - 1) + k, tm), :])

    zm, zh, w, vec = _cv_specs(tm)
    row = pl.BlockSpec((tm, CV_WIDTH), lambda i: (i, 0))
    return _pc(body, name=name, grid=(t // tm,), in_specs=[zm, zh, row, w, vec, vec, vec],
               out_specs=[row, w, vec, vec, vec],
               out_shape=[_sds((t, CV_WIDTH), F32), _sds((CV_KERNEL, CV_WIDTH), F32), _sds((1, CV_WIDTH), F32),
                          _sds((1, CV_WIDTH), F32), _sds((1, CV_WIDTH), F32)],
               scratch_shapes=[pltpu.VMEM((CV_HALO + tm, CV_WIDTH), F32)],
               compiler_params=_cp("arbitrary"))(z, z, dc3, cv_w, cv_b, ln_g, ln_b)


def mixb_bwd2(z, dc1, cv_w, *, name):
    t = z.shape[0]
    tm = _pick(t, (256,))
    nsteps = t // tm

    def body(zm_ref, dm_ref, dh_ref, w_ref, dz_ref, y_s):
        i = pl.program_id(0)
        y_s[0:tm, :] = dm_ref[...]
        y_s[tm:tm + CV_HALO, :] = jnp.where(i < nsteps - 1, dh_ref[...], 0.0)
        dc0 = jnp.zeros((tm, CV_WIDTH), F32)
        for k in range(CV_KERNEL):
            dc0 = dc0 + w_ref[k:k + 1, :] * y_s[pl.ds(CV_KERNEL - 1 - k, tm), :]
        zv = zm_ref[...].astype(F32)
        p = zv[:, :CV_WIDTH]
        s = _sigmoid(zv[:, CV_WIDTH:])
        dz_ref[:, :CV_WIDTH] = (dc0 * s).astype(dz_ref.dtype)
        dz_ref[:, CV_WIDTH:] = (dc0 * p * s * (1.0 - s)).astype(dz_ref.dtype)

    last = t // CV_HALO - 1
    return _pc(body, name=name, grid=(nsteps,),
               in_specs=[pl.BlockSpec((tm, 2 * CV_WIDTH), lambda i: (i, 1)),
                         pl.BlockSpec((tm, CV_WIDTH), lambda i: (i, 0)),
                         pl.BlockSpec((CV_HALO, CV_WIDTH), lambda i: (jnp.minimum((i + 1) * (tm // CV_HALO), last), 0)),
                         pl.BlockSpec((CV_KERNEL, CV_WIDTH), lambda i: (0, 0))],
               out_specs=pl.BlockSpec((tm, 2 * CV_WIDTH), lambda i: (i, 0)),
               out_shape=_sds((t, 2 * CV_WIDTH), BF16),
               scratch_shapes=[pltpu.VMEM((tm + CV_HALO, CV_WIDTH), F32)],
               compiler_params=_cp("parallel"))(z, dc1, dc1, cv_w)


def _group_ones():
    r = lax.broadcasted_iota(jnp.int32, (SB_WIDTH, SB_WIDTH), 0) // SB_HEAD_DIM
    c = lax.broadcasted_iota(jnp.int32, (SB_WIDTH, SB_WIDTH), 1) // SB_HEAD_DIM
    return (r == c).astype(BF16)


def attn_prep(z, gq, gk, *, name):
    t = z.shape[0]
    tm = _pick(t, (256,))
    scale = 1.0 / math.sqrt(SB_HEAD_DIM)

    def body(q_ref, k_ref, gq_ref, gk_ref, qo_ref, ko_ref):
        ones = _group_ones()
        for src, g_ref, dst, mul in ((q_ref, gq_ref, qo_ref, scale), (k_ref, gk_ref, ko_ref, 1.0)):
            v = src[...].astype(F32)
            r = lax.rsqrt(_split_dot(v * v, ones) * (1.0 / SB_HEAD_DIM) + EPS)
            dst[...] = ((v * r * g_ref[...]).astype(BF16).astype(F32) * mul).astype(dst.dtype)

    vec = pl.BlockSpec((1, SB_WIDTH), lambda i: (0, 0))
    row = pl.BlockSpec((tm, SB_WIDTH), lambda i: (i, 0))
    return _pc(body, name=name, grid=(t // tm,),
               in_specs=[pl.BlockSpec((tm, SB_WIDTH), lambda i: (i, COL_C // SB_WIDTH)),
                         pl.BlockSpec((tm, SB_WIDTH), lambda i: (i, COL_C // SB_WIDTH + 1)), vec, vec],
               out_specs=[row, row], out_shape=[_sds((t, SB_WIDTH), BF16), _sds((t, SB_WIDTH), BF16)],
               compiler_params=_cp("parallel"))(z, z, gq, gk)


def _attn_consts():
    lane = lax.broadcasted_iota(jnp.int32, (Q_BLOCK, Q_BLOCK), 1)
    row = lax.broadcasted_iota(jnp.int32, (Q_BLOCK, Q_BLOCK), 0)
    return lane < SB_HEAD_DIM, lane < row, row, lane


def _sb_logits(z, causal, diag):
    sp = jnp.log(1.0 + jnp.exp(-jnp.abs(z)))
    g = jnp.minimum(z, 0.0) - sp
    l1m = g - z
    if diag:
        l1m = jnp.where(causal, l1m, 0.0)
    return g, l1m


def _mask_heads(first_head, v):
    zero = jnp.zeros_like(v)
    return jnp.where(first_head, v, zero), jnp.where(first_head, zero, v)


def _attn_specs(t):
    qspec = pl.BlockSpec((Q_BLOCK, 128), lambda h, i: (i, h))
    kspec = pl.BlockSpec((t, 128), lambda h, i: (0, h))
    vspec = pl.BlockSpec((t, 128), lambda h, i: (0, (COL_C + 2 * SB_WIDTH) // 128 + h))
    return qspec, kspec, vspec


def attn_fwd(q, k, z, *, name):
    t = q.shape[0]
    nq = t // Q_BLOCK

    def body(q_ref, k_ref, v_ref, o_ref):
        qb = pl.program_id(1)
        first_head, causal, row, lane = _attn_consts()
        m_suffix = (row > lane).astype(BF16)
        qv = q_ref[...]

        def head(zh, v, carry, diag):
            g, l1m = _sb_logits(zh, causal, diag)
            a = jnp.exp(g + _split_dot(l1m, m_suffix) + carry)
            if diag:
                a = jnp.where(causal, a, 0.0)
            return jnp.dot(a.astype(BF16), v, preferred_element_type=F32), carry + jnp.sum(l1m, axis=1, keepdims=True)

        def step(kb, state, diag):
            acc, ca, cb = state
            off = pl.multiple_of(kb * Q_BLOCK, Q_BLOCK)
            ka, kbb = _mask_heads(first_head, k_ref[pl.ds(off, Q_BLOCK), :])
            v = v_ref[pl.ds(off, Q_BLOCK), :]
            za = lax.dot_general(qv, ka, NT_DIMS, preferred_element_type=F32)
            zb = lax.dot_general(qv, kbb, NT_DIMS, preferred_element_type=F32)
            pa, ca = head(za, v, ca, diag)
            pb, cb = head(zb, v, cb, diag)
            return acc + jnp.where(first_head, pa, pb), ca, cb

        c0 = jnp.zeros((Q_BLOCK, 1), F32)
        state = step(qb, (jnp.zeros((Q_BLOCK, 128), F32), c0, c0), True)
        state = lax.fori_loop(0, qb, lambda s, st: step(qb - 1 - s, st, False), state)
        o_ref[...] = state[0].astype(o_ref.dtype)

    qspec, kspec, vspec = _attn_specs(t)
    return _pc(body, name=name, grid=(SB_WIDTH // 128, nq), in_specs=[qspec, kspec, vspec], out_specs=qspec,
               out_shape=_sds((t, SB_WIDTH), BF16), compiler_params=_cp("parallel", "arbitrary"))(q, k, z)


def attn_bwd(q, k, z, do, *, name):
    t = q.shape[0]
    nq = t // Q_BLOCK

    def body(q_ref, k_ref, v_ref, do_ref, dq_ref, dk_ref, dv_ref, e_s, sg_s):
        qb = pl.program_id(1)
        first_head, causal, row, lane = _attn_consts()
        m_suffix = (row > lane).astype(BF16)
        m_prefix = (row < lane).astype(BF16)

        @pl.when(qb == 0)
        def _():
            dk_ref[...] = jnp.zeros_like(dk_ref)
            dv_ref[...] = jnp.zeros_like(dv_ref)

        qv = q_ref[...]
        dov = do_ref[...]
        qa, qbb = _mask_heads(first_head, qv)
        doa, dob = _mask_heads(first_head, dov)

        def head1(zh, v, doh, carry, diag):
            g, l1m = _sb_logits(zh, causal, diag)
            a = jnp.exp(g + _split_dot(l1m, m_suffix) + carry)
            if diag:
                a = jnp.where(causal, a, 0.0)
            da = lax.dot_general(doh, v, NT_DIMS, preferred_element_type=F32)
            return a.astype(BF16), a * da, jnp.exp(g), carry + jnp.sum(l1m, axis=1, keepdims=True)

        def sweep1(kb, state, diag):
            ca, cb = state
            off = pl.multiple_of(kb * Q_BLOCK, Q_BLOCK)
            ka, kbb = _mask_heads(first_head, k_ref[pl.ds(off, Q_BLOCK), :])
            v = v_ref[pl.ds(off, Q_BLOCK), :]
            za = lax.dot_general(qv, ka, NT_DIMS, preferred_element_type=F32)
            zb = lax.dot_general(qv, kbb, NT_DIMS, preferred_element_type=F32)
            aa, ea, sa, ca = head1(za, v, doa, ca, diag)
            ab, eb, sb, cb = head1(zb, v, dob, cb, diag)
            e_s[0, kb] = ea
            e_s[1, kb] = eb
            sg_s[0, kb] = sa
            sg_s[1, kb] = sb
            dv_ref[pl.ds(off, Q_BLOCK), :] += jnp.where(
                first_head, lax.dot_general(aa, dov, TN_DIMS, preferred_element_type=F32),
                lax.dot_general(ab, dov, TN_DIMS, preferred_element_type=F32))
            return ca, cb

        def head2(h, kb, carry, diag):
            e = e_s[h, kb]
            s = sg_s[h, kb]
            dz = e * (1.0 - s) - (carry + _split_dot(e, m_prefix)) * s
            if diag:
                dz = jnp.where(causal, dz, 0.0)
            return dz.astype(BF16), carry + jnp.sum(e, axis=1, keepdims=True)

        def sweep2(kb, state, diag):
            dq, pa, pb = state
            off = pl.multiple_of(kb * Q_BLOCK, Q_BLOCK)
            ka, kbb = _mask_heads(first_head, k_ref[pl.ds(off, Q_BLOCK), :])
            dza, pa = head2(0, kb, pa, diag)
            dzb, pb = head2(1, kb, pb, diag)
            dq = dq + jnp.dot(dza, ka, preferred_element_type=F32) + jnp.dot(dzb, kbb, preferred_element_type=F32)
            dk_ref[pl.ds(off, Q_BLOCK), :] += (lax.dot_general(dza, qa, TN_DIMS, preferred_element_type=F32)
                                               + lax.dot_general(dzb, qbb, TN_DIMS, preferred_element_type=F32))
            return dq, pa, pb

        c0 = jnp.zeros((Q_BLOCK, 1), F32)
        st1 = sweep1(qb, (c0, c0), True)
        lax.fori_loop(0, qb, lambda s, st: sweep1(qb - 1 - s, st, False), st1)
        st2 = lax.fori_loop(0, qb, lambda s, st: sweep2(s, st, False), (jnp.zeros((Q_BLOCK, 128), F32), c0, c0))
        st2 = sweep2(qb, st2, True)
        dq_ref[...] = st2[0]

    qspec, kspec, vspec = _attn_specs(t)
    acc = pl.BlockSpec((t, 128), lambda h, i: (0, h))
    return _pc(body, name=name, grid=(SB_WIDTH // 128, nq), in_specs=[qspec, kspec, vspec, qspec],
               out_specs=[qspec, acc, acc],
               out_shape=[_sds((t, SB_WIDTH), F32), _sds((t, SB_WIDTH), F32), _sds((t, SB_WIDTH), F32)],
               scratch_shapes=[pltpu.VMEM((2, nq, Q_BLOCK, Q_BLOCK), F32), pltpu.VMEM((2, nq, Q_BLOCK, Q_BLOCK), F32)],
               compiler_params=_cp("parallel", "arbitrary"))(q, k, z, do)


def attn_post_bwd(z, dq, dk, dv, gq, gk, *, name):
    t = z.shape[0]
    tm = _pick(t, (256,))
    scale = 1.0 / math.sqrt(SB_HEAD_DIM)

    def body(q_ref, k_ref, dq_ref, dk_ref, dv_ref, gq_ref, gk_ref, dz_ref, dgq_ref, dgk_ref):
        ones = _group_ones()

        @pl.when(pl.program_id(0) == 0)
        def _():
            dgq_ref[...] = jnp.zeros_like(dgq_ref)
            dgk_ref[...] = jnp.zeros_like(dgk_ref)

        for idx, (src, d_ref, g_ref, dg_ref, mul) in enumerate(
                ((q_ref, dq_ref, gq_ref, dgq_ref, scale), (k_ref, dk_ref, gk_ref, dgk_ref, 1.0))):
            v = src[...].astype(F32)
            r = lax.rsqrt(_split_dot(v * v, ones) * (1.0 / SB_HEAD_DIM) + EPS)
            vh = v * r
            dn = d_ref[...] * mul
            dxh = dn * g_ref[...]
            m = _split_dot(dxh * vh, ones) * (1.0 / SB_HEAD_DIM)
            dz_ref[:, idx * SB_WIDTH:(idx + 1) * SB_WIDTH] = (r * (dxh - vh * m)).astype(dz_ref.dtype)
            s = _rowsum0(dn * vh)
            f = jnp.broadcast_to(s[:, 0:128] + s[:, 128:256] + s[:, 256:384] + s[:, 384:512], dg_ref.shape)
            dg_ref[...] += f + pltpu.roll(f, 64, 1)
        dz_ref[:, 2 * SB_WIDTH:] = dv_ref[...].astype(dz_ref.dtype)

    vec = pl.BlockSpec((1, SB_WIDTH), lambda i: (0, 0))
    row = pl.BlockSpec((tm, SB_WIDTH), lambda i: (i, 0))
    fold = pl.BlockSpec((8, 128), lambda i: (0, 0))
    return _pc(body, name=name, grid=(t // tm,),
               in_specs=[pl.BlockSpec((tm, SB_WIDTH), lambda i: (i, COL_C // SB_WIDTH)),
                         pl.BlockSpec((tm, SB_WIDTH), lambda i: (i, COL_C // SB_WIDTH + 1)), row, row, row, vec, vec],
               out_specs=[pl.BlockSpec((tm, 3 * SB_WIDTH), lambda i: (i, 0)), fold, fold],
               out_shape=[_sds((t, 3 * SB_WIDTH), BF16), _sds((8, 128), F32), _sds((8, 128), F32)],
               compiler_params=_cp("arbitrary"))(z, z, dq, dk, dv, gq, gk)


_GW = 512


def merge_fwd(z, ya, yb, yc, b_gate, *, name):
    t = z.shape[0]
    tm = _pick(t, (512, 256))

    def body(za_ref, zb_ref, zc_ref, ya_ref, yb_ref, yc_ref, bg_ref, o_ref):
        acc = jnp.zeros((tm, _GW), F32)
        for b, (zr, yr) in enumerate(((za_ref, ya_ref), (zb_ref, yb_ref), (zc_ref, yc_ref))):
            acc = acc + _sigmoid(zr[...].astype(F32) + bg_ref[b:b + 1, :]) * yr[...].astype(F32)
        o_ref[...] = acc.astype(o_ref.dtype)

    def zspec(b):
        return pl.BlockSpec((tm, _GW), lambda i, j: (i, COL_G // _GW + 2 * b + j))

    yspec = pl.BlockSpec((tm, _GW), lambda i, j: (i, j))
    return _pc(body, name=name, grid=(t // tm, D_MODEL // _GW),
               in_specs=[zspec(0), zspec(1), zspec(2), yspec, yspec, yspec, pl.BlockSpec((3, _GW), lambda i, j: (0, j))],
               out_specs=yspec, out_shape=_sds((t, D_MODEL), BF16),
               compiler_params=_cp("parallel", "parallel"))(z, z, z, ya, yb, yc, b_gate)


def merge_bwd(z, ya, yb, yc, b_gate, dm, *, name):
    t = z.shape[0]
    tm = _pick(t, (512, 256))

    def body(za_ref, zb_ref, zc_ref, ya_ref, yb_ref, yc_ref, bg_ref, dm_ref,
             dya_ref, dyb_ref, dyc_ref, dza_ref, dzb_ref, dzc_ref, dbg_ref):
        @pl.when(pl.program_id(1) == 0)
        def _():
            dbg_ref[...] = jnp.zeros_like(dbg_ref)

        dmv = dm_ref[...].astype(F32)
        for b, (zr, yr, dyr, dzr) in enumerate(((za_ref, ya_ref, dya_ref, dza_ref), (zb_ref, yb_ref, dyb_ref, dzb_ref),
                                                (zc_ref, yc_ref, dyc_ref, dzc_ref))):
            s = _sigmoid(zr[...].astype(F32) + bg_ref[b:b + 1, :])
            dyr[...] = (dmv * s).astype(dyr.dtype)
            dg = dmv * yr[...].astype(F32) * s * (1.0 - s)
            dzr[...] = dg.astype(dzr.dtype)
            dbg_ref[b:b + 1, :] += _rowsum0(dg)

    def zspec(b):
        return pl.BlockSpec((tm, _GW), lambda j, i: (i, COL_G // _GW + 2 * b + j))

    yspec = pl.BlockSpec((tm, _GW), lambda j, i: (i, j))
    bspec = pl.BlockSpec((3, _GW), lambda j, i: (0, j))
    full = _sds((t, D_MODEL), BF16)
    return _pc(body, name=name, grid=(D_MODEL // _GW, t // tm),
               in_specs=[zspec(0), zspec(1), zspec(2), yspec, yspec, yspec, bspec, yspec],
               out_specs=[yspec] * 6 + [bspec], out_shape=[full] * 6 + [_sds((3, D_MODEL), F32)],
               compiler_params=_cp("parallel", "arbitrary"))(z, z, z, ya, yb, yc, b_gate, dm)


_FW = 1408
_FH = D_FF // _FW


def _ffn_fill(m_ref, h_ref, x_s, i, tm):
    x_s[0:FFN_HALO, :] = jnp.where(i > 0, h_ref[...].astype(F32), 0.0)
    x_s[FFN_HALO:FFN_HALO + tm, :] = m_ref[...].astype(F32)


def _ffn_conv(x_s, w_ref, b_ref, tm):
    acc = jnp.zeros((tm, _FW), F32) + b_ref[...]
    for k in range(FFN_KERNEL):
        acc = acc + w_ref[k:k + 1, :] * x_s[pl.ds(FFN_HALO - (FFN_KERNEL - 1) + k, tm), :]
    return acc


def ffn_mid_fwd(up, cw, cb, *, name):
    t = up.shape[0]
    tm = _pick(t, (256,))

    def body(gm_ref, gh_ref, vm_ref, vh_ref, wg_ref, wv_ref, bg_ref, bv_ref, o_ref, xg_s, xv_s):
        i = pl.program_id(0)
        _ffn_fill(gm_ref, gh_ref, xg_s, i, tm)
        _ffn_fill(vm_ref, vh_ref, xv_s, i, tm)
        o_ref[...] = (_silu(_ffn_conv(xg_s, wg_ref, bg_ref, tm)) * _ffn_conv(xv_s, wv_ref, bv_ref, tm)).astype(o_ref.dtype)

    def main(off):
        return pl.BlockSpec((tm, _FW), lambda i, j: (i, j + off))

    def halo(off):
        return pl.BlockSpec((FFN_HALO, _FW), lambda i, j: (jnp.maximum(i * (tm // FFN_HALO) - 1, 0), j + off))

    def wspec(off):
        return pl.BlockSpec((FFN_KERNEL, _FW), lambda i, j: (0, j + off))

    def bspec(off):
        return pl.BlockSpec((1, _FW), lambda i, j: (0, j + off))

    return _pc(body, name=name, grid=(t // tm, _FH),
               in_specs=[main(0), halo(0), main(_FH), halo(_FH), wspec(0), wspec(_FH), bspec(0), bspec(_FH)],
               out_specs=pl.BlockSpec((tm, _FW), lambda i, j: (i, j)), out_shape=_sds((t, D_FF), BF16),
               scratch_shapes=[pltpu.VMEM((FFN_HALO + tm, _FW), F32), pltpu.VMEM((FFN_HALO + tm, _FW), F32)],
               compiler_params=_cp("parallel", "parallel"))(up, up, up, up, cw, cw, cb, cb)


def ffn_mid_bwd1(up, dact, cw, cb, *, name):
    t = up.shape[0]
    tm = _pick(t, (256,))

    def body(gm_ref, gh_ref, vm_ref, vh_ref, da_ref, wg_ref, wv_ref, bg_ref, bv_ref, d_ref, dw_ref, db_ref, xg_s, xv_s):
        j = pl.program_id(0)
        i = pl.program_id(1)

        @pl.when(i == 0)
        def _():
            dw_ref[...] = jnp.zeros_like(dw_ref)
            db_ref[...] = jnp.zeros_like(db_ref)

        _ffn_fill(gm_ref, gh_ref, xg_s, i, tm)
        _ffn_fill(vm_ref, vh_ref, xv_s, i, tm)
        gate = _ffn_conv(xg_s, wg_ref, bg_ref, tm)
        da = da_ref[...].astype(F32)

        def finish(d, x_s):
            d_ref[...] = d.astype(d_ref.dtype)
            db_ref[...] += _rowsum0(d)
            for k in range(FFN_KERNEL):
                dw_ref[k:k + 1, :] += _rowsum0(d * x_s[pl.ds(FFN_HALO - (FFN_KERNEL - 1) + k, tm), :])

        @pl.when(j < _FH)
        def _():
            finish(da * _ffn_conv(xv_s, wv_ref, bv_ref, tm) * _silu_grad(gate), xg_s)

        @pl.when(j >= _FH)
        def _():
            finish(da * _silu(gate), xv_s)

    def main(off):
        return pl.BlockSpec((tm, _FW), lambda j, i: (i, j % _FH + off))

    def halo(off):
        return pl.BlockSpec((FFN_HALO, _FW), lambda j, i: (jnp.maximum(i * (tm // FFN_HALO) - 1, 0), j % _FH + off))

    def wspec(off):
        return pl.BlockSpec((FFN_KERNEL, _FW), lambda j, i: (0, j % _FH + off))

    def bspec(off):
        return pl.BlockSpec((1, _FW), lambda j, i: (0, j % _FH + off))

    return _pc(body, name=name, grid=(2 * _FH, t // tm),
               in_specs=[main(0), halo(0), main(_FH), halo(_FH), pl.BlockSpec((tm, _FW), lambda j, i: (i, j % _FH)),
                         wspec(0), wspec(_FH), bspec(0), bspec(_FH)],
               out_specs=[pl.BlockSpec((tm, _FW), lambda j, i: (i, j)), pl.BlockSpec((FFN_KERNEL, _FW), lambda j, i: (0, j)),
                          pl.BlockSpec((1, _FW), lambda j, i: (0, j))],
               out_shape=[_sds((t, 2 * D_FF), BF16), _sds((FFN_KERNEL, 2 * D_FF), F32), _sds((1, 2 * D_FF), F32)],
               scratch_shapes=[pltpu.VMEM((FFN_HALO + tm, _FW), F32), pltpu.VMEM((FFN_HALO + tm, _FW), F32)],
               compiler_params=_cp("parallel", "arbitrary"))(up, up, up, up, dact, cw, cw, cb, cb)


def ffn_mid_bwd2(dupc, cw, *, name):
    t = dupc.shape[0]
    tm = _pick(t, (256,))
    nsteps = t // tm
    last = t // FFN_HALO - 1

    def body(m_ref, h_ref, w_ref, o_ref, y_s):
        i = pl.program_id(0)
        y_s[0:tm, :] = m_ref[...].astype(F32)
        y_s[tm:tm + FFN_HALO, :] = jnp.where(i < nsteps - 1, h_ref[...].astype(F32), 0.0)
        acc = jnp.zeros((tm, _FW), F32)
        for k in range(FFN_KERNEL):
            acc = acc + w_ref[k:k + 1, :] * y_s[pl.ds(FFN_KERNEL - 1 - k, tm), :]
        o_ref[...] = acc.astype(o_ref.dtype)

    return _pc(body, name=name, grid=(nsteps, 2 * _FH),
               in_specs=[pl.BlockSpec((tm, _FW), lambda i, j: (i, j)),
                         pl.BlockSpec((FFN_HALO, _FW), lambda i, j: (jnp.minimum((i + 1) * (tm // FFN_HALO), last), j)),
                         pl.BlockSpec((FFN_KERNEL, _FW), lambda i, j: (0, j))],
               out_specs=pl.BlockSpec((tm, _FW), lambda i, j: (i, j)), out_shape=_sds((t, 2 * D_FF), BF16),
               scratch_shapes=[pltpu.VMEM((tm + FFN_HALO, _FW), F32)],
               compiler_params=_cp("parallel", "parallel"))(dupc, dupc, cw)


def _vec(v):
    return v.reshape(1, -1)


def _layer_consts(p):
    return dict(
        sg_bias=jnp.repeat(p['sg_b'].T, SB_HEAD_DIM, axis=1),
        sg_wt=jnp.swapaxes(p['sg_w'], 1, 2),
        gq=jnp.tile(p['q_norm_g'], SB_WIDTH // SB_HEAD_DIM).reshape(1, -1),
        gk=jnp.tile(p['k_norm_g'], SB_WIDTH // SB_HEAD_DIM).reshape(1, -1),
    )


def layer_fwd(x, p):
    c = _layer_consts(p)
    z, h = mm_norm_nn(x, _vec(p['ln1_g']), p['w_in'], name="in_proj")
    ga = mixa_fwd(z, _vec(p['sg_ln_g']), _vec(p['sg_ln_b']), p['sg_w'], c['sg_bias'], name="mixa_fwd")
    cb = mixb_fwd(z, p['cv_w'], _vec(p['cv_b']), _vec(p['cv_ln_g']), _vec(p['cv_ln_b']), name="mixb_fwd")
    q, k = attn_prep(z, c['gq'], c['gk'], name="attn_prep")
    ao = attn_fwd(q, k, z, name="attn_fwd")
    ya = mm_nn(ga, p['w_a_out'], name="a_out")
    yb = mm_nn(cb, p['w_b_out'], name="b_out")
    yc = mm_nn(ao, p['w_c_out'], name="c_out")
    merged = merge_fwd(z, ya, yb, yc, p['b_gate'], name="merge_fwd")
    x1 = mm_nn(merged, p['w_out'], res=x, out_dtype=F32, name="out_proj")
    up, h2 = mm_norm_nn(x1, _vec(p['ln2_g']), p['w_up'], name="up_proj")
    act = ffn_mid_fwd(up, p['ffn_conv_w'], _vec(p['ffn_conv_b']), name="ffn_mid_fwd")
    x2 = mm_nn(act, p['w_down'], res=x1, out_dtype=F32, name="down_proj")
    saved = dict(x=x, z=z, h=h, ga=ga, cb=cb, q=q, k=k, ao=ao, ya=ya, yb=yb, yc=yc, merged=merged, x1=x1, up=up,
                 h2=h2, act=act)
    return x2, saved


def layer_bwd(dx2, p, s):
    c = _layer_consts(p)
    g = {}
    g['w_down'] = mm_tn(s['act'], dx2, name="d_w_down")
    dact = mm_nt(dx2, p['w_down'], out_dtype=BF16, name="d_act")
    dupc, g['ffn_conv_w'], dcb = ffn_mid_bwd1(s['up'], dact, p['ffn_conv_w'], _vec(p['ffn_conv_b']), name="ffn_mid_bwd1")
    g['ffn_conv_b'] = dcb.reshape(-1)
    dup = ffn_mid_bwd2(dupc, p['ffn_conv_w'], name="ffn_mid_bwd2")
    g['w_up'] = mm_tn(s['h2'], dup, name="d_w_up")
    dh2 = mm_nt(dup, p['w_up'], out_dtype=F32, name="d_h2")
    dx1, dg2 = rms_bwd(dh2, s['x1'], _vec(p['ln2_g']), dx2, name="ln2_bwd")
    g['ln2_g'] = dg2.reshape(-1)
    g['w_out'] = mm_tn(s['merged'], dx1, name="d_w_out")
    dm = mm_nt(dx1, p['w_out'], out_dtype=BF16, name="d_merged")
    dya, dyb, dyc, dzg0, dzg1, dzg2, g['b_gate'] = merge_bwd(s['z'], s['ya'], s['yb'], s['yc'], p['b_gate'], dm,
                                                             name="merge_bwd")
    g['w_a_out'] = mm_tn(s['ga'], dya, name="d_w_a_out")
    g['w_b_out'] = mm_tn(s['cb'], dyb, name="d_w_b_out")
    g['w_c_out'] = mm_tn(s['ao'], dyc, name="d_w_c_out")
    dga = mm_nt(dya, p['w_a_out'], out_dtype=BF16, name="d_ga")
    dcb3 = mm_nt(dyb, p['w_b_out'], out_dtype=BF16, name="d_cb")
    dao = mm_nt(dyc, p['w_c_out'], out_dtype=BF16, name="d_ao")
    dza, g['sg_w'], dsgb, dlg, dlb = mixa_bwd(s['z'], dga, _vec(p['sg_ln_g']), _vec(p['sg_ln_b']), p['sg_w'],
                                               c['sg_wt'], c['sg_bias'], name="mixa_bwd")
    g['sg_b'] = dsgb[:, :SG_WIDTH // SB_HEAD_DIM].T
    g['sg_ln_g'] = dlg.reshape(-1)
    g['sg_ln_b'] = dlb.reshape(-1)
    dc1, g['cv_w'], dcvb, dcg, dcbb = mixb_bwd1(s['z'], dcb3, p['cv_w'], _vec(p['cv_b']), _vec(p['cv_ln_g']),
                                                _vec(p['cv_ln_b']), name="mixb_bwd1")
    g['cv_b'] = dcvb.reshape(-1)
    g['cv_ln_g'] = dcg.reshape(-1)
    g['cv_ln_b'] = dcbb.reshape(-1)
    dzb = mixb_bwd2(s['z'], dc1, p['cv_w'], name="mixb_bwd2")
    dq, dk, dv = attn_bwd(s['q'], s['k'], s['z'], dao, name="attn_bwd")
    dzc, dgq, dgk = attn_post_bwd(s['z'], dq, dk, dv, c['gq'], c['gk'], name="attn_post_bwd")
    g['q_norm_g'] = dgq[0, :SB_HEAD_DIM]
    g['k_norm_g'] = dgk[0, :SB_HEAD_DIM]
    dz = jnp.concatenate([dza, dzb, dzc, dzg0, dzg1, dzg2], axis=1)
    g['w_in'] = mm_tn(s['h'], dz, name="d_w_in")
    dh = mm_nt(dz, p['w_in'], out_dtype=F32, name="d_h")
    dx, dg1 = rms_bwd(dh, s['x'], _vec(p['ln1_g']), dx1, name="ln1_bwd")
    g['ln1_g'] = dg1.reshape(-1)
    return dx, g


def local_step(x, target, layers):
    saved = []
    for p in layers:
        x, s = layer_fwd(x, p)
        saved.append(s)
    loss, dx = loss_head(x, target, name="loss_head")
    grads = [None] * len(layers)
    for l in reversed(range(len(layers))):
        dx, grads[l] = layer_bwd(dx, layers[l], saved[l])
    return loss[0, 0], dx, grads


def cast_bf16(w, *, name):
    r, c = w.shape
    tr = _pick(r, (512, 704, 256))

    def body(w_ref, o_ref):
        o_ref[...] = w_ref[...].astype(BF16)

    spec = pl.BlockSpec((tr, c), lambda i: (i, 0))
    return _pc(body, name=name, grid=(r // tr,), in_specs=[spec], out_specs=spec, out_shape=_sds((r, c), BF16),
               compiler_params=_cp("parallel"))(w)


def adamw(w, g, m, v, *, name):
    r, c = w.shape
    tr = _pick(r, (256, 704)) if r * c > 512 * 1024 else r

    def body(w_ref, g_ref, m_ref, v_ref, d_ref, mo_ref, vo_ref):
        gv = g_ref[...]
        mn = ADAM_B1 * m_ref[...] + (1.0 - ADAM_B1) * gv
        vn = ADAM_B2 * v_ref[...] + (1.0 - ADAM_B2) * (gv * gv)
        m_hat = mn / (1.0 - ADAM_B1 ** ADAM_STEP)
        v_hat = vn / (1.0 - ADAM_B2 ** ADAM_STEP)
        d_ref[...] = -ADAM_LR * (m_hat / (jnp.sqrt(v_hat) + ADAM_EPS) + ADAM_WD * w_ref[...])
        mo_ref[...] = mn
        vo_ref[...] = vn

    spec = pl.BlockSpec((tr, c), lambda i: (i, 0))
    out = _sds((r, c), F32)
    return _pc(body, name=name, grid=(r // tr,), in_specs=[spec] * 4, out_specs=[spec] * 3, out_shape=[out] * 3,
               compiler_params=_cp("parallel"))(w, g, m, v)


def _as3(a):
    return a if a.ndim == 3 else a.reshape((1,) + a.shape)


def add_half(g, recv, c_idx, *, name):
    s, rh, w = recv.shape
    tr = _pick(rh, (256, 352, 128))
    nb = rh // tr

    def body(c_ref, g_ref, r_ref, o_ref):
        o_ref[...] = (g_ref[...].astype(F32) + r_ref[...].astype(F32)).astype(o_ref.dtype)

    own = pl.BlockSpec((1, tr, w), lambda k, i, c_ref: (k, c_ref[0] * nb + i, 0))
    half = pl.BlockSpec((1, tr, w), lambda k, i, c_ref: (k, i, 0))
    gs = pltpu.PrefetchScalarGridSpec(num_scalar_prefetch=1, grid=(s, nb), in_specs=[own, half], out_specs=half)
    return _pc(body, name=name, grid_spec=gs, out_shape=_sds((s, rh, w), BF16),
               compiler_params=_cp("parallel", "parallel"))(c_idx, g, recv)


def sum_shard(p, recv, me_idx, *, by_rows, name):
    _, rh, w = recv.shape
    tr = _pick(rh, (256, 352, 128))

    def body(me_ref, p_ref, r_ref, o_ref):
        acc = p_ref[0].astype(F32)
        for j in range(N_CHIPS - 1):
            acc = acc + r_ref[j].astype(F32)
        o_ref[...] = acc

    if by_rows:
        own = pl.BlockSpec((1, tr, w), lambda i, me_ref: (me_ref[0], i, 0))
    else:
        own = pl.BlockSpec((1, tr, w), lambda i, me_ref: (0, i, me_ref[0]))
    gs = pltpu.PrefetchScalarGridSpec(num_scalar_prefetch=1, grid=(rh // tr,),
                                      in_specs=[own, pl.BlockSpec((N_CHIPS - 1, tr, w), lambda i, me_ref: (0, i, 0))],
                                      out_specs=pl.BlockSpec((tr, w), lambda i, me_ref: (i, 0)))
    return _pc(body, name=name, grid_spec=gs, out_shape=_sds((rh, w), F32),
               compiler_params=_cp("parallel"))(me_idx, p, recv)


def sum_slots(slab, *, name):
    _, r, w = slab.shape
    tr = _pick(r, (512, 256, 8))

    def body(s_ref, o_ref):
        acc = s_ref[0]
        for j in range(1, N_DEV):
            acc = acc + s_ref[j]
        o_ref[...] = acc

    return _pc(body, name=name, grid=(r // tr,), in_specs=[pl.BlockSpec((N_DEV, tr, w), lambda i: (0, i, 0))],
               out_specs=pl.BlockSpec((tr, w), lambda i: (i, 0)), out_shape=_sds((r, w), F32),
               compiler_params=_cp("parallel"))(slab)


def _mesh_pos():
    x, y, c = lax.axis_index("x"), lax.axis_index("y"), lax.axis_index("c")
    others = [(1 - x, y), (x, 1 - y), (1 - x, 1 - y)]
    return x, y, c, others


def _rcopy(src, dst, ssem, rsem, k, dev):
    return pltpu.make_async_remote_copy(src_ref=src, dst_ref=dst, send_sem=ssem.at[k], recv_sem=rsem.at[k],
                                        device_id=dev, device_id_type=MESH)


def _comm_call(body, name, n_in, out_shape, n_local, n_remote):
    scratch = [pltpu.SemaphoreType.DMA((max(n_local, 1),)), pltpu.SemaphoreType.DMA((n_remote,)),
               pltpu.SemaphoreType.DMA((n_remote,))]
    return _pc(body, name=name, in_specs=[ANY] * n_in, out_specs=[ANY] * len(out_shape), out_shape=out_shape,
               scratch_shapes=scratch)


def allgather_weights(layer, shards, *, name):
    names = BIG + SMALL_COL
    nn = len(names)

    def full_shape(n):
        _, r, c = shards[n].shape
        return (N_CHIPS, r, c) if n in BIG_ROW else (r, N_CHIPS * c)

    def body(*refs):
        ins, outs = refs[:nn], refs[nn:2 * nn]
        lsem, ssem, rsem = refs[2 * nn:]
        x, y, c, others = _mesh_pos()
        me = 2 * x + y
        sib = (x, y, 1 - c)

        def dwin(i, k, h):
            n = names[i]
            _, r, cs = shards[n].shape
            if n in BIG_ROW:
                return outs[i].at[k] if h is None else outs[i].at[k, pl.ds(h * (r // 2), r // 2), :]
            cols = pl.ds(pl.multiple_of(k * cs, 128), cs)
            return outs[i].at[:, cols] if h is None else outs[i].at[pl.ds(h * (r // 2), r // 2), cols]

        def swin(i, h):
            r = shards[names[i]].shape[1]
            return ins[i].at[layer] if h is None else ins[i].at[layer, pl.ds(h * (r // 2), r // 2), :]

        def half(i):
            return c if names[i] in BIG else None

        started = []
        for i in range(nn):
            mine = pltpu.make_async_copy(swin(i, None), dwin(i, me, None), lsem.at[i])
            mine.start()
            started.append(mine)
        sends = []
        for i in range(nn):
            for j, chip in enumerate(others):
                cp = _rcopy(swin(i, half(i)), dwin(i, me, half(i)), ssem, rsem, 6 * i + j, (*chip, c))
                cp.start()
                sends.append(cp)
        for j, chip in enumerate(others):
            kk = 2 * chip[0] + chip[1]
            for i in range(nn):
                got = dwin(i, kk, half(i))
                _rcopy(got, got, ssem, rsem, 6 * i + j, (*chip, c)).wait_recv()
                if names[i] in BIG:
                    fwd = _rcopy(got, got, ssem, rsem, 6 * i + 3 + j, sib)
                    fwd.start()
                    sends.append(fwd)
        for j, chip in enumerate(others):
            kk = 2 * chip[0] + chip[1]
            for i in range(nn):
                if names[i] in BIG:
                    got = dwin(i, kk, 1 - c)
                    _rcopy(got, got, ssem, rsem, 6 * i + 3 + j, sib).wait_recv()
        for cp in sends:
            cp.wait_send()
        for cp in started:
            cp.wait()

    out_shape = [_sds(full_shape(n), shards[n].dtype) for n in names]
    outs = _comm_call(body, name, nn, out_shape, nn, 6 * nn)(*[shards[n] for n in names])
    full = {}
    for n, o in zip(names, outs):
        full[n] = o.reshape(o.shape[0] * o.shape[1], o.shape[2]) if n in BIG_ROW else o
    return full


def _grad_view(n, g):
    return g.reshape(N_CHIPS, g.shape[0] // N_CHIPS, g.shape[1]) if n in BIG_ROW else g.reshape((1,) + g.shape)


def exchange_halves(gv, *, name):
    nn = len(BIG)

    def body(*refs):
        ins, outs = refs[:nn], refs[nn:2 * nn]
        _, ssem, rsem = refs[2 * nn:]
        x, y, c, _ = _mesh_pos()
        cps = []
        for i in range(nn):
            rh = gv[i].shape[1] // 2
            cp = _rcopy(ins[i].at[:, pl.ds((1 - c) * rh, rh), :], outs[i], ssem, rsem, i, (x, y, 1 - c))
            cp.start()
            cps.append(cp)
        for cp in cps:
            cp.wait()

    out_shape = [_sds((a.shape[0], a.shape[1] // 2, a.shape[2]), a.dtype) for a in gv]
    return _comm_call(body, name, nn, out_shape, 0, nn)(*gv)


def scatter_partials(pv, *, name):
    nn = len(BIG)

    def shard_shape(i):
        s, rh, w = pv[i].shape
        return (rh, w) if BIG[i] in BIG_ROW else (rh, w // N_CHIPS)

    def body(*refs):
        ins, outs = refs[:nn], refs[nn:2 * nn]
        _, ssem, rsem = refs[2 * nn:]
        x, y, c, others = _mesh_pos()
        cps = []
        for i in range(nn):
            rh, ws = shard_shape(i)
            for j, chip in enumerate(others):
                kk = 2 * chip[0] + chip[1]
                if BIG[i] in BIG_ROW:
                    src = ins[i].at[kk]
                else:
                    src = ins[i].at[0, :, pl.ds(pl.multiple_of(kk * ws, 128), ws)]
                cp = _rcopy(src, outs[i].at[j], ssem, rsem, 3 * i + j, (*chip, c))
                cp.start()
                cps.append(cp)
        for cp in cps:
            cp.wait()

    out_shape = [_sds((N_CHIPS - 1,) + shard_shape(i), pv[i].dtype) for i in range(nn)]
    return _comm_call(body, name, nn, out_shape, 0, 3 * nn)(*pv)


def join_halves(rv, *, name):
    nn = len(BIG)

    def body(*refs):
        ins, outs = refs[:nn], refs[nn:2 * nn]
        lsem, ssem, rsem = refs[2 * nn:]
        x, y, c, _ = _mesh_pos()
        cps = []
        for i in range(nn):
            rh = rv[i].shape[0]
            dst = outs[i].at[pl.ds(c * rh, rh), :]
            loc = pltpu.make_async_copy(ins[i], dst, lsem.at[i])
            loc.start()
            cp = _rcopy(ins[i], dst, ssem, rsem, i, (x, y, 1 - c))
            cp.start()
            cps += [loc, cp]
        for cp in cps:
            cp.wait()

    out_shape = [_sds((2 * a.shape[0], a.shape[1]), a.dtype) for a in rv]
    return _comm_call(body, name, nn, out_shape, nn, nn)(*rv)


def reduce_scatter_layer(grads, c_idx, me_idx):
    gv = [_grad_view(n, grads[n]) for n in BIG]
    recv = exchange_halves(gv, name="rs_exchange_halves")
    pv = [add_half(g, r, c_idx, name="rs_add_" + n) for n, g, r in zip(BIG, gv, recv)]
    got = scatter_partials(pv, name="rs_scatter_partials")
    rv = [sum_shard(p, r, me_idx, by_rows=n in BIG_ROW, name="rs_sum_" + n) for n, p, r in zip(BIG, pv, got)]
    return dict(zip(BIG, join_halves(rv, name="rs_join_halves")))


def allgather_slab(slab, *, name):
    r, w = slab.shape

    def body(x_ref, out_ref, lsem, ssem, rsem):
        x, y, c, others = _mesh_pos()
        me, sib = (x, y, c), (x, y, 1 - c)

        def slot(px, py, pc):
            return out_ref.at[4 * px + 2 * py + pc]

        mine = pltpu.make_async_copy(x_ref, slot(*me), lsem.at[0])
        mine.start()
        first = [_rcopy(x_ref, slot(*me), ssem, rsem, 0, sib)]
        first += [_rcopy(x_ref, slot(*me), ssem, rsem, 1 + j, (*chip, c)) for j, chip in enumerate(others)]
        for cp in first:
            cp.start()
        passed = [_rcopy(slot(*chip, c), slot(*chip, c), ssem, rsem, 4 + j, sib) for j, chip in enumerate(others)]
        for j, chip in enumerate(others):
            _rcopy(slot(*chip, c), slot(*chip, c), ssem, rsem, 1 + j, me).wait_recv()
            passed[j].start()
        _rcopy(slot(*sib), slot(*sib), ssem, rsem, 0, me).wait_recv()
        for j, chip in enumerate(others):
            _rcopy(slot(*chip, 1 - c), slot(*chip, 1 - c), ssem, rsem, 4 + j, me).wait_recv()
        for cp in first + passed:
            cp.wait_send()
        mine.wait()

    return _comm_call(body, name, 1, [_sds((N_DEV, r, w), slab.dtype)], 1, 7)(slab)[0]


def _pad128(n):
    return -(-n // 128) * 128


def _pack_small(grads, shapes):
    parts = []
    for g in grads:
        for n in SMALL:
            v = g[n].astype(F32).reshape(-1)
            parts.append(jnp.pad(v, (0, _pad128(v.shape[0]) - v.shape[0])))
    flat = jnp.concatenate(parts)
    rows = -(-flat.shape[0] // (128 * 8)) * 8
    return jnp.pad(flat, (0, rows * 128 - flat.shape[0])).reshape(rows, 128)


def _unpack_small(slab, shapes, depth):
    flat = slab.reshape(-1)
    out = {n: [] for n in SMALL}
    off = 0
    for _ in range(depth):
        for n in SMALL:
            size = math.prod(shapes[n])
            out[n].append(flat[off:off + size].reshape(shapes[n]))
            off += _pad128(size)
    return {n: jnp.stack(v) for n, v in out.items()}


def _adamw_nd(w, g, m, v, name):
    shp = w.shape
    two = lambda a: a.reshape(-1, shp[-1])
    return tuple(o.reshape(shp) for o in adamw(two(w), two(g), two(m), two(v), name=name))


def kernel(x, ln1_g, w_in, b_gate, sg_ln_g, sg_ln_b, sg_w, sg_b, w_a_out, cv_w, cv_b, cv_ln_g, cv_ln_b, w_b_out, q_norm_g, k_norm_g, w_c_out, w_out, ln2_g, w_up, ffn_conv_w, ffn_conv_b, w_down, loss_target, m_ln1_g, m_w_in, m_b_gate, m_sg_ln_g, m_sg_ln_b, m_sg_w, m_sg_b, m_w_a_out, m_cv_w, m_cv_b, m_cv_ln_g, m_cv_ln_b, m_w_b_out, m_q_norm_g, m_k_norm_g, m_w_c_out, m_w_out, m_ln2_g, m_w_up, m_ffn_conv_w, m_ffn_conv_b, m_w_down, v_ln1_g, v_w_in, v_b_gate, v_sg_ln_g, v_sg_ln_b, v_sg_w, v_sg_b, v_w_a_out, v_cv_w, v_cv_b, v_cv_ln_g, v_cv_ln_b, v_w_b_out, v_q_norm_g, v_k_norm_g, v_w_c_out, v_w_out, v_ln2_g, v_w_up, v_ffn_conv_w, v_ffn_conv_b, v_w_down):
    w = dict(ln1_g=ln1_g, w_in=w_in, b_gate=b_gate, sg_ln_g=sg_ln_g, sg_ln_b=sg_ln_b, sg_w=sg_w, sg_b=sg_b,
             w_a_out=w_a_out, cv_w=cv_w, cv_b=cv_b, cv_ln_g=cv_ln_g, cv_ln_b=cv_ln_b, w_b_out=w_b_out,
             q_norm_g=q_norm_g, k_norm_g=k_norm_g, w_c_out=w_c_out, w_out=w_out, ln2_g=ln2_g, w_up=w_up,
             ffn_conv_w=ffn_conv_w, ffn_conv_b=ffn_conv_b, w_down=w_down)
    m = dict(ln1_g=m_ln1_g, w_in=m_w_in, b_gate=m_b_gate, sg_ln_g=m_sg_ln_g, sg_ln_b=m_sg_ln_b, sg_w=m_sg_w,
             sg_b=m_sg_b, w_a_out=m_w_a_out, cv_w=m_cv_w, cv_b=m_cv_b, cv_ln_g=m_cv_ln_g, cv_ln_b=m_cv_ln_b,
             w_b_out=m_w_b_out, q_norm_g=m_q_norm_g, k_norm_g=m_k_norm_g, w_c_out=m_w_c_out, w_out=m_w_out,
             ln2_g=m_ln2_g, w_up=m_w_up, ffn_conv_w=m_ffn_conv_w, ffn_conv_b=m_ffn_conv_b, w_down=m_w_down)
    v = dict(ln1_g=v_ln1_g, w_in=v_w_in, b_gate=v_b_gate, sg_ln_g=v_sg_ln_g, sg_ln_b=v_sg_ln_b, sg_w=v_sg_w,
             sg_b=v_sg_b, w_a_out=v_w_a_out, cv_w=v_cv_w, cv_b=v_cv_b, cv_ln_g=v_cv_ln_g, cv_ln_b=v_cv_ln_b,
             w_b_out=v_w_b_out, q_norm_g=v_q_norm_g, k_norm_g=v_k_norm_g, w_c_out=v_w_c_out, w_out=v_w_out,
             ln2_g=v_ln2_g, w_up=v_w_up, ffn_conv_w=v_ffn_conv_w, ffn_conv_b=v_ffn_conv_b, w_down=v_w_down)
    depth = ln1_g.shape[0]
    cx, cy, cc = lax.axis_index("x"), lax.axis_index("y"), lax.axis_index("c")
    me = 2 * cx + cy
    me_idx = jnp.reshape(me, (1,)).astype(jnp.int32)
    c_idx = jnp.reshape(cc, (1,)).astype(jnp.int32)

    shards = {}
    for n in BIG:
        a = w[n]
        shards[n] = cast_bf16(a.reshape(-1, a.shape[-1]), name="cast_" + n).reshape(a.shape)
    for n in SMALL_COL:
        shards[n] = jnp.pad(w[n], ((0, 0), (0, -w[n].shape[1] % 8), (0, 0)))
    layers = []
    for l in range(depth):
        p = allgather_weights(l, shards, name="allgather_weights")
        for n in SMALL:
            p[n] = p[n][:w[n].shape[1]] if n in SMALL_COL else w[n][l]
        layers.append(p)

    loss, dx, grads = local_step(x[0], loss_target[0], layers)
    loss = lax.psum(loss, ("x", "y", "c"))

    big = [reduce_scatter_layer(g, c_idx, me_idx) for g in grads]
    full_shapes = {n: layers[0][n].shape for n in SMALL}
    slab = sum_slots(allgather_slab(_pack_small(grads, full_shapes), name="allgather_small_grads"), name="sum_small_grads")
    small = _unpack_small(slab, full_shapes, depth)
    grad = {n: jnp.stack([b[n] for b in big]) for n in BIG}
    for n in SMALL:
        if n in SMALL_COL:
            cs = w[n].shape[-1]
            grad[n] = lax.dynamic_slice_in_dim(small[n], me * cs, cs, axis=small[n].ndim - 1)
        else:
            grad[n] = small[n]

    delta, new_m, new_v = {}, {}, {}
    for n in WEIGHTS:
        delta[n], new_m[n], new_v[n] = _adamw_nd(w[n], grad[n], m[n], v[n], "adamw_" + n)
    return (loss, dx[None], *[grad[n] for n in WEIGHTS], *[delta[n] for n in WEIGHTS],
            *[new_m[n] for n in WEIGHTS], *[new_v[n] for n in WEIGHTS])
```

```python
import functools
import math

import jax
import jax.numpy as jnp
from jax import lax
from jax.experimental import pallas as pl
from jax.experimental.pallas import tpu as pltpu

F32 = jnp.float32
BF16 = jnp.bfloat16
MESH = pl.DeviceIdType.MESH
ANY = pl.BlockSpec(memory_space=pl.ANY)

EPS = 1e-6
D_MODEL = 1024
DEPTH = 4
SG_WIDTH = 512
CHUNK = 128
CV_WIDTH = 512
CV_KERNEL = 31
SB_WIDTH = 512
SB_HEAD_DIM = 64
Q_BLOCK = 128
D_FF = 2816
FFN_KERNEL = 3
COL_B = 1024
COL_C = 2048
COL_G = 3584
IN_COLS = 6656
N_CHIPS = 4
N_DEV = 8
CV_HALO = 32
FFN_HALO = 16

ADAM_LR = 0.001
ADAM_B1 = 0.9
ADAM_B2 = 0.999
ADAM_EPS = 1e-08
ADAM_WD = 0.01
ADAM_STEP = 10

VMEM_LIMIT_BYTES = 56 * 1024 * 1024

NT_DIMS = (((1,), (1,)), ((), ()))
TN_DIMS = (((0,), (0,)), ((), ()))

WEIGHTS = ['ln1_g', 'w_in', 'b_gate', 'sg_ln_g', 'sg_ln_b', 'sg_w', 'sg_b', 'w_a_out', 'cv_w', 'cv_b',
           'cv_ln_g', 'cv_ln_b', 'w_b_out', 'q_norm_g', 'k_norm_g', 'w_c_out', 'w_out', 'ln2_g', 'w_up',
           'ffn_conv_w', 'ffn_conv_b', 'w_down']
BIG_COL = ['w_in', 'w_a_out', 'w_b_out', 'w_c_out', 'w_up']
BIG_ROW = ['w_out', 'w_down']
BIG = BIG_COL + BIG_ROW
SMALL_COL = ['b_gate', 'cv_w', 'ffn_conv_w']
SMALL = [n for n in WEIGHTS if n not in BIG]


def _pc(body, **kw):
    return pl.pallas_call(body, **kw)


def _cp(*sem):
    return pltpu.CompilerParams(dimension_semantics=sem, vmem_limit_bytes=VMEM_LIMIT_BYTES)


def _sds(shape, dtype):
    return jax.ShapeDtypeStruct(shape, dtype)


_GELU_C = math.sqrt(2.0 / math.pi)
_GELU_A = 0.044715


def _sigmoid(x):
    return jax.nn.sigmoid(x)


def _gelu(x):
    return 0.5 * x * (1.0 + jnp.tanh(_GELU_C * (x + _GELU_A * x * x * x)))


def _gelu_grad(x):
    t = jnp.tanh(_GELU_C * (x + _GELU_A * x * x * x))
    return 0.5 * (1.0 + t) + 0.5 * x * (1.0 - t * t) * _GELU_C * (1.0 + 3.0 * _GELU_A * x * x)


def _silu(x):
    return x * _sigmoid(x)


def _silu_grad(x):
    s = _sigmoid(x)
    return s * (1.0 + x * (1.0 - s))


def _ln_stats(x):
    mu = jnp.mean(x, axis=-1, keepdims=True)
    xc = x - mu
    r = lax.rsqrt(jnp.mean(xc * xc, axis=-1, keepdims=True) + EPS)
    return xc * r, r


def _ln_bwd(dy, xhat, r, g):
    dxh = dy * g
    return r * (dxh - jnp.mean(dxh, axis=-1, keepdims=True) - xhat * jnp.mean(dxh * xhat, axis=-1, keepdims=True))


def _split_dot(x, m):
    hi = x.astype(BF16)
    lo = (x - hi.astype(F32)).astype(BF16)
    return jnp.dot(hi, m, preferred_element_type=F32) + jnp.dot(lo, m, preferred_element_type=F32)


def _rowsum0(x):
    return jnp.sum(x, axis=0, keepdims=True)


def _pick(n, prefs):
    for p in prefs:
        if n % p == 0:
            return p
    return n


def mm_nn(a, w, *, name, res=None, out_dtype=BF16):
    t, k = a.shape
    n = w.shape[1]
    tm = _pick(t, (512, 256))
    tn = _pick(n, (1024, 512, 256))

    def body(*refs):
        if res is None:
            a_ref, w_ref, o_ref = refs
        else:
            a_ref, w_ref, r_ref, o_ref = refs
        acc = jnp.dot(a_ref[...], w_ref[...], preferred_element_type=F32)
        if res is not None:
            acc = acc + r_ref[...]
        o_ref[...] = acc.astype(o_ref.dtype)

    in_specs = [pl.BlockSpec((tm, k), lambda i, j: (i, 0)), pl.BlockSpec((k, tn), lambda i, j: (0, j))]
    args = [a, w]
    if res is not None:
        in_specs.append(pl.BlockSpec((tm, tn), lambda i, j: (i, j)))
        args.append(res)
    return _pc(body, name=name, grid=(t // tm, n // tn), in_specs=in_specs,
               out_specs=pl.BlockSpec((tm, tn), lambda i, j: (i, j)),
               out_shape=_sds((t, n), out_dtype), compiler_params=_cp("parallel", "parallel"))(*args)


def mm_norm_nn(x, g, w, *, name):
    t, k = x.shape
    n = w.shape[1]
    tm = _pick(t, (512, 256))
    tn = _pick(n, (1664, 1408, 512))

    def body(x_ref, g_ref, w_ref, z_ref, h_ref):
        @pl.when(pl.program_id(1) == 0)
        def _():
            xv = x_ref[...]
            r = lax.rsqrt(jnp.mean(xv * xv, axis=-1, keepdims=True) + EPS)
            h_ref[...] = (xv * r * g_ref[...]).astype(BF16)

        z_ref[...] = jnp.dot(h_ref[...], w_ref[...], preferred_element_type=F32).astype(z_ref.dtype)

    return _pc(body, name=name, grid=(t // tm, n // tn),
               in_specs=[pl.BlockSpec((tm, k), lambda i, j: (i, 0)), pl.BlockSpec((1, k), lambda i, j: (0, 0)),
                         pl.BlockSpec((k, tn), lambda i, j: (0, j))],
               out_specs=[pl.BlockSpec((tm, tn), lambda i, j: (i, j)), pl.BlockSpec((tm, k), lambda i, j: (i, 0))],
               out_shape=[_sds((t, n), BF16), _sds((t, k), BF16)],
               compiler_params=_cp("parallel", "arbitrary"))(x, g, w)


def mm_nt(dy, w, *, name, out_dtype):
    t, n = dy.shape
    k = w.shape[0]
    tm = _pick(t, (512, 256))
    tn = _pick(n, (1664, 1408, 1024, 512))
    nj = n // tn

    def body(dy_ref, w_ref, o_ref, acc_ref):
        j = pl.program_id(1)
        p = lax.dot_general(dy_ref[...].astype(BF16), w_ref[...], NT_DIMS, preferred_element_type=F32)

        @pl.when(j == 0)
        def _():
            acc_ref[...] = p

        @pl.when(j > 0)
        def _():
            acc_ref[...] += p

        @pl.when(j == nj - 1)
        def _():
            o_ref[...] = acc_ref[...].astype(o_ref.dtype)

    return _pc(body, name=name, grid=(t // tm, nj),
               in_specs=[pl.BlockSpec((tm, tn), lambda i, j: (i, j)), pl.BlockSpec((k, tn), lambda i, j: (0, j))],
               out_specs=pl.BlockSpec((tm, k), lambda i, j: (i, 0)),
               out_shape=_sds((t, k), out_dtype), scratch_shapes=[pltpu.VMEM((tm, k), F32)],
               compiler_params=_cp("parallel", "arbitrary"))(dy, w)


def mm_tn(a, dy, *, name, out_dtype=BF16):
    t, k = a.shape
    n = dy.shape[1]
    tk = _pick(k, (512, 1408))
    tn = _pick(n, (1664, 1408, 1024, 512))
    tt = _pick(t, (512, 256))
    nt = t // tt

    def body(a_ref, dy_ref, o_ref, acc_ref):
        s = pl.program_id(2)
        p = lax.dot_general(a_ref[...], dy_ref[...].astype(BF16), TN_DIMS, preferred_element_type=F32)

        @pl.when(s == 0)
        def _():
            acc_ref[...] = p

        @pl.when(s > 0)
        def _():
            acc_ref[...] += p

        @pl.when(s == nt - 1)
        def _():
            o_ref[...] = acc_ref[...].astype(o_ref.dtype)

    return _pc(body, name=name, grid=(k // tk, n // tn, nt),
               in_specs=[pl.BlockSpec((tt, tk), lambda i, j, s: (s, i)), pl.BlockSpec((tt, tn), lambda i, j, s: (s, j))],
               out_specs=pl.BlockSpec((tk, tn), lambda i, j, s: (i, j)),
               out_shape=_sds((k, n), out_dtype), scratch_shapes=[pltpu.VMEM((tk, tn), F32)],
               compiler_params=_cp("parallel", "parallel", "arbitrary"))(a, dy)


def rms_bwd(dh, x, g, dres, *, name):
    t, d = x.shape
    tm = _pick(t, (256,))

    def body(dh_ref, x_ref, g_ref, dres_ref, dx_ref, dg_ref):
        xv = x_ref[...]
        r = lax.rsqrt(jnp.mean(xv * xv, axis=-1, keepdims=True) + EPS)
        xh = xv * r
        dy = dh_ref[...].astype(F32)
        dxh = dy * g_ref[...]
        dx_ref[...] = dres_ref[...] + r * (dxh - xh * jnp.mean(dxh * xh, axis=-1, keepdims=True))

        @pl.when(pl.program_id(0) == 0)
        def _():
            dg_ref[...] = jnp.zeros_like(dg_ref)

        dg_ref[...] += _rowsum0(dy * xh)

    row = pl.BlockSpec((tm, d), lambda i: (i, 0))
    vec = pl.BlockSpec((1, d), lambda i: (0, 0))
    return _pc(body, name=name, grid=(t // tm,), in_specs=[row, row, vec, row], out_specs=[row, vec],
               out_shape=[_sds((t, d), F32), _sds((1, d), F32)], compiler_params=_cp("arbitrary"))(dh, x, g, dres)


def loss_head(y, target, *, name):
    t, d = y.shape
    tm = _pick(t, (256,))

    def body(y_ref, t_ref, loss_ref, dy_ref):
        e = y_ref[...] - t_ref[...]
        dy_ref[...] = e * (1.0 / d)

        @pl.when(pl.program_id(0) == 0)
        def _():
            loss_ref[...] = jnp.zeros_like(loss_ref)

        loss_ref[...] += _rowsum0(jnp.sum(e * e, axis=1, keepdims=True)) * (0.5 / d)

    row = pl.BlockSpec((tm, d), lambda i: (i, 0))
    return _pc(body, name=name, grid=(t // tm,), in_specs=[row, row],
               out_specs=[pl.BlockSpec((1, 1), lambda i: (0, 0)), row],
               out_shape=[_sds((1, 1), F32), _sds((t, d), F32)], compiler_params=_cp("arbitrary"))(y, target)


def _sg_masks():
    lane = lax.broadcasted_iota(jnp.int32, (CHUNK, CHUNK), 1)
    row = lax.broadcasted_iota(jnp.int32, (CHUNK, CHUNK), 0)
    return lane < 64, lane <= row, row <= lane


def _sg_gate(vn_chunk, w_ref, bias_ref, p, first_group, tril):
    wa = jnp.where(tril, w_ref[2 * p], 0.0).astype(BF16)
    wb = jnp.where(tril, w_ref[2 * p + 1], 0.0).astype(BF16)
    oa = jnp.dot(wa, vn_chunk, preferred_element_type=F32)
    ob = jnp.dot(wb, vn_chunk, preferred_element_type=F32)
    return jnp.where(first_group, oa, ob) + bias_ref[:, p * 128:(p + 1) * 128]


def mixa_fwd(z, ln_g, ln_b, sg_w, sg_bias, *, name):
    t = z.shape[0]
    tm = _pick(t, (256,))

    def body(z_ref, g_ref, b_ref, w_ref, bias_ref, o_ref):
        first_group, tril, _ = _sg_masks()
        zv = z_ref[...].astype(F32)
        u = _gelu(zv[:, :SG_WIDTH])
        v = _gelu(zv[:, SG_WIDTH:])
        vh, _ = _ln_stats(v)
        vn = (vh * g_ref[...] + b_ref[...]).astype(BF16)
        for c in range(tm // CHUNK):
            rows = slice(c * CHUNK, (c + 1) * CHUNK)
            for p in range(4):
                cols = slice(p * 128, (p + 1) * 128)
                o = _sg_gate(vn[rows, cols], w_ref, bias_ref, p, first_group, tril)
                o_ref[rows, cols] = (u[rows, cols] * o).astype(o_ref.dtype)

    vec = pl.BlockSpec((1, SG_WIDTH), lambda i: (0, 0))
    return _pc(body, name=name, grid=(t // tm,),
               in_specs=[pl.BlockSpec((tm, 2 * SG_WIDTH), lambda i: (i, 0)), vec, vec,
                         pl.BlockSpec((8, CHUNK, CHUNK), lambda i: (0, 0, 0)),
                         pl.BlockSpec((CHUNK, SG_WIDTH), lambda i: (0, 0))],
               out_specs=pl.BlockSpec((tm, SG_WIDTH), lambda i: (i, 0)),
               out_shape=_sds((t, SG_WIDTH), BF16), compiler_params=_cp("parallel"))(z, ln_g, ln_b, sg_w, sg_bias)


def mixa_bwd(z, dga, ln_g, ln_b, sg_w, sg_wt, sg_bias, *, name):
    t = z.shape[0]
    tm = _pick(t, (256,))
    nsteps = t // tm

    def body(z_ref, dga_ref, g_ref, b_ref, w_ref, wt_ref, bias_ref, dz_ref, dw_ref, dsgb_ref, dg_ref, db_ref, dvn_s,
             dbias_ref):
        i = pl.program_id(0)
        first_group, tril, triu = _sg_masks()

        @pl.when(i == 0)
        def _():
            dw_ref[...] = jnp.zeros_like(dw_ref)
            dbias_ref[...] = jnp.zeros_like(dbias_ref)
            dg_ref[...] = jnp.zeros_like(dg_ref)
            db_ref[...] = jnp.zeros_like(db_ref)

        zv = z_ref[...].astype(F32)
        zu = zv[:, :SG_WIDTH]
        zg = zv[:, SG_WIDTH:]
        u = _gelu(zu)
        v = _gelu(zg)
        vh, r = _ln_stats(v)
        vn = (vh * g_ref[...] + b_ref[...]).astype(BF16)
        dga_v = dga_ref[...].astype(F32)
        d_o = dga_v * u
        for c in range(tm // CHUNK):
            rows = slice(c * CHUNK, (c + 1) * CHUNK)
            dbias_ref[...] += d_o[rows, :]
            for p in range(4):
                cols = slice(p * 128, (p + 1) * 128)
                vp = vn[rows, cols]
                o = _sg_gate(vp, w_ref, bias_ref, p, first_group, tril)
                dz_ref[rows, cols] = (dga_v[rows, cols] * o * _gelu_grad(zu[rows, cols])).astype(dz_ref.dtype)
                dop = d_o[rows, cols]
                dop_a = jnp.where(first_group, dop, 0.0).astype(BF16)
                dop_b = jnp.where(first_group, 0.0, dop).astype(BF16)
                dw_ref[2 * p] += lax.dot_general(dop_a, vp, NT_DIMS, preferred_element_type=F32)
                dw_ref[2 * p + 1] += lax.dot_general(dop_b, vp, NT_DIMS, preferred_element_type=F32)
                wta = jnp.where(triu, wt_ref[2 * p], 0.0).astype(BF16)
                wtb = jnp.where(triu, wt_ref[2 * p + 1], 0.0).astype(BF16)
                dop16 = dop.astype(BF16)
                dvn_s[rows, cols] = jnp.where(first_group, jnp.dot(wta, dop16, preferred_element_type=F32),
                                              jnp.dot(wtb, dop16, preferred_element_type=F32))
        dvn = dvn_s[...]
        dg_ref[...] += _rowsum0(dvn * vh)
        db_ref[...] += _rowsum0(dvn)
        dv = _ln_bwd(dvn, vh, r, g_ref[...])
        dz_ref[:, SG_WIDTH:] = (dv * _gelu_grad(zg)).astype(dz_ref.dtype)

        @pl.when(i == nsteps - 1)
        def _():
            for gi in range(8):
                dw_ref[gi] = jnp.where(tril, dw_ref[gi], 0.0)
            r_id = lax.broadcasted_iota(jnp.int32, (SG_WIDTH, 128), 0) // SB_HEAD_DIM
            c_id = lax.broadcasted_iota(jnp.int32, (SG_WIDTH, 128), 1)
            dsgb_ref[...] = _split_dot(dbias_ref[...], (r_id == c_id).astype(BF16))

    vec = pl.BlockSpec((1, SG_WIDTH), lambda i: (0, 0))
    wspec = pl.BlockSpec((8, CHUNK, CHUNK), lambda i: (0, 0, 0))
    bspec = pl.BlockSpec((CHUNK, SG_WIDTH), lambda i: (0, 0))
    sgb = pl.BlockSpec((CHUNK, 128), lambda i: (0, 0))
    return _pc(body, name=name, grid=(nsteps,),
               in_specs=[pl.BlockSpec((tm, 2 * SG_WIDTH), lambda i: (i, 0)), pl.BlockSpec((tm, SG_WIDTH), lambda i: (i, 0)),
                         vec, vec, wspec, wspec, bspec],
               out_specs=[pl.BlockSpec((tm, 2 * SG_WIDTH), lambda i: (i, 0)), wspec, sgb, vec, vec],
               out_shape=[_sds((t, 2 * SG_WIDTH), BF16), _sds((8, CHUNK, CHUNK), F32), _sds((CHUNK, 128), F32),
                          _sds((1, SG_WIDTH), F32), _sds((1, SG_WIDTH), F32)],
               scratch_shapes=[pltpu.VMEM((tm, SG_WIDTH), F32), pltpu.VMEM((CHUNK, SG_WIDTH), F32)],
               compiler_params=_cp("arbitrary"))(z, dga, ln_g, ln_b, sg_w, sg_wt, sg_bias)


def _glu(zv):
    return zv[:, :CV_WIDTH] * _sigmoid(zv[:, CV_WIDTH:])


def _cv_fill(zm_ref, zh_ref, x_s, i, tm):
    x_s[0:CV_HALO, :] = jnp.where(i > 0, _glu(zh_ref[...].astype(F32)), 0.0)
    x_s[CV_HALO:CV_HALO + tm, :] = _glu(zm_ref[...].astype(F32))


def _cv_conv(x_s, w_ref, cb_ref, tm):
    acc = jnp.zeros((tm, CV_WIDTH), F32) + cb_ref[...]
    for k in range(CV_KERNEL):
        acc = acc + w_ref[k:k + 1, :] * x_s[pl.ds(CV_HALO - (CV_KERNEL - 1) + k, tm), :]
    return acc


def _cv_specs(tm):
    zm = pl.BlockSpec((tm, 2 * CV_WIDTH), lambda i: (i, 1))
    zh = pl.BlockSpec((CV_HALO, 2 * CV_WIDTH), lambda i: (jnp.maximum(i * (tm // CV_HALO) - 1, 0), 1))
    w = pl.BlockSpec((CV_KERNEL, CV_WIDTH), lambda i: (0, 0))
    vec = pl.BlockSpec((1, CV_WIDTH), lambda i: (0, 0))
    return zm, zh, w, vec


def mixb_fwd(z, cv_w, cv_b, ln_g, ln_b, *, name):
    t = z.shape[0]
    tm = _pick(t, (256,))

    def body(zm_ref, zh_ref, w_ref, cb_ref, g_ref, b_ref, o_ref, x_s):
        _cv_fill(zm_ref, zh_ref, x_s, pl.program_id(0), tm)
        c1 = _cv_conv(x_s, w_ref, cb_ref, tm)
        ch, _ = _ln_stats(c1)
        o_ref[...] = _silu(ch * g_ref[...] + b_ref[...]).astype(o_ref.dtype)

    zm, zh, w, vec = _cv_specs(tm)
    return _pc(body, name=name, grid=(t // tm,), in_specs=[zm, zh, w, vec, vec, vec],
               out_specs=pl.BlockSpec((tm, CV_WIDTH), lambda i: (i, 0)), out_shape=_sds((t, CV_WIDTH), BF16),
               scratch_shapes=[pltpu.VMEM((CV_HALO + tm, CV_WIDTH), F32)],
               compiler_params=_cp("parallel"))(z, z, cv_w, cv_b, ln_g, ln_b)


def mixb_bwd1(z, dc3, cv_w, cv_b, ln_g, ln_b, *, name):
    t = z.shape[0]
    tm = _pick(t, (256,))

    def body(zm_ref, zh_ref, dc3_ref, w_ref, cb_ref, g_ref, b_ref, dc1_ref, dw_ref, dcb_ref, dg_ref, db_ref, x_s):
        i = pl.program_id(0)

        @pl.when(i == 0)
        def _():
            dw_ref[...] = jnp.zeros_like(dw_ref)
            dcb_ref[...] = jnp.zeros_like(dcb_ref)
            dg_ref[...] = jnp.zeros_like(dg_ref)
            db_ref[...] = jnp.zeros_like(db_ref)

        _cv_fill(zm_ref, zh_ref, x_s, i, tm)
        c1 = _cv_conv(x_s, w_ref, cb_ref, tm)
        ch, r = _ln_stats(c1)
        c2 = ch * g_ref[...] + b_ref[...]
        dc2 = dc3_ref[...].astype(F32) * _silu_grad(c2)
        dg_ref[...] += _rowsum0(dc2 * ch)
        db_ref[...] += _rowsum0(dc2)
        dc1 = _ln_bwd(dc2, ch, r, g_ref[...])
        dc1_ref[...] = dc1
        dcb_ref[...] += _rowsum0(dc1)
        for k in range(CV_KERNEL):
            dw_ref[k:k + 1, :] += _rowsum0(dc1 * x_s[pl.ds(CV_HALO - (CV_KERNEL - 1) + k, tm), :])

    zm, zh, w, vec = _cv_specs(tm)
    row = pl.BlockSpec((tm, CV_WIDTH), lambda i: (i, 0))
    return _pc(body, name=name, grid=(t // tm,), in_specs=[zm, zh, row, w, vec, vec, vec],
               out_specs=[row, w, vec, vec, vec],
               out_shape=[_sds((t, CV_WIDTH), F32), _sds((CV_KERNEL, CV_WIDTH), F32), _sds((1, CV_WIDTH), F32),
                          _sds((1, CV_WIDTH), F32), _sds((1, CV_WIDTH), F32)],
               scratch_shapes=[pltpu.VMEM((CV_HALO + tm, CV_WIDTH), F32)],
               compiler_params=_cp("arbitrary"))(z, z, dc3, cv_w, cv_b, ln_g, ln_b)


def mixb_bwd2(z, dc1, cv_w, *, name):
    t = z.shape[0]
    tm = _pick(t, (256,))
    nsteps = t // tm

    def body(zm_ref, dm_ref, dh_ref, w_ref, dz_ref, y_s):
        i = pl.program_id(0)
        y_s[0:tm, :] = dm_ref[...]
        y_s[tm:tm + CV_HALO, :] = jnp.where(i < nsteps - 1, dh_ref[...], 0.0)
        dc0 = jnp.zeros((tm, CV_WIDTH), F32)
        for k in range(CV_KERNEL):
            dc0 = dc0 + w_ref[k:k + 1, :] * y_s[pl.ds(CV_KERNEL - 1 - k, tm), :]
        zv = zm_ref[...].astype(F32)
        p = zv[:, :CV_WIDTH]
        s = _sigmoid(zv[:, CV_WIDTH:])
        dz_ref[:, :CV_WIDTH] = (dc0 * s).astype(dz_ref.dtype)
        dz_ref[:, CV_WIDTH:] = (dc0 * p * s * (1.0 - s)).astype(dz_ref.dtype)

    last = t // CV_HALO - 1
    return _pc(body, name=name, grid=(nsteps,),
               in_specs=[pl.BlockSpec((tm, 2 * CV_WIDTH), lambda i: (i, 1)),
                         pl.BlockSpec((tm, CV_WIDTH), lambda i: (i, 0)),
                         pl.BlockSpec((CV_HALO, CV_WIDTH), lambda i: (jnp.minimum((i + 1) * (tm // CV_HALO), last), 0)),
                         pl.BlockSpec((CV_KERNEL, CV_WIDTH), lambda i: (0, 0))],
               out_specs=pl.BlockSpec((tm, 2 * CV_WIDTH), lambda i: (i, 0)),
               out_shape=_sds((t, 2 * CV_WIDTH), BF16),
               scratch_shapes=[pltpu.VMEM((tm + CV_HALO, CV_WIDTH), F32)],
               compiler_params=_cp("parallel"))(z, dc1, dc1, cv_w)


def _group_ones():
    r = lax.broadcasted_iota(jnp.int32, (SB_WIDTH, SB_WIDTH), 0) // SB_HEAD_DIM
    c = lax.broadcasted_iota(jnp.int32, (SB_WIDTH, SB_WIDTH), 1) // SB_HEAD_DIM
    return (r == c).astype(BF16)


def attn_prep(z, gq, gk, *, name):
    t = z.shape[0]
    tm = _pick(t, (256,))
    scale = 1.0 / math.sqrt(SB_HEAD_DIM)

    def body(q_ref, k_ref, gq_ref, gk_ref, qo_ref, ko_ref):
        ones = _group_ones()
        for src, g_ref, dst, mul in ((q_ref, gq_ref, qo_ref, scale), (k_ref, gk_ref, ko_ref, 1.0)):
            v = src[...].astype(F32)
            r = lax.rsqrt(_split_dot(v * v, ones) * (1.0 / SB_HEAD_DIM) + EPS)
            dst[...] = ((v * r * g_ref[...]).astype(BF16).astype(F32) * mul).astype(dst.dtype)

    vec = pl.BlockSpec((1, SB_WIDTH), lambda i: (0, 0))
    row = pl.BlockSpec((tm, SB_WIDTH), lambda i: (i, 0))
    return _pc(body, name=name, grid=(t // tm,),
               in_specs=[pl.BlockSpec((tm, SB_WIDTH), lambda i: (i, COL_C // SB_WIDTH)),
                         pl.BlockSpec((tm, SB_WIDTH), lambda i: (i, COL_C // SB_WIDTH + 1)), vec, vec],
               out_specs=[row, row], out_shape=[_sds((t, SB_WIDTH), BF16), _sds((t, SB_WIDTH), BF16)],
               compiler_params=_cp("parallel"))(z, z, gq, gk)


_KB = Q_BLOCK
_PAIR = 2 * _KB


def _attn_tq(t):
    return _pick(t, (512, 256, 128))


def _attn_consts(tq):
    first_head = lax.broadcasted_iota(jnp.int32, (_KB, 128), 1) < SB_HEAD_DIM
    r2 = lax.broadcasted_iota(jnp.int32, (_PAIR, _PAIR), 0)
    c2 = lax.broadcasted_iota(jnp.int32, (_PAIR, _PAIR), 1)
    same = (r2 // _KB) == (c2 // _KB)
    m_suffix = (same & (r2 > c2)).astype(BF16)
    m_prefix = (same & (r2 < c2)).astype(BF16)
    row = lax.broadcasted_iota(jnp.int32, (tq, _PAIR), 0)
    col = lax.broadcasted_iota(jnp.int32, (tq, _PAIR), 1)
    return first_head, m_suffix, m_prefix, row, col & (_KB - 1), col < _KB


def _sb_logits(z, causal):
    sp = jnp.log(1.0 + jnp.exp(-jnp.abs(z)))
    g = jnp.minimum(z, 0.0) - sp
    l1m = g - z
    if causal is not None:
        l1m = jnp.where(causal, l1m, 0.0)
    return g, l1m


def _stack_heads(first_head, v):
    zero = jnp.zeros_like(v)
    return jnp.concatenate([jnp.where(first_head, v, zero), jnp.where(first_head, zero, v)], axis=0)


def _pair_sums(x):
    return jnp.sum(x[:, :_KB], axis=1, keepdims=True), jnp.sum(x[:, _KB:], axis=1, keepdims=True)


def _attn_specs(t, tq):
    qspec = pl.BlockSpec((tq, 128), lambda h, i: (i, h))
    kspec = pl.BlockSpec((t, 128), lambda h, i: (0, h))
    vspec = pl.BlockSpec((t, 128), lambda h, i: (0, (COL_C + 2 * SB_WIDTH) // 128 + h))
    return qspec, kspec, vspec


def attn_fwd(q, k, z, *, name):
    t = q.shape[0]
    tq = _attn_tq(t)
    nd = tq // _KB

    def body(q_ref, k_ref, v_ref, o_ref):
        qt = pl.program_id(1)
        first_head, m_suffix, _, row, key, is_first = _attn_consts(tq)
        qv = q_ref[...]

        def step(kb, state, causal):
            acc, ca, cb = state
            off = pl.multiple_of(kb * _KB, _KB)
            kcat = _stack_heads(first_head, k_ref[pl.ds(off, _KB), :])
            vcat = _stack_heads(first_head, v_ref[pl.ds(off, _KB), :])
            zz = lax.dot_general(qv, kcat, NT_DIMS, preferred_element_type=F32)
            g, l1m = _sb_logits(zz, causal)
            a = jnp.exp(g + _split_dot(l1m, m_suffix) + jnp.where(is_first, ca, cb))
            if causal is not None:
                a = jnp.where(causal, a, 0.0)
            sa, sb = _pair_sums(l1m)
            return acc + jnp.dot(a.astype(BF16), vcat, preferred_element_type=F32), ca + sa, cb + sb

        c0 = jnp.zeros((tq, 1), F32)
        state = (jnp.zeros((tq, 128), F32), c0, c0)
        for d in reversed(range(nd)):
            state = step(qt * nd + d, state, key + d * _KB < row)
        state = lax.fori_loop(0, qt * nd, lambda s, st: step(qt * nd - 1 - s, st, None), state)
        o_ref[...] = state[0].astype(o_ref.dtype)

    qspec, kspec, vspec = _attn_specs(t, tq)
    return _pc(body, name=name, grid=(SB_WIDTH // 128, t // tq), in_specs=[qspec, kspec, vspec], out_specs=qspec,
               out_shape=_sds((t, SB_WIDTH), BF16), compiler_params=_cp("parallel", "arbitrary"))(q, k, z)


def attn_bwd(q, k, z, do, *, name):
    t = q.shape[0]
    tq = _attn_tq(t)
    nd = tq // _KB
    nk = t // _KB

    def body(q_ref, k_ref, v_ref, do_ref, dq_ref, dk_ref, dv_ref, e_s, sg_s):
        qt = pl.program_id(1)
        first_head, m_suffix, m_prefix, row, key, is_first = _attn_consts(tq)

        @pl.when(qt == 0)
        def _():
            dk_ref[...] = jnp.zeros_like(dk_ref)
            dv_ref[...] = jnp.zeros_like(dv_ref)

        qv = q_ref[...]
        dov = do_ref[...]

        def halves(x):
            return jnp.where(first_head, x[:_KB], x[_KB:])

        def sweep1(kb, state, causal):
            ca, cb = state
            off = pl.multiple_of(kb * _KB, _KB)
            kcat = _stack_heads(first_head, k_ref[pl.ds(off, _KB), :])
            vcat = _stack_heads(first_head, v_ref[pl.ds(off, _KB), :])
            zz = lax.dot_general(qv, kcat, NT_DIMS, preferred_element_type=F32)
            g, l1m = _sb_logits(zz, causal)
            a = jnp.exp(g + _split_dot(l1m, m_suffix) + jnp.where(is_first, ca, cb))
            if causal is not None:
                a = jnp.where(causal, a, 0.0)
            da = lax.dot_general(dov, vcat, NT_DIMS, preferred_element_type=F32)
            e_s[kb] = a * da
            sg_s[kb] = jnp.exp(g).astype(BF16)
            dv_ref[pl.ds(off, _KB), :] += halves(lax.dot_general(a.astype(BF16), dov, TN_DIMS, preferred_element_type=F32))
            sa, sb = _pair_sums(l1m)
            return ca + sa, cb + sb

        def sweep2(kb, state, causal):
            dq, pa, pb = state
            off = pl.multiple_of(kb * _KB, _KB)
            kcat = _stack_heads(first_head, k_ref[pl.ds(off, _KB), :])
            e = e_s[kb]
            s = sg_s[kb].astype(F32)
            dz = e * (1.0 - s) - (jnp.where(is_first, pa, pb) + _split_dot(e, m_prefix)) * s
            if causal is not None:
                dz = jnp.where(causal, dz, 0.0)
            dz = dz.astype(BF16)
            dq = dq + jnp.dot(dz, kcat, preferred_element_type=F32)
            dk_ref[pl.ds(off, _KB), :] += halves(lax.dot_general(dz, qv, TN_DIMS, preferred_element_type=F32))
            sa, sb = _pair_sums(e)
            return dq, pa + sa, pb + sb

        c0 = jnp.zeros((tq, 1), F32)
        st1 = (c0, c0)
        for d in reversed(range(nd)):
            st1 = sweep1(qt * nd + d, st1, key + d * _KB < row)
        lax.fori_loop(0, qt * nd, lambda s, st: sweep1(qt * nd - 1 - s, st, None), st1)
        st2 = lax.fori_loop(0, qt * nd, lambda s, st: sweep2(s, st, None), (jnp.zeros((tq, 128), F32), c0, c0))
        for d in range(nd):
            st2 = sweep2(qt * nd + d, st2, key + d * _KB < row)
        dq_ref[...] = st2[0]

    qspec, kspec, vspec = _attn_specs(t, tq)
    acc = pl.BlockSpec((t, 128), lambda h, i: (0, h))
    return _pc(body, name=name, grid=(SB_WIDTH // 128, t // tq), in_specs=[qspec, kspec, vspec, qspec],
               out_specs=[qspec, acc, acc],
               out_shape=[_sds((t, SB_WIDTH), F32), _sds((t, SB_WIDTH), F32), _sds((t, SB_WIDTH), F32)],
               scratch_shapes=[pltpu.VMEM((nk, tq, _PAIR), F32), pltpu.VMEM((nk, tq, _PAIR), BF16)],
               compiler_params=_cp("parallel", "arbitrary"))(q, k, z, do)


def attn_post_bwd(z, dq, dk, dv, gq, gk, *, name):
    t = z.shape[0]
    tm = _pick(t, (256,))
    scale = 1.0 / math.sqrt(SB_HEAD_DIM)

    def body(q_ref, k_ref, dq_ref, dk_ref, dv_ref, gq_ref, gk_ref, dz_ref, dgq_ref, dgk_ref):
        ones = _group_ones()

        @pl.when(pl.program_id(0) == 0)
        def _():
            dgq_ref[...] = jnp.zeros_like(dgq_ref)
            dgk_ref[...] = jnp.zeros_like(dgk_ref)

        for idx, (src, d_ref, g_ref, dg_ref, mul) in enumerate(
                ((q_ref, dq_ref, gq_ref, dgq_ref, scale), (k_ref, dk_ref, gk_ref, dgk_ref, 1.0))):
            v = src[...].astype(F32)
            r = lax.rsqrt(_split_dot(v * v, ones) * (1.0 / SB_HEAD_DIM) + EPS)
            vh = v * r
            dn = d_ref[...] * mul
            dxh = dn * g_ref[...]
            m = _split_dot(dxh * vh, ones) * (1.0 / SB_HEAD_DIM)
            dz_ref[:, idx * SB_WIDTH:(idx + 1) * SB_WIDTH] = (r * (dxh - vh * m)).astype(dz_ref.dtype)
            s = _rowsum0(dn * vh)
            f = jnp.broadcast_to(s[:, 0:128] + s[:, 128:256] + s[:, 256:384] + s[:, 384:512], dg_ref.shape)
            dg_ref[...] += f + pltpu.roll(f, 64, 1)
        dz_ref[:, 2 * SB_WIDTH:] = dv_ref[...].astype(dz_ref.dtype)

    vec = pl.BlockSpec((1, SB_WIDTH), lambda i: (0, 0))
    row = pl.BlockSpec((tm, SB_WIDTH), lambda i: (i, 0))
    fold = pl.BlockSpec((8, 128), lambda i: (0, 0))
    return _pc(body, name=name, grid=(t // tm,),
               in_specs=[pl.BlockSpec((tm, SB_WIDTH), lambda i: (i, COL_C // SB_WIDTH)),
                         pl.BlockSpec((tm, SB_WIDTH), lambda i: (i, COL_C // SB_WIDTH + 1)), row, row, row, vec, vec],
               out_specs=[pl.BlockSpec((tm, 3 * SB_WIDTH), lambda i: (i, 0)), fold, fold],
               out_shape=[_sds((t, 3 * SB_WIDTH), BF16), _sds((8, 128), F32), _sds((8, 128), F32)],
               compiler_params=_cp("arbitrary"))(z, z, dq, dk, dv, gq, gk)


_GW = 512


def merge_fwd(z, ya, yb, yc, b_gate, *, name):
    t = z.shape[0]
    tm = _pick(t, (512, 256))

    def body(za_ref, zb_ref, zc_ref, ya_ref, yb_ref, yc_ref, bg_ref, o_ref):
        acc = jnp.zeros((tm, _GW), F32)
        for b, (zr, yr) in enumerate(((za_ref, ya_ref), (zb_ref, yb_ref), (zc_ref, yc_ref))):
            acc = acc + _sigmoid(zr[...].astype(F32) + bg_ref[b:b + 1, :]) * yr[...].astype(F32)
        o_ref[...] = acc.astype(o_ref.dtype)

    def zspec(b):
        return pl.BlockSpec((tm, _GW), lambda i, j: (i, COL_G // _GW + 2 * b + j))

    yspec = pl.BlockSpec((tm, _GW), lambda i, j: (i, j))
    return _pc(body, name=name, grid=(t // tm, D_MODEL // _GW),
               in_specs=[zspec(0), zspec(1), zspec(2), yspec, yspec, yspec, pl.BlockSpec((3, _GW), lambda i, j: (0, j))],
               out_specs=yspec, out_shape=_sds((t, D_MODEL), BF16),
               compiler_params=_cp("parallel", "parallel"))(z, z, z, ya, yb, yc, b_gate)


def merge_bwd(z, ya, yb, yc, b_gate, dm, *, name):
    t = z.shape[0]
    tm = _pick(t, (512, 256))

    def body(za_ref, zb_ref, zc_ref, ya_ref, yb_ref, yc_ref, bg_ref, dm_ref,
             dya_ref, dyb_ref, dyc_ref, dza_ref, dzb_ref, dzc_ref, dbg_ref):
        @pl.when(pl.program_id(1) == 0)
        def _():
            dbg_ref[...] = jnp.zeros_like(dbg_ref)

        dmv = dm_ref[...].astype(F32)
        for b, (zr, yr, dyr, dzr) in enumerate(((za_ref, ya_ref, dya_ref, dza_ref), (zb_ref, yb_ref, dyb_ref, dzb_ref),
                                                (zc_ref, yc_ref, dyc_ref, dzc_ref))):
            s = _sigmoid(zr[...].astype(F32) + bg_ref[b:b + 1, :])
            dyr[...] = (dmv * s).astype(dyr.dtype)
            dg = dmv * yr[...].astype(F32) * s * (1.0 - s)
            dzr[...] = dg.astype(dzr.dtype)
            dbg_ref[b:b + 1, :] += _rowsum0(dg)

    def zspec(b):
        return pl.BlockSpec((tm, _GW), lambda j, i: (i, COL_G // _GW + 2 * b + j))

    yspec = pl.BlockSpec((tm, _GW), lambda j, i: (i, j))
    bspec = pl.BlockSpec((3, _GW), lambda j, i: (0, j))
    full = _sds((t, D_MODEL), BF16)
    return _pc(body, name=name, grid=(D_MODEL // _GW, t // tm),
               in_specs=[zspec(0), zspec(1), zspec(2), yspec, yspec, yspec, bspec, yspec],
               out_specs=[yspec] * 6 + [bspec], out_shape=[full] * 6 + [_sds((3, D_MODEL), F32)],
               compiler_params=_cp("parallel", "arbitrary"))(z, z, z, ya, yb, yc, b_gate, dm)


_FW = 1408
_FH = D_FF // _FW


def _ffn_fill(m_ref, h_ref, x_s, i, tm):
    x_s[0:FFN_HALO, :] = jnp.where(i > 0, h_ref[...].astype(F32), 0.0)
    x_s[FFN_HALO:FFN_HALO + tm, :] = m_ref[...].astype(F32)


def _ffn_conv(x_s, w_ref, b_ref, tm):
    acc = jnp.zeros((tm, _FW), F32) + b_ref[...]
    for k in range(FFN_KERNEL):
        acc = acc + w_ref[k:k + 1, :] * x_s[pl.ds(FFN_HALO - (FFN_KERNEL - 1) + k, tm), :]
    return acc


def ffn_mid_fwd(up, cw, cb, *, name):
    t = up.shape[0]
    tm = _pick(t, (256,))

    def body(gm_ref, gh_ref, vm_ref, vh_ref, wg_ref, wv_ref, bg_ref, bv_ref, o_ref, xg_s, xv_s):
        i = pl.program_id(0)
        _ffn_fill(gm_ref, gh_ref, xg_s, i, tm)
        _ffn_fill(vm_ref, vh_ref, xv_s, i, tm)
        o_ref[...] = (_silu(_ffn_conv(xg_s, wg_ref, bg_ref, tm)) * _ffn_conv(xv_s, wv_ref, bv_ref, tm)).astype(o_ref.dtype)

    def main(off):
        return pl.BlockSpec((tm, _FW), lambda i, j: (i, j + off))

    def halo(off):
        return pl.BlockSpec((FFN_HALO, _FW), lambda i, j: (jnp.maximum(i * (tm // FFN_HALO) - 1, 0), j + off))

    def wspec(off):
        return pl.BlockSpec((FFN_KERNEL, _FW), lambda i, j: (0, j + off))

    def bspec(off):
        return pl.BlockSpec((1, _FW), lambda i, j: (0, j + off))

    return _pc(body, name=name, grid=(t // tm, _FH),
               in_specs=[main(0), halo(0), main(_FH), halo(_FH), wspec(0), wspec(_FH), bspec(0), bspec(_FH)],
               out_specs=pl.BlockSpec((tm, _FW), lambda i, j: (i, j)), out_shape=_sds((t, D_FF), BF16),
               scratch_shapes=[pltpu.VMEM((FFN_HALO + tm, _FW), F32), pltpu.VMEM((FFN_HALO + tm, _FW), F32)],
               compiler_params=_cp("parallel", "parallel"))(up, up, up, up, cw, cw, cb, cb)


def ffn_mid_bwd1(up, dact, cw, cb, *, name):
    t = up.shape[0]
    tm = _pick(t, (256,))

    def body(gm_ref, gh_ref, vm_ref, vh_ref, da_ref, wg_ref, wv_ref, bg_ref, bv_ref, d_ref, dw_ref, db_ref, xg_s, xv_s):
        j = pl.program_id(0)
        i = pl.program_id(1)

        @pl.when(i == 0)
        def _():
            dw_ref[...] = jnp.zeros_like(dw_ref)
            db_ref[...] = jnp.zeros_like(db_ref)

        _ffn_fill(gm_ref, gh_ref, xg_s, i, tm)
        _ffn_fill(vm_ref, vh_ref, xv_s, i, tm)
        gate = _ffn_conv(xg_s, wg_ref, bg_ref, tm)
        da = da_ref[...].astype(F32)

        def finish(d, x_s):
            d_ref[...] = d.astype(d_ref.dtype)
            db_ref[...] += _rowsum0(d)
            for k in range(FFN_KERNEL):
                dw_ref[k:k + 1, :] += _rowsum0(d * x_s[pl.ds(FFN_HALO - (FFN_KERNEL - 1) + k, tm), :])

        @pl.when(j < _FH)
        def _():
            finish(da * _ffn_conv(xv_s, wv_ref, bv_ref, tm) * _silu_grad(gate), xg_s)

        @pl.when(j >= _FH)
        def _():
            finish(da * _silu(gate), xv_s)

    def main(off):
        return pl.BlockSpec((tm, _FW), lambda j, i: (i, j % _FH + off))

    def halo(off):
        return pl.BlockSpec((FFN_HALO, _FW), lambda j, i: (jnp.maximum(i * (tm // FFN_HALO) - 1, 0), j % _FH + off))

    def wspec(off):
        return pl.BlockSpec((FFN_KERNEL, _FW), lambda j, i: (0, j % _FH + off))

    def bspec(off):
        return pl.BlockSpec((1, _FW), lambda j, i: (0, j % _FH + off))

    return _pc(body, name=name, grid=(2 * _FH, t // tm),
               in_specs=[main(0), halo(0), main(_FH), halo(_FH), pl.BlockSpec((tm, _FW), lambda j, i: (i, j % _FH)),
                         wspec(0), wspec(_FH), bspec(0), bspec(_FH)],
               out_specs=[pl.BlockSpec((tm, _FW), lambda j, i: (i, j)), pl.BlockSpec((FFN_KERNEL, _FW), lambda j, i: (0, j)),
                          pl.BlockSpec((1, _FW), lambda j, i: (0, j))],
               out_shape=[_sds((t, 2 * D_FF), BF16), _sds((FFN_KERNEL, 2 * D_FF), F32), _sds((1, 2 * D_FF), F32)],
               scratch_shapes=[pltpu.VMEM((FFN_HALO + tm, _FW), F32), pltpu.VMEM((FFN_HALO + tm, _FW), F32)],
               compiler_params=_cp("parallel", "arbitrary"))(up, up, up, up, dact, cw, cw, cb, cb)


def ffn_mid_bwd2(dupc, cw, *, name):
    t = dupc.shape[0]
    tm = _pick(t, (256,))
    nsteps = t // tm
    last = t // FFN_HALO - 1

    def body(m_ref, h_ref, w_ref, o_ref, y_s):
        i = pl.program_id(0)
        y_s[0:tm, :] = m_ref[...].astype(F32)
        y_s[tm:tm + FFN_HALO, :] = jnp.where(i < nsteps - 1, h_ref[...].astype(F32), 0.0)
        acc = jnp.zeros((tm, _FW), F32)
        for k in range(FFN_KERNEL):
            acc = acc + w_ref[k:k + 1, :] * y_s[pl.ds(FFN_KERNEL - 1 - k, tm), :]
        o_ref[...] = acc.astype(o_ref.dtype)

    return _pc(body, name=name, grid=(nsteps, 2 * _FH),
               in_specs=[pl.BlockSpec((tm, _FW), lambda i, j: (i, j)),
                         pl.BlockSpec((FFN_HALO, _FW), lambda i, j: (jnp.minimum((i + 1) * (tm // FFN_HALO), last), j)),
                         pl.BlockSpec((FFN_KERNEL, _FW), lambda i, j: (0, j))],
               out_specs=pl.BlockSpec((tm, _FW), lambda i, j: (i, j)), out_shape=_sds((t, 2 * D_FF), BF16),
               scratch_shapes=[pltpu.VMEM((tm + FFN_HALO, _FW), F32)],
               compiler_params=_cp("parallel", "parallel"))(dupc, dupc, cw)


def _vec(v):
    return v.reshape(1, -1)


def _layer_consts(p):
    return dict(
        sg_bias=jnp.repeat(p['sg_b'].T, SB_HEAD_DIM, axis=1),
        sg_wt=jnp.swapaxes(p['sg_w'], 1, 2),
        gq=jnp.tile(p['q_norm_g'], SB_WIDTH // SB_HEAD_DIM).reshape(1, -1),
        gk=jnp.tile(p['k_norm_g'], SB_WIDTH // SB_HEAD_DIM).reshape(1, -1),
    )


def layer_fwd(x, p):
    c = _layer_consts(p)
    z, h = mm_norm_nn(x, _vec(p['ln1_g']), p['w_in'], name="in_proj")
    ga = mixa_fwd(z, _vec(p['sg_ln_g']), _vec(p['sg_ln_b']), p['sg_w'], c['sg_bias'], name="mixa_fwd")
    cb = mixb_fwd(z, p['cv_w'], _vec(p['cv_b']), _vec(p['cv_ln_g']), _vec(p['cv_ln_b']), name="mixb_fwd")
    q, k = attn_prep(z, c['gq'], c['gk'], name="attn_prep")
    ao = attn_fwd(q, k, z, name="attn_fwd")
    ya = mm_nn(ga, p['w_a_out'], name="a_out")
    yb = mm_nn(cb, p['w_b_out'], name="b_out")
    yc = mm_nn(ao, p['w_c_out'], name="c_out")
    merged = merge_fwd(z, ya, yb, yc, p['b_gate'], name="merge_fwd")
    x1 = mm_nn(merged, p['w_out'], res=x, out_dtype=F32, name="out_proj")
    up, h2 = mm_norm_nn(x1, _vec(p['ln2_g']), p['w_up'], name="up_proj")
    act = ffn_mid_fwd(up, p['ffn_conv_w'], _vec(p['ffn_conv_b']), name="ffn_mid_fwd")
    x2 = mm_nn(act, p['w_down'], res=x1, out_dtype=F32, name="down_proj")
    saved = dict(x=x, z=z, h=h, ga=ga, cb=cb, q=q, k=k, ao=ao, ya=ya, yb=yb, yc=yc, merged=merged, x1=x1, up=up,
                 h2=h2, act=act)
    return x2, saved


def layer_bwd(dx2, p, s):
    c = _layer_consts(p)
    g = {}
    g['w_down'] = mm_tn(s['act'], dx2, name="d_w_down")
    dact = mm_nt(dx2, p['w_down'], out_dtype=BF16, name="d_act")
    dupc, g['ffn_conv_w'], dcb = ffn_mid_bwd1(s['up'], dact, p['ffn_conv_w'], _vec(p['ffn_conv_b']), name="ffn_mid_bwd1")
    g['ffn_conv_b'] = dcb.reshape(-1)
    dup = ffn_mid_bwd2(dupc, p['ffn_conv_w'], name="ffn_mid_bwd2")
    g['w_up'] = mm_tn(s['h2'], dup, name="d_w_up")
    dh2 = mm_nt(dup, p['w_up'], out_dtype=F32, name="d_h2")
    dx1, dg2 = rms_bwd(dh2, s['x1'], _vec(p['ln2_g']), dx2, name="ln2_bwd")
    g['ln2_g'] = dg2.reshape(-1)
    g['w_out'] = mm_tn(s['merged'], dx1, name="d_w_out")
    dm = mm_nt(dx1, p['w_out'], out_dtype=BF16, name="d_merged")
    dya, dyb, dyc, dzg0, dzg1, dzg2, g['b_gate'] = merge_bwd(s['z'], s['ya'], s['yb'], s['yc'], p['b_gate'], dm,
                                                             name="merge_bwd")
    g['w_a_out'] = mm_tn(s['ga'], dya, name="d_w_a_out")
    g['w_b_out'] = mm_tn(s['cb'], dyb, name="d_w_b_out")
    g['w_c_out'] = mm_tn(s['ao'], dyc, name="d_w_c_out")
    dga = mm_nt(dya, p['w_a_out'], out_dtype=BF16, name="d_ga")
    dcb3 = mm_nt(dyb, p['w_b_out'], out_dtype=BF16, name="d_cb")
    dao = mm_nt(dyc, p['w_c_out'], out_dtype=BF16, name="d_ao")
    dza, g['sg_w'], dsgb, dlg, dlb = mixa_bwd(s['z'], dga, _vec(p['sg_ln_g']), _vec(p['sg_ln_b']), p['sg_w'],
                                               c['sg_wt'], c['sg_bias'], name="mixa_bwd")
    g['sg_b'] = dsgb[:, :SG_WIDTH // SB_HEAD_DIM].T
    g['sg_ln_g'] = dlg.reshape(-1)
    g['sg_ln_b'] = dlb.reshape(-1)
    dc1, g['cv_w'], dcvb, dcg, dcbb = mixb_bwd1(s['z'], dcb3, p['cv_w'], _vec(p['cv_b']), _vec(p['cv_ln_g']),
                                                _vec(p['cv_ln_b']), name="mixb_bwd1")
    g['cv_b'] = dcvb.reshape(-1)
    g['cv_ln_g'] = dcg.reshape(-1)
    g['cv_ln_b'] = dcbb.reshape(-1)
    dzb = mixb_bwd2(s['z'], dc1, p['cv_w'], name="mixb_bwd2")
    dq, dk, dv = attn_bwd(s['q'], s['k'], s['z'], dao, name="attn_bwd")
    dzc, dgq, dgk = attn_post_bwd(s['z'], dq, dk, dv, c['gq'], c['gk'], name="attn_post_bwd")
    g['q_norm_g'] = dgq[0, :SB_HEAD_DIM]
    g['k_norm_g'] = dgk[0, :SB_HEAD_DIM]
    dz = jnp.concatenate([dza, dzb, dzc, dzg0, dzg1, dzg2], axis=1)
    g['w_in'] = mm_tn(s['h'], dz, name="d_w_in")
    dh = mm_nt(dz, p['w_in'], out_dtype=F32, name="d_h")
    dx, dg1 = rms_bwd(dh, s['x'], _vec(p['ln1_g']), dx1, name="ln1_bwd")
    g['ln1_g'] = dg1.reshape(-1)
    return dx, g


def local_step(x, target, layers):
    saved = []
    for p in layers:
        x, s = layer_fwd(x, p)
        saved.append(s)
    loss, dx = loss_head(x, target, name="loss_head")
    grads = [None] * len(layers)
    for l in reversed(range(len(layers))):
        dx, grads[l] = layer_bwd(dx, layers[l], saved[l])
    return loss[0, 0], dx, grads


def cast_bf16(w, *, name):
    r, c = w.shape
    tr = _pick(r, (512, 704, 256))

    def body(w_ref, o_ref):
        o_ref[...] = w_ref[...].astype(BF16)

    spec = pl.BlockSpec((tr, c), lambda i: (i, 0))
    return _pc(body, name=name, grid=(r // tr,), in_specs=[spec], out_specs=spec, out_shape=_sds((r, c), BF16),
               compiler_params=_cp("parallel"))(w)


def adamw(w, g, m, v, *, name):
    r, c = w.shape
    tr = _pick(r, (256, 704)) if r * c > 512 * 1024 else r

    def body(w_ref, g_ref, m_ref, v_ref, d_ref, mo_ref, vo_ref):
        gv = g_ref[...]
        mn = ADAM_B1 * m_ref[...] + (1.0 - ADAM_B1) * gv
        vn = ADAM_B2 * v_ref[...] + (1.0 - ADAM_B2) * (gv * gv)
        m_hat = mn / (1.0 - ADAM_B1 ** ADAM_STEP)
        v_hat = vn / (1.0 - ADAM_B2 ** ADAM_STEP)
        d_ref[...] = -ADAM_LR * (m_hat / (jnp.sqrt(v_hat) + ADAM_EPS) + ADAM_WD * w_ref[...])
        mo_ref[...] = mn
        vo_ref[...] = vn

    spec = pl.BlockSpec((tr, c), lambda i: (i, 0))
    out = _sds((r, c), F32)
    return _pc(body, name=name, grid=(r // tr,), in_specs=[spec] * 4, out_specs=[spec] * 3, out_shape=[out] * 3,
               compiler_params=_cp("parallel"))(w, g, m, v)


def _as3(a):
    return a if a.ndim == 3 else a.reshape((1,) + a.shape)


def add_half(g, recv, c_idx, *, name):
    s, rh, w = recv.shape
    tr = _pick(rh, (256, 352, 128))
    nb = rh // tr

    def body(c_ref, g_ref, r_ref, o_ref):
        o_ref[...] = (g_ref[...].astype(F32) + r_ref[...].astype(F32)).astype(o_ref.dtype)

    own = pl.BlockSpec((1, tr, w), lambda k, i, c_ref: (k, c_ref[0] * nb + i, 0))
    half = pl.BlockSpec((1, tr, w), lambda k, i, c_ref: (k, i, 0))
    gs = pltpu.PrefetchScalarGridSpec(num_scalar_prefetch=1, grid=(s, nb), in_specs=[own, half], out_specs=half)
    return _pc(body, name=name, grid_spec=gs, out_shape=_sds((s, rh, w), BF16),
               compiler_params=_cp("parallel", "parallel"))(c_idx, g, recv)


def sum_shard(p, recv, pos_idx, *, by_rows, name):
    _, rh, w = recv.shape
    tr = _pick(rh, (256, 352, 128))
    nb = rh // tr

    def body(pos_ref, p_ref, r_ref, o_ref):
        acc = p_ref[0].astype(F32)
        for j in range(N_CHIPS - 1):
            acc = acc + r_ref[j].astype(F32)
        o_ref[...] = acc

    if by_rows:
        own = pl.BlockSpec((1, tr, w), lambda i, pos_ref: (pos_ref[0], i, 0))
    else:
        own = pl.BlockSpec((1, tr, w), lambda i, pos_ref: (0, i, pos_ref[0]))
    gs = pltpu.PrefetchScalarGridSpec(num_scalar_prefetch=1, grid=(nb,),
                                      in_specs=[own, pl.BlockSpec((N_CHIPS - 1, tr, w), lambda i, pos_ref: (0, i, 0))],
                                      out_specs=pl.BlockSpec((tr, w), lambda i, pos_ref: (pos_ref[1] * nb + i, 0)))
    return _pc(body, name=name, grid_spec=gs, out_shape=_sds((2 * rh, w), F32),
               compiler_params=_cp("parallel"))(pos_idx, p, recv)


def sum_slots(slab, *, name):
    _, r, w = slab.shape
    tr = _pick(r, (512, 256, 8))

    def body(s_ref, o_ref):
        acc = s_ref[0]
        for j in range(1, N_DEV):
            acc = acc + s_ref[j]
        o_ref[...] = acc

    return _pc(body, name=name, grid=(r // tr,), in_specs=[pl.BlockSpec((N_DEV, tr, w), lambda i: (0, i, 0))],
               out_specs=pl.BlockSpec((tr, w), lambda i: (i, 0)), out_shape=_sds((r, w), F32),
               compiler_params=_cp("parallel"))(slab)


def _mesh_pos():
    x, y, c = lax.axis_index("x"), lax.axis_index("y"), lax.axis_index("c")
    others = [(1 - x, y), (x, 1 - y), (1 - x, 1 - y)]
    return x, y, c, others


def _rcopy(src, dst, ssem, rsem, k, dev):
    return pltpu.make_async_remote_copy(src_ref=src, dst_ref=dst, send_sem=ssem.at[k], recv_sem=rsem.at[k],
                                        device_id=dev, device_id_type=MESH)


def _comm_call(body, name, n_in, out_shape, n_local, n_remote):
    scratch = [pltpu.SemaphoreType.DMA((max(n_local, 1),)), pltpu.SemaphoreType.DMA((n_remote,)),
               pltpu.SemaphoreType.DMA((n_remote,))]
    return _pc(body, name=name, in_specs=[ANY] * n_in, out_specs=[ANY] * len(out_shape), out_shape=out_shape,
               scratch_shapes=scratch)


def allgather_weights(layer, shards, *, name):
    names = BIG + SMALL_COL
    nn = len(names)

    def full_shape(n):
        _, r, c = shards[n].shape
        return (N_CHIPS, r, c) if n in BIG_ROW else (r, N_CHIPS * c)

    def body(*refs):
        ins, outs = refs[:nn], refs[nn:2 * nn]
        lsem, ssem, rsem = refs[2 * nn:]
        x, y, c, others = _mesh_pos()
        me = 2 * x + y
        sib = (x, y, 1 - c)

        def dwin(i, k, h):
            n = names[i]
            _, r, cs = shards[n].shape
            if n in BIG_ROW:
                return outs[i].at[k] if h is None else outs[i].at[k, pl.ds(h * (r // 2), r // 2), :]
            cols = pl.ds(pl.multiple_of(k * cs, 128), cs)
            return outs[i].at[:, cols] if h is None else outs[i].at[pl.ds(h * (r // 2), r // 2), cols]

        def swin(i, h):
            r = shards[names[i]].shape[1]
            return ins[i].at[layer] if h is None else ins[i].at[layer, pl.ds(h * (r // 2), r // 2), :]

        def half(i):
            return c if names[i] in BIG else None

        started = []
        for i in range(nn):
            mine = pltpu.make_async_copy(swin(i, None), dwin(i, me, None), lsem.at[i])
            mine.start()
            started.append(mine)
        sends = []
        for i in range(nn):
            for j, chip in enumerate(others):
                cp = _rcopy(swin(i, half(i)), dwin(i, me, half(i)), ssem, rsem, 6 * i + j, (*chip, c))
                cp.start()
                sends.append(cp)
        for j, chip in enumerate(others):
            kk = 2 * chip[0] + chip[1]
            for i in range(nn):
                got = dwin(i, kk, half(i))
                _rcopy(got, got, ssem, rsem, 6 * i + j, (*chip, c)).wait_recv()
                if names[i] in BIG:
                    fwd = _rcopy(got, got, ssem, rsem, 6 * i + 3 + j, sib)
                    fwd.start()
                    sends.append(fwd)
        for j, chip in enumerate(others):
            kk = 2 * chip[0] + chip[1]
            for i in range(nn):
                if names[i] in BIG:
                    got = dwin(i, kk, 1 - c)
                    _rcopy(got, got, ssem, rsem, 6 * i + 3 + j, sib).wait_recv()
        for cp in sends:
            cp.wait_send()
        for cp in started:
            cp.wait()

    out_shape = [_sds(full_shape(n), shards[n].dtype) for n in names]
    outs = _comm_call(body, name, nn, out_shape, nn, 6 * nn)(*[shards[n] for n in names])
    full = {}
    for n, o in zip(names, outs):
        full[n] = o.reshape(o.shape[0] * o.shape[1], o.shape[2]) if n in BIG_ROW else o
    return full


def _grad_view(n, g):
    return g.reshape(N_CHIPS, g.shape[0] // N_CHIPS, g.shape[1]) if n in BIG_ROW else g.reshape((1,) + g.shape)


def exchange_halves(gv, *, name):
    nn = len(BIG)

    def body(*refs):
        ins, outs = refs[:nn], refs[nn:2 * nn]
        _, ssem, rsem = refs[2 * nn:]
        x, y, c, _ = _mesh_pos()
        cps = []
        for i in range(nn):
            rh = gv[i].shape[1] // 2
            cp = _rcopy(ins[i].at[:, pl.ds((1 - c) * rh, rh), :], outs[i], ssem, rsem, i, (x, y, 1 - c))
            cp.start()
            cps.append(cp)
        for cp in cps:
            cp.wait()

    out_shape = [_sds((a.shape[0], a.shape[1] // 2, a.shape[2]), a.dtype) for a in gv]
    return _comm_call(body, name, nn, out_shape, 0, nn)(*gv)


def scatter_partials(pv, *, name):
    nn = len(BIG)

    def shard_shape(i):
        s, rh, w = pv[i].shape
        return (rh, w) if BIG[i] in BIG_ROW else (rh, w // N_CHIPS)

    def body(*refs):
        ins, outs = refs[:nn], refs[nn:2 * nn]
        _, ssem, rsem = refs[2 * nn:]
        x, y, c, others = _mesh_pos()
        cps = []
        for i in range(nn):
            rh, ws = shard_shape(i)
            for j, chip in enumerate(others):
                kk = 2 * chip[0] + chip[1]
                if BIG[i] in BIG_ROW:
                    src = ins[i].at[kk]
                else:
                    src = ins[i].at[0, :, pl.ds(pl.multiple_of(kk * ws, 128), ws)]
                cp = _rcopy(src, outs[i].at[j], ssem, rsem, 3 * i + j, (*chip, c))
                cp.start()
                cps.append(cp)
        for cp in cps:
            cp.wait()

    out_shape = [_sds((N_CHIPS - 1,) + shard_shape(i), pv[i].dtype) for i in range(nn)]
    return _comm_call(body, name, nn, out_shape, 0, 3 * nn)(*pv)


def join_halves(rv, *, name):
    nn = len(BIG)

    def body(*refs):
        ins, outs = refs[:nn], refs[nn:2 * nn]
        _, ssem, rsem = refs[2 * nn:]
        x, y, c, _ = _mesh_pos()
        cps = []
        for i in range(nn):
            rh = rv[i].shape[0] // 2
            rows = pl.ds(c * rh, rh)
            cp = _rcopy(ins[i].at[rows, :], outs[i].at[rows, :], ssem, rsem, i, (x, y, 1 - c))
            cp.start()
            cps.append(cp)
        for i, cp in enumerate(cps):
            cp.wait_send()
            rh = rv[i].shape[0] // 2
            got = outs[i].at[pl.ds((1 - c) * rh, rh), :]
            _rcopy(got, got, ssem, rsem, i, (x, y, 1 - c)).wait_recv()

    out_shape = [_sds(a.shape, a.dtype) for a in rv]
    scratch = [pltpu.SemaphoreType.DMA((1,)), pltpu.SemaphoreType.DMA((nn,)), pltpu.SemaphoreType.DMA((nn,))]
    return _pc(body, name=name, in_specs=[ANY] * nn, out_specs=[ANY] * nn, out_shape=out_shape, scratch_shapes=scratch,
               input_output_aliases={i: i for i in range(nn)})(*rv)


def reduce_scatter_layer(grads, c_idx, pos_idx):
    gv = [_grad_view(n, grads[n]) for n in BIG]
    recv = exchange_halves(gv, name="rs_exchange_halves")
    pv = [add_half(g, r, c_idx, name="rs_add_" + n) for n, g, r in zip(BIG, gv, recv)]
    got = scatter_partials(pv, name="rs_scatter_partials")
    rv = [sum_shard(p, r, pos_idx, by_rows=n in BIG_ROW, name="rs_sum_" + n) for n, p, r in zip(BIG, pv, got)]
    return dict(zip(BIG, join_halves(rv, name="rs_join_halves")))


def allgather_slab(slab, *, name):
    r, w = slab.shape

    def body(x_ref, out_ref, lsem, ssem, rsem):
        x, y, c, others = _mesh_pos()
        me, sib = (x, y, c), (x, y, 1 - c)

        def slot(px, py, pc):
            return out_ref.at[4 * px + 2 * py + pc]

        mine = pltpu.make_async_copy(x_ref, slot(*me), lsem.at[0])
        mine.start()
        first = [_rcopy(x_ref, slot(*me), ssem, rsem, 0, sib)]
        first += [_rcopy(x_ref, slot(*me), ssem, rsem, 1 + j, (*chip, c)) for j, chip in enumerate(others)]
        for cp in first:
            cp.start()
        passed = [_rcopy(slot(*chip, c), slot(*chip, c), ssem, rsem, 4 + j, sib) for j, chip in enumerate(others)]
        for j, chip in enumerate(others):
            _rcopy(slot(*chip, c), slot(*chip, c), ssem, rsem, 1 + j, me).wait_recv()
            passed[j].start()
        _rcopy(slot(*sib), slot(*sib), ssem, rsem, 0, me).wait_recv()
        for j, chip in enumerate(others):
            _rcopy(slot(*chip, 1 - c), slot(*chip, 1 - c), ssem, rsem, 4 + j, me).wait_recv()
        for cp in first + passed:
            cp.wait_send()
        mine.wait()

    return _comm_call(body, name, 1, [_sds((N_DEV, r, w), slab.dtype)], 1, 7)(slab)[0]


def _pad128(n):
    return -(-n // 128) * 128


def _pack_small(grads, shapes):
    parts = []
    for g in grads:
        for n in SMALL:
            v = g[n].astype(F32).reshape(-1)
            parts.append(jnp.pad(v, (0, _pad128(v.shape[0]) - v.shape[0])))
    flat = jnp.concatenate(parts)
    rows = -(-flat.shape[0] // (128 * 512)) * 512
    return jnp.pad(flat, (0, rows * 128 - flat.shape[0])).reshape(rows, 128)


def _unpack_small(slab, shapes, depth):
    flat = slab.reshape(-1)
    out = {n: [] for n in SMALL}
    off = 0
    for _ in range(depth):
        for n in SMALL:
            size = math.prod(shapes[n])
            out[n].append(flat[off:off + size].reshape(shapes[n]))
            off += _pad128(size)
    return {n: jnp.stack(v) for n, v in out.items()}


def _adamw_nd(w, g, m, v, name):
    shp = w.shape
    two = lambda a: a.reshape(-1, shp[-1])
    return tuple(o.reshape(shp) for o in adamw(two(w), two(g), two(m), two(v), name=name))


def kernel(x, ln1_g, w_in, b_gate, sg_ln_g, sg_ln_b, sg_w, sg_b, w_a_out, cv_w, cv_b, cv_ln_g, cv_ln_b, w_b_out, q_norm_g, k_norm_g, w_c_out, w_out, ln2_g, w_up, ffn_conv_w, ffn_conv_b, w_down, loss_target, m_ln1_g, m_w_in, m_b_gate, m_sg_ln_g, m_sg_ln_b, m_sg_w, m_sg_b, m_w_a_out, m_cv_w, m_cv_b, m_cv_ln_g, m_cv_ln_b, m_w_b_out, m_q_norm_g, m_k_norm_g, m_w_c_out, m_w_out, m_ln2_g, m_w_up, m_ffn_conv_w, m_ffn_conv_b, m_w_down, v_ln1_g, v_w_in, v_b_gate, v_sg_ln_g, v_sg_ln_b, v_sg_w, v_sg_b, v_w_a_out, v_cv_w, v_cv_b, v_cv_ln_g, v_cv_ln_b, v_w_b_out, v_q_norm_g, v_k_norm_g, v_w_c_out, v_w_out, v_ln2_g, v_w_up, v_ffn_conv_w, v_ffn_conv_b, v_w_down):
    w = dict(ln1_g=ln1_g, w_in=w_in, b_gate=b_gate, sg_ln_g=sg_ln_g, sg_ln_b=sg_ln_b, sg_w=sg_w, sg_b=sg_b,
             w_a_out=w_a_out, cv_w=cv_w, cv_b=cv_b, cv_ln_g=cv_ln_g, cv_ln_b=cv_ln_b, w_b_out=w_b_out,
             q_norm_g=q_norm_g, k_norm_g=k_norm_g, w_c_out=w_c_out, w_out=w_out, ln2_g=ln2_g, w_up=w_up,
             ffn_conv_w=ffn_conv_w, ffn_conv_b=ffn_conv_b, w_down=w_down)
    m = dict(ln1_g=m_ln1_g, w_in=m_w_in, b_gate=m_b_gate, sg_ln_g=m_sg_ln_g, sg_ln_b=m_sg_ln_b, sg_w=m_sg_w,
             sg_b=m_sg_b, w_a_out=m_w_a_out, cv_w=m_cv_w, cv_b=m_cv_b, cv_ln_g=m_cv_ln_g, cv_ln_b=m_cv_ln_b,
             w_b_out=m_w_b_out, q_norm_g=m_q_norm_g, k_norm_g=m_k_norm_g, w_c_out=m_w_c_out, w_out=m_w_out,
             ln2_g=m_ln2_g, w_up=m_w_up, ffn_conv_w=m_ffn_conv_w, ffn_conv_b=m_ffn_conv_b, w_down=m_w_down)
    v = dict(ln1_g=v_ln1_g, w_in=v_w_in, b_gate=v_b_gate, sg_ln_g=v_sg_ln_g, sg_ln_b=v_sg_ln_b, sg_w=v_sg_w,
             sg_b=v_sg_b, w_a_out=v_w_a_out, cv_w=v_cv_w, cv_b=v_cv_b, cv_ln_g=v_cv_ln_g, cv_ln_b=v_cv_ln_b,
             w_b_out=v_w_b_out, q_norm_g=v_q_norm_g, k_norm_g=v_k_norm_g, w_c_out=v_w_c_out, w_out=v_w_out,
             ln2_g=v_ln2_g, w_up=v_w_up, ffn_conv_w=v_ffn_conv_w, ffn_conv_b=v_ffn_conv_b, w_down=v_w_down)
    depth = ln1_g.shape[0]
    cx, cy, cc = lax.axis_index("x"), lax.axis_index("y"), lax.axis_index("c")
    me = 2 * cx + cy
    pos_idx = jnp.stack([me, cc]).astype(jnp.int32)
    c_idx = jnp.reshape(cc, (1,)).astype(jnp.int32)

    shards = {}
    for n in BIG:
        a = w[n]
        shards[n] = cast_bf16(a.reshape(-1, a.shape[-1]), name="cast_" + n).reshape(a.shape)
    for n in SMALL_COL:
        shards[n] = jnp.pad(w[n], ((0, 0), (0, -w[n].shape[1] % 8), (0, 0)))
    layers = []
    for l in range(depth):
        p = allgather_weights(l, shards, name="allgather_weights")
        for n in SMALL:
            p[n] = p[n][:w[n].shape[1]] if n in SMALL_COL else w[n][l]
        layers.append(p)

    loss, dx, grads = local_step(x[0], loss_target[0], layers)
    loss = lax.psum(loss, ("x", "y", "c"))

    big = [reduce_scatter_layer(g, c_idx, pos_idx) for g in grads]
    full_shapes = {n: layers[0][n].shape for n in SMALL}
    slab = sum_slots(allgather_slab(_pack_small(grads, full_shapes), name="allgather_small_grads"), name="sum_small_grads")
    small = _unpack_small(slab, full_shapes, depth)
    grad = {n: jnp.stack([b[n] for b in big]) for n in BIG}
    for n in SMALL:
        if n in SMALL_COL:
            cs = w[n].shape[-1]
            grad[n] = lax.dynamic_slice_in_dim(small[n], me * cs, cs, axis=small[n].ndim - 1)
        else:
            grad[n] = small[n]

    delta, new_m, new_v = {}, {}, {}
    for n in WEIGHTS:
        delta[n], new_m[n], new_v[n] = _adamw_nd(w[n], grad[n], m[n], v[n], "adamw_" + n)
    return (loss, dx[None], *[grad[n] for n in WEIGHTS], *[delta[n] for n in WEIGHTS],
            *[new_m[n] for n in WEIGHTS], *[new_v[n] for n in WEIGHTS])
```

```python
import functools
import math

import jax
import jax.numpy as jnp
from jax import lax
from jax.experimental import pallas as pl
from jax.experimental.pallas import tpu as pltpu

F32 = jnp.float32
BF16 = jnp.bfloat16
MESH = pl.DeviceIdType.MESH
ANY = pl.BlockSpec(memory_space=pl.ANY)

EPS = 1e-6
D_MODEL = 1024
DEPTH = 4
SG_WIDTH = 512
CHUNK = 128
CV_WIDTH = 512
CV_KERNEL = 31
SB_WIDTH = 512
SB_HEAD_DIM = 64
Q_BLOCK = 128
D_FF = 2816
FFN_KERNEL = 3
COL_B = 1024
COL_C = 2048
COL_G = 3584
IN_COLS = 6656
N_CHIPS = 4
N_DEV = 8
CV_HALO = 32
FFN_HALO = 16

ADAM_LR = 0.001
ADAM_B1 = 0.9
ADAM_B2 = 0.999
ADAM_EPS = 1e-08
ADAM_WD = 0.01
ADAM_STEP = 10

VMEM_LIMIT_BYTES = 56 * 1024 * 1024

NT_DIMS = (((1,), (1,)), ((), ()))
TN_DIMS = (((0,), (0,)), ((), ()))

WEIGHTS = ['ln1_g', 'w_in', 'b_gate', 'sg_ln_g', 'sg_ln_b', 'sg_w', 'sg_b', 'w_a_out', 'cv_w', 'cv_b',
           'cv_ln_g', 'cv_ln_b', 'w_b_out', 'q_norm_g', 'k_norm_g', 'w_c_out', 'w_out', 'ln2_g', 'w_up',
           'ffn_conv_w', 'ffn_conv_b', 'w_down']
BIG_COL = ['w_in', 'w_a_out', 'w_b_out', 'w_c_out', 'w_up']
BIG_ROW = ['w_out', 'w_down']
BIG = BIG_COL + BIG_ROW
SMALL_COL = ['b_gate', 'cv_w', 'ffn_conv_w']
SMALL = [n for n in WEIGHTS if n not in BIG]


def _pc(body, **kw):
    return pl.pallas_call(body, **kw)


def _cp(*sem):
    return pltpu.CompilerParams(dimension_semantics=sem, vmem_limit_bytes=VMEM_LIMIT_BYTES)


def _sds(shape, dtype):
    return jax.ShapeDtypeStruct(shape, dtype)


_GELU_C = math.sqrt(2.0 / math.pi)
_GELU_A = 0.044715


def _sigmoid(x):
    return jax.nn.sigmoid(x)


def _gelu(x):
    return 0.5 * x * (1.0 + jnp.tanh(_GELU_C * (x + _GELU_A * x * x * x)))


def _gelu_grad(x):
    t = jnp.tanh(_GELU_C * (x + _GELU_A * x * x * x))
    return 0.5 * (1.0 + t) + 0.5 * x * (1.0 - t * t) * _GELU_C * (1.0 + 3.0 * _GELU_A * x * x)


def _silu(x):
    return x * _sigmoid(x)


def _silu_grad(x):
    s = _sigmoid(x)
    return s * (1.0 + x * (1.0 - s))


def _ln_stats(x):
    mu = jnp.mean(x, axis=-1, keepdims=True)
    xc = x - mu
    r = lax.rsqrt(jnp.mean(xc * xc, axis=-1, keepdims=True) + EPS)
    return xc * r, r


def _ln_bwd(dy, xhat, r, g):
    dxh = dy * g
    return r * (dxh - jnp.mean(dxh, axis=-1, keepdims=True) - xhat * jnp.mean(dxh * xhat, axis=-1, keepdims=True))


def _split_dot(x, m):
    hi = x.astype(BF16)
    lo = (x - hi.astype(F32)).astype(BF16)
    return jnp.dot(hi, m, preferred_element_type=F32) + jnp.dot(lo, m, preferred_element_type=F32)


def _rowsum0(x):
    return jnp.sum(x, axis=0, keepdims=True)


def _pick(n, prefs):
    for p in prefs:
        if n % p == 0:
            return p
    return n


def mm_nn(a, w, *, name, res=None, out_dtype=BF16):
    t, k = a.shape
    n = w.shape[1]
    tm = _pick(t, (512, 256))
    tn = _pick(n, (1024, 512, 256))

    def body(*refs):
        if res is None:
            a_ref, w_ref, o_ref = refs
        else:
            a_ref, w_ref, r_ref, o_ref = refs
        acc = jnp.dot(a_ref[...], w_ref[...], preferred_element_type=F32)
        if res is not None:
            acc = acc + r_ref[...]
        o_ref[...] = acc.astype(o_ref.dtype)

    in_specs = [pl.BlockSpec((tm, k), lambda i, j: (i, 0)), pl.BlockSpec((k, tn), lambda i, j: (0, j))]
    args = [a, w]
    if res is not None:
        in_specs.append(pl.BlockSpec((tm, tn), lambda i, j: (i, j)))
        args.append(res)
    return _pc(body, name=name, grid=(t // tm, n // tn), in_specs=in_specs,
               out_specs=pl.BlockSpec((tm, tn), lambda i, j: (i, j)),
               out_shape=_sds((t, n), out_dtype), compiler_params=_cp("parallel", "parallel"))(*args)


def mm_norm_nn(x, g, w, *, name, after=()):
    t, k = x.shape
    n = w.shape[1]
    tm = _pick(t, (512, 256))
    tn = _pick(n, (1664, 1408, 512))

    def body(x_ref, g_ref, w_ref, *rest):
        z_ref, h_ref = rest[len(after):]

        @pl.when(pl.program_id(1) == 0)
        def _():
            xv = x_ref[...]
            r = lax.rsqrt(jnp.mean(xv * xv, axis=-1, keepdims=True) + EPS)
            h_ref[...] = (xv * r * g_ref[...]).astype(BF16)

        z_ref[...] = jnp.dot(h_ref[...], w_ref[...], preferred_element_type=F32).astype(z_ref.dtype)

    return _pc(body, name=name, grid=(t // tm, n // tn),
               in_specs=[pl.BlockSpec((tm, k), lambda i, j: (i, 0)), pl.BlockSpec((1, k), lambda i, j: (0, 0)),
                         pl.BlockSpec((k, tn), lambda i, j: (0, j))]
               + [pl.BlockSpec(a.shape, lambda i, j: (0, 0)) for a in after],
               out_specs=[pl.BlockSpec((tm, tn), lambda i, j: (i, j)), pl.BlockSpec((tm, k), lambda i, j: (i, 0))],
               out_shape=[_sds((t, n), BF16), _sds((t, k), BF16)],
               compiler_params=_cp("parallel", "arbitrary"))(x, g, w, *after)


def mm_nt(dy, w, *, name, out_dtype):
    t, n = dy.shape
    k = w.shape[0]
    tm = _pick(t, (512, 256))
    tn = _pick(n, (1664, 1408, 1024, 512))
    nj = n // tn

    def body(dy_ref, w_ref, o_ref, acc_ref):
        j = pl.program_id(1)
        p = lax.dot_general(dy_ref[...].astype(BF16), w_ref[...], NT_DIMS, preferred_element_type=F32)

        @pl.when(j == 0)
        def _():
            acc_ref[...] = p

        @pl.when(j > 0)
        def _():
            acc_ref[...] += p

        @pl.when(j == nj - 1)
        def _():
            o_ref[...] = acc_ref[...].astype(o_ref.dtype)

    return _pc(body, name=name, grid=(t // tm, nj),
               in_specs=[pl.BlockSpec((tm, tn), lambda i, j: (i, j)), pl.BlockSpec((k, tn), lambda i, j: (0, j))],
               out_specs=pl.BlockSpec((tm, k), lambda i, j: (i, 0)),
               out_shape=_sds((t, k), out_dtype), scratch_shapes=[pltpu.VMEM((tm, k), F32)],
               compiler_params=_cp("parallel", "arbitrary"))(dy, w)


def mm_tn(a, dy, *, name, out_dtype=BF16, after=()):
    t, k = a.shape
    n = dy.shape[1]
    tk = _pick(k, (512, 1408))
    tn = _pick(n, (1664, 1408, 1024, 512))
    tt = _pick(t, (512, 256))
    nt = t // tt

    def body(a_ref, dy_ref, *rest):
        o_ref, acc_ref = rest[len(after):]
        s = pl.program_id(2)
        p = lax.dot_general(a_ref[...], dy_ref[...].astype(BF16), TN_DIMS, preferred_element_type=F32)

        @pl.when(s == 0)
        def _():
            acc_ref[...] = p

        @pl.when(s > 0)
        def _():
            acc_ref[...] += p

        @pl.when(s == nt - 1)
        def _():
            o_ref[...] = acc_ref[...].astype(o_ref.dtype)

    return _pc(body, name=name, grid=(k // tk, n // tn, nt),
               in_specs=[pl.BlockSpec((tt, tk), lambda i, j, s: (s, i)), pl.BlockSpec((tt, tn), lambda i, j, s: (s, j))]
               + [pl.BlockSpec(tok.shape, lambda i, j, s: (0, 0)) for tok in after],
               out_specs=pl.BlockSpec((tk, tn), lambda i, j, s: (i, j)),
               out_shape=_sds((k, n), out_dtype), scratch_shapes=[pltpu.VMEM((tk, tn), F32)],
               compiler_params=_cp("parallel", "parallel", "arbitrary"))(a, dy, *after)


def rms_bwd(dh, x, g, dres, *, name):
    t, d = x.shape
    tm = _pick(t, (256,))

    def body(dh_ref, x_ref, g_ref, dres_ref, dx_ref, dg_ref):
        xv = x_ref[...]
        r = lax.rsqrt(jnp.mean(xv * xv, axis=-1, keepdims=True) + EPS)
        xh = xv * r
        dy = dh_ref[...].astype(F32)
        dxh = dy * g_ref[...]
        dx_ref[...] = dres_ref[...] + r * (dxh - xh * jnp.mean(dxh * xh, axis=-1, keepdims=True))

        @pl.when(pl.program_id(0) == 0)
        def _():
            dg_ref[...] = jnp.zeros_like(dg_ref)

        dg_ref[...] += _rowsum0(dy * xh)

    row = pl.BlockSpec((tm, d), lambda i: (i, 0))
    vec = pl.BlockSpec((1, d), lambda i: (0, 0))
    return _pc(body, name=name, grid=(t // tm,), in_specs=[row, row, vec, row], out_specs=[row, vec],
               out_shape=[_sds((t, d), F32), _sds((1, d), F32)], compiler_params=_cp("arbitrary"))(dh, x, g, dres)


def loss_head(y, target, *, name):
    t, d = y.shape
    tm = _pick(t, (256,))

    def body(y_ref, t_ref, loss_ref, dy_ref):
        e = y_ref[...] - t_ref[...]
        dy_ref[...] = e * (1.0 / d)

        @pl.when(pl.program_id(0) == 0)
        def _():
            loss_ref[...] = jnp.zeros_like(loss_ref)

        loss_ref[...] += _rowsum0(jnp.sum(e * e, axis=1, keepdims=True)) * (0.5 / d)

    row = pl.BlockSpec((tm, d), lambda i: (i, 0))
    return _pc(body, name=name, grid=(t // tm,), in_specs=[row, row],
               out_specs=[pl.BlockSpec((1, 1), lambda i: (0, 0)), row],
               out_shape=[_sds((1, 1), F32), _sds((t, d), F32)], compiler_params=_cp("arbitrary"))(y, target)


def _sg_masks():
    lane = lax.broadcasted_iota(jnp.int32, (CHUNK, CHUNK), 1)
    row = lax.broadcasted_iota(jnp.int32, (CHUNK, CHUNK), 0)
    return lane < 64, lane <= row, row <= lane


def _sg_gate(vn_chunk, w_ref, bias_ref, p, first_group, tril):
    wa = jnp.where(tril, w_ref[2 * p], 0.0).astype(BF16)
    wb = jnp.where(tril, w_ref[2 * p + 1], 0.0).astype(BF16)
    oa = jnp.dot(wa, vn_chunk, preferred_element_type=F32)
    ob = jnp.dot(wb, vn_chunk, preferred_element_type=F32)
    return jnp.where(first_group, oa, ob) + bias_ref[:, p * 128:(p + 1) * 128]


def mixa_fwd(z, ln_g, ln_b, sg_w, sg_bias, *, name):
    t = z.shape[0]
    tm = _pick(t, (256,))

    def body(z_ref, g_ref, b_ref, w_ref, bias_ref, o_ref):
        first_group, tril, _ = _sg_masks()
        zv = z_ref[...].astype(F32)
        u = _gelu(zv[:, :SG_WIDTH])
        v = _gelu(zv[:, SG_WIDTH:])
        vh, _ = _ln_stats(v)
        vn = (vh * g_ref[...] + b_ref[...]).astype(BF16)
        for c in range(tm // CHUNK):
            rows = slice(c * CHUNK, (c + 1) * CHUNK)
            for p in range(4):
                cols = slice(p * 128, (p + 1) * 128)
                o = _sg_gate(vn[rows, cols], w_ref, bias_ref, p, first_group, tril)
                o_ref[rows, cols] = (u[rows, cols] * o).astype(o_ref.dtype)

    vec = pl.BlockSpec((1, SG_WIDTH), lambda i: (0, 0))
    return _pc(body, name=name, grid=(t // tm,),
               in_specs=[pl.BlockSpec((tm, 2 * SG_WIDTH), lambda i: (i, 0)), vec, vec,
                         pl.BlockSpec((8, CHUNK, CHUNK), lambda i: (0, 0, 0)),
                         pl.BlockSpec((CHUNK, SG_WIDTH), lambda i: (0, 0))],
               out_specs=pl.BlockSpec((tm, SG_WIDTH), lambda i: (i, 0)),
               out_shape=_sds((t, SG_WIDTH), BF16), compiler_params=_cp("parallel"))(z, ln_g, ln_b, sg_w, sg_bias)


def mixa_bwd(z, dga, ln_g, ln_b, sg_w, sg_wt, sg_bias, *, name):
    t = z.shape[0]
    tm = _pick(t, (256,))
    nsteps = t // tm

    def body(z_ref, dga_ref, g_ref, b_ref, w_ref, wt_ref, bias_ref, dz_ref, dw_ref, dsgb_ref, dg_ref, db_ref, dvn_s,
             dbias_ref):
        i = pl.program_id(0)
        first_group, tril, triu = _sg_masks()

        @pl.when(i == 0)
        def _():
            dw_ref[...] = jnp.zeros_like(dw_ref)
            dbias_ref[...] = jnp.zeros_like(dbias_ref)
            dg_ref[...] = jnp.zeros_like(dg_ref)
            db_ref[...] = jnp.zeros_like(db_ref)

        zv = z_ref[...].astype(F32)
        zu = zv[:, :SG_WIDTH]
        zg = zv[:, SG_WIDTH:]
        u = _gelu(zu)
        v = _gelu(zg)
        vh, r = _ln_stats(v)
        vn = (vh * g_ref[...] + b_ref[...]).astype(BF16)
        dga_v = dga_ref[...].astype(F32)
        d_o = dga_v * u
        for c in range(tm // CHUNK):
            rows = slice(c * CHUNK, (c + 1) * CHUNK)
            dbias_ref[...] += d_o[rows, :]
            for p in range(4):
                cols = slice(p * 128, (p + 1) * 128)
                vp = vn[rows, cols]
                o = _sg_gate(vp, w_ref, bias_ref, p, first_group, tril)
                dz_ref[rows, cols] = (dga_v[rows, cols] * o * _gelu_grad(zu[rows, cols])).astype(dz_ref.dtype)
                dop = d_o[rows, cols]
                dop_a = jnp.where(first_group, dop, 0.0).astype(BF16)
                dop_b = jnp.where(first_group, 0.0, dop).astype(BF16)
                dw_ref[2 * p] += lax.dot_general(dop_a, vp, NT_DIMS, preferred_element_type=F32)
                dw_ref[2 * p + 1] += lax.dot_general(dop_b, vp, NT_DIMS, preferred_element_type=F32)
                wta = jnp.where(triu, wt_ref[2 * p], 0.0).astype(BF16)
                wtb = jnp.where(triu, wt_ref[2 * p + 1], 0.0).astype(BF16)
                dop16 = dop.astype(BF16)
                dvn_s[rows, cols] = jnp.where(first_group, jnp.dot(wta, dop16, preferred_element_type=F32),
                                              jnp.dot(wtb, dop16, preferred_element_type=F32))
        dvn = dvn_s[...]
        dg_ref[...] += _rowsum0(dvn * vh)
        db_ref[...] += _rowsum0(dvn)
        dv = _ln_bwd(dvn, vh, r, g_ref[...])
        dz_ref[:, SG_WIDTH:] = (dv * _gelu_grad(zg)).astype(dz_ref.dtype)

        @pl.when(i == nsteps - 1)
        def _():
            for gi in range(8):
                dw_ref[gi] = jnp.where(tril, dw_ref[gi], 0.0)
            r_id = lax.broadcasted_iota(jnp.int32, (SG_WIDTH, 128), 0) // SB_HEAD_DIM
            c_id = lax.broadcasted_iota(jnp.int32, (SG_WIDTH, 128), 1)
            dsgb_ref[...] = _split_dot(dbias_ref[...], (r_id == c_id).astype(BF16))

    vec = pl.BlockSpec((1, SG_WIDTH), lambda i: (0, 0))
    wspec = pl.BlockSpec((8, CHUNK, CHUNK), lambda i: (0, 0, 0))
    bspec = pl.BlockSpec((CHUNK, SG_WIDTH), lambda i: (0, 0))
    sgb = pl.BlockSpec((CHUNK, 128), lambda i: (0, 0))
    return _pc(body, name=name, grid=(nsteps,),
               in_specs=[pl.BlockSpec((tm, 2 * SG_WIDTH), lambda i: (i, 0)), pl.BlockSpec((tm, SG_WIDTH), lambda i: (i, 0)),
                         vec, vec, wspec, wspec, bspec],
               out_specs=[pl.BlockSpec((tm, 2 * SG_WIDTH), lambda i: (i, 0)), wspec, sgb, vec, vec],
               out_shape=[_sds((t, 2 * SG_WIDTH), BF16), _sds((8, CHUNK, CHUNK), F32), _sds((CHUNK, 128), F32),
                          _sds((1, SG_WIDTH), F32), _sds((1, SG_WIDTH), F32)],
               scratch_shapes=[pltpu.VMEM((tm, SG_WIDTH), F32), pltpu.VMEM((CHUNK, SG_WIDTH), F32)],
               compiler_params=_cp("arbitrary"))(z, dga, ln_g, ln_b, sg_w, sg_wt, sg_bias)


def _glu(zv):
    return zv[:, :CV_WIDTH] * _sigmoid(zv[:, CV_WIDTH:])


def _cv_fill(zm_ref, zh_ref, x_s, i, tm):
    x_s[0:CV_HALO, :] = jnp.where(i > 0, _glu(zh_ref[...].astype(F32)), 0.0)
    x_s[CV_HALO:CV_HALO + tm, :] = _glu(zm_ref[...].astype(F32))


def _cv_conv(x_s, w_ref, cb_ref, tm):
    acc = jnp.zeros((tm, CV_WIDTH), F32) + cb_ref[...]
    for k in range(CV_KERNEL):
        acc = acc + w_ref[k:k + 1, :] * x_s[pl.ds(CV_HALO - (CV_KERNEL - 1) + k, tm), :]
    return acc


def _cv_specs(tm):
    zm = pl.BlockSpec((tm, 2 * CV_WIDTH), lambda i: (i, 1))
    zh = pl.BlockSpec((CV_HALO, 2 * CV_WIDTH), lambda i: (jnp.maximum(i * (tm // CV_HALO) - 1, 0), 1))
    w = pl.BlockSpec((CV_KERNEL, CV_WIDTH), lambda i: (0, 0))
    vec = pl.BlockSpec((1, CV_WIDTH), lambda i: (0, 0))
    return zm, zh, w, vec


def mixb_fwd(z, cv_w, cv_b, ln_g, ln_b, *, name):
    t = z.shape[0]
    tm = _pick(t, (256,))

    def body(zm_ref, zh_ref, w_ref, cb_ref, g_ref, b_ref, o_ref, x_s):
        _cv_fill(zm_ref, zh_ref, x_s, pl.program_id(0), tm)
        c1 = _cv_conv(x_s, w_ref, cb_ref, tm)
        ch, _ = _ln_stats(c1)
        o_ref[...] = _silu(ch * g_ref[...] + b_ref[...]).astype(o_ref.dtype)

    zm, zh, w, vec = _cv_specs(tm)
    return _pc(body, name=name, grid=(t // tm,), in_specs=[zm, zh, w, vec, vec, vec],
               out_specs=pl.BlockSpec((tm, CV_WIDTH), lambda i: (i, 0)), out_shape=_sds((t, CV_WIDTH), BF16),
               scratch_shapes=[pltpu.VMEM((CV_HALO + tm, CV_WIDTH), F32)],
               compiler_params=_cp("parallel"))(z, z, cv_w, cv_b, ln_g, ln_b)


def mixb_bwd1(z, dc3, cv_w, cv_b, ln_g, ln_b, *, name):
    t = z.shape[0]
    tm = _pick(t, (256,))

    def body(zm_ref, zh_ref, dc3_ref, w_ref, cb_ref, g_ref, b_ref, dc1_ref, dw_ref, dcb_ref, dg_ref, db_ref, x_s):
        i = pl.program_id(0)

        @pl.when(i == 0)
        def _():
            dw_ref[...] = jnp.zeros_like(dw_ref)
            dcb_ref[...] = jnp.zeros_like(dcb_ref)
            dg_ref[...] = jnp.zeros_like(dg_ref)
            db_ref[...] = jnp.zeros_like(db_ref)

        _cv_fill(zm_ref, zh_ref, x_s, i, tm)
        c1 = _cv_conv(x_s, w_ref, cb_ref, tm)
        ch, r = _ln_stats(c1)
        c2 = ch * g_ref[...] + b_ref[...]
        dc2 = dc3_ref[...].astype(F32) * _silu_grad(c2)
        dg_ref[...] += _rowsum0(dc2 * ch)
        db_ref[...] += _rowsum0(dc2)
        dc1 = _ln_bwd(dc2, ch, r, g_ref[...])
        dc1_ref[...] = dc1
        dcb_ref[...] += _rowsum0(dc1)
        for k in range(CV_KERNEL):
            dw_ref[k:k + 1, :] += _rowsum0(dc1 * x_s[pl.ds(CV_HALO - (CV_KERNEL - 1) + k, tm), :])

    zm, zh, w, vec = _cv_specs(tm)
    row = pl.BlockSpec((tm, CV_WIDTH), lambda i: (i, 0))
    return _pc(body, name=name, grid=(t // tm,), in_specs=[zm, zh, row, w, vec, vec, vec],
               out_specs=[row, w, vec, vec, vec],
               out_shape=[_sds((t, CV_WIDTH), F32), _sds((CV_KERNEL, CV_WIDTH), F32), _sds((1, CV_WIDTH), F32),
                          _sds((1, CV_WIDTH), F32), _sds((1, CV_WIDTH), F32)],
               scratch_shapes=[pltpu.VMEM((CV_HALO + tm, CV_WIDTH), F32)],
               compiler_params=_cp("arbitrary"))(z, z, dc3, cv_w, cv_b, ln_g, ln_b)


def mixb_bwd2(z, dc1, cv_w, *, name):
    t = z.shape[0]
    tm = _pick(t, (256,))
    nsteps = t // tm

    def body(zm_ref, dm_ref, dh_ref, w_ref, dz_ref, y_s):
        i = pl.program_id(0)
        y_s[0:tm, :] = dm_ref[...]
        y_s[tm:tm + CV_HALO, :] = jnp.where(i < nsteps - 1, dh_ref[...], 0.0)
        dc0 = jnp.zeros((tm, CV_WIDTH), F32)
        for k in range(CV_KERNEL):
            dc0 = dc0 + w_ref[k:k + 1, :] * y_s[pl.ds(CV_KERNEL - 1 - k, tm), :]
        zv = zm_ref[...].astype(F32)
        p = zv[:, :CV_WIDTH]
        s = _sigmoid(zv[:, CV_WIDTH:])
        dz_ref[:, :CV_WIDTH] = (dc0 * s).astype(dz_ref.dtype)
        dz_ref[:, CV_WIDTH:] = (dc0 * p * s * (1.0 - s)).astype(dz_ref.dtype)

    last = t // CV_HALO - 1
    return _pc(body, name=name, grid=(nsteps,),
               in_specs=[pl.BlockSpec((tm, 2 * CV_WIDTH), lambda i: (i, 1)),
                         pl.BlockSpec((tm, CV_WIDTH), lambda i: (i, 0)),
                         pl.BlockSpec((CV_HALO, CV_WIDTH), lambda i: (jnp.minimum((i + 1) * (tm // CV_HALO), last), 0)),
                         pl.BlockSpec((CV_KERNEL, CV_WIDTH), lambda i: (0, 0))],
               out_specs=pl.BlockSpec((tm, 2 * CV_WIDTH), lambda i: (i, 0)),
               out_shape=_sds((t, 2 * CV_WIDTH), BF16),
               scratch_shapes=[pltpu.VMEM((tm + CV_HALO, CV_WIDTH), F32)],
               compiler_params=_cp("parallel"))(z, dc1, dc1, cv_w)


def _group_ones():
    r = lax.broadcasted_iota(jnp.int32, (SB_WIDTH, SB_WIDTH), 0) // SB_HEAD_DIM
    c = lax.broadcasted_iota(jnp.int32, (SB_WIDTH, SB_WIDTH), 1) // SB_HEAD_DIM
    return (r == c).astype(BF16)


def attn_prep(z, gq, gk, *, name):
    t = z.shape[0]
    tm = _pick(t, (256,))
    scale = 1.0 / math.sqrt(SB_HEAD_DIM)

    def body(q_ref, k_ref, gq_ref, gk_ref, qo_ref, ko_ref):
        ones = _group_ones()
        for src, g_ref, dst, mul in ((q_ref, gq_ref, qo_ref, scale), (k_ref, gk_ref, ko_ref, 1.0)):
            v = src[...].astype(F32)
            r = lax.rsqrt(_split_dot(v * v, ones) * (1.0 / SB_HEAD_DIM) + EPS)
            dst[...] = ((v * r * g_ref[...]).astype(BF16).astype(F32) * mul).astype(dst.dtype)

    vec = pl.BlockSpec((1, SB_WIDTH), lambda i: (0, 0))
    row = pl.BlockSpec((tm, SB_WIDTH), lambda i: (i, 0))
    return _pc(body, name=name, grid=(t // tm,),
               in_specs=[pl.BlockSpec((tm, SB_WIDTH), lambda i: (i, COL_C // SB_WIDTH)),
                         pl.BlockSpec((tm, SB_WIDTH), lambda i: (i, COL_C // SB_WIDTH + 1)), vec, vec],
               out_specs=[row, row], out_shape=[_sds((t, SB_WIDTH), BF16), _sds((t, SB_WIDTH), BF16)],
               compiler_params=_cp("parallel"))(z, z, gq, gk)


_KB = Q_BLOCK
_PAIR = 2 * _KB


def _attn_tq(t):
    return _pick(t, (512, 256, 128))


def _attn_consts(tq):
    first_head = lax.broadcasted_iota(jnp.int32, (_KB, 128), 1) < SB_HEAD_DIM
    r2 = lax.broadcasted_iota(jnp.int32, (_PAIR, _PAIR), 0)
    c2 = lax.broadcasted_iota(jnp.int32, (_PAIR, _PAIR), 1)
    same = (r2 // _KB) == (c2 // _KB)
    m_suffix = (same & (r2 > c2)).astype(BF16)
    m_prefix = (same & (r2 < c2)).astype(BF16)
    row = lax.broadcasted_iota(jnp.int32, (tq, _PAIR), 0)
    col = lax.broadcasted_iota(jnp.int32, (tq, _PAIR), 1)
    return first_head, m_suffix, m_prefix, row, col & (_KB - 1), col < _KB


def _sb_logits(z, causal):
    sp = jnp.log(1.0 + jnp.exp(-jnp.abs(z)))
    g = jnp.minimum(z, 0.0) - sp
    l1m = g - z
    if causal is not None:
        l1m = jnp.where(causal, l1m, 0.0)
    return g, l1m


def _stack_heads(first_head, v):
    zero = jnp.zeros_like(v)
    return jnp.concatenate([jnp.where(first_head, v, zero), jnp.where(first_head, zero, v)], axis=0)


def _pair_sums(x):
    return jnp.sum(x[:, :_KB], axis=1, keepdims=True), jnp.sum(x[:, _KB:], axis=1, keepdims=True)


def _attn_specs(t, tq):
    qspec = pl.BlockSpec((tq, 128), lambda h, i: (i, h))
    kspec = pl.BlockSpec((t, 128), lambda h, i: (0, h))
    vspec = pl.BlockSpec((t, 128), lambda h, i: (0, (COL_C + 2 * SB_WIDTH) // 128 + h))
    return qspec, kspec, vspec


def attn_fwd(q, k, z, *, name):
    t = q.shape[0]
    tq = _attn_tq(t)
    nd = tq // _KB

    def body(q_ref, k_ref, v_ref, o_ref):
        qt = pl.program_id(1)
        first_head, m_suffix, _, row, key, is_first = _attn_consts(tq)
        qv = q_ref[...]

        def step(kb, state, causal):
            acc, ca, cb = state
            off = pl.multiple_of(kb * _KB, _KB)
            kcat = _stack_heads(first_head, k_ref[pl.ds(off, _KB), :])
            vcat = _stack_heads(first_head, v_ref[pl.ds(off, _KB), :])
            zz = lax.dot_general(qv, kcat, NT_DIMS, preferred_element_type=F32)
            g, l1m = _sb_logits(zz, causal)
            a = jnp.exp(g + _split_dot(l1m, m_suffix) + jnp.where(is_first, ca, cb))
            if causal is not None:
                a = jnp.where(causal, a, 0.0)
            sa, sb = _pair_sums(l1m)
            return acc + jnp.dot(a.astype(BF16), vcat, preferred_element_type=F32), ca + sa, cb + sb

        c0 = jnp.zeros((tq, 1), F32)
        state = (jnp.zeros((tq, 128), F32), c0, c0)
        for d in reversed(range(nd)):
            state = step(qt * nd + d, state, key + d * _KB < row)
        state = lax.fori_loop(0, qt * nd, lambda s, st: step(qt * nd - 1 - s, st, None), state)
        o_ref[...] = state[0].astype(o_ref.dtype)

    qspec, kspec, vspec = _attn_specs(t, tq)
    return _pc(body, name=name, grid=(SB_WIDTH // 128, t // tq), in_specs=[qspec, kspec, vspec], out_specs=qspec,
               out_shape=_sds((t, SB_WIDTH), BF16), compiler_params=_cp("parallel", "arbitrary"))(q, k, z)


def attn_bwd(q, k, z, do, *, name):
    t = q.shape[0]
    tq = _attn_tq(t)
    nd = tq // _KB
    nk = t // _KB

    def body(q_ref, k_ref, v_ref, do_ref, dq_ref, dk_ref, dv_ref, e_s, sg_s):
        qt = pl.program_id(1)
        first_head, m_suffix, m_prefix, row, key, is_first = _attn_consts(tq)

        @pl.when(qt == 0)
        def _():
            dk_ref[...] = jnp.zeros_like(dk_ref)
            dv_ref[...] = jnp.zeros_like(dv_ref)

        qv = q_ref[...]
        dov = do_ref[...]

        def halves(x):
            return jnp.where(first_head, x[:_KB], x[_KB:])

        def sweep1(kb, state, causal):
            ca, cb = state
            off = pl.multiple_of(kb * _KB, _KB)
            kcat = _stack_heads(first_head, k_ref[pl.ds(off, _KB), :])
            vcat = _stack_heads(first_head, v_ref[pl.ds(off, _KB), :])
            zz = lax.dot_general(qv, kcat, NT_DIMS, preferred_element_type=F32)
            g, l1m = _sb_logits(zz, causal)
            a = jnp.exp(g + _split_dot(l1m, m_suffix) + jnp.where(is_first, ca, cb))
            if causal is not None:
                a = jnp.where(causal, a, 0.0)
            da = lax.dot_general(dov, vcat, NT_DIMS, preferred_element_type=F32)
            e_s[kb] = a * da
            sg_s[kb] = jnp.exp(g).astype(BF16)
            dv_ref[pl.ds(off, _KB), :] += halves(lax.dot_general(a.astype(BF16), dov, TN_DIMS, preferred_element_type=F32))
            sa, sb = _pair_sums(l1m)
            return ca + sa, cb + sb

        def sweep2(kb, state, causal):
            dq, pa, pb = state
            off = pl.multiple_of(kb * _KB, _KB)
            kcat = _stack_heads(first_head, k_ref[pl.ds(off, _KB), :])
            e = e_s[kb]
            s = sg_s[kb].astype(F32)
            dz = e * (1.0 - s) - (jnp.where(is_first, pa, pb) + _split_dot(e, m_prefix)) * s
            if causal is not None:
                dz = jnp.where(causal, dz, 0.0)
            dz = dz.astype(BF16)
            dq = dq + jnp.dot(dz, kcat, preferred_element_type=F32)
            dk_ref[pl.ds(off, _KB), :] += halves(lax.dot_general(dz, qv, TN_DIMS, preferred_element_type=F32))
            sa, sb = _pair_sums(e)
            return dq, pa + sa, pb + sb

        c0 = jnp.zeros((tq, 1), F32)
        st1 = (c0, c0)
        for d in reversed(range(nd)):
            st1 = sweep1(qt * nd + d, st1, key + d * _KB < row)
        lax.fori_loop(0, qt * nd, lambda s, st: sweep1(qt * nd - 1 - s, st, None), st1)
        st2 = lax.fori_loop(0, qt * nd, lambda s, st: sweep2(s, st, None), (jnp.zeros((tq, 128), F32), c0, c0))
        for d in range(nd):
            st2 = sweep2(qt * nd + d, st2, key + d * _KB < row)
        dq_ref[...] = st2[0]

    qspec, kspec, vspec = _attn_specs(t, tq)
    acc = pl.BlockSpec((t, 128), lambda h, i: (0, h))
    return _pc(body, name=name, grid=(SB_WIDTH // 128, t // tq), in_specs=[qspec, kspec, vspec, qspec],
               out_specs=[qspec, acc, acc],
               out_shape=[_sds((t, SB_WIDTH), F32), _sds((t, SB_WIDTH), F32), _sds((t, SB_WIDTH), F32)],
               scratch_shapes=[pltpu.VMEM((nk, tq, _PAIR), F32), pltpu.VMEM((nk, tq, _PAIR), BF16)],
               compiler_params=_cp("parallel", "arbitrary"))(q, k, z, do)


def attn_post_bwd(z, dq, dk, dv, gq, gk, *, name):
    t = z.shape[0]
    tm = _pick(t, (256,))
    scale = 1.0 / math.sqrt(SB_HEAD_DIM)

    def body(q_ref, k_ref, dq_ref, dk_ref, dv_ref, gq_ref, gk_ref, dz_ref, dgq_ref, dgk_ref):
        ones = _group_ones()

        @pl.when(pl.program_id(0) == 0)
        def _():
            dgq_ref[...] = jnp.zeros_like(dgq_ref)
            dgk_ref[...] = jnp.zeros_like(dgk_ref)

        for idx, (src, d_ref, g_ref, dg_ref, mul) in enumerate(
                ((q_ref, dq_ref, gq_ref, dgq_ref, scale), (k_ref, dk_ref, gk_ref, dgk_ref, 1.0))):
            v = src[...].astype(F32)
            r = lax.rsqrt(_split_dot(v * v, ones) * (1.0 / SB_HEAD_DIM) + EPS)
            vh = v * r
            dn = d_ref[...] * mul
            dxh = dn * g_ref[...]
            m = _split_dot(dxh * vh, ones) * (1.0 / SB_HEAD_DIM)
            dz_ref[:, idx * SB_WIDTH:(idx + 1) * SB_WIDTH] = (r * (dxh - vh * m)).astype(dz_ref.dtype)
            s = _rowsum0(dn * vh)
            f = jnp.broadcast_to(s[:, 0:128] + s[:, 128:256] + s[:, 256:384] + s[:, 384:512], dg_ref.shape)
            dg_ref[...] += f + pltpu.roll(f, 64, 1)
        dz_ref[:, 2 * SB_WIDTH:] = dv_ref[...].astype(dz_ref.dtype)

    vec = pl.BlockSpec((1, SB_WIDTH), lambda i: (0, 0))
    row = pl.BlockSpec((tm, SB_WIDTH), lambda i: (i, 0))
    fold = pl.BlockSpec((8, 128), lambda i: (0, 0))
    return _pc(body, name=name, grid=(t // tm,),
               in_specs=[pl.BlockSpec((tm, SB_WIDTH), lambda i: (i, COL_C // SB_WIDTH)),
                         pl.BlockSpec((tm, SB_WIDTH), lambda i: (i, COL_C // SB_WIDTH + 1)), row, row, row, vec, vec],
               out_specs=[pl.BlockSpec((tm, 3 * SB_WIDTH), lambda i: (i, 0)), fold, fold],
               out_shape=[_sds((t, 3 * SB_WIDTH), BF16), _sds((8, 128), F32), _sds((8, 128), F32)],
               compiler_params=_cp("arbitrary"))(z, z, dq, dk, dv, gq, gk)


_GW = 512


def merge_fwd(z, ya, yb, yc, b_gate, *, name):
    t = z.shape[0]
    tm = _pick(t, (512, 256))

    def body(za_ref, zb_ref, zc_ref, ya_ref, yb_ref, yc_ref, bg_ref, o_ref):
        acc = jnp.zeros((tm, _GW), F32)
        for b, (zr, yr) in enumerate(((za_ref, ya_ref), (zb_ref, yb_ref), (zc_ref, yc_ref))):
            acc = acc + _sigmoid(zr[...].astype(F32) + bg_ref[b:b + 1, :]) * yr[...].astype(F32)
        o_ref[...] = acc.astype(o_ref.dtype)

    def zspec(b):
        return pl.BlockSpec((tm, _GW), lambda i, j: (i, COL_G // _GW + 2 * b + j))

    yspec = pl.BlockSpec((tm, _GW), lambda i, j: (i, j))
    return _pc(body, name=name, grid=(t // tm, D_MODEL // _GW),
               in_specs=[zspec(0), zspec(1), zspec(2), yspec, yspec, yspec, pl.BlockSpec((3, _GW), lambda i, j: (0, j))],
               out_specs=yspec, out_shape=_sds((t, D_MODEL), BF16),
               compiler_params=_cp("parallel", "parallel"))(z, z, z, ya, yb, yc, b_gate)


def merge_bwd(z, ya, yb, yc, b_gate, dm, *, name):
    t = z.shape[0]
    tm = _pick(t, (512, 256))

    def body(za_ref, zb_ref, zc_ref, ya_ref, yb_ref, yc_ref, bg_ref, dm_ref,
             dya_ref, dyb_ref, dyc_ref, dza_ref, dzb_ref, dzc_ref, dbg_ref):
        @pl.when(pl.program_id(1) == 0)
        def _():
            dbg_ref[...] = jnp.zeros_like(dbg_ref)

        dmv = dm_ref[...].astype(F32)
        for b, (zr, yr, dyr, dzr) in enumerate(((za_ref, ya_ref, dya_ref, dza_ref), (zb_ref, yb_ref, dyb_ref, dzb_ref),
                                                (zc_ref, yc_ref, dyc_ref, dzc_ref))):
            s = _sigmoid(zr[...].astype(F32) + bg_ref[b:b + 1, :])
            dyr[...] = (dmv * s).astype(dyr.dtype)
            dg = dmv * yr[...].astype(F32) * s * (1.0 - s)
            dzr[...] = dg.astype(dzr.dtype)
            dbg_ref[b:b + 1, :] += _rowsum0(dg)

    def zspec(b):
        return pl.BlockSpec((tm, _GW), lambda j, i: (i, COL_G // _GW + 2 * b + j))

    yspec = pl.BlockSpec((tm, _GW), lambda j, i: (i, j))
    bspec = pl.BlockSpec((3, _GW), lambda j, i: (0, j))
    full = _sds((t, D_MODEL), BF16)
    return _pc(body, name=name, grid=(D_MODEL // _GW, t // tm),
               in_specs=[zspec(0), zspec(1), zspec(2), yspec, yspec, yspec, bspec, yspec],
               out_specs=[yspec] * 6 + [bspec], out_shape=[full] * 6 + [_sds((3, D_MODEL), F32)],
               compiler_params=_cp("parallel", "arbitrary"))(z, z, z, ya, yb, yc, b_gate, dm)


_FW = 1408
_FH = D_FF // _FW


def _ffn_fill(m_ref, h_ref, x_s, i, tm):
    x_s[0:FFN_HALO, :] = jnp.where(i > 0, h_ref[...].astype(F32), 0.0)
    x_s[FFN_HALO:FFN_HALO + tm, :] = m_ref[...].astype(F32)


def _ffn_conv(x_s, w_ref, b_ref, tm):
    acc = jnp.zeros((tm, _FW), F32) + b_ref[...]
    for k in range(FFN_KERNEL):
        acc = acc + w_ref[k:k + 1, :] * x_s[pl.ds(FFN_HALO - (FFN_KERNEL - 1) + k, tm), :]
    return acc


def ffn_mid_fwd(up, cw, cb, *, name):
    t = up.shape[0]
    tm = _pick(t, (256,))

    def body(gm_ref, gh_ref, vm_ref, vh_ref, wg_ref, wv_ref, bg_ref, bv_ref, o_ref, xg_s, xv_s):
        i = pl.program_id(0)
        _ffn_fill(gm_ref, gh_ref, xg_s, i, tm)
        _ffn_fill(vm_ref, vh_ref, xv_s, i, tm)
        o_ref[...] = (_silu(_ffn_conv(xg_s, wg_ref, bg_ref, tm)) * _ffn_conv(xv_s, wv_ref, bv_ref, tm)).astype(o_ref.dtype)

    def main(off):
        return pl.BlockSpec((tm, _FW), lambda i, j: (i, j + off))

    def halo(off):
        return pl.BlockSpec((FFN_HALO, _FW), lambda i, j: (jnp.maximum(i * (tm // FFN_HALO) - 1, 0), j + off))

    def wspec(off):
        return pl.BlockSpec((FFN_KERNEL, _FW), lambda i, j: (0, j + off))

    def bspec(off):
        return pl.BlockSpec((1, _FW), lambda i, j: (0, j + off))

    return _pc(body, name=name, grid=(t // tm, _FH),
               in_specs=[main(0), halo(0), main(_FH), halo(_FH), wspec(0), wspec(_FH), bspec(0), bspec(_FH)],
               out_specs=pl.BlockSpec((tm, _FW), lambda i, j: (i, j)), out_shape=_sds((t, D_FF), BF16),
               scratch_shapes=[pltpu.VMEM((FFN_HALO + tm, _FW), F32), pltpu.VMEM((FFN_HALO + tm, _FW), F32)],
               compiler_params=_cp("parallel", "parallel"))(up, up, up, up, cw, cw, cb, cb)


def ffn_mid_bwd1(up, dact, cw, cb, *, name):
    t = up.shape[0]
    tm = _pick(t, (256,))

    def body(gm_ref, gh_ref, vm_ref, vh_ref, da_ref, wg_ref, wv_ref, bg_ref, bv_ref, d_ref, dw_ref, db_ref, xg_s, xv_s):
        j = pl.program_id(0)
        i = pl.program_id(1)

        @pl.when(i == 0)
        def _():
            dw_ref[...] = jnp.zeros_like(dw_ref)
            db_ref[...] = jnp.zeros_like(db_ref)

        _ffn_fill(gm_ref, gh_ref, xg_s, i, tm)
        _ffn_fill(vm_ref, vh_ref, xv_s, i, tm)
        gate = _ffn_conv(xg_s, wg_ref, bg_ref, tm)
        da = da_ref[...].astype(F32)

        def finish(d, x_s):
            d_ref[...] = d.astype(d_ref.dtype)
            db_ref[...] += _rowsum0(d)
            for k in range(FFN_KERNEL):
                dw_ref[k:k + 1, :] += _rowsum0(d * x_s[pl.ds(FFN_HALO - (FFN_KERNEL - 1) + k, tm), :])

        @pl.when(j < _FH)
        def _():
            finish(da * _ffn_conv(xv_s, wv_ref, bv_ref, tm) * _silu_grad(gate), xg_s)

        @pl.when(j >= _FH)
        def _():
            finish(da * _silu(gate), xv_s)

    def main(off):
        return pl.BlockSpec((tm, _FW), lambda j, i: (i, j % _FH + off))

    def halo(off):
        return pl.BlockSpec((FFN_HALO, _FW), lambda j, i: (jnp.maximum(i * (tm // FFN_HALO) - 1, 0), j % _FH + off))

    def wspec(off):
        return pl.BlockSpec((FFN_KERNEL, _FW), lambda j, i: (0, j % _FH + off))

    def bspec(off):
        return pl.BlockSpec((1, _FW), lambda j, i: (0, j % _FH + off))

    return _pc(body, name=name, grid=(2 * _FH, t // tm),
               in_specs=[main(0), halo(0), main(_FH), halo(_FH), pl.BlockSpec((tm, _FW), lambda j, i: (i, j % _FH)),
                         wspec(0), wspec(_FH), bspec(0), bspec(_FH)],
               out_specs=[pl.BlockSpec((tm, _FW), lambda j, i: (i, j)), pl.BlockSpec((FFN_KERNEL, _FW), lambda j, i: (0, j)),
                          pl.BlockSpec((1, _FW), lambda j, i: (0, j))],
               out_shape=[_sds((t, 2 * D_FF), BF16), _sds((FFN_KERNEL, 2 * D_FF), F32), _sds((1, 2 * D_FF), F32)],
               scratch_shapes=[pltpu.VMEM((FFN_HALO + tm, _FW), F32), pltpu.VMEM((FFN_HALO + tm, _FW), F32)],
               compiler_params=_cp("parallel", "arbitrary"))(up, up, up, up, dact, cw, cw, cb, cb)


def ffn_mid_bwd2(dupc, cw, *, name):
    t = dupc.shape[0]
    tm = _pick(t, (256,))
    nsteps = t // tm
    last = t // FFN_HALO - 1

    def body(m_ref, h_ref, w_ref, o_ref, y_s):
        i = pl.program_id(0)
        y_s[0:tm, :] = m_ref[...].astype(F32)
        y_s[tm:tm + FFN_HALO, :] = jnp.where(i < nsteps - 1, h_ref[...].astype(F32), 0.0)
        acc = jnp.zeros((tm, _FW), F32)
        for k in range(FFN_KERNEL):
            acc = acc + w_ref[k:k + 1, :] * y_s[pl.ds(FFN_KERNEL - 1 - k, tm), :]
        o_ref[...] = acc.astype(o_ref.dtype)

    return _pc(body, name=name, grid=(nsteps, 2 * _FH),
               in_specs=[pl.BlockSpec((tm, _FW), lambda i, j: (i, j)),
                         pl.BlockSpec((FFN_HALO, _FW), lambda i, j: (jnp.minimum((i + 1) * (tm // FFN_HALO), last), j)),
                         pl.BlockSpec((FFN_KERNEL, _FW), lambda i, j: (0, j))],
               out_specs=pl.BlockSpec((tm, _FW), lambda i, j: (i, j)), out_shape=_sds((t, 2 * D_FF), BF16),
               scratch_shapes=[pltpu.VMEM((tm + FFN_HALO, _FW), F32)],
               compiler_params=_cp("parallel", "parallel"))(dupc, dupc, cw)


def _vec(v):
    return v.reshape(1, -1)


def _layer_consts(p):
    return dict(
        sg_bias=jnp.repeat(p['sg_b'].T, SB_HEAD_DIM, axis=1),
        sg_wt=jnp.swapaxes(p['sg_w'], 1, 2),
        gq=jnp.tile(p['q_norm_g'], SB_WIDTH // SB_HEAD_DIM).reshape(1, -1),
        gk=jnp.tile(p['k_norm_g'], SB_WIDTH // SB_HEAD_DIM).reshape(1, -1),
    )


def layer_fwd(x, p, after=()):
    c = _layer_consts(p)
    z, h = mm_norm_nn(x, _vec(p['ln1_g']), p['w_in'], name="in_proj", after=after)
    ga = mixa_fwd(z, _vec(p['sg_ln_g']), _vec(p['sg_ln_b']), p['sg_w'], c['sg_bias'], name="mixa_fwd")
    cb = mixb_fwd(z, p['cv_w'], _vec(p['cv_b']), _vec(p['cv_ln_g']), _vec(p['cv_ln_b']), name="mixb_fwd")
    q, k = attn_prep(z, c['gq'], c['gk'], name="attn_prep")
    ao = attn_fwd(q, k, z, name="attn_fwd")
    ya = mm_nn(ga, p['w_a_out'], name="a_out")
    yb = mm_nn(cb, p['w_b_out'], name="b_out")
    yc = mm_nn(ao, p['w_c_out'], name="c_out")
    merged = merge_fwd(z, ya, yb, yc, p['b_gate'], name="merge_fwd")
    x1 = mm_nn(merged, p['w_out'], res=x, out_dtype=F32, name="out_proj")
    up, h2 = mm_norm_nn(x1, _vec(p['ln2_g']), p['w_up'], name="up_proj")
    act = ffn_mid_fwd(up, p['ffn_conv_w'], _vec(p['ffn_conv_b']), name="ffn_mid_fwd")
    x2 = mm_nn(act, p['w_down'], res=x1, out_dtype=F32, name="down_proj")
    saved = dict(x=x, z=z, h=h, ga=ga, cb=cb, q=q, k=k, ao=ao, ya=ya, yb=yb, yc=yc, merged=merged, x1=x1, up=up,
                 h2=h2, act=act)
    return x2, saved


def layer_bwd(dx2, p, s, after=()):
    c = _layer_consts(p)
    g = {}
    g['w_down'] = mm_tn(s['act'], dx2, name="d_w_down", after=after)
    dact = mm_nt(dx2, p['w_down'], out_dtype=BF16, name="d_act")
    dupc, g['ffn_conv_w'], dcb = ffn_mid_bwd1(s['up'], dact, p['ffn_conv_w'], _vec(p['ffn_conv_b']), name="ffn_mid_bwd1")
    g['ffn_conv_b'] = dcb.reshape(-1)
    dup = ffn_mid_bwd2(dupc, p['ffn_conv_w'], name="ffn_mid_bwd2")
    g['w_up'] = mm_tn(s['h2'], dup, name="d_w_up")
    dh2 = mm_nt(dup, p['w_up'], out_dtype=F32, name="d_h2")
    dx1, dg2 = rms_bwd(dh2, s['x1'], _vec(p['ln2_g']), dx2, name="ln2_bwd")
    g['ln2_g'] = dg2.reshape(-1)
    g['w_out'] = mm_tn(s['merged'], dx1, name="d_w_out")
    dm = mm_nt(dx1, p['w_out'], out_dtype=BF16, name="d_merged")
    dya, dyb, dyc, dzg0, dzg1, dzg2, g['b_gate'] = merge_bwd(s['z'], s['ya'], s['yb'], s['yc'], p['b_gate'], dm,
                                                             name="merge_bwd")
    g['w_a_out'] = mm_tn(s['ga'], dya, name="d_w_a_out")
    g['w_b_out'] = mm_tn(s['cb'], dyb, name="d_w_b_out")
    g['w_c_out'] = mm_tn(s['ao'], dyc, name="d_w_c_out")
    dga = mm_nt(dya, p['w_a_out'], out_dtype=BF16, name="d_ga")
    dcb3 = mm_nt(dyb, p['w_b_out'], out_dtype=BF16, name="d_cb")
    dao = mm_nt(dyc, p['w_c_out'], out_dtype=BF16, name="d_ao")
    dza, g['sg_w'], dsgb, dlg, dlb = mixa_bwd(s['z'], dga, _vec(p['sg_ln_g']), _vec(p['sg_ln_b']), p['sg_w'],
                                               c['sg_wt'], c['sg_bias'], name="mixa_bwd")
    g['sg_b'] = dsgb[:, :SG_WIDTH // SB_HEAD_DIM].T
    g['sg_ln_g'] = dlg.reshape(-1)
    g['sg_ln_b'] = dlb.reshape(-1)
    dc1, g['cv_w'], dcvb, dcg, dcbb = mixb_bwd1(s['z'], dcb3, p['cv_w'], _vec(p['cv_b']), _vec(p['cv_ln_g']),
                                                _vec(p['cv_ln_b']), name="mixb_bwd1")
    g['cv_b'] = dcvb.reshape(-1)
    g['cv_ln_g'] = dcg.reshape(-1)
    g['cv_ln_b'] = dcbb.reshape(-1)
    dzb = mixb_bwd2(s['z'], dc1, p['cv_w'], name="mixb_bwd2")
    dq, dk, dv = attn_bwd(s['q'], s['k'], s['z'], dao, name="attn_bwd")
    dzc, dgq, dgk = attn_post_bwd(s['z'], dq, dk, dv, c['gq'], c['gk'], name="attn_post_bwd")
    g['q_norm_g'] = dgq[0, :SB_HEAD_DIM]
    g['k_norm_g'] = dgk[0, :SB_HEAD_DIM]
    dz = jnp.concatenate([dza, dzb, dzc, dzg0, dzg1, dzg2], axis=1)
    g['w_in'] = mm_tn(s['h'], dz, name="d_w_in")
    dh = mm_nt(dz, p['w_in'], out_dtype=F32, name="d_h")
    dx, dg1 = rms_bwd(dh, s['x'], _vec(p['ln1_g']), dx1, name="ln1_bwd")
    g['ln1_g'] = dg1.reshape(-1)
    return dx, g


def local_step(x, target, depth, get_layer, on_grads):
    saved, layers = [], []
    for l in range(depth):
        p, after = get_layer(l, x)
        x, s = layer_fwd(x, p, after)
        layers.append(p)
        saved.append(s)
    loss, dx = loss_head(x, target, name="loss_head")
    after = ()
    for l in reversed(range(depth)):
        dx, g = layer_bwd(dx, layers[l], saved[l], after)
        after = on_grads(l, g)
    return loss[0, 0], dx


def cast_bf16(w, layer, *, name):
    _, r, c = w.shape
    tr = _pick(r, (512, 704, 256))

    def body(w_ref, o_ref):
        o_ref[...] = w_ref[0].astype(BF16)

    return _pc(body, name=name, grid=(r // tr,), in_specs=[pl.BlockSpec((1, tr, c), lambda i: (layer, i, 0))],
               out_specs=pl.BlockSpec((tr, c), lambda i: (i, 0)), out_shape=_sds((r, c), BF16),
               compiler_params=_cp("parallel"))(w)


def adamw(w, g, m, v, *, name):
    r, c = w.shape
    tr = _pick(r, (256, 704)) if r * c > 512 * 1024 else r

    def body(w_ref, g_ref, m_ref, v_ref, d_ref, mo_ref, vo_ref):
        gv = g_ref[...]
        mn = ADAM_B1 * m_ref[...] + (1.0 - ADAM_B1) * gv
        vn = ADAM_B2 * v_ref[...] + (1.0 - ADAM_B2) * (gv * gv)
        m_hat = mn / (1.0 - ADAM_B1 ** ADAM_STEP)
        v_hat = vn / (1.0 - ADAM_B2 ** ADAM_STEP)
        d_ref[...] = -ADAM_LR * (m_hat / (jnp.sqrt(v_hat) + ADAM_EPS) + ADAM_WD * w_ref[...])
        mo_ref[...] = mn
        vo_ref[...] = vn

    spec = pl.BlockSpec((tr, c), lambda i: (i, 0))
    out = _sds((r, c), F32)
    return _pc(body, name=name, grid=(r // tr,), in_specs=[spec] * 4, out_specs=[spec] * 3, out_shape=[out] * 3,
               compiler_params=_cp("parallel"))(w, g, m, v)


def _as3(a):
    return a if a.ndim == 3 else a.reshape((1,) + a.shape)


def add_half(g, recv, c_idx, *, name):
    s, rh, w = recv.shape
    tr = _pick(rh, (256, 352, 128))
    nb = rh // tr

    def body(c_ref, g_ref, r_ref, o_ref):
        o_ref[...] = (g_ref[...].astype(F32) + r_ref[...].astype(F32)).astype(o_ref.dtype)

    own = pl.BlockSpec((1, tr, w), lambda k, i, c_ref: (k, c_ref[0] * nb + i, 0))
    half = pl.BlockSpec((1, tr, w), lambda k, i, c_ref: (k, i, 0))
    gs = pltpu.PrefetchScalarGridSpec(num_scalar_prefetch=1, grid=(s, nb), in_specs=[own, half], out_specs=half)
    return _pc(body, name=name, grid_spec=gs, out_shape=_sds((s, rh, w), BF16),
               compiler_params=_cp("parallel", "parallel"))(c_idx, g, recv)


def sum_shard(p, recv, pos_idx, *, by_rows, name):
    _, rh, w = recv.shape
    tr = _pick(rh, (256, 352, 128))
    nb = rh // tr

    def body(pos_ref, p_ref, r_ref, o_ref):
        acc = p_ref[0].astype(F32)
        for j in range(N_CHIPS - 1):
            acc = acc + r_ref[j].astype(F32)
        o_ref[...] = acc

    if by_rows:
        own = pl.BlockSpec((1, tr, w), lambda i, pos_ref: (pos_ref[0], i, 0))
    else:
        own = pl.BlockSpec((1, tr, w), lambda i, pos_ref: (0, i, pos_ref[0]))
    gs = pltpu.PrefetchScalarGridSpec(num_scalar_prefetch=1, grid=(nb,),
                                      in_specs=[own, pl.BlockSpec((N_CHIPS - 1, tr, w), lambda i, pos_ref: (0, i, 0))],
                                      out_specs=pl.BlockSpec((tr, w), lambda i, pos_ref: (pos_ref[1] * nb + i, 0)))
    return _pc(body, name=name, grid_spec=gs, out_shape=_sds((2 * rh, w), F32),
               compiler_params=_cp("parallel"))(pos_idx, p, recv)


def sum_slots(slab, *, name):
    _, r, w = slab.shape
    tr = _pick(r, (512, 256, 8))

    def body(s_ref, o_ref):
        acc = s_ref[0]
        for j in range(1, N_DEV):
            acc = acc + s_ref[j]
        o_ref[...] = acc

    return _pc(body, name=name, grid=(r // tr,), in_specs=[pl.BlockSpec((N_DEV, tr, w), lambda i: (0, i, 0))],
               out_specs=pl.BlockSpec((tr, w), lambda i: (i, 0)), out_shape=_sds((r, w), F32),
               compiler_params=_cp("parallel"))(slab)


def _mesh_pos():
    x, y, c = lax.axis_index("x"), lax.axis_index("y"), lax.axis_index("c")
    others = [(1 - x, y), (x, 1 - y), (1 - x, 1 - y)]
    return x, y, c, others


def _rcopy(src, dst, ssem, rsem, k, dev):
    return pltpu.make_async_remote_copy(src_ref=src, dst_ref=dst, send_sem=ssem.at[k], recv_sem=rsem.at[k],
                                        device_id=dev, device_id_type=MESH)


def _comm_call(body, name, n_in, out_shape, n_local, n_remote):
    scratch = [pltpu.SemaphoreType.DMA((max(n_local, 1),)), pltpu.SemaphoreType.DMA((n_remote,)),
               pltpu.SemaphoreType.DMA((n_remote,))]
    return _pc(body, name=name, in_specs=[ANY] * n_in, out_specs=[ANY] * len(out_shape), out_shape=out_shape,
               scratch_shapes=scratch)


GATHERED = BIG + SMALL_COL
HBM_SPEC = pl.BlockSpec(memory_space=pltpu.HBM)
SEM_SPEC = pl.BlockSpec(memory_space=pltpu.SEMAPHORE)
TOKEN_SHAPE = (8, 128)


def _full_shape(n, block):
    r, c = block.shape
    return (N_CHIPS, r, c) if n in BIG_ROW else (r, N_CHIPS * c)


def _gather_windows(srcs):
    def dwin(outs, i, k, h):
        n = GATHERED[i]
        r, cs = srcs[i].shape
        if n in BIG_ROW:
            return outs[i].at[k] if h is None else outs[i].at[k, pl.ds(h * (r // 2), r // 2), :]
        cols = pl.ds(pl.multiple_of(k * cs, 128), cs)
        return outs[i].at[:, cols] if h is None else outs[i].at[pl.ds(h * (r // 2), r // 2), cols]

    def swin(ins, i, h):
        r = srcs[i].shape[0]
        return ins[i] if h is None else ins[i].at[pl.ds(h * (r // 2), r // 2), :]

    return dwin, swin


def _gather_send(ins, outs, ssem, rsem, dwin, swin, stride):
    x, y, c, others = _mesh_pos()
    sends = []
    for i, n in enumerate(GATHERED):
        h = c if n in BIG else None
        for j, chip in enumerate(others):
            cp = _rcopy(swin(ins, i, h), dwin(outs, i, 2 * x + y, h), ssem, rsem, stride * i + j, (*chip, c))
            cp.start()
            sends.append(cp)
    return sends


def _gather_pass_on(outs, ssem, rsem, dwin, stride, first_off, pass_off):
    x, y, c, others = _mesh_pos()
    sib = (x, y, 1 - c)
    sends = []
    for j, chip in enumerate(others):
        kk = 2 * chip[0] + chip[1]
        for i, n in enumerate(GATHERED):
            got = dwin(outs, i, kk, c if n in BIG else None)
            if first_off is not None:
                _rcopy(got, got, ssem, rsem, stride * i + first_off + j, (*chip, c)).wait_recv()
            if n in BIG:
                fwd = _rcopy(got, got, ssem, rsem, stride * i + pass_off + j, sib)
                fwd.start()
                sends.append(fwd)
    for j, chip in enumerate(others):
        kk = 2 * chip[0] + chip[1]
        for i, n in enumerate(GATHERED):
            if n in BIG:
                got = dwin(outs, i, kk, 1 - c)
                _rcopy(got, got, ssem, rsem, stride * i + pass_off + j, sib).wait_recv()
    return sends


def _own_blocks(ins, outs, lsem, dwin, swin):
    x, y, _, _ = _mesh_pos()
    cps = []
    for i in range(len(GATHERED)):
        cp = pltpu.make_async_copy(swin(ins, i, None), dwin(outs, i, 2 * x + y, None), lsem.at[i])
        cp.start()
        cps.append(cp)
    return cps


def _name_full(outs):
    return {n: (o.reshape(o.shape[0] * o.shape[1], o.shape[2]) if n in BIG_ROW else o) for n, o in zip(GATHERED, outs)}


def allgather_weights(srcs, *, name):
    nn = len(GATHERED)
    dwin, swin = _gather_windows(srcs)

    def body(*refs):
        ins, outs = refs[:nn], refs[nn:2 * nn]
        lsem, ssem, rsem = refs[2 * nn:]
        own = _own_blocks(ins, outs, lsem, dwin, swin)
        sends = _gather_send(ins, outs, ssem, rsem, dwin, swin, 6)
        sends += _gather_pass_on(outs, ssem, rsem, dwin, 6, 0, 3)
        for cp in sends:
            cp.wait_send()
        for cp in own:
            cp.wait()

    out_shape = [_sds(_full_shape(n, s), s.dtype) for n, s in zip(GATHERED, srcs)]
    return _name_full(_comm_call(body, name, nn, out_shape, nn, 6 * nn)(*srcs))


def gather_start(srcs, *, name):
    nn = len(GATHERED)
    dwin, swin = _gather_windows(srcs)

    def body(*refs):
        ins, lands = refs[:nn], refs[nn:2 * nn]
        ssem, rsem = refs[2 * nn], refs[2 * nn + 1]
        token = refs[-1]
        _gather_send(ins, lands, ssem, rsem, dwin, swin, 3)
        token[...] = jnp.zeros_like(token)

    hbm = lambda a: pltpu.with_memory_space_constraint(a, pltpu.HBM)
    lands = [hbm(lax.empty(_full_shape(n, s), s.dtype)) for n, s in zip(GATHERED, srcs)]
    out_shape = ([pltpu.SemaphoreType.DMA((3 * nn,)), pltpu.SemaphoreType.DMA((3 * nn,))]
                 + [pltpu.HBM(s.shape, s.dtype) for s in srcs] + [pltpu.HBM(a.shape, a.dtype) for a in lands]
                 + [_sds(TOKEN_SHAPE, F32)])
    outs = _pc(body, name=name, in_specs=[HBM_SPEC] * (2 * nn),
               out_specs=[SEM_SPEC, SEM_SPEC] + [HBM_SPEC] * (2 * nn) + [pl.BlockSpec(memory_space=pltpu.VMEM)],
               out_shape=out_shape, input_output_aliases={i: 2 + i for i in range(2 * nn)},
               compiler_params=pltpu.CompilerParams(has_side_effects=pltpu.SideEffectType.DATAFLOW_SIDE_EFFECTING),
               )(*[hbm(s) for s in srcs], *lands)
    return dict(ssem=outs[0], rsem=outs[1], srcs=outs[2:2 + nn], lands=outs[2 + nn:2 + 2 * nn], token=outs[-1])


def gather_wait(handle, after, *, name):
    nn = len(GATHERED)
    srcs, lands = handle['srcs'], handle['lands']
    dwin, swin = _gather_windows(srcs)

    def body(*refs):
        ins, zones = refs[:nn], refs[nn:2 * nn]
        ssem, rsem = refs[2 * nn], refs[2 * nn + 1]
        x, y, c, others = _mesh_pos()
        for i, n in enumerate(GATHERED):
            h = c if n in BIG else None
            for j, chip in enumerate(others):
                kk = 2 * chip[0] + chip[1]
                cp = _rcopy(swin(ins, i, h), dwin(zones, i, kk, h), ssem, rsem, 3 * i + j, (*chip, c))
                cp.wait_send()
                cp.wait_recv()

    outs = _pc(body, name=name, in_specs=[HBM_SPEC] * (2 * nn) + [SEM_SPEC, SEM_SPEC] + [ANY] * len(after),
               out_specs=[HBM_SPEC] * (2 * nn),
               out_shape=[pltpu.HBM(s.shape, s.dtype) for s in srcs] + [pltpu.HBM(a.shape, a.dtype) for a in lands],
               input_output_aliases={i: i for i in range(2 * nn)},
               compiler_params=pltpu.CompilerParams(has_side_effects=pltpu.SideEffectType.DATAFLOW_SIDE_EFFECTING),
               )(*srcs, *lands, handle['ssem'], handle['rsem'], *after)
    return outs[:nn], outs[nn:]


def gather_finish(srcs, lands, *, name):
    nn = len(GATHERED)
    dwin, swin = _gather_windows(srcs)

    def body(*refs):
        ins, zones, outs = refs[:nn], refs[nn:2 * nn], refs[2 * nn:3 * nn]
        lsem, ssem, rsem = refs[3 * nn:]
        own = _own_blocks(ins, outs, lsem, dwin, swin)
        sends = _gather_pass_on(outs, ssem, rsem, dwin, 3, None, 0)
        for cp in sends:
            cp.wait_send()
        for cp in own:
            cp.wait()

    scratch = [pltpu.SemaphoreType.DMA((nn,)), pltpu.SemaphoreType.DMA((3 * nn,)), pltpu.SemaphoreType.DMA((3 * nn,))]
    outs = _pc(body, name=name, in_specs=[ANY] * (2 * nn), out_specs=[ANY] * nn,
               out_shape=[_sds(a.shape, a.dtype) for a in lands], scratch_shapes=scratch,
               input_output_aliases={nn + i: i for i in range(nn)})(*srcs, *lands)
    return _name_full(outs)


def _grad_view(n, g):
    return g.reshape(N_CHIPS, g.shape[0] // N_CHIPS, g.shape[1]) if n in BIG_ROW else g.reshape((1,) + g.shape)


def exchange_halves(gv, *, name):
    nn = len(BIG)

    def body(*refs):
        ins, outs = refs[:nn], refs[nn:2 * nn]
        _, ssem, rsem = refs[2 * nn:]
        x, y, c, _ = _mesh_pos()
        cps = []
        for i in range(nn):
            rh = gv[i].shape[1] // 2
            cp = _rcopy(ins[i].at[:, pl.ds((1 - c) * rh, rh), :], outs[i], ssem, rsem, i, (x, y, 1 - c))
            cp.start()
            cps.append(cp)
        for cp in cps:
            cp.wait()

    out_shape = [_sds((a.shape[0], a.shape[1] // 2, a.shape[2]), a.dtype) for a in gv]
    return _comm_call(body, name, nn, out_shape, 0, nn)(*gv)


def _shard_shape(n, p):
    _, rh, w = p.shape
    return (rh, w) if n in BIG_ROW else (rh, w // N_CHIPS)


def _scatter_copies(pv, ins, outs, ssem, rsem):
    x, y, c, others = _mesh_pos()
    cps = []
    for i, n in enumerate(BIG):
        _, ws = _shard_shape(n, pv[i])
        for j, chip in enumerate(others):
            kk = 2 * chip[0] + chip[1]
            if n in BIG_ROW:
                src = ins[i].at[kk]
            else:
                src = ins[i].at[0, :, pl.ds(pl.multiple_of(kk * ws, 128), ws)]
            cps.append(_rcopy(src, outs[i].at[j], ssem, rsem, 3 * i + j, (*chip, c)))
    return cps


def _recv_shapes(pv):
    return [(N_CHIPS - 1,) + _shard_shape(n, p) for n, p in zip(BIG, pv)]


def scatter_partials(pv, *, name):
    nn = len(BIG)

    def body(*refs):
        ins, outs = refs[:nn], refs[nn:2 * nn]
        _, ssem, rsem = refs[2 * nn:]
        cps = _scatter_copies(pv, ins, outs, ssem, rsem)
        for cp in cps:
            cp.start()
        for cp in cps:
            cp.wait()

    out_shape = [_sds(s, p.dtype) for s, p in zip(_recv_shapes(pv), pv)]
    return pv, _comm_call(body, name, nn, out_shape, 0, 3 * nn)(*pv)


def scatter_start(pv, *, name):
    nn = len(BIG)

    def body(*refs):
        ins, lands = refs[:nn], refs[nn:2 * nn]
        ssem, rsem = refs[2 * nn], refs[2 * nn + 1]
        token = refs[-1]
        for cp in _scatter_copies(pv, ins, lands, ssem, rsem):
            cp.start()
        token[...] = jnp.zeros_like(token)

    hbm = lambda a: pltpu.with_memory_space_constraint(a, pltpu.HBM)
    lands = [hbm(lax.empty(s, p.dtype)) for s, p in zip(_recv_shapes(pv), pv)]
    out_shape = ([pltpu.SemaphoreType.DMA((3 * nn,)), pltpu.SemaphoreType.DMA((3 * nn,))]
                 + [pltpu.HBM(p.shape, p.dtype) for p in pv] + [pltpu.HBM(a.shape, a.dtype) for a in lands]
                 + [_sds(TOKEN_SHAPE, F32)])
    outs = _pc(body, name=name, in_specs=[HBM_SPEC] * (2 * nn),
               out_specs=[SEM_SPEC, SEM_SPEC] + [HBM_SPEC] * (2 * nn) + [pl.BlockSpec(memory_space=pltpu.VMEM)],
               out_shape=out_shape, input_output_aliases={i: 2 + i for i in range(2 * nn)},
               compiler_params=pltpu.CompilerParams(has_side_effects=pltpu.SideEffectType.DATAFLOW_SIDE_EFFECTING),
               )(*[hbm(p) for p in pv], *lands)
    return dict(ssem=outs[0], rsem=outs[1], pv=outs[2:2 + nn], lands=outs[2 + nn:2 + 2 * nn], token=outs[-1])


def scatter_wait(handle, after, *, name):
    nn = len(BIG)
    pv, lands = handle['pv'], handle['lands']

    def body(*refs):
        ins, zones = refs[:nn], refs[nn:2 * nn]
        ssem, rsem = refs[2 * nn], refs[2 * nn + 1]
        for cp in _scatter_copies(pv, ins, zones, ssem, rsem):
            cp.wait_send()
            cp.wait_recv()

    outs = _pc(body, name=name, in_specs=[HBM_SPEC] * (2 * nn) + [SEM_SPEC, SEM_SPEC] + [ANY] * len(after),
               out_specs=[HBM_SPEC] * (2 * nn),
               out_shape=[pltpu.HBM(p.shape, p.dtype) for p in pv] + [pltpu.HBM(a.shape, a.dtype) for a in lands],
               input_output_aliases={i: i for i in range(2 * nn)},
               compiler_params=pltpu.CompilerParams(has_side_effects=pltpu.SideEffectType.DATAFLOW_SIDE_EFFECTING),
               )(*pv, *lands, handle['ssem'], handle['rsem'], *after)
    return outs[:nn], outs[nn:]


def join_halves(rv, *, name):
    nn = len(BIG)

    def body(*refs):
        ins, outs = refs[:nn], refs[nn:2 * nn]
        _, ssem, rsem = refs[2 * nn:]
        x, y, c, _ = _mesh_pos()
        cps = []
        for i in range(nn):
            rh = rv[i].shape[0] // 2
            rows = pl.ds(c * rh, rh)
            cp = _rcopy(ins[i].at[rows, :], outs[i].at[rows, :], ssem, rsem, i, (x, y, 1 - c))
            cp.start()
            cps.append(cp)
        for i, cp in enumerate(cps):
            cp.wait_send()
            rh = rv[i].shape[0] // 2
            got = outs[i].at[pl.ds((1 - c) * rh, rh), :]
            _rcopy(got, got, ssem, rsem, i, (x, y, 1 - c)).wait_recv()

    out_shape = [_sds(a.shape, a.dtype) for a in rv]
    scratch = [pltpu.SemaphoreType.DMA((1,)), pltpu.SemaphoreType.DMA((nn,)), pltpu.SemaphoreType.DMA((nn,))]
    return _pc(body, name=name, in_specs=[ANY] * nn, out_specs=[ANY] * nn, out_shape=out_shape, scratch_shapes=scratch,
               input_output_aliases={i: i for i in range(nn)})(*rv)


def chip_partials(grads, c_idx):
    gv = [_grad_view(n, grads[n]) for n in BIG]
    recv = exchange_halves(gv, name="rs_exchange_halves")
    return [add_half(g, r, c_idx, name="rs_add_" + n) for n, g, r in zip(BIG, gv, recv)]


def reduce_shards(pv, got, pos_idx):
    rv = [sum_shard(p, r, pos_idx, by_rows=n in BIG_ROW, name="rs_sum_" + n) for n, p, r in zip(BIG, pv, got)]
    return dict(zip(BIG, join_halves(rv, name="rs_join_halves")))


def allgather_slab(slab, *, name):
    r, w = slab.shape

    def body(x_ref, out_ref, lsem, ssem, rsem):
        x, y, c, others = _mesh_pos()
        me, sib = (x, y, c), (x, y, 1 - c)

        def slot(px, py, pc):
            return out_ref.at[4 * px + 2 * py + pc]

        mine = pltpu.make_async_copy(x_ref, slot(*me), lsem.at[0])
        mine.start()
        first = [_rcopy(x_ref, slot(*me), ssem, rsem, 0, sib)]
        first += [_rcopy(x_ref, slot(*me), ssem, rsem, 1 + j, (*chip, c)) for j, chip in enumerate(others)]
        for cp in first:
            cp.start()
        passed = [_rcopy(slot(*chip, c), slot(*chip, c), ssem, rsem, 4 + j, sib) for j, chip in enumerate(others)]
        for j, chip in enumerate(others):
            _rcopy(slot(*chip, c), slot(*chip, c), ssem, rsem, 1 + j, me).wait_recv()
            passed[j].start()
        _rcopy(slot(*sib), slot(*sib), ssem, rsem, 0, me).wait_recv()
        for j, chip in enumerate(others):
            _rcopy(slot(*chip, 1 - c), slot(*chip, 1 - c), ssem, rsem, 4 + j, me).wait_recv()
        for cp in first + passed:
            cp.wait_send()
        mine.wait()

    return _comm_call(body, name, 1, [_sds((N_DEV, r, w), slab.dtype)], 1, 7)(slab)[0]


def _pad128(n):
    return -(-n // 128) * 128


def _pack_small(grads, shapes):
    parts = []
    for g in grads:
        for n in SMALL:
            v = g[n].astype(F32).reshape(-1)
            parts.append(jnp.pad(v, (0, _pad128(v.shape[0]) - v.shape[0])))
    flat = jnp.concatenate(parts)
    rows = -(-flat.shape[0] // (128 * 512)) * 512
    return jnp.pad(flat, (0, rows * 128 - flat.shape[0])).reshape(rows, 128)


def _unpack_small(slab, shapes, depth):
    flat = slab.reshape(-1)
    out = {n: [] for n in SMALL}
    off = 0
    for _ in range(depth):
        for n in SMALL:
            size = math.prod(shapes[n])
            out[n].append(flat[off:off + size].reshape(shapes[n]))
            off += _pad128(size)
    return {n: jnp.stack(v) for n, v in out.items()}


def _adamw_nd(w, g, m, v, name):
    shp = w.shape
    two = lambda a: a.reshape(-1, shp[-1])
    return tuple(o.reshape(shp) for o in adamw(two(w), two(g), two(m), two(v), name=name))


def kernel(x, ln1_g, w_in, b_gate, sg_ln_g, sg_ln_b, sg_w, sg_b, w_a_out, cv_w, cv_b, cv_ln_g, cv_ln_b, w_b_out, q_norm_g, k_norm_g, w_c_out, w_out, ln2_g, w_up, ffn_conv_w, ffn_conv_b, w_down, loss_target, m_ln1_g, m_w_in, m_b_gate, m_sg_ln_g, m_sg_ln_b, m_sg_w, m_sg_b, m_w_a_out, m_cv_w, m_cv_b, m_cv_ln_g, m_cv_ln_b, m_w_b_out, m_q_norm_g, m_k_norm_g, m_w_c_out, m_w_out, m_ln2_g, m_w_up, m_ffn_conv_w, m_ffn_conv_b, m_w_down, v_ln1_g, v_w_in, v_b_gate, v_sg_ln_g, v_sg_ln_b, v_sg_w, v_sg_b, v_w_a_out, v_cv_w, v_cv_b, v_cv_ln_g, v_cv_ln_b, v_w_b_out, v_q_norm_g, v_k_norm_g, v_w_c_out, v_w_out, v_ln2_g, v_w_up, v_ffn_conv_w, v_ffn_conv_b, v_w_down):
    w = dict(ln1_g=ln1_g, w_in=w_in, b_gate=b_gate, sg_ln_g=sg_ln_g, sg_ln_b=sg_ln_b, sg_w=sg_w, sg_b=sg_b,
             w_a_out=w_a_out, cv_w=cv_w, cv_b=cv_b, cv_ln_g=cv_ln_g, cv_ln_b=cv_ln_b, w_b_out=w_b_out,
             q_norm_g=q_norm_g, k_norm_g=k_norm_g, w_c_out=w_c_out, w_out=w_out, ln2_g=ln2_g, w_up=w_up,
             ffn_conv_w=ffn_conv_w, ffn_conv_b=ffn_conv_b, w_down=w_down)
    m = dict(ln1_g=m_ln1_g, w_in=m_w_in, b_gate=m_b_gate, sg_ln_g=m_sg_ln_g, sg_ln_b=m_sg_ln_b, sg_w=m_sg_w,
             sg_b=m_sg_b, w_a_out=m_w_a_out, cv_w=m_cv_w, cv_b=m_cv_b, cv_ln_g=m_cv_ln_g, cv_ln_b=m_cv_ln_b,
             w_b_out=m_w_b_out, q_norm_g=m_q_norm_g, k_norm_g=m_k_norm_g, w_c_out=m_w_c_out, w_out=m_w_out,
             ln2_g=m_ln2_g, w_up=m_w_up, ffn_conv_w=m_ffn_conv_w, ffn_conv_b=m_ffn_conv_b, w_down=m_w_down)
    v = dict(ln1_g=v_ln1_g, w_in=v_w_in, b_gate=v_b_gate, sg_ln_g=v_sg_ln_g, sg_ln_b=v_sg_ln_b, sg_w=v_sg_w,
             sg_b=v_sg_b, w_a_out=v_w_a_out, cv_w=v_cv_w, cv_b=v_cv_b, cv_ln_g=v_cv_ln_g, cv_ln_b=v_cv_ln_b,
             w_b_out=v_w_b_out, q_norm_g=v_q_norm_g, k_norm_g=v_k_norm_g, w_c_out=v_w_c_out, w_out=v_w_out,
             ln2_g=v_ln2_g, w_up=v_w_up, ffn_conv_w=v_ffn_conv_w, ffn_conv_b=v_ffn_conv_b, w_down=v_w_down)
    depth = ln1_g.shape[0]
    cx, cy, cc = lax.axis_index("x"), lax.axis_index("y"), lax.axis_index("c")
    me = 2 * cx + cy
    pos_idx = jnp.stack([me, cc]).astype(jnp.int32)
    c_idx = jnp.reshape(cc, (1,)).astype(jnp.int32)

    def blocks(l):
        out = [cast_bf16(w[n], l, name="cast_" + n) for n in BIG]
        return out + [jnp.pad(w[n][l], ((0, -w[n].shape[1] % 8), (0, 0))) for n in SMALL_COL]

    full0 = allgather_weights(blocks(0), name="allgather_weights")
    gathers = [gather_start(blocks(l), name="gather_start_%d" % l) for l in range(1, depth)]

    def get_layer(l, x_in):
        if l == 0:
            p, after = full0, tuple(h['token'] for h in gathers)
        else:
            srcs, lands = gather_wait(gathers[l - 1], [x_in], name="gather_wait_%d" % l)
            p, after = gather_finish(srcs, lands, name="gather_finish"), ()
        for n in SMALL:
            p[n] = p[n][:w[n].shape[1]] if n in SMALL_COL else w[n][l]
        return p, after

    grads, scatters = [None] * depth, [None] * depth

    def on_grads(l, g):
        grads[l] = g
        pv = chip_partials(g, c_idx)
        if l == 0:
            scatters[l] = scatter_partials(pv, name="rs_scatter_partials")
            return ()
        scatters[l] = scatter_start(pv, name="scatter_start_%d" % l)
        return (scatters[l]['token'],)

    loss, dx = local_step(x[0], loss_target[0], depth, get_layer, on_grads)
    loss = lax.psum(loss, ("x", "y", "c"))
    big = [None] * depth
    for l in range(depth):
        pv, got = scatters[l] if l == 0 else scatter_wait(scatters[l], [dx], name="scatter_wait_%d" % l)
        big[l] = reduce_shards(pv, got, pos_idx)
    full_shapes = {n: (w[n].shape[1], N_CHIPS * w[n].shape[2]) if n in SMALL_COL else w[n].shape[1:] for n in SMALL}
    slab = sum_slots(allgather_slab(_pack_small(grads, full_shapes), name="allgather_small_grads"), name="sum_small_grads")
    small = _unpack_small(slab, full_shapes, depth)
    grad = {n: jnp.stack([b[n] for b in big]) for n in BIG}
    for n in SMALL:
        if n in SMALL_COL:
            cs = w[n].shape[-1]
            grad[n] = lax.dynamic_slice_in_dim(small[n], me * cs, cs, axis=small[n].ndim - 1)
        else:
            grad[n] = small[n]

    delta, new_m, new_v = {}, {}, {}
    for n in WEIGHTS:
        delta[n], new_m[n], new_v[n] = _adamw_nd(w[n], grad[n], m[n], v[n], "adamw_" + n)
    return (loss, dx[None], *[grad[n] for n in WEIGHTS], *[delta[n] for n in WEIGHTS],
            *[new_m[n] for n in WEIGHTS], *[new_v[n] for n in WEIGHTS])
```

```python
import functools
import math

import jax
import jax.numpy as jnp
from jax import lax
from jax.experimental import pallas as pl
from jax.experimental.pallas import tpu as pltpu

F32 = jnp.float32
BF16 = jnp.bfloat16
MESH = pl.DeviceIdType.MESH
ANY = pl.BlockSpec(memory_space=pl.ANY)

EPS = 1e-6
D_MODEL = 1024
DEPTH = 4
SG_WIDTH = 512
CHUNK = 128
CV_WIDTH = 512
CV_KERNEL = 31
SB_WIDTH = 512
SB_HEAD_DIM = 64
Q_BLOCK = 128
D_FF = 2816
FFN_KERNEL = 3
COL_B = 1024
COL_C = 2048
COL_G = 3584
IN_COLS = 6656
N_CHIPS = 4
N_DEV = 8
CV_HALO = 32
FFN_HALO = 16

ADAM_LR = 0.001
ADAM_B1 = 0.9
ADAM_B2 = 0.999
ADAM_EPS = 1e-08
ADAM_WD = 0.01
ADAM_STEP = 10

VMEM_LIMIT_BYTES = 56 * 1024 * 1024

NT_DIMS = (((1,), (1,)), ((), ()))
TN_DIMS = (((0,), (0,)), ((), ()))

WEIGHTS = ['ln1_g', 'w_in', 'b_gate', 'sg_ln_g', 'sg_ln_b', 'sg_w', 'sg_b', 'w_a_out', 'cv_w', 'cv_b',
           'cv_ln_g', 'cv_ln_b', 'w_b_out', 'q_norm_g', 'k_norm_g', 'w_c_out', 'w_out', 'ln2_g', 'w_up',
           'ffn_conv_w', 'ffn_conv_b', 'w_down']
BIG_COL = ['w_in', 'w_a_out', 'w_b_out', 'w_c_out', 'w_up']
BIG_ROW = ['w_out', 'w_down']
BIG = BIG_COL + BIG_ROW
SMALL_COL = ['b_gate', 'cv_w', 'ffn_conv_w']
SMALL = [n for n in WEIGHTS if n not in BIG]


def _pc(body, **kw):
    return pl.pallas_call(body, **kw)


def _cp(*sem):
    return pltpu.CompilerParams(dimension_semantics=sem, vmem_limit_bytes=VMEM_LIMIT_BYTES)


def _sds(shape, dtype):
    return jax.ShapeDtypeStruct(shape, dtype)


_GELU_C = math.sqrt(2.0 / math.pi)
_GELU_A = 0.044715


def _sigmoid(x):
    return jax.nn.sigmoid(x)


def _gelu(x):
    return 0.5 * x * (1.0 + jnp.tanh(_GELU_C * (x + _GELU_A * x * x * x)))


def _gelu_grad(x):
    t = jnp.tanh(_GELU_C * (x + _GELU_A * x * x * x))
    return 0.5 * (1.0 + t) + 0.5 * x * (1.0 - t * t) * _GELU_C * (1.0 + 3.0 * _GELU_A * x * x)


def _silu(x):
    return x * _sigmoid(x)


def _silu_grad(x):
    s = _sigmoid(x)
    return s * (1.0 + x * (1.0 - s))


def _ln_stats(x):
    mu = jnp.mean(x, axis=-1, keepdims=True)
    xc = x - mu
    r = lax.rsqrt(jnp.mean(xc * xc, axis=-1, keepdims=True) + EPS)
    return xc * r, r


def _ln_bwd(dy, xhat, r, g):
    dxh = dy * g
    return r * (dxh - jnp.mean(dxh, axis=-1, keepdims=True) - xhat * jnp.mean(dxh * xhat, axis=-1, keepdims=True))


def _split_dot(x, m):
    hi = x.astype(BF16)
    lo = (x - hi.astype(F32)).astype(BF16)
    return jnp.dot(hi, m, preferred_element_type=F32) + jnp.dot(lo, m, preferred_element_type=F32)


def _rowsum0(x):
    return jnp.sum(x, axis=0, keepdims=True)


def _pick(n, prefs):
    for p in prefs:
        if n % p == 0:
            return p
    return n


def mm_nn(a, w, *, name, res=None, out_dtype=BF16):
    t, k = a.shape
    n = w.shape[1]
    tm = _pick(t, (512, 256))
    tn = _pick(n, (1024, 512, 256))

    def body(*refs):
        if res is None:
            a_ref, w_ref, o_ref = refs
        else:
            a_ref, w_ref, r_ref, o_ref = refs
        acc = jnp.dot(a_ref[...], w_ref[...], preferred_element_type=F32)
        if res is not None:
            acc = acc + r_ref[...]
        o_ref[...] = acc.astype(o_ref.dtype)

    in_specs = [pl.BlockSpec((tm, k), lambda i, j: (i, 0)), pl.BlockSpec((k, tn), lambda i, j: (0, j))]
    args = [a, w]
    if res is not None:
        in_specs.append(pl.BlockSpec((tm, tn), lambda i, j: (i, j)))
        args.append(res)
    return _pc(body, name=name, grid=(t // tm, n // tn), in_specs=in_specs,
               out_specs=pl.BlockSpec((tm, tn), lambda i, j: (i, j)),
               out_shape=_sds((t, n), out_dtype), compiler_params=_cp("parallel", "parallel"))(*args)


def mm_norm_nn(x, g, w, *, name, after=()):
    t, k = x.shape
    n = w.shape[1]
    tm = _pick(t, (512, 256))
    tn = _pick(n, (1664, 1408, 512))

    def body(x_ref, g_ref, w_ref, *rest):
        z_ref, h_ref = rest[len(after):]

        @pl.when(pl.program_id(1) == 0)
        def _():
            xv = x_ref[...]
            r = lax.rsqrt(jnp.mean(xv * xv, axis=-1, keepdims=True) + EPS)
            h_ref[...] = (xv * r * g_ref[...]).astype(BF16)

        z_ref[...] = jnp.dot(h_ref[...], w_ref[...], preferred_element_type=F32).astype(z_ref.dtype)

    return _pc(body, name=name, grid=(t // tm, n // tn),
               in_specs=[pl.BlockSpec((tm, k), lambda i, j: (i, 0)), pl.BlockSpec((1, k), lambda i, j: (0, 0)),
                         pl.BlockSpec((k, tn), lambda i, j: (0, j))]
               + [pl.BlockSpec(a.shape, lambda i, j: (0, 0)) for a in after],
               out_specs=[pl.BlockSpec((tm, tn), lambda i, j: (i, j)), pl.BlockSpec((tm, k), lambda i, j: (i, 0))],
               out_shape=[_sds((t, n), BF16), _sds((t, k), BF16)],
               compiler_params=_cp("parallel", "arbitrary"))(x, g, w, *after)


def mm_nt(dy, w, *, name, out_dtype):
    t, n = dy.shape
    k = w.shape[0]
    tm = _pick(t, (512, 256))
    tn = _pick(n, (1664, 1408, 1024, 512))
    nj = n // tn

    def body(dy_ref, w_ref, o_ref, acc_ref):
        j = pl.program_id(1)
        p = lax.dot_general(dy_ref[...].astype(BF16), w_ref[...], NT_DIMS, preferred_element_type=F32)

        @pl.when(j == 0)
        def _():
            acc_ref[...] = p

        @pl.when(j > 0)
        def _():
            acc_ref[...] += p

        @pl.when(j == nj - 1)
        def _():
            o_ref[...] = acc_ref[...].astype(o_ref.dtype)

    return _pc(body, name=name, grid=(t // tm, nj),
               in_specs=[pl.BlockSpec((tm, tn), lambda i, j: (i, j)), pl.BlockSpec((k, tn), lambda i, j: (0, j))],
               out_specs=pl.BlockSpec((tm, k), lambda i, j: (i, 0)),
               out_shape=_sds((t, k), out_dtype), scratch_shapes=[pltpu.VMEM((tm, k), F32)],
               compiler_params=_cp("parallel", "arbitrary"))(dy, w)


def mm_tn(a, dy, *, name, out_dtype=BF16, after=()):
    t, k = a.shape
    n = dy.shape[1]
    tk = _pick(k, (512, 1408))
    tn = _pick(n, (1664, 1408, 1024, 512))
    tt = _pick(t, (512, 256))
    nt = t // tt

    def body(a_ref, dy_ref, *rest):
        o_ref, acc_ref = rest[len(after):]
        s = pl.program_id(2)
        p = lax.dot_general(a_ref[...], dy_ref[...].astype(BF16), TN_DIMS, preferred_element_type=F32)

        @pl.when(s == 0)
        def _():
            acc_ref[...] = p

        @pl.when(s > 0)
        def _():
            acc_ref[...] += p

        @pl.when(s == nt - 1)
        def _():
            o_ref[...] = acc_ref[...].astype(o_ref.dtype)

    return _pc(body, name=name, grid=(k // tk, n // tn, nt),
               in_specs=[pl.BlockSpec((tt, tk), lambda i, j, s: (s, i)), pl.BlockSpec((tt, tn), lambda i, j, s: (s, j))]
               + [pl.BlockSpec(tok.shape, lambda i, j, s: (0, 0)) for tok in after],
               out_specs=pl.BlockSpec((tk, tn), lambda i, j, s: (i, j)),
               out_shape=_sds((k, n), out_dtype), scratch_shapes=[pltpu.VMEM((tk, tn), F32)],
               compiler_params=_cp("parallel", "parallel", "arbitrary"))(a, dy, *after)


def rms_bwd(dh, x, g, dres, *, name):
    t, d = x.shape
    tm = _pick(t, (256,))

    def body(dh_ref, x_ref, g_ref, dres_ref, dx_ref, dg_ref):
        xv = x_ref[...]
        r = lax.rsqrt(jnp.mean(xv * xv, axis=-1, keepdims=True) + EPS)
        xh = xv * r
        dy = dh_ref[...].astype(F32)
        dxh = dy * g_ref[...]
        dx_ref[...] = dres_ref[...] + r * (dxh - xh * jnp.mean(dxh * xh, axis=-1, keepdims=True))

        @pl.when(pl.program_id(0) == 0)
        def _():
            dg_ref[...] = jnp.zeros_like(dg_ref)

        dg_ref[...] += _rowsum0(dy * xh)

    row = pl.BlockSpec((tm, d), lambda i: (i, 0))
    vec = pl.BlockSpec((1, d), lambda i: (0, 0))
    return _pc(body, name=name, grid=(t // tm,), in_specs=[row, row, vec, row], out_specs=[row, vec],
               out_shape=[_sds((t, d), F32), _sds((1, d), F32)], compiler_params=_cp("arbitrary"))(dh, x, g, dres)


def loss_head(y, target, *, name):
    t, d = y.shape
    tm = _pick(t, (256,))

    def body(y_ref, t_ref, loss_ref, dy_ref):
        e = y_ref[...] - t_ref[...]
        dy_ref[...] = e * (1.0 / d)

        @pl.when(pl.program_id(0) == 0)
        def _():
            loss_ref[...] = jnp.zeros_like(loss_ref)

        loss_ref[...] += _rowsum0(jnp.sum(e * e, axis=1, keepdims=True)) * (0.5 / d)

    row = pl.BlockSpec((tm, d), lambda i: (i, 0))
    return _pc(body, name=name, grid=(t // tm,), in_specs=[row, row],
               out_specs=[pl.BlockSpec((1, 1), lambda i: (0, 0)), row],
               out_shape=[_sds((1, 1), F32), _sds((t, d), F32)], compiler_params=_cp("arbitrary"))(y, target)


def _sg_masks():
    lane = lax.broadcasted_iota(jnp.int32, (CHUNK, CHUNK), 1)
    row = lax.broadcasted_iota(jnp.int32, (CHUNK, CHUNK), 0)
    return lane < 64, lane <= row, row <= lane


def _sg_gate(vn_chunk, w_ref, bias_ref, p, first_group, tril):
    wa = jnp.where(tril, w_ref[2 * p], 0.0).astype(BF16)
    wb = jnp.where(tril, w_ref[2 * p + 1], 0.0).astype(BF16)
    oa = jnp.dot(wa, vn_chunk, preferred_element_type=F32)
    ob = jnp.dot(wb, vn_chunk, preferred_element_type=F32)
    return jnp.where(first_group, oa, ob) + bias_ref[:, p * 128:(p + 1) * 128]


def mixa_fwd(z, ln_g, ln_b, sg_w, sg_bias, *, name):
    t = z.shape[0]
    tm = _pick(t, (256,))

    def body(z_ref, g_ref, b_ref, w_ref, bias_ref, o_ref):
        first_group, tril, _ = _sg_masks()
        zv = z_ref[...].astype(F32)
        u = _gelu(zv[:, :SG_WIDTH])
        v = _gelu(zv[:, SG_WIDTH:])
        vh, _ = _ln_stats(v)
        vn = (vh * g_ref[...] + b_ref[...]).astype(BF16)
        for c in range(tm // CHUNK):
            rows = slice(c * CHUNK, (c + 1) * CHUNK)
            for p in range(4):
                cols = slice(p * 128, (p + 1) * 128)
                o = _sg_gate(vn[rows, cols], w_ref, bias_ref, p, first_group, tril)
                o_ref[rows, cols] = (u[rows, cols] * o).astype(o_ref.dtype)

    vec = pl.BlockSpec((1, SG_WIDTH), lambda i: (0, 0))
    return _pc(body, name=name, grid=(t // tm,),
               in_specs=[pl.BlockSpec((tm, 2 * SG_WIDTH), lambda i: (i, 0)), vec, vec,
                         pl.BlockSpec((8, CHUNK, CHUNK), lambda i: (0, 0, 0)),
                         pl.BlockSpec((CHUNK, SG_WIDTH), lambda i: (0, 0))],
               out_specs=pl.BlockSpec((tm, SG_WIDTH), lambda i: (i, 0)),
               out_shape=_sds((t, SG_WIDTH), BF16), compiler_params=_cp("parallel"))(z, ln_g, ln_b, sg_w, sg_bias)


def mixa_bwd(z, dga, ln_g, ln_b, sg_w, sg_wt, sg_bias, *, name):
    t = z.shape[0]
    tm = _pick(t, (256,))
    nsteps = t // tm

    def body(z_ref, dga_ref, g_ref, b_ref, w_ref, wt_ref, bias_ref, dz_ref, dw_ref, dsgb_ref, dg_ref, db_ref, dvn_s,
             dbias_ref):
        i = pl.program_id(0)
        first_group, tril, triu = _sg_masks()

        @pl.when(i == 0)
        def _():
            dw_ref[...] = jnp.zeros_like(dw_ref)
            dbias_ref[...] = jnp.zeros_like(dbias_ref)
            dg_ref[...] = jnp.zeros_like(dg_ref)
            db_ref[...] = jnp.zeros_like(db_ref)

        zv = z_ref[...].astype(F32)
        zu = zv[:, :SG_WIDTH]
        zg = zv[:, SG_WIDTH:]
        u = _gelu(zu)
        v = _gelu(zg)
        vh, r = _ln_stats(v)
        vn = (vh * g_ref[...] + b_ref[...]).astype(BF16)
        dga_v = dga_ref[...].astype(F32)
        d_o = dga_v * u
        for c in range(tm // CHUNK):
            rows = slice(c * CHUNK, (c + 1) * CHUNK)
            dbias_ref[...] += d_o[rows, :]
            for p in range(4):
                cols = slice(p * 128, (p + 1) * 128)
                vp = vn[rows, cols]
                o = _sg_gate(vp, w_ref, bias_ref, p, first_group, tril)
                dz_ref[rows, cols] = (dga_v[rows, cols] * o * _gelu_grad(zu[rows, cols])).astype(dz_ref.dtype)
                dop = d_o[rows, cols]
                dop_a = jnp.where(first_group, dop, 0.0).astype(BF16)
                dop_b = jnp.where(first_group, 0.0, dop).astype(BF16)
                dw_ref[2 * p] += lax.dot_general(dop_a, vp, NT_DIMS, preferred_element_type=F32)
                dw_ref[2 * p + 1] += lax.dot_general(dop_b, vp, NT_DIMS, preferred_element_type=F32)
                wta = jnp.where(triu, wt_ref[2 * p], 0.0).astype(BF16)
                wtb = jnp.where(triu, wt_ref[2 * p + 1], 0.0).astype(BF16)
                dop16 = dop.astype(BF16)
                dvn_s[rows, cols] = jnp.where(first_group, jnp.dot(wta, dop16, preferred_element_type=F32),
                                              jnp.dot(wtb, dop16, preferred_element_type=F32))
        dvn = dvn_s[...]
        dg_ref[...] += _rowsum0(dvn * vh)
        db_ref[...] += _rowsum0(dvn)
        dv = _ln_bwd(dvn, vh, r, g_ref[...])
        dz_ref[:, SG_WIDTH:] = (dv * _gelu_grad(zg)).astype(dz_ref.dtype)

        @pl.when(i == nsteps - 1)
        def _():
            for gi in range(8):
                dw_ref[gi] = jnp.where(tril, dw_ref[gi], 0.0)
            r_id = lax.broadcasted_iota(jnp.int32, (SG_WIDTH, 128), 0) // SB_HEAD_DIM
            c_id = lax.broadcasted_iota(jnp.int32, (SG_WIDTH, 128), 1)
            dsgb_ref[...] = _split_dot(dbias_ref[...], (r_id == c_id).astype(BF16))

    vec = pl.BlockSpec((1, SG_WIDTH), lambda i: (0, 0))
    wspec = pl.BlockSpec((8, CHUNK, CHUNK), lambda i: (0, 0, 0))
    bspec = pl.BlockSpec((CHUNK, SG_WIDTH), lambda i: (0, 0))
    sgb = pl.BlockSpec((CHUNK, 128), lambda i: (0, 0))
    return _pc(body, name=name, grid=(nsteps,),
               in_specs=[pl.BlockSpec((tm, 2 * SG_WIDTH), lambda i: (i, 0)), pl.BlockSpec((tm, SG_WIDTH), lambda i: (i, 0)),
                         vec, vec, wspec, wspec, bspec],
               out_specs=[pl.BlockSpec((tm, 2 * SG_WIDTH), lambda i: (i, 0)), wspec, sgb, vec, vec],
               out_shape=[_sds((t, 2 * SG_WIDTH), BF16), _sds((8, CHUNK, CHUNK), F32), _sds((CHUNK, 128), F32),
                          _sds((1, SG_WIDTH), F32), _sds((1, SG_WIDTH), F32)],
               scratch_shapes=[pltpu.VMEM((tm, SG_WIDTH), F32), pltpu.VMEM((CHUNK, SG_WIDTH), F32)],
               compiler_params=_cp("arbitrary"))(z, dga, ln_g, ln_b, sg_w, sg_wt, sg_bias)


def _glu(zv):
    return zv[:, :CV_WIDTH] * _sigmoid(zv[:, CV_WIDTH:])


def _cv_fill(zm_ref, zh_ref, x_s, i, tm):
    x_s[0:CV_HALO, :] = jnp.where(i > 0, _glu(zh_ref[...].astype(F32)), 0.0)
    x_s[CV_HALO:CV_HALO + tm, :] = _glu(zm_ref[...].astype(F32))


def _cv_conv(x_s, w_ref, cb_ref, tm):
    acc = jnp.zeros((tm, CV_WIDTH), F32) + cb_ref[...]
    for k in range(CV_KERNEL):
        acc = acc + w_ref[k:k + 1, :] * x_s[pl.ds(CV_HALO - (CV_KERNEL - 1) + k, tm), :]
    return acc


def _cv_specs(tm):
    zm = pl.BlockSpec((tm, 2 * CV_WIDTH), lambda i: (i, 1))
    zh = pl.BlockSpec((CV_HALO, 2 * CV_WIDTH), lambda i: (jnp.maximum(i * (tm // CV_HALO) - 1, 0), 1))
    w = pl.BlockSpec((CV_KERNEL, CV_WIDTH), lambda i: (0, 0))
    vec = pl.BlockSpec((1, CV_WIDTH), lambda i: (0, 0))
    return zm, zh, w, vec


def mixb_fwd(z, cv_w, cv_b, ln_g, ln_b, *, name):
    t = z.shape[0]
    tm = _pick(t, (256,))

    def body(zm_ref, zh_ref, w_ref, cb_ref, g_ref, b_ref, o_ref, x_s):
        _cv_fill(zm_ref, zh_ref, x_s, pl.program_id(0), tm)
        c1 = _cv_conv(x_s, w_ref, cb_ref, tm)
        ch, _ = _ln_stats(c1)
        o_ref[...] = _silu(ch * g_ref[...] + b_ref[...]).astype(o_ref.dtype)

    zm, zh, w, vec = _cv_specs(tm)
    return _pc(body, name=name, grid=(t // tm,), in_specs=[zm, zh, w, vec, vec, vec],
               out_specs=pl.BlockSpec((tm, CV_WIDTH), lambda i: (i, 0)), out_shape=_sds((t, CV_WIDTH), BF16),
               scratch_shapes=[pltpu.VMEM((CV_HALO + tm, CV_WIDTH), F32)],
               compiler_params=_cp("parallel"))(z, z, cv_w, cv_b, ln_g, ln_b)


def mixb_bwd1(z, dc3, cv_w, cv_b, ln_g, ln_b, *, name):
    t = z.shape[0]
    tm = _pick(t, (256,))

    def body(zm_ref, zh_ref, dc3_ref, w_ref, cb_ref, g_ref, b_ref, dc1_ref, dw_ref, dcb_ref, dg_ref, db_ref, x_s):
        i = pl.program_id(0)

        @pl.when(i == 0)
        def _():
            dw_ref[...] = jnp.zeros_like(dw_ref)
            dcb_ref[...] = jnp.zeros_like(dcb_ref)
            dg_ref[...] = jnp.zeros_like(dg_ref)
            db_ref[...] = jnp.zeros_like(db_ref)

        _cv_fill(zm_ref, zh_ref, x_s, i, tm)
        c1 = _cv_conv(x_s, w_ref, cb_ref, tm)
        ch, r = _ln_stats(c1)
        c2 = ch * g_ref[...] + b_ref[...]
        dc2 = dc3_ref[...].astype(F32) * _silu_grad(c2)
        dg_ref[...] += _rowsum0(dc2 * ch)
        db_ref[...] += _rowsum0(dc2)
        dc1 = _ln_bwd(dc2, ch, r, g_ref[...])
        dc1_ref[...] = dc1
        dcb_ref[...] += _rowsum0(dc1)
        for k in range(CV_KERNEL):
            dw_ref[k:k + 1, :] += _rowsum0(dc1 * x_s[pl.ds(CV_HALO - (CV_KERNEL - 1) + k, tm), :])

    zm, zh, w, vec = _cv_specs(tm)
    row = pl.BlockSpec((tm, CV_WIDTH), lambda i: (i, 0))
    return _pc(body, name=name, grid=(t // tm,), in_specs=[zm, zh, row, w, vec, vec, vec],
               out_specs=[row, w, vec, vec, vec],
               out_shape=[_sds((t, CV_WIDTH), F32), _sds((CV_KERNEL, CV_WIDTH), F32), _sds((1, CV_WIDTH), F32),
                          _sds((1, CV_WIDTH), F32), _sds((1, CV_WIDTH), F32)],
               scratch_shapes=[pltpu.VMEM((CV_HALO + tm, CV_WIDTH), F32)],
               compiler_params=_cp("arbitrary"))(z, z, dc3, cv_w, cv_b, ln_g, ln_b)


def mixb_bwd2(z, dc1, cv_w, *, name):
    t = z.shape[0]
    tm = _pick(t, (256,))
    nsteps = t // tm

    def body(zm_ref, dm_ref, dh_ref, w_ref, dz_ref, y_s):
        i = pl.program_id(0)
        y_s[0:tm, :] = dm_ref[...]
        y_s[tm:tm + CV_HALO, :] = jnp.where(i < nsteps - 1, dh_ref[...], 0.0)
        dc0 = jnp.zeros((tm, CV_WIDTH), F32)
        for k in range(CV_KERNEL):
            dc0 = dc0 + w_ref[k:k + 1, :] * y_s[pl.ds(CV_KERNEL - 1 - k, tm), :]
        zv = zm_ref[...].astype(F32)
        p = zv[:, :CV_WIDTH]
        s = _sigmoid(zv[:, CV_WIDTH:])
        dz_ref[:, :CV_WIDTH] = (dc0 * s).astype(dz_ref.dtype)
        dz_ref[:, CV_WIDTH:] = (dc0 * p * s * (1.0 - s)).astype(dz_ref.dtype)

    last = t // CV_HALO - 1
    return _pc(body, name=name, grid=(nsteps,),
               in_specs=[pl.BlockSpec((tm, 2 * CV_WIDTH), lambda i: (i, 1)),
                         pl.BlockSpec((tm, CV_WIDTH), lambda i: (i, 0)),
                         pl.BlockSpec((CV_HALO, CV_WIDTH), lambda i: (jnp.minimum((i + 1) * (tm // CV_HALO), last), 0)),
                         pl.BlockSpec((CV_KERNEL, CV_WIDTH), lambda i: (0, 0))],
               out_specs=pl.BlockSpec((tm, 2 * CV_WIDTH), lambda i: (i, 0)),
               out_shape=_sds((t, 2 * CV_WIDTH), BF16),
               scratch_shapes=[pltpu.VMEM((tm + CV_HALO, CV_WIDTH), F32)],
               compiler_params=_cp("parallel"))(z, dc1, dc1, cv_w)


def _group_ones():
    r = lax.broadcasted_iota(jnp.int32, (SB_WIDTH, SB_WIDTH), 0) // SB_HEAD_DIM
    c = lax.broadcasted_iota(jnp.int32, (SB_WIDTH, SB_WIDTH), 1) // SB_HEAD_DIM
    return (r == c).astype(BF16)


def attn_prep(z, gq, gk, *, name):
    t = z.shape[0]
    tm = _pick(t, (256,))
    scale = 1.0 / math.sqrt(SB_HEAD_DIM)

    def body(q_ref, k_ref, gq_ref, gk_ref, qo_ref, ko_ref):
        ones = _group_ones()
        for src, g_ref, dst, mul in ((q_ref, gq_ref, qo_ref, scale), (k_ref, gk_ref, ko_ref, 1.0)):
            v = src[...].astype(F32)
            r = lax.rsqrt(_split_dot(v * v, ones) * (1.0 / SB_HEAD_DIM) + EPS)
            dst[...] = ((v * r * g_ref[...]).astype(BF16).astype(F32) * mul).astype(dst.dtype)

    vec = pl.BlockSpec((1, SB_WIDTH), lambda i: (0, 0))
    row = pl.BlockSpec((tm, SB_WIDTH), lambda i: (i, 0))
    return _pc(body, name=name, grid=(t // tm,),
               in_specs=[pl.BlockSpec((tm, SB_WIDTH), lambda i: (i, COL_C // SB_WIDTH)),
                         pl.BlockSpec((tm, SB_WIDTH), lambda i: (i, COL_C // SB_WIDTH + 1)), vec, vec],
               out_specs=[row, row], out_shape=[_sds((t, SB_WIDTH), BF16), _sds((t, SB_WIDTH), BF16)],
               compiler_params=_cp("parallel"))(z, z, gq, gk)


_KB = Q_BLOCK
_PAIR = 2 * _KB


def _attn_tq(t):
    return _pick(t, (512, 256, 128))


def _attn_consts(tq):
    first_head = lax.broadcasted_iota(jnp.int32, (_KB, 128), 1) < SB_HEAD_DIM
    r2 = lax.broadcasted_iota(jnp.int32, (_PAIR, _PAIR), 0)
    c2 = lax.broadcasted_iota(jnp.int32, (_PAIR, _PAIR), 1)
    same = (r2 // _KB) == (c2 // _KB)
    m_suffix = (same & (r2 > c2)).astype(BF16)
    m_prefix = (same & (r2 < c2)).astype(BF16)
    row = lax.broadcasted_iota(jnp.int32, (tq, _PAIR), 0)
    col = lax.broadcasted_iota(jnp.int32, (tq, _PAIR), 1)
    return first_head, m_suffix, m_prefix, row, col & (_KB - 1), col < _KB


def _sb_logits(z, causal):
    sp = jnp.log(1.0 + jnp.exp(-jnp.abs(z)))
    g = jnp.minimum(z, 0.0) - sp
    l1m = g - z
    if causal is not None:
        l1m = jnp.where(causal, l1m, 0.0)
    return g, l1m


def _stack_heads(first_head, v):
    zero = jnp.zeros_like(v)
    return jnp.concatenate([jnp.where(first_head, v, zero), jnp.where(first_head, zero, v)], axis=0)


def _pair_sums(x):
    return jnp.sum(x[:, :_KB], axis=1, keepdims=True), jnp.sum(x[:, _KB:], axis=1, keepdims=True)


def _attn_specs(t, tq):
    qspec = pl.BlockSpec((tq, 128), lambda h, i: (i, h))
    kspec = pl.BlockSpec((t, 128), lambda h, i: (0, h))
    vspec = pl.BlockSpec((t, 128), lambda h, i: (0, (COL_C + 2 * SB_WIDTH) // 128 + h))
    return qspec, kspec, vspec


def attn_fwd(q, k, z, *, name):
    t = q.shape[0]
    tq = _attn_tq(t)
    nd = tq // _KB

    def body(q_ref, k_ref, v_ref, o_ref):
        qt = pl.program_id(1)
        first_head, m_suffix, _, row, key, is_first = _attn_consts(tq)
        qv = q_ref[...]

        def step(kb, state, causal):
            acc, ca, cb = state
            off = pl.multiple_of(kb * _KB, _KB)
            kcat = _stack_heads(first_head, k_ref[pl.ds(off, _KB), :])
            vcat = _stack_heads(first_head, v_ref[pl.ds(off, _KB), :])
            zz = lax.dot_general(qv, kcat, NT_DIMS, preferred_element_type=F32)
            g, l1m = _sb_logits(zz, causal)
            a = jnp.exp(g + _split_dot(l1m, m_suffix) + jnp.where(is_first, ca, cb))
            if causal is not None:
                a = jnp.where(causal, a, 0.0)
            sa, sb = _pair_sums(l1m)
            return acc + jnp.dot(a.astype(BF16), vcat, preferred_element_type=F32), ca + sa, cb + sb

        c0 = jnp.zeros((tq, 1), F32)
        state = (jnp.zeros((tq, 128), F32), c0, c0)
        for d in reversed(range(nd)):
            state = step(qt * nd + d, state, key + d * _KB < row)
        state = lax.fori_loop(0, qt * nd, lambda s, st: step(qt * nd - 1 - s, st, None), state)
        o_ref[...] = state[0].astype(o_ref.dtype)

    qspec, kspec, vspec = _attn_specs(t, tq)
    return _pc(body, name=name, grid=(SB_WIDTH // 128, t // tq), in_specs=[qspec, kspec, vspec], out_specs=qspec,
               out_shape=_sds((t, SB_WIDTH), BF16), compiler_params=_cp("parallel", "arbitrary"))(q, k, z)


def attn_bwd(q, k, z, do, *, name):
    t = q.shape[0]
    tq = _attn_tq(t)
    nd = tq // _KB
    nk = t // _KB

    def body(q_ref, k_ref, v_ref, do_ref, dq_ref, dk_ref, dv_ref, e_s, sg_s):
        qt = pl.program_id(1)
        first_head, m_suffix, m_prefix, row, key, is_first = _attn_consts(tq)

        @pl.when(qt == 0)
        def _():
            dk_ref[...] = jnp.zeros_like(dk_ref)
            dv_ref[...] = jnp.zeros_like(dv_ref)

        qv = q_ref[...]
        dov = do_ref[...]

        def halves(x):
            return jnp.where(first_head, x[:_KB], x[_KB:])

        def sweep1(kb, state, causal):
            ca, cb = state
            off = pl.multiple_of(kb * _KB, _KB)
            kcat = _stack_heads(first_head, k_ref[pl.ds(off, _KB), :])
            vcat = _stack_heads(first_head, v_ref[pl.ds(off, _KB), :])
            zz = lax.dot_general(qv, kcat, NT_DIMS, preferred_element_type=F32)
            g, l1m = _sb_logits(zz, causal)
            a = jnp.exp(g + _split_dot(l1m, m_suffix) + jnp.where(is_first, ca, cb))
            if causal is not None:
                a = jnp.where(causal, a, 0.0)
            da = lax.dot_general(dov, vcat, NT_DIMS, preferred_element_type=F32)
            e_s[kb] = a * da
            sg_s[kb] = jnp.exp(g).astype(BF16)
            dv_ref[pl.ds(off, _KB), :] += halves(lax.dot_general(a.astype(BF16), dov, TN_DIMS, preferred_element_type=F32))
            sa, sb = _pair_sums(l1m)
            return ca + sa, cb + sb

        def sweep2(kb, state, causal):
            dq, pa, pb = state
            off = pl.multiple_of(kb * _KB, _KB)
            kcat = _stack_heads(first_head, k_ref[pl.ds(off, _KB), :])
            e = e_s[kb]
            s = sg_s[kb].astype(F32)
            dz = e * (1.0 - s) - (jnp.where(is_first, pa, pb) + _split_dot(e, m_prefix)) * s
            if causal is not None:
                dz = jnp.where(causal, dz, 0.0)
            dz = dz.astype(BF16)
            dq = dq + jnp.dot(dz, kcat, preferred_element_type=F32)
            dk_ref[pl.ds(off, _KB), :] += halves(lax.dot_general(dz, qv, TN_DIMS, preferred_element_type=F32))
            sa, sb = _pair_sums(e)
            return dq, pa + sa, pb + sb

        c0 = jnp.zeros((tq, 1), F32)
        st1 = (c0, c0)
        for d in reversed(range(nd)):
            st1 = sweep1(qt * nd + d, st1, key + d * _KB < row)
        lax.fori_loop(0, qt * nd, lambda s, st: sweep1(qt * nd - 1 - s, st, None), st1)
        st2 = lax.fori_loop(0, qt * nd, lambda s, st: sweep2(s, st, None), (jnp.zeros((tq, 128), F32), c0, c0))
        for d in range(nd):
            st2 = sweep2(qt * nd + d, st2, key + d * _KB < row)
        dq_ref[...] = st2[0]

    qspec, kspec, vspec = _attn_specs(t, tq)
    acc = pl.BlockSpec((t, 128), lambda h, i: (0, h))
    return _pc(body, name=name, grid=(SB_WIDTH // 128, t // tq), in_specs=[qspec, kspec, vspec, qspec],
               out_specs=[qspec, acc, acc],
               out_shape=[_sds((t, SB_WIDTH), F32), _sds((t, SB_WIDTH), F32), _sds((t, SB_WIDTH), F32)],
               scratch_shapes=[pltpu.VMEM((nk, tq, _PAIR), F32), pltpu.VMEM((nk, tq, _PAIR), BF16)],
               compiler_params=_cp("parallel", "arbitrary"))(q, k, z, do)


def attn_post_bwd(z, dq, dk, dv, gq, gk, *, name):
    t = z.shape[0]
    tm = _pick(t, (256,))
    scale = 1.0 / math.sqrt(SB_HEAD_DIM)

    def body(q_ref, k_ref, dq_ref, dk_ref, dv_ref, gq_ref, gk_ref, dz_ref, dgq_ref, dgk_ref):
        ones = _group_ones()

        @pl.when(pl.program_id(0) == 0)
        def _():
            dgq_ref[...] = jnp.zeros_like(dgq_ref)
            dgk_ref[...] = jnp.zeros_like(dgk_ref)

        for idx, (src, d_ref, g_ref, dg_ref, mul) in enumerate(
                ((q_ref, dq_ref, gq_ref, dgq_ref, scale), (k_ref, dk_ref, gk_ref, dgk_ref, 1.0))):
            v = src[...].astype(F32)
            r = lax.rsqrt(_split_dot(v * v, ones) * (1.0 / SB_HEAD_DIM) + EPS)
            vh = v * r
            dn = d_ref[...] * mul
            dxh = dn * g_ref[...]
            m = _split_dot(dxh * vh, ones) * (1.0 / SB_HEAD_DIM)
            dz_ref[:, idx * SB_WIDTH:(idx + 1) * SB_WIDTH] = (r * (dxh - vh * m)).astype(dz_ref.dtype)
            s = _rowsum0(dn * vh)
            f = jnp.broadcast_to(s[:, 0:128] + s[:, 128:256] + s[:, 256:384] + s[:, 384:512], dg_ref.shape)
            dg_ref[...] += f + pltpu.roll(f, 64, 1)
        dz_ref[:, 2 * SB_WIDTH:] = dv_ref[...].astype(dz_ref.dtype)

    vec = pl.BlockSpec((1, SB_WIDTH), lambda i: (0, 0))
    row = pl.BlockSpec((tm, SB_WIDTH), lambda i: (i, 0))
    fold = pl.BlockSpec((8, 128), lambda i: (0, 0))
    return _pc(body, name=name, grid=(t // tm,),
               in_specs=[pl.BlockSpec((tm, SB_WIDTH), lambda i: (i, COL_C // SB_WIDTH)),
                         pl.BlockSpec((tm, SB_WIDTH), lambda i: (i, COL_C // SB_WIDTH + 1)), row, row, row, vec, vec],
               out_specs=[pl.BlockSpec((tm, 3 * SB_WIDTH), lambda i: (i, 0)), fold, fold],
               out_shape=[_sds((t, 3 * SB_WIDTH), BF16), _sds((8, 128), F32), _sds((8, 128), F32)],
               compiler_params=_cp("arbitrary"))(z, z, dq, dk, dv, gq, gk)


_GW = 512


def merge_fwd(z, ya, yb, yc, b_gate, *, name):
    t = z.shape[0]
    tm = _pick(t, (512, 256))

    def body(za_ref, zb_ref, zc_ref, ya_ref, yb_ref, yc_ref, bg_ref, o_ref):
        acc = jnp.zeros((tm, _GW), F32)
        for b, (zr, yr) in enumerate(((za_ref, ya_ref), (zb_ref, yb_ref), (zc_ref, yc_ref))):
            acc = acc + _sigmoid(zr[...].astype(F32) + bg_ref[b:b + 1, :]) * yr[...].astype(F32)
        o_ref[...] = acc.astype(o_ref.dtype)

    def zspec(b):
        return pl.BlockSpec((tm, _GW), lambda i, j: (i, COL_G // _GW + 2 * b + j))

    yspec = pl.BlockSpec((tm, _GW), lambda i, j: (i, j))
    return _pc(body, name=name, grid=(t // tm, D_MODEL // _GW),
               in_specs=[zspec(0), zspec(1), zspec(2), yspec, yspec, yspec, pl.BlockSpec((3, _GW), lambda i, j: (0, j))],
               out_specs=yspec, out_shape=_sds((t, D_MODEL), BF16),
               compiler_params=_cp("parallel", "parallel"))(z, z, z, ya, yb, yc, b_gate)


def merge_bwd(z, ya, yb, yc, b_gate, dm, *, name):
    t = z.shape[0]
    tm = _pick(t, (512, 256))

    def body(za_ref, zb_ref, zc_ref, ya_ref, yb_ref, yc_ref, bg_ref, dm_ref,
             dya_ref, dyb_ref, dyc_ref, dza_ref, dzb_ref, dzc_ref, dbg_ref):
        @pl.when(pl.program_id(1) == 0)
        def _():
            dbg_ref[...] = jnp.zeros_like(dbg_ref)

        dmv = dm_ref[...].astype(F32)
        for b, (zr, yr, dyr, dzr) in enumerate(((za_ref, ya_ref, dya_ref, dza_ref), (zb_ref, yb_ref, dyb_ref, dzb_ref),
                                                (zc_ref, yc_ref, dyc_ref, dzc_ref))):
            s = _sigmoid(zr[...].astype(F32) + bg_ref[b:b + 1, :])
            dyr[...] = (dmv * s).astype(dyr.dtype)
            dg = dmv * yr[...].astype(F32) * s * (1.0 - s)
            dzr[...] = dg.astype(dzr.dtype)
            dbg_ref[b:b + 1, :] += _rowsum0(dg)

    def zspec(b):
        return pl.BlockSpec((tm, _GW), lambda j, i: (i, COL_G // _GW + 2 * b + j))

    yspec = pl.BlockSpec((tm, _GW), lambda j, i: (i, j))
    bspec = pl.BlockSpec((3, _GW), lambda j, i: (0, j))
    full = _sds((t, D_MODEL), BF16)
    return _pc(body, name=name, grid=(D_MODEL // _GW, t // tm),
               in_specs=[zspec(0), zspec(1), zspec(2), yspec, yspec, yspec, bspec, yspec],
               out_specs=[yspec] * 6 + [bspec], out_shape=[full] * 6 + [_sds((3, D_MODEL), F32)],
               compiler_params=_cp("parallel", "arbitrary"))(z, z, z, ya, yb, yc, b_gate, dm)


_FW = 1408
_FH = D_FF // _FW


def _ffn_fill(m_ref, h_ref, x_s, i, tm):
    x_s[0:FFN_HALO, :] = jnp.where(i > 0, h_ref[...].astype(F32), 0.0)
    x_s[FFN_HALO:FFN_HALO + tm, :] = m_ref[...].astype(F32)


def _ffn_conv(x_s, w_ref, b_ref, tm):
    acc = jnp.zeros((tm, _FW), F32) + b_ref[...]
    for k in range(FFN_KERNEL):
        acc = acc + w_ref[k:k + 1, :] * x_s[pl.ds(FFN_HALO - (FFN_KERNEL - 1) + k, tm), :]
    return acc


def ffn_mid_fwd(up, cw, cb, *, name):
    t = up.shape[0]
    tm = _pick(t, (256,))

    def body(gm_ref, gh_ref, vm_ref, vh_ref, wg_ref, wv_ref, bg_ref, bv_ref, o_ref, xg_s, xv_s):
        i = pl.program_id(0)
        _ffn_fill(gm_ref, gh_ref, xg_s, i, tm)
        _ffn_fill(vm_ref, vh_ref, xv_s, i, tm)
        o_ref[...] = (_silu(_ffn_conv(xg_s, wg_ref, bg_ref, tm)) * _ffn_conv(xv_s, wv_ref, bv_ref, tm)).astype(o_ref.dtype)

    def main(off):
        return pl.BlockSpec((tm, _FW), lambda i, j: (i, j + off))

    def halo(off):
        return pl.BlockSpec((FFN_HALO, _FW), lambda i, j: (jnp.maximum(i * (tm // FFN_HALO) - 1, 0), j + off))

    def wspec(off):
        return pl.BlockSpec((FFN_KERNEL, _FW), lambda i, j: (0, j + off))

    def bspec(off):
        return pl.BlockSpec((1, _FW), lambda i, j: (0, j + off))

    return _pc(body, name=name, grid=(t // tm, _FH),
               in_specs=[main(0), halo(0), main(_FH), halo(_FH), wspec(0), wspec(_FH), bspec(0), bspec(_FH)],
               out_specs=pl.BlockSpec((tm, _FW), lambda i, j: (i, j)), out_shape=_sds((t, D_FF), BF16),
               scratch_shapes=[pltpu.VMEM((FFN_HALO + tm, _FW), F32), pltpu.VMEM((FFN_HALO + tm, _FW), F32)],
               compiler_params=_cp("parallel", "parallel"))(up, up, up, up, cw, cw, cb, cb)


def ffn_mid_bwd1(up, dact, cw, cb, *, name):
    t = up.shape[0]
    tm = _pick(t, (256,))

    def body(gm_ref, gh_ref, vm_ref, vh_ref, da_ref, wg_ref, wv_ref, bg_ref, bv_ref, d_ref, dw_ref, db_ref, xg_s, xv_s):
        j = pl.program_id(0)
        i = pl.program_id(1)

        @pl.when(i == 0)
        def _():
            dw_ref[...] = jnp.zeros_like(dw_ref)
            db_ref[...] = jnp.zeros_like(db_ref)

        _ffn_fill(gm_ref, gh_ref, xg_s, i, tm)
        _ffn_fill(vm_ref, vh_ref, xv_s, i, tm)
        gate = _ffn_conv(xg_s, wg_ref, bg_ref, tm)
        da = da_ref[...].astype(F32)

        def finish(d, x_s):
            d_ref[...] = d.astype(d_ref.dtype)
            db_ref[...] += _rowsum0(d)
            for k in range(FFN_KERNEL):
                dw_ref[k:k + 1, :] += _rowsum0(d * x_s[pl.ds(FFN_HALO - (FFN_KERNEL - 1) + k, tm), :])

        @pl.when(j < _FH)
        def _():
            finish(da * _ffn_conv(xv_s, wv_ref, bv_ref, tm) * _silu_grad(gate), xg_s)

        @pl.when(j >= _FH)
        def _():
            finish(da * _silu(gate), xv_s)

    def main(off):
        return pl.BlockSpec((tm, _FW), lambda j, i: (i, j % _FH + off))

    def halo(off):
        return pl.BlockSpec((FFN_HALO, _FW), lambda j, i: (jnp.maximum(i * (tm // FFN_HALO) - 1, 0), j % _FH + off))

    def wspec(off):
        return pl.BlockSpec((FFN_KERNEL, _FW), lambda j, i: (0, j % _FH + off))

    def bspec(off):
        return pl.BlockSpec((1, _FW), lambda j, i: (0, j % _FH + off))

    return _pc(body, name=name, grid=(2 * _FH, t // tm),
               in_specs=[main(0), halo(0), main(_FH), halo(_FH), pl.BlockSpec((tm, _FW), lambda j, i: (i, j % _FH)),
                         wspec(0), wspec(_FH), bspec(0), bspec(_FH)],
               out_specs=[pl.BlockSpec((tm, _FW), lambda j, i: (i, j)), pl.BlockSpec((FFN_KERNEL, _FW), lambda j, i: (0, j)),
                          pl.BlockSpec((1, _FW), lambda j, i: (0, j))],
               out_shape=[_sds((t, 2 * D_FF), BF16), _sds((FFN_KERNEL, 2 * D_FF), F32), _sds((1, 2 * D_FF), F32)],
               scratch_shapes=[pltpu.VMEM((FFN_HALO + tm, _FW), F32), pltpu.VMEM((FFN_HALO + tm, _FW), F32)],
               compiler_params=_cp("parallel", "arbitrary"))(up, up, up, up, dact, cw, cw, cb, cb)


def ffn_mid_bwd2(dupc, cw, *, name):
    t = dupc.shape[0]
    tm = _pick(t, (256,))
    nsteps = t // tm
    last = t // FFN_HALO - 1

    def body(m_ref, h_ref, w_ref, o_ref, y_s):
        i = pl.program_id(0)
        y_s[0:tm, :] = m_ref[...].astype(F32)
        y_s[tm:tm + FFN_HALO, :] = jnp.where(i < nsteps - 1, h_ref[...].astype(F32), 0.0)
        acc = jnp.zeros((tm, _FW), F32)
        for k in range(FFN_KERNEL):
            acc = acc + w_ref[k:k + 1, :] * y_s[pl.ds(FFN_KERNEL - 1 - k, tm), :]
        o_ref[...] = acc.astype(o_ref.dtype)

    return _pc(body, name=name, grid=(nsteps, 2 * _FH),
               in_specs=[pl.BlockSpec((tm, _FW), lambda i, j: (i, j)),
                         pl.BlockSpec((FFN_HALO, _FW), lambda i, j: (jnp.minimum((i + 1) * (tm // FFN_HALO), last), j)),
                         pl.BlockSpec((FFN_KERNEL, _FW), lambda i, j: (0, j))],
               out_specs=pl.BlockSpec((tm, _FW), lambda i, j: (i, j)), out_shape=_sds((t, 2 * D_FF), BF16),
               scratch_shapes=[pltpu.VMEM((tm + FFN_HALO, _FW), F32)],
               compiler_params=_cp("parallel", "parallel"))(dupc, dupc, cw)


def _vec(v):
    return v.reshape(1, -1)


def _layer_consts(p):
    return dict(
        sg_bias=jnp.repeat(p['sg_b'].T, SB_HEAD_DIM, axis=1),
        sg_wt=jnp.swapaxes(p['sg_w'], 1, 2),
        gq=jnp.tile(p['q_norm_g'], SB_WIDTH // SB_HEAD_DIM).reshape(1, -1),
        gk=jnp.tile(p['k_norm_g'], SB_WIDTH // SB_HEAD_DIM).reshape(1, -1),
    )


def layer_fwd(x, p, after=()):
    c = _layer_consts(p)
    z, h = mm_norm_nn(x, _vec(p['ln1_g']), p['w_in'], name="in_proj", after=after)
    ga = mixa_fwd(z, _vec(p['sg_ln_g']), _vec(p['sg_ln_b']), p['sg_w'], c['sg_bias'], name="mixa_fwd")
    cb = mixb_fwd(z, p['cv_w'], _vec(p['cv_b']), _vec(p['cv_ln_g']), _vec(p['cv_ln_b']), name="mixb_fwd")
    q, k = attn_prep(z, c['gq'], c['gk'], name="attn_prep")
    ao = attn_fwd(q, k, z, name="attn_fwd")
    ya = mm_nn(ga, p['w_a_out'], name="a_out")
    yb = mm_nn(cb, p['w_b_out'], name="b_out")
    yc = mm_nn(ao, p['w_c_out'], name="c_out")
    merged = merge_fwd(z, ya, yb, yc, p['b_gate'], name="merge_fwd")
    x1 = mm_nn(merged, p['w_out'], res=x, out_dtype=F32, name="out_proj")
    up, h2 = mm_norm_nn(x1, _vec(p['ln2_g']), p['w_up'], name="up_proj")
    act = ffn_mid_fwd(up, p['ffn_conv_w'], _vec(p['ffn_conv_b']), name="ffn_mid_fwd")
    x2 = mm_nn(act, p['w_down'], res=x1, out_dtype=F32, name="down_proj")
    saved = dict(x=x, z=z, h=h, ga=ga, cb=cb, q=q, k=k, ao=ao, ya=ya, yb=yb, yc=yc, merged=merged, x1=x1, up=up,
                 h2=h2, act=act)
    return x2, saved


def layer_bwd(dx2, p, s, after=()):
    c = _layer_consts(p)
    g = {}
    g['w_down'] = mm_tn(s['act'], dx2, name="d_w_down", after=after)
    dact = mm_nt(dx2, p['w_down'], out_dtype=BF16, name="d_act")
    dupc, g['ffn_conv_w'], dcb = ffn_mid_bwd1(s['up'], dact, p['ffn_conv_w'], _vec(p['ffn_conv_b']), name="ffn_mid_bwd1")
    g['ffn_conv_b'] = dcb.reshape(-1)
    dup = ffn_mid_bwd2(dupc, p['ffn_conv_w'], name="ffn_mid_bwd2")
    g['w_up'] = mm_tn(s['h2'], dup, name="d_w_up")
    dh2 = mm_nt(dup, p['w_up'], out_dtype=F32, name="d_h2")
    dx1, dg2 = rms_bwd(dh2, s['x1'], _vec(p['ln2_g']), dx2, name="ln2_bwd")
    g['ln2_g'] = dg2.reshape(-1)
    g['w_out'] = mm_tn(s['merged'], dx1, name="d_w_out")
    dm = mm_nt(dx1, p['w_out'], out_dtype=BF16, name="d_merged")
    dya, dyb, dyc, dzg0, dzg1, dzg2, g['b_gate'] = merge_bwd(s['z'], s['ya'], s['yb'], s['yc'], p['b_gate'], dm,
                                                             name="merge_bwd")
    g['w_a_out'] = mm_tn(s['ga'], dya, name="d_w_a_out")
    g['w_b_out'] = mm_tn(s['cb'], dyb, name="d_w_b_out")
    g['w_c_out'] = mm_tn(s['ao'], dyc, name="d_w_c_out")
    dga = mm_nt(dya, p['w_a_out'], out_dtype=BF16, name="d_ga")
    dcb3 = mm_nt(dyb, p['w_b_out'], out_dtype=BF16, name="d_cb")
    dao = mm_nt(dyc, p['w_c_out'], out_dtype=BF16, name="d_ao")
    dza, g['sg_w'], dsgb, dlg, dlb = mixa_bwd(s['z'], dga, _vec(p['sg_ln_g']), _vec(p['sg_ln_b']), p['sg_w'],
                                               c['sg_wt'], c['sg_bias'], name="mixa_bwd")
    g['sg_b'] = dsgb[:, :SG_WIDTH // SB_HEAD_DIM].T
    g['sg_ln_g'] = dlg.reshape(-1)
    g['sg_ln_b'] = dlb.reshape(-1)
    dc1, g['cv_w'], dcvb, dcg, dcbb = mixb_bwd1(s['z'], dcb3, p['cv_w'], _vec(p['cv_b']), _vec(p['cv_ln_g']),
                                                _vec(p['cv_ln_b']), name="mixb_bwd1")
    g['cv_b'] = dcvb.reshape(-1)
    g['cv_ln_g'] = dcg.reshape(-1)
    g['cv_ln_b'] = dcbb.reshape(-1)
    dzb = mixb_bwd2(s['z'], dc1, p['cv_w'], name="mixb_bwd2")
    dq, dk, dv = attn_bwd(s['q'], s['k'], s['z'], dao, name="attn_bwd")
    dzc, dgq, dgk = attn_post_bwd(s['z'], dq, dk, dv, c['gq'], c['gk'], name="attn_post_bwd")
    g['q_norm_g'] = dgq[0, :SB_HEAD_DIM]
    g['k_norm_g'] = dgk[0, :SB_HEAD_DIM]
    dz = jnp.concatenate([dza, dzb, dzc, dzg0, dzg1, dzg2], axis=1)
    g['w_in'] = mm_tn(s['h'], dz, name="d_w_in")
    dh = mm_nt(dz, p['w_in'], out_dtype=F32, name="d_h")
    dx, dg1 = rms_bwd(dh, s['x'], _vec(p['ln1_g']), dx1, name="ln1_bwd")
    g['ln1_g'] = dg1.reshape(-1)
    return dx, g


def local_step(x, target, depth, get_layer, on_grads):
    saved, layers = [], []
    for l in range(depth):
        p, after = get_layer(l, x)
        x, s = layer_fwd(x, p, after)
        layers.append(p)
        saved.append(s)
    loss, dx = loss_head(x, target, name="loss_head")
    after = ()
    for l in reversed(range(depth)):
        dx, g = layer_bwd(dx, layers[l], saved[l], after)
        after = on_grads(l, g)
    return loss[0, 0], dx


def adamw(w, g, m, v, *, name):
    r, c = w.shape
    tr = _pick(r, (256, 704)) if r * c > 512 * 1024 else r

    def body(w_ref, g_ref, m_ref, v_ref, d_ref, mo_ref, vo_ref):
        gv = g_ref[...]
        mn = ADAM_B1 * m_ref[...] + (1.0 - ADAM_B1) * gv
        vn = ADAM_B2 * v_ref[...] + (1.0 - ADAM_B2) * (gv * gv)
        m_hat = mn / (1.0 - ADAM_B1 ** ADAM_STEP)
        v_hat = vn / (1.0 - ADAM_B2 ** ADAM_STEP)
        d_ref[...] = -ADAM_LR * (m_hat / (jnp.sqrt(v_hat) + ADAM_EPS) + ADAM_WD * w_ref[...])
        mo_ref[...] = mn
        vo_ref[...] = vn

    spec = pl.BlockSpec((tr, c), lambda i: (i, 0))
    out = _sds((r, c), F32)
    return _pc(body, name=name, grid=(r // tr,), in_specs=[spec] * 4, out_specs=[spec] * 3, out_shape=[out] * 3,
               compiler_params=_cp("parallel"))(w, g, m, v)


def _as3(a):
    return a if a.ndim == 3 else a.reshape((1,) + a.shape)


def add_half(g, recv, c_idx, *, name):
    s, rh, w = recv.shape
    tr = _pick(rh, (256, 352, 128))
    nb = rh // tr

    def body(c_ref, g_ref, r_ref, o_ref):
        o_ref[...] = (g_ref[...].astype(F32) + r_ref[...].astype(F32)).astype(o_ref.dtype)

    own = pl.BlockSpec((1, tr, w), lambda k, i, c_ref: (k, c_ref[0] * nb + i, 0))
    half = pl.BlockSpec((1, tr, w), lambda k, i, c_ref: (k, i, 0))
    gs = pltpu.PrefetchScalarGridSpec(num_scalar_prefetch=1, grid=(s, nb), in_specs=[own, half], out_specs=half)
    return _pc(body, name=name, grid_spec=gs, out_shape=_sds((s, rh, w), BF16),
               compiler_params=_cp("parallel", "parallel"))(c_idx, g, recv)


def sum_shard(p, recv, pos_idx, *, by_rows, name):
    _, rh, w = recv.shape
    tr = _pick(rh, (256, 352, 128))
    nb = rh // tr

    def body(pos_ref, p_ref, r_ref, o_ref):
        acc = p_ref[0].astype(F32)
        for j in range(N_CHIPS - 1):
            acc = acc + r_ref[j].astype(F32)
        o_ref[...] = acc

    if by_rows:
        own = pl.BlockSpec((1, tr, w), lambda i, pos_ref: (pos_ref[0], i, 0))
    else:
        own = pl.BlockSpec((1, tr, w), lambda i, pos_ref: (0, i, pos_ref[0]))
    gs = pltpu.PrefetchScalarGridSpec(num_scalar_prefetch=1, grid=(nb,),
                                      in_specs=[own, pl.BlockSpec((N_CHIPS - 1, tr, w), lambda i, pos_ref: (0, i, 0))],
                                      out_specs=pl.BlockSpec((tr, w), lambda i, pos_ref: (pos_ref[1] * nb + i, 0)))
    return _pc(body, name=name, grid_spec=gs, out_shape=_sds((2 * rh, w), F32),
               compiler_params=_cp("parallel"))(pos_idx, p, recv)


def sum_slots(slab, *, name):
    _, r, w = slab.shape
    tr = _pick(r, (512, 256, 8))

    def body(s_ref, o_ref):
        acc = s_ref[0]
        for j in range(1, N_DEV):
            acc = acc + s_ref[j]
        o_ref[...] = acc

    return _pc(body, name=name, grid=(r // tr,), in_specs=[pl.BlockSpec((N_DEV, tr, w), lambda i: (0, i, 0))],
               out_specs=pl.BlockSpec((tr, w), lambda i: (i, 0)), out_shape=_sds((r, w), F32),
               compiler_params=_cp("parallel"))(slab)


def _mesh_pos():
    x, y, c = lax.axis_index("x"), lax.axis_index("y"), lax.axis_index("c")
    others = [(1 - x, y), (x, 1 - y), (1 - x, 1 - y)]
    return x, y, c, others


def _rcopy(src, dst, ssem, rsem, k, dev):
    return pltpu.make_async_remote_copy(src_ref=src, dst_ref=dst, send_sem=ssem.at[k], recv_sem=rsem.at[k],
                                        device_id=dev, device_id_type=MESH)


def _comm_call(body, name, n_in, out_shape, n_local, n_remote):
    scratch = [pltpu.SemaphoreType.DMA((max(n_local, 1),)), pltpu.SemaphoreType.DMA((n_remote,)),
               pltpu.SemaphoreType.DMA((n_remote,))]
    return _pc(body, name=name, in_specs=[ANY] * n_in, out_specs=[ANY] * len(out_shape), out_shape=out_shape,
               scratch_shapes=scratch)


GATHERED = BIG + SMALL_COL
HBM_SPEC = pl.BlockSpec(memory_space=pltpu.HBM)
SEM_SPEC = pl.BlockSpec(memory_space=pltpu.SEMAPHORE)
TOKEN_SHAPE = (8, 128)


def place_block(w, layer, pos_idx, *, by_rows, dtype, name):
    _, r, c = w.shape
    tr = _pick(r, (512, 704, 256))

    def body(pos_ref, w_ref, o_ref):
        if by_rows:
            o_ref[0] = w_ref[0].astype(dtype)
        else:
            o_ref[...] = w_ref[0].astype(dtype)

    if by_rows:
        out_spec, shape = pl.BlockSpec((1, tr, c), lambda i, pos_ref: (pos_ref[0], i, 0)), (N_CHIPS, r, c)
    else:
        out_spec, shape = pl.BlockSpec((tr, c), lambda i, pos_ref: (i, pos_ref[0])), (r, N_CHIPS * c)
    gs = pltpu.PrefetchScalarGridSpec(num_scalar_prefetch=1, grid=(r // tr,),
                                      in_specs=[pl.BlockSpec((1, tr, c), lambda i, pos_ref: (layer, i, 0))],
                                      out_specs=out_spec)
    return _pc(body, name=name, grid_spec=gs, out_shape=_sds(shape, dtype), compiler_params=_cp("parallel"))(pos_idx, w)


def _gather_windows(bufs):
    def dwin(refs, i, k, h):
        if GATHERED[i] in BIG_ROW:
            _, r, _ = bufs[i].shape
            return refs[i].at[k] if h is None else refs[i].at[k, pl.ds(h * (r // 2), r // 2), :]
        r, cs = bufs[i].shape[0], bufs[i].shape[1] // N_CHIPS
        cols = pl.ds(pl.multiple_of(k * cs, 128), cs)
        return refs[i].at[:, cols] if h is None else refs[i].at[pl.ds(h * (r // 2), r // 2), cols]

    def swin(refs, i, h):
        x, y, _, _ = _mesh_pos()
        return dwin(refs, i, 2 * x + y, h)

    return dwin, swin


def _gather_send(ins, outs, ssem, rsem, dwin, swin, stride):
    x, y, c, others = _mesh_pos()
    sends = []
    for i, n in enumerate(GATHERED):
        h = c if n in BIG else None
        for j, chip in enumerate(others):
            cp = _rcopy(swin(ins, i, h), swin(outs, i, h), ssem, rsem, stride * i + j, (*chip, c))
            cp.start()
            sends.append(cp)
    return sends


def _gather_pass_on(outs, ssem, rsem, dwin, stride, first_off, pass_off):
    x, y, c, others = _mesh_pos()
    sib = (x, y, 1 - c)
    sends = []
    for j, chip in enumerate(others):
        kk = 2 * chip[0] + chip[1]
        for i, n in enumerate(GATHERED):
            got = dwin(outs, i, kk, c if n in BIG else None)
            if first_off is not None:
                _rcopy(got, got, ssem, rsem, stride * i + first_off + j, (*chip, c)).wait_recv()
            if n in BIG:
                fwd = _rcopy(got, got, ssem, rsem, stride * i + pass_off + j, sib)
                fwd.start()
                sends.append(fwd)
    for j, chip in enumerate(others):
        kk = 2 * chip[0] + chip[1]
        for i, n in enumerate(GATHERED):
            if n in BIG:
                got = dwin(outs, i, kk, 1 - c)
                _rcopy(got, got, ssem, rsem, stride * i + pass_off + j, sib).wait_recv()
    return sends


def _name_full(outs):
    return {n: (o.reshape(o.shape[0] * o.shape[1], o.shape[2]) if n in BIG_ROW else o) for n, o in zip(GATHERED, outs)}


def _comm_in_place(body, name, bufs, n_sems):
    nn = len(bufs)
    scratch = [pltpu.SemaphoreType.DMA((n_sems,)), pltpu.SemaphoreType.DMA((n_sems,))]
    return _pc(body, name=name, in_specs=[ANY] * nn, out_specs=[ANY] * nn, out_shape=[_sds(a.shape, a.dtype) for a in bufs],
               scratch_shapes=scratch, input_output_aliases={i: i for i in range(nn)})(*bufs)


def allgather_weights(bufs, *, name):
    nn = len(GATHERED)
    dwin, swin = _gather_windows(bufs)

    def body(*refs):
        ins, outs = refs[:nn], refs[nn:2 * nn]
        ssem, rsem = refs[2 * nn:]
        sends = _gather_send(ins, outs, ssem, rsem, dwin, swin, 6)
        sends += _gather_pass_on(outs, ssem, rsem, dwin, 6, 0, 3)
        for cp in sends:
            cp.wait_send()

    return _comm_in_place(body, name, bufs, 6 * nn)


def gather_start(bufs, after, *, name):
    nn = len(GATHERED)
    dwin, swin = _gather_windows(bufs)

    def body(*refs):
        ins = refs[:nn]
        ssem, rsem = refs[nn + len(after)], refs[nn + len(after) + 1]
        token = refs[-1]
        _gather_send(ins, ins, ssem, rsem, dwin, swin, 3)
        token[...] = jnp.zeros_like(token)

    hbm = lambda a: pltpu.with_memory_space_constraint(a, pltpu.HBM)
    out_shape = ([pltpu.SemaphoreType.DMA((3 * nn,)), pltpu.SemaphoreType.DMA((3 * nn,))]
                 + [pltpu.HBM(a.shape, a.dtype) for a in bufs] + [_sds(TOKEN_SHAPE, F32)])
    outs = _pc(body, name=name, in_specs=[HBM_SPEC] * nn + [ANY] * len(after),
               out_specs=[SEM_SPEC, SEM_SPEC] + [HBM_SPEC] * nn + [pl.BlockSpec(memory_space=pltpu.VMEM)],
               out_shape=out_shape, input_output_aliases={i: 2 + i for i in range(nn)},
               compiler_params=pltpu.CompilerParams(has_side_effects=pltpu.SideEffectType.DATAFLOW_SIDE_EFFECTING),
               )(*[hbm(a) for a in bufs], *after)
    return dict(ssem=outs[0], rsem=outs[1], bufs=outs[2:2 + nn], token=outs[-1])


def gather_wait(handle, after, *, name):
    nn = len(GATHERED)
    bufs = handle['bufs']
    dwin, swin = _gather_windows(bufs)

    def body(*refs):
        ins = refs[:nn]
        ssem, rsem = refs[nn], refs[nn + 1]
        x, y, c, others = _mesh_pos()
        for i, n in enumerate(GATHERED):
            h = c if n in BIG else None
            for j, chip in enumerate(others):
                kk = 2 * chip[0] + chip[1]
                cp = _rcopy(swin(ins, i, h), dwin(ins, i, kk, h), ssem, rsem, 3 * i + j, (*chip, c))
                cp.wait_send()
                cp.wait_recv()

    return _pc(body, name=name, in_specs=[HBM_SPEC] * nn + [SEM_SPEC, SEM_SPEC] + [ANY] * len(after),
               out_specs=[HBM_SPEC] * nn, out_shape=[pltpu.HBM(a.shape, a.dtype) for a in bufs],
               input_output_aliases={i: i for i in range(nn)},
               compiler_params=pltpu.CompilerParams(has_side_effects=pltpu.SideEffectType.DATAFLOW_SIDE_EFFECTING),
               )(*bufs, handle['ssem'], handle['rsem'], *after)


def gather_finish(bufs, *, name):
    nn = len(GATHERED)
    dwin, _ = _gather_windows(bufs)

    def body(*refs):
        outs = refs[nn:2 * nn]
        ssem, rsem = refs[2 * nn:]
        for cp in _gather_pass_on(outs, ssem, rsem, dwin, 3, None, 0):
            cp.wait_send()

    return _comm_in_place(body, name, bufs, 3 * nn)


def _grad_view(n, g):
    return g.reshape(N_CHIPS, g.shape[0] // N_CHIPS, g.shape[1]) if n in BIG_ROW else g.reshape((1,) + g.shape)


def exchange_halves(gv, *, name):
    nn = len(BIG)

    def body(*refs):
        ins, outs = refs[:nn], refs[nn:2 * nn]
        _, ssem, rsem = refs[2 * nn:]
        x, y, c, _ = _mesh_pos()
        cps = []
        for i in range(nn):
            rh = gv[i].shape[1] // 2
            cp = _rcopy(ins[i].at[:, pl.ds((1 - c) * rh, rh), :], outs[i], ssem, rsem, i, (x, y, 1 - c))
            cp.start()
            cps.append(cp)
        for cp in cps:
            cp.wait()

    out_shape = [_sds((a.shape[0], a.shape[1] // 2, a.shape[2]), a.dtype) for a in gv]
    return _comm_call(body, name, nn, out_shape, 0, nn)(*gv)


def _shard_shape(n, p):
    _, rh, w = p.shape
    return (rh, w) if n in BIG_ROW else (rh, w // N_CHIPS)


def _scatter_copies(pv, ins, outs, ssem, rsem):
    x, y, c, others = _mesh_pos()
    cps = []
    for i, n in enumerate(BIG):
        _, ws = _shard_shape(n, pv[i])
        for j, chip in enumerate(others):
            kk = 2 * chip[0] + chip[1]
            if n in BIG_ROW:
                src = ins[i].at[kk]
            else:
                src = ins[i].at[0, :, pl.ds(pl.multiple_of(kk * ws, 128), ws)]
            cps.append(_rcopy(src, outs[i].at[j], ssem, rsem, 3 * i + j, (*chip, c)))
    return cps


def _recv_shapes(pv):
    return [(N_CHIPS - 1,) + _shard_shape(n, p) for n, p in zip(BIG, pv)]


def scatter_partials(pv, *, name):
    nn = len(BIG)

    def body(*refs):
        ins, outs = refs[:nn], refs[nn:2 * nn]
        _, ssem, rsem = refs[2 * nn:]
        cps = _scatter_copies(pv, ins, outs, ssem, rsem)
        for cp in cps:
            cp.start()
        for cp in cps:
            cp.wait()

    out_shape = [_sds(s, p.dtype) for s, p in zip(_recv_shapes(pv), pv)]
    return pv, _comm_call(body, name, nn, out_shape, 0, 3 * nn)(*pv)


def scatter_start(pv, *, name):
    nn = len(BIG)

    def body(*refs):
        ins, lands = refs[:nn], refs[nn:2 * nn]
        ssem, rsem = refs[2 * nn], refs[2 * nn + 1]
        token = refs[-1]
        for cp in _scatter_copies(pv, ins, lands, ssem, rsem):
            cp.start()
        token[...] = jnp.zeros_like(token)

    hbm = lambda a: pltpu.with_memory_space_constraint(a, pltpu.HBM)
    lands = [hbm(lax.empty(s, p.dtype)) for s, p in zip(_recv_shapes(pv), pv)]
    out_shape = ([pltpu.SemaphoreType.DMA((3 * nn,)), pltpu.SemaphoreType.DMA((3 * nn,))]
                 + [pltpu.HBM(p.shape, p.dtype) for p in pv] + [pltpu.HBM(a.shape, a.dtype) for a in lands]
                 + [_sds(TOKEN_SHAPE, F32)])
    outs = _pc(body, name=name, in_specs=[HBM_SPEC] * (2 * nn),
               out_specs=[SEM_SPEC, SEM_SPEC] + [HBM_SPEC] * (2 * nn) + [pl.BlockSpec(memory_space=pltpu.VMEM)],
               out_shape=out_shape, input_output_aliases={i: 2 + i for i in range(2 * nn)},
               compiler_params=pltpu.CompilerParams(has_side_effects=pltpu.SideEffectType.DATAFLOW_SIDE_EFFECTING),
               )(*[hbm(p) for p in pv], *lands)
    return dict(ssem=outs[0], rsem=outs[1], pv=outs[2:2 + nn], lands=outs[2 + nn:2 + 2 * nn], token=outs[-1])


def scatter_wait(handle, after, *, name):
    nn = len(BIG)
    pv, lands = handle['pv'], handle['lands']

    def body(*refs):
        ins, zones = refs[:nn], refs[nn:2 * nn]
        ssem, rsem = refs[2 * nn], refs[2 * nn + 1]
        for cp in _scatter_copies(pv, ins, zones, ssem, rsem):
            cp.wait_send()
            cp.wait_recv()

    outs = _pc(body, name=name, in_specs=[HBM_SPEC] * (2 * nn) + [SEM_SPEC, SEM_SPEC] + [ANY] * len(after),
               out_specs=[HBM_SPEC] * (2 * nn),
               out_shape=[pltpu.HBM(p.shape, p.dtype) for p in pv] + [pltpu.HBM(a.shape, a.dtype) for a in lands],
               input_output_aliases={i: i for i in range(2 * nn)},
               compiler_params=pltpu.CompilerParams(has_side_effects=pltpu.SideEffectType.DATAFLOW_SIDE_EFFECTING),
               )(*pv, *lands, handle['ssem'], handle['rsem'], *after)
    return outs[:nn], outs[nn:]


def join_halves(rv, *, name):
    nn = len(BIG)

    def body(*refs):
        ins, outs = refs[:nn], refs[nn:2 * nn]
        _, ssem, rsem = refs[2 * nn:]
        x, y, c, _ = _mesh_pos()
        cps = []
        for i in range(nn):
            rh = rv[i].shape[0] // 2
            rows = pl.ds(c * rh, rh)
            cp = _rcopy(ins[i].at[rows, :], outs[i].at[rows, :], ssem, rsem, i, (x, y, 1 - c))
            cp.start()
            cps.append(cp)
        for i, cp in enumerate(cps):
            cp.wait_send()
            rh = rv[i].shape[0] // 2
            got = outs[i].at[pl.ds((1 - c) * rh, rh), :]
            _rcopy(got, got, ssem, rsem, i, (x, y, 1 - c)).wait_recv()

    out_shape = [_sds(a.shape, a.dtype) for a in rv]
    scratch = [pltpu.SemaphoreType.DMA((1,)), pltpu.SemaphoreType.DMA((nn,)), pltpu.SemaphoreType.DMA((nn,))]
    return _pc(body, name=name, in_specs=[ANY] * nn, out_specs=[ANY] * nn, out_shape=out_shape, scratch_shapes=scratch,
               input_output_aliases={i: i for i in range(nn)})(*rv)


def chip_partials(grads, c_idx):
    gv = [_grad_view(n, grads[n]) for n in BIG]
    recv = exchange_halves(gv, name="rs_exchange_halves")
    return [add_half(g, r, c_idx, name="rs_add_" + n) for n, g, r in zip(BIG, gv, recv)]


def reduce_shards(pv, got, pos_idx):
    rv = [sum_shard(p, r, pos_idx, by_rows=n in BIG_ROW, name="rs_sum_" + n) for n, p, r in zip(BIG, pv, got)]
    return dict(zip(BIG, join_halves(rv, name="rs_join_halves")))


def allgather_slab(slab, *, name):
    def body(in_ref, out_ref, ssem, rsem):
        x, y, c, others = _mesh_pos()
        me, sib = (x, y, c), (x, y, 1 - c)

        def slot(px, py, pc):
            return out_ref.at[4 * px + 2 * py + pc]

        x_ref = in_ref.at[4 * x + 2 * y + c]
        first = [_rcopy(x_ref, slot(*me), ssem, rsem, 0, sib)]
        first += [_rcopy(x_ref, slot(*me), ssem, rsem, 1 + j, (*chip, c)) for j, chip in enumerate(others)]
        for cp in first:
            cp.start()
        passed = [_rcopy(slot(*chip, c), slot(*chip, c), ssem, rsem, 4 + j, sib) for j, chip in enumerate(others)]
        for j, chip in enumerate(others):
            _rcopy(slot(*chip, c), slot(*chip, c), ssem, rsem, 1 + j, me).wait_recv()
            passed[j].start()
        _rcopy(slot(*sib), slot(*sib), ssem, rsem, 0, me).wait_recv()
        for j, chip in enumerate(others):
            _rcopy(slot(*chip, 1 - c), slot(*chip, 1 - c), ssem, rsem, 4 + j, me).wait_recv()
        for cp in first + passed:
            cp.wait_send()

    return _comm_in_place(body, name, [slab], 7)[0]


def _pad128(n):
    return -(-n // 128) * 128


def _pack_small(grads, shapes):
    parts = []
    for g in grads:
        for n in SMALL:
            v = g[n].astype(F32).reshape(-1)
            parts.append(jnp.pad(v, (0, _pad128(v.shape[0]) - v.shape[0])))
    flat = jnp.concatenate(parts)
    rows = -(-flat.shape[0] // (128 * 512)) * 512
    return jnp.pad(flat, (0, rows * 128 - flat.shape[0])).reshape(rows, 128)


def _unpack_small(slab, shapes, depth):
    flat = slab.reshape(-1)
    out = {n: [] for n in SMALL}
    off = 0
    for _ in range(depth):
        for n in SMALL:
            size = math.prod(shapes[n])
            out[n].append(flat[off:off + size].reshape(shapes[n]))
            off += _pad128(size)
    return {n: jnp.stack(v) for n, v in out.items()}


def _adamw_nd(w, g, m, v, name):
    shp = w.shape
    two = lambda a: a.reshape(-1, shp[-1])
    return tuple(o.reshape(shp) for o in adamw(two(w), two(g), two(m), two(v), name=name))


def kernel(x, ln1_g, w_in, b_gate, sg_ln_g, sg_ln_b, sg_w, sg_b, w_a_out, cv_w, cv_b, cv_ln_g, cv_ln_b, w_b_out, q_norm_g, k_norm_g, w_c_out, w_out, ln2_g, w_up, ffn_conv_w, ffn_conv_b, w_down, loss_target, m_ln1_g, m_w_in, m_b_gate, m_sg_ln_g, m_sg_ln_b, m_sg_w, m_sg_b, m_w_a_out, m_cv_w, m_cv_b, m_cv_ln_g, m_cv_ln_b, m_w_b_out, m_q_norm_g, m_k_norm_g, m_w_c_out, m_w_out, m_ln2_g, m_w_up, m_ffn_conv_w, m_ffn_conv_b, m_w_down, v_ln1_g, v_w_in, v_b_gate, v_sg_ln_g, v_sg_ln_b, v_sg_w, v_sg_b, v_w_a_out, v_cv_w, v_cv_b, v_cv_ln_g, v_cv_ln_b, v_w_b_out, v_q_norm_g, v_k_norm_g, v_w_c_out, v_w_out, v_ln2_g, v_w_up, v_ffn_conv_w, v_ffn_conv_b, v_w_down):
    w = dict(ln1_g=ln1_g, w_in=w_in, b_gate=b_gate, sg_ln_g=sg_ln_g, sg_ln_b=sg_ln_b, sg_w=sg_w, sg_b=sg_b,
             w_a_out=w_a_out, cv_w=cv_w, cv_b=cv_b, cv_ln_g=cv_ln_g, cv_ln_b=cv_ln_b, w_b_out=w_b_out,
             q_norm_g=q_norm_g, k_norm_g=k_norm_g, w_c_out=w_c_out, w_out=w_out, ln2_g=ln2_g, w_up=w_up,
             ffn_conv_w=ffn_conv_w, ffn_conv_b=ffn_conv_b, w_down=w_down)
    m = dict(ln1_g=m_ln1_g, w_in=m_w_in, b_gate=m_b_gate, sg_ln_g=m_sg_ln_g, sg_ln_b=m_sg_ln_b, sg_w=m_sg_w,
             sg_b=m_sg_b, w_a_out=m_w_a_out, cv_w=m_cv_w, cv_b=m_cv_b, cv_ln_g=m_cv_ln_g, cv_ln_b=m_cv_ln_b,
             w_b_out=m_w_b_out, q_norm_g=m_q_norm_g, k_norm_g=m_k_norm_g, w_c_out=m_w_c_out, w_out=m_w_out,
             ln2_g=m_ln2_g, w_up=m_w_up, ffn_conv_w=m_ffn_conv_w, ffn_conv_b=m_ffn_conv_b, w_down=m_w_down)
    v = dict(ln1_g=v_ln1_g, w_in=v_w_in, b_gate=v_b_gate, sg_ln_g=v_sg_ln_g, sg_ln_b=v_sg_ln_b, sg_w=v_sg_w,
             sg_b=v_sg_b, w_a_out=v_w_a_out, cv_w=v_cv_w, cv_b=v_cv_b, cv_ln_g=v_cv_ln_g, cv_ln_b=v_cv_ln_b,
             w_b_out=v_w_b_out, q_norm_g=v_q_norm_g, k_norm_g=v_k_norm_g, w_c_out=v_w_c_out, w_out=v_w_out,
             ln2_g=v_ln2_g, w_up=v_w_up, ffn_conv_w=v_ffn_conv_w, ffn_conv_b=v_ffn_conv_b, w_down=v_w_down)
    depth = ln1_g.shape[0]
    cx, cy, cc = lax.axis_index("x"), lax.axis_index("y"), lax.axis_index("c")
    me = 2 * cx + cy
    pos_idx = jnp.stack([me, cc]).astype(jnp.int32)
    c_idx = jnp.reshape(cc, (1,)).astype(jnp.int32)

    padded = {n: jnp.pad(w[n], ((0, 0), (0, -w[n].shape[1] % 8), (0, 0))) for n in SMALL_COL}

    def blocks(l):
        out = [place_block(w[n], l, pos_idx, by_rows=n in BIG_ROW, dtype=BF16, name="place_" + n) for n in BIG]
        return out + [place_block(padded[n], l, pos_idx, by_rows=False, dtype=F32, name="place_" + n) for n in SMALL_COL]

    full0 = allgather_weights(blocks(0), name="allgather_weights")
    gathers, prev = [], full0[:1]
    for l in range(1, depth):
        gathers.append(gather_start(blocks(l), prev, name="gather_start_%d" % l))
        prev = [gathers[-1]['token']]

    def get_layer(l, x_in):
        if l == 0:
            p, after = _name_full(full0), tuple(h['token'] for h in gathers)
        else:
            bufs = gather_wait(gathers[l - 1], [x_in], name="gather_wait_%d" % l)
            p, after = _name_full(gather_finish(bufs, name="gather_finish")), ()
        for n in SMALL:
            p[n] = p[n][:w[n].shape[1]] if n in SMALL_COL else w[n][l]
        return p, after

    grads, scatters = [None] * depth, [None] * depth

    def on_grads(l, g):
        grads[l] = g
        pv = chip_partials(g, c_idx)
        if l == 0:
            scatters[l] = scatter_partials(pv, name="rs_scatter_partials")
            return ()
        scatters[l] = scatter_start(pv, name="scatter_start_%d" % l)
        return (scatters[l]['token'],)

    loss, dx = local_step(x[0], loss_target[0], depth, get_layer, on_grads)
    loss = lax.psum(loss, ("x", "y", "c"))
    big = [None] * depth
    for l in range(depth):
        pv, got = scatters[l] if l == 0 else scatter_wait(scatters[l], [dx], name="scatter_wait_%d" % l)
        big[l] = reduce_shards(pv, got, pos_idx)
    full_shapes = {n: (w[n].shape[1], N_CHIPS * w[n].shape[2]) if n in SMALL_COL else w[n].shape[1:] for n in SMALL}
    mine = _pack_small(grads, full_shapes)
    slots = lax.dynamic_update_slice(lax.empty((N_DEV,) + mine.shape, F32), mine[None], (4 * cx + 2 * cy + cc, 0, 0))
    slab = sum_slots(allgather_slab(slots, name="allgather_small_grads"), name="sum_small_grads")
    small = _unpack_small(slab, full_shapes, depth)
    grad = {n: jnp.stack([b[n] for b in big]) for n in BIG}
    for n in SMALL:
        if n in SMALL_COL:
            cs = w[n].shape[-1]
            grad[n] = lax.dynamic_slice_in_dim(small[n], me * cs, cs, axis=small[n].ndim - 1)
        else:
            grad[n] = small[n]

    delta, new_m, new_v = {}, {}, {}
    for n in WEIGHTS:
        delta[n], new_m[n], new_v[n] = _adamw_nd(w[n], grad[n], m[n], v[n], "adamw_" + n)
    return (loss, dx[None], *[grad[n] for n in WEIGHTS], *[delta[n] for n in WEIGHTS],
            *[new_m[n] for n in WEIGHTS], *[new_v[n] for n in WEIGHTS])
```

```python
import functools
import math

import jax
import jax.numpy as jnp
from jax import lax
from jax.experimental import pallas as pl
from jax.experimental.pallas import tpu as pltpu

F32 = jnp.float32
BF16 = jnp.bfloat16
MESH = pl.DeviceIdType.MESH
ANY = pl.BlockSpec(memory_space=pl.ANY)

EPS = 1e-6
D_MODEL = 1024
DEPTH = 4
SG_WIDTH = 512
CHUNK = 128
CV_WIDTH = 512
CV_KERNEL = 31
SB_WIDTH = 512
SB_HEAD_DIM = 64
Q_BLOCK = 128
D_FF = 2816
FFN_KERNEL = 3
COL_B = 1024
COL_C = 2048
COL_G = 3584
IN_COLS = 6656
N_CHIPS = 4
N_DEV = 8
CV_HALO = 32
FFN_HALO = 16

ADAM_LR = 0.001
ADAM_B1 = 0.9
ADAM_B2 = 0.999
ADAM_EPS = 1e-08
ADAM_WD = 0.01
ADAM_STEP = 10

VMEM_LIMIT_BYTES = 56 * 1024 * 1024

NT_DIMS = (((1,), (1,)), ((), ()))
TN_DIMS = (((0,), (0,)), ((), ()))

WEIGHTS = ['ln1_g', 'w_in', 'b_gate', 'sg_ln_g', 'sg_ln_b', 'sg_w', 'sg_b', 'w_a_out', 'cv_w', 'cv_b',
           'cv_ln_g', 'cv_ln_b', 'w_b_out', 'q_norm_g', 'k_norm_g', 'w_c_out', 'w_out', 'ln2_g', 'w_up',
           'ffn_conv_w', 'ffn_conv_b', 'w_down']
BIG_COL = ['w_in', 'w_a_out', 'w_b_out', 'w_c_out', 'w_up']
BIG_ROW = ['w_out', 'w_down']
BIG = BIG_COL + BIG_ROW
SMALL_COL = ['b_gate', 'cv_w', 'ffn_conv_w']
SMALL = [n for n in WEIGHTS if n not in BIG]


def _pc(body, **kw):
    return pl.pallas_call(body, **kw)


def _cp(*sem):
    return pltpu.CompilerParams(dimension_semantics=sem, vmem_limit_bytes=VMEM_LIMIT_BYTES)


def _sds(shape, dtype):
    return jax.ShapeDtypeStruct(shape, dtype)


_GELU_C = math.sqrt(2.0 / math.pi)
_GELU_A = 0.044715


def _sigmoid(x):
    return jax.nn.sigmoid(x)


def _gelu(x):
    return 0.5 * x * (1.0 + jnp.tanh(_GELU_C * (x + _GELU_A * x * x * x)))


def _gelu_grad(x):
    t = jnp.tanh(_GELU_C * (x + _GELU_A * x * x * x))
    return 0.5 * (1.0 + t) + 0.5 * x * (1.0 - t * t) * _GELU_C * (1.0 + 3.0 * _GELU_A * x * x)


def _silu(x):
    return x * _sigmoid(x)


def _silu_grad(x):
    s = _sigmoid(x)
    return s * (1.0 + x * (1.0 - s))


def _ln_stats(x):
    mu = jnp.mean(x, axis=-1, keepdims=True)
    xc = x - mu
    r = lax.rsqrt(jnp.mean(xc * xc, axis=-1, keepdims=True) + EPS)
    return xc * r, r


def _ln_bwd(dy, xhat, r, g):
    dxh = dy * g
    return r * (dxh - jnp.mean(dxh, axis=-1, keepdims=True) - xhat * jnp.mean(dxh * xhat, axis=-1, keepdims=True))


def _split_dot(x, m):
    hi = x.astype(BF16)
    lo = (x - hi.astype(F32)).astype(BF16)
    return jnp.dot(hi, m, preferred_element_type=F32) + jnp.dot(lo, m, preferred_element_type=F32)


def _block_sums(x, m):
    return jnp.dot(x.astype(BF16), m, preferred_element_type=F32)


def _rowsum0(x):
    return jnp.sum(x, axis=0, keepdims=True)


def _pick(n, prefs):
    for p in prefs:
        if n % p == 0:
            return p
    return n


def mm_nn(a, w, *, name, res=None, out_dtype=BF16):
    t, k = a.shape
    n = w.shape[1]
    tm = _pick(t, (512, 256))
    tn = _pick(n, (1024, 512, 256))

    def body(*refs):
        if res is None:
            a_ref, w_ref, o_ref = refs
        else:
            a_ref, w_ref, r_ref, o_ref = refs
        acc = jnp.dot(a_ref[...], w_ref[...], preferred_element_type=F32)
        if res is not None:
            acc = acc + r_ref[...]
        o_ref[...] = acc.astype(o_ref.dtype)

    in_specs = [pl.BlockSpec((tm, k), lambda i, j: (i, 0)), pl.BlockSpec((k, tn), lambda i, j: (0, j))]
    args = [a, w]
    if res is not None:
        in_specs.append(pl.BlockSpec((tm, tn), lambda i, j: (i, j)))
        args.append(res)
    return _pc(body, name=name, grid=(t // tm, n // tn), in_specs=in_specs,
               out_specs=pl.BlockSpec((tm, tn), lambda i, j: (i, j)),
               out_shape=_sds((t, n), out_dtype), compiler_params=_cp("parallel", "parallel"))(*args)


def mm_norm_nn(x, g, w, *, name, after=()):
    t, k = x.shape
    n = w.shape[1]
    tm = _pick(t, (512, 256))
    tn = _pick(n, (1664, 1408, 512))

    def body(x_ref, g_ref, w_ref, *rest):
        z_ref, h_ref = rest[len(after):]
        xv = x_ref[...]
        r = lax.rsqrt(jnp.mean(xv * xv, axis=-1, keepdims=True) + EPS)
        h = (xv * r * g_ref[...]).astype(BF16)
        h_ref[...] = h
        for c in range(n // tn):
            cols = slice(c * tn, (c + 1) * tn)
            z_ref[:, cols] = jnp.dot(h, w_ref[:, cols], preferred_element_type=F32).astype(z_ref.dtype)

    return _pc(body, name=name, grid=(t // tm,),
               in_specs=[pl.BlockSpec((tm, k), lambda i: (i, 0)), pl.BlockSpec((1, k), lambda i: (0, 0)),
                         pl.BlockSpec((k, n), lambda i: (0, 0), pipeline_mode=pl.Buffered(1))]
               + [pl.BlockSpec(a.shape, lambda i: (0, 0)) for a in after],
               out_specs=[pl.BlockSpec((tm, n), lambda i: (i, 0)), pl.BlockSpec((tm, k), lambda i: (i, 0))],
               out_shape=[_sds((t, n), BF16), _sds((t, k), BF16)],
               compiler_params=_cp("parallel"))(x, g, w, *after)


def mm_nt(dy, w, *, name, out_dtype):
    t, n = dy.shape
    k = w.shape[0]
    tm = _pick(t, (512, 256))

    def body(dy_ref, w_ref, o_ref):
        o_ref[...] = lax.dot_general(dy_ref[...].astype(BF16), w_ref[...], NT_DIMS,
                                     preferred_element_type=F32).astype(o_ref.dtype)

    return _pc(body, name=name, grid=(t // tm,),
               in_specs=[pl.BlockSpec((tm, n), lambda i: (i, 0)),
                         pl.BlockSpec((k, n), lambda i: (0, 0), pipeline_mode=pl.Buffered(1))],
               out_specs=pl.BlockSpec((tm, k), lambda i: (i, 0)),
               out_shape=_sds((t, k), out_dtype), compiler_params=_cp("parallel"))(dy, w)


def mm_tn(a, dy, *, name, out_dtype=BF16, after=()):
    t, k = a.shape
    n = dy.shape[1]
    tk = _pick(k, (1024, 1408, 512))
    tn = _pick(n, (512,) if dy.dtype == F32 else (1664, 1408, 1024, 512))

    def body(a_ref, dy_ref, *rest):
        o_ref = rest[len(after)]
        o_ref[...] = lax.dot_general(a_ref[...], dy_ref[...].astype(BF16), TN_DIMS,
                                     preferred_element_type=F32).astype(o_ref.dtype)

    return _pc(body, name=name, grid=(k // tk, n // tn),
               in_specs=[pl.BlockSpec((t, tk), lambda i, j: (0, i)), pl.BlockSpec((t, tn), lambda i, j: (0, j))]
               + [pl.BlockSpec(tok.shape, lambda i, j: (0, 0)) for tok in after],
               out_specs=pl.BlockSpec((tk, tn), lambda i, j: (i, j)),
               out_shape=_sds((k, n), out_dtype), compiler_params=_cp("parallel", "parallel"))(a, dy, *after)


def rms_bwd(dh, x, g, dres, *, name):
    t, d = x.shape
    tm = _pick(t, (256,))

    def body(dh_ref, x_ref, g_ref, dres_ref, dx_ref, dg_ref):
        xv = x_ref[...]
        r = lax.rsqrt(jnp.mean(xv * xv, axis=-1, keepdims=True) + EPS)
        xh = xv * r
        dy = dh_ref[...].astype(F32)
        dxh = dy * g_ref[...]
        dx_ref[...] = dres_ref[...] + r * (dxh - xh * jnp.mean(dxh * xh, axis=-1, keepdims=True))

        @pl.when(pl.program_id(0) == 0)
        def _():
            dg_ref[...] = jnp.zeros_like(dg_ref)

        dg_ref[...] += _rowsum0(dy * xh)

    row = pl.BlockSpec((tm, d), lambda i: (i, 0))
    vec = pl.BlockSpec((1, d), lambda i: (0, 0))
    return _pc(body, name=name, grid=(t // tm,), in_specs=[row, row, vec, row], out_specs=[row, vec],
               out_shape=[_sds((t, d), F32), _sds((1, d), F32)], compiler_params=_cp("arbitrary"))(dh, x, g, dres)


def loss_head(y, target, *, name):
    t, d = y.shape
    tm = _pick(t, (256,))

    def body(y_ref, t_ref, loss_ref, dy_ref):
        e = y_ref[...] - t_ref[...]
        dy_ref[...] = e * (1.0 / d)

        @pl.when(pl.program_id(0) == 0)
        def _():
            loss_ref[...] = jnp.zeros_like(loss_ref)

        loss_ref[...] += _rowsum0(jnp.sum(e * e, axis=1, keepdims=True)) * (0.5 / d)

    row = pl.BlockSpec((tm, d), lambda i: (i, 0))
    return _pc(body, name=name, grid=(t // tm,), in_specs=[row, row],
               out_specs=[pl.BlockSpec((1, 1), lambda i: (0, 0)), row],
               out_shape=[_sds((1, 1), F32), _sds((t, d), F32)], compiler_params=_cp("arbitrary"))(y, target)


def _sg_masks():
    lane = lax.broadcasted_iota(jnp.int32, (CHUNK, CHUNK), 1)
    row = lax.broadcasted_iota(jnp.int32, (CHUNK, CHUNK), 0)
    return lane < 64, lane <= row, row <= lane


def _sg_gate(vn_chunk, w_ref, bias_ref, p, first_group, tril):
    wa = jnp.where(tril, w_ref[2 * p], 0.0).astype(BF16)
    wb = jnp.where(tril, w_ref[2 * p + 1], 0.0).astype(BF16)
    oa = jnp.dot(wa, vn_chunk, preferred_element_type=F32)
    ob = jnp.dot(wb, vn_chunk, preferred_element_type=F32)
    return jnp.where(first_group, oa, ob) + bias_ref[:, p * 128:(p + 1) * 128]


def mixa_fwd(z, ln_g, ln_b, sg_w, sg_bias, *, name):
    t = z.shape[0]
    tm = _pick(t, (256,))

    def body(z_ref, g_ref, b_ref, w_ref, bias_ref, o_ref):
        first_group, tril, _ = _sg_masks()
        zv = z_ref[...].astype(F32)
        u = _gelu(zv[:, :SG_WIDTH])
        v = _gelu(zv[:, SG_WIDTH:])
        vh, _ = _ln_stats(v)
        vn = (vh * g_ref[...] + b_ref[...]).astype(BF16)
        for c in range(tm // CHUNK):
            rows = slice(c * CHUNK, (c + 1) * CHUNK)
            for p in range(4):
                cols = slice(p * 128, (p + 1) * 128)
                o = _sg_gate(vn[rows, cols], w_ref, bias_ref, p, first_group, tril)
                o_ref[rows, cols] = (u[rows, cols] * o).astype(o_ref.dtype)

    vec = pl.BlockSpec((1, SG_WIDTH), lambda i: (0, 0))
    return _pc(body, name=name, grid=(t // tm,),
               in_specs=[pl.BlockSpec((tm, 2 * SG_WIDTH), lambda i: (i, 0)), vec, vec,
                         pl.BlockSpec((8, CHUNK, CHUNK), lambda i: (0, 0, 0)),
                         pl.BlockSpec((CHUNK, SG_WIDTH), lambda i: (0, 0))],
               out_specs=pl.BlockSpec((tm, SG_WIDTH), lambda i: (i, 0)),
               out_shape=_sds((t, SG_WIDTH), BF16), compiler_params=_cp("parallel"))(z, ln_g, ln_b, sg_w, sg_bias)


def mixa_bwd(z, dga, ln_g, ln_b, sg_w, sg_wt, sg_bias, *, name):
    t = z.shape[0]
    tm = _pick(t, (256,))
    nsteps = t // tm

    def body(z_ref, dga_ref, g_ref, b_ref, w_ref, wt_ref, bias_ref, dz_ref, dw_ref, dsgb_ref, dg_ref, db_ref, dvn_s,
             dbias_ref):
        i = pl.program_id(0)
        first_group, tril, triu = _sg_masks()

        @pl.when(i == 0)
        def _():
            dw_ref[...] = jnp.zeros_like(dw_ref)
            dbias_ref[...] = jnp.zeros_like(dbias_ref)
            dg_ref[...] = jnp.zeros_like(dg_ref)
            db_ref[...] = jnp.zeros_like(db_ref)

        zv = z_ref[...].astype(F32)
        zu = zv[:, :SG_WIDTH]
        zg = zv[:, SG_WIDTH:]
        u = _gelu(zu)
        v = _gelu(zg)
        vh, r = _ln_stats(v)
        vn = (vh * g_ref[...] + b_ref[...]).astype(BF16)
        dga_v = dga_ref[...].astype(F32)
        d_o = dga_v * u
        for c in range(tm // CHUNK):
            rows = slice(c * CHUNK, (c + 1) * CHUNK)
            dbias_ref[...] += d_o[rows, :]
            for p in range(4):
                cols = slice(p * 128, (p + 1) * 128)
                vp = vn[rows, cols]
                o = _sg_gate(vp, w_ref, bias_ref, p, first_group, tril)
                dz_ref[rows, cols] = (dga_v[rows, cols] * o * _gelu_grad(zu[rows, cols])).astype(dz_ref.dtype)
                dop = d_o[rows, cols]
                dop_a = jnp.where(first_group, dop, 0.0).astype(BF16)
                dop_b = jnp.where(first_group, 0.0, dop).astype(BF16)
                dw_ref[2 * p] += lax.dot_general(dop_a, vp, NT_DIMS, preferred_element_type=F32)
                dw_ref[2 * p + 1] += lax.dot_general(dop_b, vp, NT_DIMS, preferred_element_type=F32)
                wta = jnp.where(triu, wt_ref[2 * p], 0.0).astype(BF16)
                wtb = jnp.where(triu, wt_ref[2 * p + 1], 0.0).astype(BF16)
                dop16 = dop.astype(BF16)
                dvn_s[rows, cols] = jnp.where(first_group, jnp.dot(wta, dop16, preferred_element_type=F32),
                                              jnp.dot(wtb, dop16, preferred_element_type=F32))
        dvn = dvn_s[...]
        dg_ref[...] += _rowsum0(dvn * vh)
        db_ref[...] += _rowsum0(dvn)
        dv = _ln_bwd(dvn, vh, r, g_ref[...])
        dz_ref[:, SG_WIDTH:] = (dv * _gelu_grad(zg)).astype(dz_ref.dtype)

        @pl.when(i == nsteps - 1)
        def _():
            for gi in range(8):
                dw_ref[gi] = jnp.where(tril, dw_ref[gi], 0.0)
            r_id = lax.broadcasted_iota(jnp.int32, (SG_WIDTH, 128), 0) // SB_HEAD_DIM
            c_id = lax.broadcasted_iota(jnp.int32, (SG_WIDTH, 128), 1)
            dsgb_ref[...] = _split_dot(dbias_ref[...], (r_id == c_id).astype(BF16))

    vec = pl.BlockSpec((1, SG_WIDTH), lambda i: (0, 0))
    wspec = pl.BlockSpec((8, CHUNK, CHUNK), lambda i: (0, 0, 0))
    bspec = pl.BlockSpec((CHUNK, SG_WIDTH), lambda i: (0, 0))
    sgb = pl.BlockSpec((CHUNK, 128), lambda i: (0, 0))
    return _pc(body, name=name, grid=(nsteps,),
               in_specs=[pl.BlockSpec((tm, 2 * SG_WIDTH), lambda i: (i, 0)), pl.BlockSpec((tm, SG_WIDTH), lambda i: (i, 0)),
                         vec, vec, wspec, wspec, bspec],
               out_specs=[pl.BlockSpec((tm, 2 * SG_WIDTH), lambda i: (i, 0)), wspec, sgb, vec, vec],
               out_shape=[_sds((t, 2 * SG_WIDTH), BF16), _sds((8, CHUNK, CHUNK), F32), _sds((CHUNK, 128), F32),
                          _sds((1, SG_WIDTH), F32), _sds((1, SG_WIDTH), F32)],
               scratch_shapes=[pltpu.VMEM((tm, SG_WIDTH), F32), pltpu.VMEM((CHUNK, SG_WIDTH), F32)],
               compiler_params=_cp("arbitrary"))(z, dga, ln_g, ln_b, sg_w, sg_wt, sg_bias)


def _glu(zv):
    return zv[:, :CV_WIDTH] * _sigmoid(zv[:, CV_WIDTH:])


_SUB = 8


def _cv_phases(x_s, tm):
    rows = CV_HALO + tm - _SUB
    for r in range(1, _SUB):
        x_s[r, 0:rows, :] = x_s[0, pl.ds(r, rows), :]


def _cv_tap(x_s, o, tm):
    return x_s[o % _SUB, pl.ds(o - o % _SUB, tm), :]


def _cv_fill(zm_ref, zh_ref, x_s, i, tm):
    x_s[0, 0:CV_HALO, :] = jnp.where(i > 0, _glu(zh_ref[...].astype(F32)), 0.0)
    x_s[0, CV_HALO:CV_HALO + tm, :] = _glu(zm_ref[...].astype(F32))
    _cv_phases(x_s, tm)


def _cv_conv(x_s, w_ref, cb_ref, tm):
    acc = jnp.zeros((tm, CV_WIDTH), F32) + cb_ref[...]
    for k in range(CV_KERNEL):
        acc = acc + w_ref[k:k + 1, :] * _cv_tap(x_s, CV_HALO - (CV_KERNEL - 1) + k, tm)
    return acc


def _cv_specs(tm):
    zm = pl.BlockSpec((tm, 2 * CV_WIDTH), lambda i: (i, 1))
    zh = pl.BlockSpec((CV_HALO, 2 * CV_WIDTH), lambda i: (jnp.maximum(i * (tm // CV_HALO) - 1, 0), 1))
    w = pl.BlockSpec((CV_KERNEL, CV_WIDTH), lambda i: (0, 0))
    vec = pl.BlockSpec((1, CV_WIDTH), lambda i: (0, 0))
    return zm, zh, w, vec


def mixb_fwd(z, cv_w, cv_b, ln_g, ln_b, *, name):
    t = z.shape[0]
    tm = _pick(t, (256,))

    def body(zm_ref, zh_ref, w_ref, cb_ref, g_ref, b_ref, o_ref, x_s):
        _cv_fill(zm_ref, zh_ref, x_s, pl.program_id(0), tm)
        c1 = _cv_conv(x_s, w_ref, cb_ref, tm)
        ch, _ = _ln_stats(c1)
        o_ref[...] = _silu(ch * g_ref[...] + b_ref[...]).astype(o_ref.dtype)

    zm, zh, w, vec = _cv_specs(tm)
    return _pc(body, name=name, grid=(t // tm,), in_specs=[zm, zh, w, vec, vec, vec],
               out_specs=pl.BlockSpec((tm, CV_WIDTH), lambda i: (i, 0)), out_shape=_sds((t, CV_WIDTH), BF16),
               scratch_shapes=[pltpu.VMEM((_SUB, CV_HALO + tm, CV_WIDTH), F32)],
               compiler_params=_cp("parallel"))(z, z, cv_w, cv_b, ln_g, ln_b)


def mixb_bwd1(z, dc3, cv_w, cv_b, ln_g, ln_b, *, name):
    t = z.shape[0]
    tm = _pick(t, (256,))

    def body(zm_ref, zh_ref, dc3_ref, w_ref, cb_ref, g_ref, b_ref, dc1_ref, dw_ref, dcb_ref, dg_ref, db_ref, x_s):
        i = pl.program_id(0)

        @pl.when(i == 0)
        def _():
            dw_ref[...] = jnp.zeros_like(dw_ref)
            dcb_ref[...] = jnp.zeros_like(dcb_ref)
            dg_ref[...] = jnp.zeros_like(dg_ref)
            db_ref[...] = jnp.zeros_like(db_ref)

        _cv_fill(zm_ref, zh_ref, x_s, i, tm)
        c1 = _cv_conv(x_s, w_ref, cb_ref, tm)
        ch, r = _ln_stats(c1)
        c2 = ch * g_ref[...] + b_ref[...]
        dc2 = dc3_ref[...].astype(F32) * _silu_grad(c2)
        dg_ref[...] += _rowsum0(dc2 * ch)
        db_ref[...] += _rowsum0(dc2)
        dc1 = _ln_bwd(dc2, ch, r, g_ref[...])
        dc1_ref[...] = dc1
        dcb_ref[...] += _rowsum0(dc1)
        for k in range(CV_KERNEL):
            dw_ref[k:k + 1, :] += _rowsum0(dc1 * _cv_tap(x_s, CV_HALO - (CV_KERNEL - 1) + k, tm))

    zm, zh, w, vec = _cv_specs(tm)
    row = pl.BlockSpec((tm, CV_WIDTH), lambda i: (i, 0))
    return _pc(body, name=name, grid=(t // tm,), in_specs=[zm, zh, row, w, vec, vec, vec],
               out_specs=[row, w, vec, vec, vec],
               out_shape=[_sds((t, CV_WIDTH), F32), _sds((CV_KERNEL, CV_WIDTH), F32), _sds((1, CV_WIDTH), F32),
                          _sds((1, CV_WIDTH), F32), _sds((1, CV_WIDTH), F32)],
               scratch_shapes=[pltpu.VMEM((_SUB, CV_HALO + tm, CV_WIDTH), F32)],
               compiler_params=_cp("arbitrary"))(z, z, dc3, cv_w, cv_b, ln_g, ln_b)


def mixb_bwd2(z, dc1, cv_w, *, name):
    t = z.shape[0]
    tm = _pick(t, (256,))
    nsteps = t // tm

    def body(zm_ref, dm_ref, dh_ref, w_ref, dz_ref, y_s):
        i = pl.program_id(0)
        y_s[0, 0:tm, :] = dm_ref[...]
        y_s[0, tm:tm + CV_HALO, :] = jnp.where(i < nsteps - 1, dh_ref[...], 0.0)
        _cv_phases(y_s, tm)
        dc0 = jnp.zeros((tm, CV_WIDTH), F32)
        for k in range(CV_KERNEL):
            dc0 = dc0 + w_ref[k:k + 1, :] * _cv_tap(y_s, CV_KERNEL - 1 - k, tm)
        zv = zm_ref[...].astype(F32)
        p = zv[:, :CV_WIDTH]
        s = _sigmoid(zv[:, CV_WIDTH:])
        dz_ref[:, :CV_WIDTH] = (dc0 * s).astype(dz_ref.dtype)
        dz_ref[:, CV_WIDTH:] = (dc0 * p * s * (1.0 - s)).astype(dz_ref.dtype)

    last = t // CV_HALO - 1
    return _pc(body, name=name, grid=(nsteps,),
               in_specs=[pl.BlockSpec((tm, 2 * CV_WIDTH), lambda i: (i, 1)),
                         pl.BlockSpec((tm, CV_WIDTH), lambda i: (i, 0)),
                         pl.BlockSpec((CV_HALO, CV_WIDTH), lambda i: (jnp.minimum((i + 1) * (tm // CV_HALO), last), 0)),
                         pl.BlockSpec((CV_KERNEL, CV_WIDTH), lambda i: (0, 0))],
               out_specs=pl.BlockSpec((tm, 2 * CV_WIDTH), lambda i: (i, 0)),
               out_shape=_sds((t, 2 * CV_WIDTH), BF16),
               scratch_shapes=[pltpu.VMEM((_SUB, tm + CV_HALO, CV_WIDTH), F32)],
               compiler_params=_cp("parallel"))(z, dc1, dc1, cv_w)


def _group_ones():
    r = lax.broadcasted_iota(jnp.int32, (SB_WIDTH, SB_WIDTH), 0) // SB_HEAD_DIM
    c = lax.broadcasted_iota(jnp.int32, (SB_WIDTH, SB_WIDTH), 1) // SB_HEAD_DIM
    return (r == c).astype(BF16)


def attn_prep(z, gq, gk, *, name):
    t = z.shape[0]
    tm = _pick(t, (256,))
    scale = 1.0 / math.sqrt(SB_HEAD_DIM)

    def body(q_ref, k_ref, gq_ref, gk_ref, qo_ref, ko_ref):
        ones = _group_ones()
        for src, g_ref, dst, mul in ((q_ref, gq_ref, qo_ref, scale), (k_ref, gk_ref, ko_ref, 1.0)):
            v = src[...].astype(F32)
            r = lax.rsqrt(_split_dot(v * v, ones) * (1.0 / SB_HEAD_DIM) + EPS)
            dst[...] = ((v * r * g_ref[...]).astype(BF16).astype(F32) * mul).astype(dst.dtype)

    vec = pl.BlockSpec((1, SB_WIDTH), lambda i: (0, 0))
    row = pl.BlockSpec((tm, SB_WIDTH), lambda i: (i, 0))
    return _pc(body, name=name, grid=(t // tm,),
               in_specs=[pl.BlockSpec((tm, SB_WIDTH), lambda i: (i, COL_C // SB_WIDTH)),
                         pl.BlockSpec((tm, SB_WIDTH), lambda i: (i, COL_C // SB_WIDTH + 1)), vec, vec],
               out_specs=[row, row], out_shape=[_sds((t, SB_WIDTH), BF16), _sds((t, SB_WIDTH), BF16)],
               compiler_params=_cp("parallel"))(z, z, gq, gk)


_KB = Q_BLOCK
_PAIR = 2 * _KB


def _attn_tq(t):
    return _pick(t, (512, 256, 128))


def _attn_consts(tq):
    first_head = lax.broadcasted_iota(jnp.int32, (_KB, 128), 1) < SB_HEAD_DIM
    r2 = lax.broadcasted_iota(jnp.int32, (_PAIR, _PAIR), 0)
    c2 = lax.broadcasted_iota(jnp.int32, (_PAIR, _PAIR), 1)
    same = (r2 // _KB) == (c2 // _KB)
    m_suffix = (same & (r2 > c2)).astype(BF16)
    m_prefix = (same & (r2 < c2)).astype(BF16)
    row = lax.broadcasted_iota(jnp.int32, (tq, _PAIR), 0)
    col = lax.broadcasted_iota(jnp.int32, (tq, _PAIR), 1)
    return first_head, m_suffix, m_prefix, row, col & (_KB - 1), col < _KB


def _sb_logits(z, causal):
    sp = jnp.log(1.0 + jnp.exp(-jnp.abs(z)))
    g = jnp.minimum(z, 0.0) - sp
    l1m = g - z
    if causal is not None:
        l1m = jnp.where(causal, l1m, 0.0)
    return g, l1m


def _stack_heads(first_head, v):
    zero = jnp.zeros_like(v)
    return jnp.concatenate([jnp.where(first_head, v, zero), jnp.where(first_head, zero, v)], axis=0)


def _add_rows(x, upd, r0):
    return x + upd if r0 == 0 else jnp.concatenate([x[:r0], x[r0:] + upd], axis=0)


def _pair_sums(x):
    return jnp.sum(x[:, :_KB], axis=1, keepdims=True), jnp.sum(x[:, _KB:], axis=1, keepdims=True)


def _attn_specs(t, tq):
    qspec = pl.BlockSpec((tq, 128), lambda h, i: (i, h))
    kspec = pl.BlockSpec((t, 128), lambda h, i: (0, h))
    vspec = pl.BlockSpec((t, 128), lambda h, i: (0, (COL_C + 2 * SB_WIDTH) // 128 + h))
    return qspec, kspec, vspec


def attn_fwd(q, k, z, *, name):
    t = q.shape[0]
    tq = _attn_tq(t)
    nd = tq // _KB

    def body(q_ref, k_ref, v_ref, o_ref):
        qt = pl.program_id(1)
        first_head, m_suffix, _, row, key, is_first = _attn_consts(tq)
        qv = q_ref[...]

        def step(kb, state, causal, r0=0):
            acc, ca, cb = state
            off = pl.multiple_of(kb * _KB, _KB)
            kcat = _stack_heads(first_head, k_ref[pl.ds(off, _KB), :])
            vcat = _stack_heads(first_head, v_ref[pl.ds(off, _KB), :])
            zz = lax.dot_general(qv[r0:], kcat, NT_DIMS, preferred_element_type=F32)
            g, l1m = _sb_logits(zz, None if causal is None else causal[r0:])
            a = jnp.exp(g + _block_sums(l1m, m_suffix) + jnp.where(is_first[r0:], ca[r0:], cb[r0:]))
            if causal is not None:
                a = jnp.where(causal[r0:], a, 0.0)
            sa, sb = _pair_sums(l1m)
            pv = jnp.dot(a.astype(BF16), vcat, preferred_element_type=F32)
            return _add_rows(acc, pv, r0), _add_rows(ca, sa, r0), _add_rows(cb, sb, r0)

        c0 = jnp.zeros((tq, 1), F32)
        state = (jnp.zeros((tq, 128), F32), c0, c0)
        for d in reversed(range(nd)):
            state = step(qt * nd + d, state, key + d * _KB < row, d * _KB)
        state = lax.fori_loop(0, qt * nd, lambda s, st: step(qt * nd - 1 - s, st, None), state)
        o_ref[...] = state[0].astype(o_ref.dtype)

    qspec, kspec, vspec = _attn_specs(t, tq)
    return _pc(body, name=name, grid=(SB_WIDTH // 128, t // tq), in_specs=[qspec, kspec, vspec], out_specs=qspec,
               out_shape=_sds((t, SB_WIDTH), BF16), compiler_params=_cp("parallel", "arbitrary"))(q, k, z)


def attn_bwd(q, k, z, do, *, name):
    t = q.shape[0]
    tq = _attn_tq(t)
    nd = tq // _KB
    nk = t // _KB

    def body(q_ref, k_ref, v_ref, do_ref, dq_ref, dk_ref, dv_ref, e_s, sg_s):
        qt = pl.program_id(1)
        first_head, m_suffix, m_prefix, row, key, is_first = _attn_consts(tq)

        @pl.when(qt == 0)
        def _():
            dk_ref[...] = jnp.zeros_like(dk_ref)
            dv_ref[...] = jnp.zeros_like(dv_ref)

        qv = q_ref[...]
        dov = do_ref[...]

        def halves(x):
            return jnp.where(first_head, x[:_KB], x[_KB:])

        def sweep1(kb, state, causal, r0=0):
            ca, cb = state
            off = pl.multiple_of(kb * _KB, _KB)
            kcat = _stack_heads(first_head, k_ref[pl.ds(off, _KB), :])
            vcat = _stack_heads(first_head, v_ref[pl.ds(off, _KB), :])
            zz = lax.dot_general(qv[r0:], kcat, NT_DIMS, preferred_element_type=F32)
            g, l1m = _sb_logits(zz, None if causal is None else causal[r0:])
            a = jnp.exp(g + _block_sums(l1m, m_suffix) + jnp.where(is_first[r0:], ca[r0:], cb[r0:]))
            if causal is not None:
                a = jnp.where(causal[r0:], a, 0.0)
            da = lax.dot_general(dov[r0:], vcat, NT_DIMS, preferred_element_type=F32)
            e_s[kb, r0:, :] = a * da
            sg_s[kb, r0:, :] = jnp.exp(g).astype(BF16)
            dv_ref[pl.ds(off, _KB), :] += halves(lax.dot_general(a.astype(BF16), dov[r0:], TN_DIMS,
                                                                 preferred_element_type=F32))
            sa, sb = _pair_sums(l1m)
            return _add_rows(ca, sa, r0), _add_rows(cb, sb, r0)

        def sweep2(kb, state, causal, r0=0):
            dq, pa, pb = state
            off = pl.multiple_of(kb * _KB, _KB)
            kcat = _stack_heads(first_head, k_ref[pl.ds(off, _KB), :])
            e = e_s[kb, r0:, :]
            s = sg_s[kb, r0:, :].astype(F32)
            dz = e * (1.0 - s) - (jnp.where(is_first[r0:], pa[r0:], pb[r0:]) + _block_sums(e, m_prefix)) * s
            if causal is not None:
                dz = jnp.where(causal[r0:], dz, 0.0)
            dz = dz.astype(BF16)
            dk_ref[pl.ds(off, _KB), :] += halves(lax.dot_general(dz, qv[r0:], TN_DIMS, preferred_element_type=F32))
            sa, sb = _pair_sums(e)
            return (_add_rows(dq, jnp.dot(dz, kcat, preferred_element_type=F32), r0), _add_rows(pa, sa, r0),
                    _add_rows(pb, sb, r0))

        c0 = jnp.zeros((tq, 1), F32)
        st1 = (c0, c0)
        for d in reversed(range(nd)):
            st1 = sweep1(qt * nd + d, st1, key + d * _KB < row, d * _KB)
        lax.fori_loop(0, qt * nd, lambda s, st: sweep1(qt * nd - 1 - s, st, None), st1)
        st2 = lax.fori_loop(0, qt * nd, lambda s, st: sweep2(s, st, None), (jnp.zeros((tq, 128), F32), c0, c0))
        for d in range(nd):
            st2 = sweep2(qt * nd + d, st2, key + d * _KB < row, d * _KB)
        dq_ref[...] = st2[0]

    qspec, kspec, vspec = _attn_specs(t, tq)
    acc = pl.BlockSpec((t, 128), lambda h, i: (0, h))
    return _pc(body, name=name, grid=(SB_WIDTH // 128, t // tq), in_specs=[qspec, kspec, vspec, qspec],
               out_specs=[qspec, acc, acc],
               out_shape=[_sds((t, SB_WIDTH), F32), _sds((t, SB_WIDTH), F32), _sds((t, SB_WIDTH), F32)],
               scratch_shapes=[pltpu.VMEM((nk, tq, _PAIR), F32), pltpu.VMEM((nk, tq, _PAIR), BF16)],
               compiler_params=_cp("parallel", "arbitrary"))(q, k, z, do)


def attn_post_bwd(z, dq, dk, dv, gq, gk, *, name):
    t = z.shape[0]
    tm = _pick(t, (256,))
    scale = 1.0 / math.sqrt(SB_HEAD_DIM)

    def body(q_ref, k_ref, dq_ref, dk_ref, dv_ref, gq_ref, gk_ref, dz_ref, dgq_ref, dgk_ref):
        ones = _group_ones()

        @pl.when(pl.program_id(0) == 0)
        def _():
            dgq_ref[...] = jnp.zeros_like(dgq_ref)
            dgk_ref[...] = jnp.zeros_like(dgk_ref)

        for idx, (src, d_ref, g_ref, dg_ref, mul) in enumerate(
                ((q_ref, dq_ref, gq_ref, dgq_ref, scale), (k_ref, dk_ref, gk_ref, dgk_ref, 1.0))):
            v = src[...].astype(F32)
            r = lax.rsqrt(_split_dot(v * v, ones) * (1.0 / SB_HEAD_DIM) + EPS)
            vh = v * r
            dn = d_ref[...] * mul
            dxh = dn * g_ref[...]
            m = _split_dot(dxh * vh, ones) * (1.0 / SB_HEAD_DIM)
            dz_ref[:, idx * SB_WIDTH:(idx + 1) * SB_WIDTH] = (r * (dxh - vh * m)).astype(dz_ref.dtype)
            s = _rowsum0(dn * vh)
            f = jnp.broadcast_to(s[:, 0:128] + s[:, 128:256] + s[:, 256:384] + s[:, 384:512], dg_ref.shape)
            dg_ref[...] += f + pltpu.roll(f, 64, 1)
        dz_ref[:, 2 * SB_WIDTH:] = dv_ref[...].astype(dz_ref.dtype)

    vec = pl.BlockSpec((1, SB_WIDTH), lambda i: (0, 0))
    row = pl.BlockSpec((tm, SB_WIDTH), lambda i: (i, 0))
    fold = pl.BlockSpec((8, 128), lambda i: (0, 0))
    return _pc(body, name=name, grid=(t // tm,),
               in_specs=[pl.BlockSpec((tm, SB_WIDTH), lambda i: (i, COL_C // SB_WIDTH)),
                         pl.BlockSpec((tm, SB_WIDTH), lambda i: (i, COL_C // SB_WIDTH + 1)), row, row, row, vec, vec],
               out_specs=[pl.BlockSpec((tm, 3 * SB_WIDTH), lambda i: (i, 0)), fold, fold],
               out_shape=[_sds((t, 3 * SB_WIDTH), BF16), _sds((8, 128), F32), _sds((8, 128), F32)],
               compiler_params=_cp("arbitrary"))(z, z, dq, dk, dv, gq, gk)


_GW = 512


def merge_fwd(z, ya, yb, yc, b_gate, *, name):
    t = z.shape[0]
    tm = _pick(t, (512, 256))

    def body(za_ref, zb_ref, zc_ref, ya_ref, yb_ref, yc_ref, bg_ref, o_ref):
        acc = jnp.zeros((tm, _GW), F32)
        for b, (zr, yr) in enumerate(((za_ref, ya_ref), (zb_ref, yb_ref), (zc_ref, yc_ref))):
            acc = acc + _sigmoid(zr[...].astype(F32) + bg_ref[b:b + 1, :]) * yr[...].astype(F32)
        o_ref[...] = acc.astype(o_ref.dtype)

    def zspec(b):
        return pl.BlockSpec((tm, _GW), lambda i, j: (i, COL_G // _GW + 2 * b + j))

    yspec = pl.BlockSpec((tm, _GW), lambda i, j: (i, j))
    return _pc(body, name=name, grid=(t // tm, D_MODEL // _GW),
               in_specs=[zspec(0), zspec(1), zspec(2), yspec, yspec, yspec, pl.BlockSpec((3, _GW), lambda i, j: (0, j))],
               out_specs=yspec, out_shape=_sds((t, D_MODEL), BF16),
               compiler_params=_cp("parallel", "parallel"))(z, z, z, ya, yb, yc, b_gate)


def merge_bwd(z, ya, yb, yc, b_gate, dm, *, name):
    t = z.shape[0]
    tm = _pick(t, (512, 256))

    def body(za_ref, zb_ref, zc_ref, ya_ref, yb_ref, yc_ref, bg_ref, dm_ref,
             dya_ref, dyb_ref, dyc_ref, dza_ref, dzb_ref, dzc_ref, dbg_ref):
        @pl.when(pl.program_id(1) == 0)
        def _():
            dbg_ref[...] = jnp.zeros_like(dbg_ref)

        dmv = dm_ref[...].astype(F32)
        for b, (zr, yr, dyr, dzr) in enumerate(((za_ref, ya_ref, dya_ref, dza_ref), (zb_ref, yb_ref, dyb_ref, dzb_ref),
                                                (zc_ref, yc_ref, dyc_ref, dzc_ref))):
            s = _sigmoid(zr[...].astype(F32) + bg_ref[b:b + 1, :])
            dyr[...] = (dmv * s).astype(dyr.dtype)
            dg = dmv * yr[...].astype(F32) * s * (1.0 - s)
            dzr[...] = dg.astype(dzr.dtype)
            dbg_ref[b:b + 1, :] += _rowsum0(dg)

    def zspec(b):
        return pl.BlockSpec((tm, _GW), lambda j, i: (i, COL_G // _GW + 2 * b + j))

    yspec = pl.BlockSpec((tm, _GW), lambda j, i: (i, j))
    bspec = pl.BlockSpec((3, _GW), lambda j, i: (0, j))
    full = _sds((t, D_MODEL), BF16)
    return _pc(body, name=name, grid=(D_MODEL // _GW, t // tm),
               in_specs=[zspec(0), zspec(1), zspec(2), yspec, yspec, yspec, bspec, yspec],
               out_specs=[yspec] * 6 + [bspec], out_shape=[full] * 6 + [_sds((3, D_MODEL), F32)],
               compiler_params=_cp("parallel", "arbitrary"))(z, z, z, ya, yb, yc, b_gate, dm)


_FW = 1408
_FH = D_FF // _FW


def _ffn_fill(m_ref, h_ref, x_s, i, tm):
    x_s[0:FFN_HALO, :] = jnp.where(i > 0, h_ref[...].astype(F32), 0.0)
    x_s[FFN_HALO:FFN_HALO + tm, :] = m_ref[...].astype(F32)


def _ffn_taps(x_s, tm):
    return [x_s[pl.ds(FFN_HALO - (FFN_KERNEL - 1) + k, tm), :] for k in range(FFN_KERNEL)]


def _ffn_conv(x_s, w_ref, b_ref, tm, taps=None):
    taps = _ffn_taps(x_s, tm) if taps is None else taps
    acc = jnp.zeros((tm, _FW), F32) + b_ref[...]
    for k in range(FFN_KERNEL):
        acc = acc + w_ref[k:k + 1, :] * taps[k]
    return acc


def ffn_mid_fwd(up, cw, cb, *, name):
    t = up.shape[0]
    tm = _pick(t, (256,))

    def body(gm_ref, gh_ref, vm_ref, vh_ref, wg_ref, wv_ref, bg_ref, bv_ref, o_ref, xg_s, xv_s):
        i = pl.program_id(0)
        _ffn_fill(gm_ref, gh_ref, xg_s, i, tm)
        _ffn_fill(vm_ref, vh_ref, xv_s, i, tm)
        o_ref[...] = (_silu(_ffn_conv(xg_s, wg_ref, bg_ref, tm)) * _ffn_conv(xv_s, wv_ref, bv_ref, tm)).astype(o_ref.dtype)

    def main(off):
        return pl.BlockSpec((tm, _FW), lambda i, j: (i, j + off))

    def halo(off):
        return pl.BlockSpec((FFN_HALO, _FW), lambda i, j: (jnp.maximum(i * (tm // FFN_HALO) - 1, 0), j + off))

    def wspec(off):
        return pl.BlockSpec((FFN_KERNEL, _FW), lambda i, j: (0, j + off))

    def bspec(off):
        return pl.BlockSpec((1, _FW), lambda i, j: (0, j + off))

    return _pc(body, name=name, grid=(t // tm, _FH),
               in_specs=[main(0), halo(0), main(_FH), halo(_FH), wspec(0), wspec(_FH), bspec(0), bspec(_FH)],
               out_specs=pl.BlockSpec((tm, _FW), lambda i, j: (i, j)), out_shape=_sds((t, D_FF), BF16),
               scratch_shapes=[pltpu.VMEM((FFN_HALO + tm, _FW), F32), pltpu.VMEM((FFN_HALO + tm, _FW), F32)],
               compiler_params=_cp("parallel", "parallel"))(up, up, up, up, cw, cw, cb, cb)


def ffn_mid_bwd1(up, dact, cw, cb, *, name):
    t = up.shape[0]
    tm = _pick(t, (256,))

    def body(gm_ref, gh_ref, vm_ref, vh_ref, da_ref, wg_ref, wv_ref, bg_ref, bv_ref, d_ref, dw_ref, db_ref, xg_s, xv_s):
        j = pl.program_id(0)
        i = pl.program_id(1)

        @pl.when(i == 0)
        def _():
            dw_ref[...] = jnp.zeros_like(dw_ref)
            db_ref[...] = jnp.zeros_like(db_ref)

        _ffn_fill(gm_ref, gh_ref, xg_s, i, tm)
        _ffn_fill(vm_ref, vh_ref, xv_s, i, tm)
        gate_taps = _ffn_taps(xg_s, tm)
        gate = _ffn_conv(xg_s, wg_ref, bg_ref, tm, gate_taps)
        da = da_ref[...].astype(F32)

        def finish(d, taps):
            d_ref[...] = d.astype(d_ref.dtype)
            db_ref[...] += _rowsum0(d)
            for k in range(FFN_KERNEL):
                dw_ref[k:k + 1, :] += _rowsum0(d * taps[k])

        @pl.when(j < _FH)
        def _():
            finish(da * _ffn_conv(xv_s, wv_ref, bv_ref, tm) * _silu_grad(gate), gate_taps)

        @pl.when(j >= _FH)
        def _():
            finish(da * _silu(gate), _ffn_taps(xv_s, tm))

    def main(off):
        return pl.BlockSpec((tm, _FW), lambda j, i: (i, j % _FH + off))

    def halo(off):
        return pl.BlockSpec((FFN_HALO, _FW), lambda j, i: (jnp.maximum(i * (tm // FFN_HALO) - 1, 0), j % _FH + off))

    def wspec(off):
        return pl.BlockSpec((FFN_KERNEL, _FW), lambda j, i: (0, j % _FH + off))

    def bspec(off):
        return pl.BlockSpec((1, _FW), lambda j, i: (0, j % _FH + off))

    return _pc(body, name=name, grid=(2 * _FH, t // tm),
               in_specs=[main(0), halo(0), main(_FH), halo(_FH), pl.BlockSpec((tm, _FW), lambda j, i: (i, j % _FH)),
                         wspec(0), wspec(_FH), bspec(0), bspec(_FH)],
               out_specs=[pl.BlockSpec((tm, _FW), lambda j, i: (i, j)), pl.BlockSpec((FFN_KERNEL, _FW), lambda j, i: (0, j)),
                          pl.BlockSpec((1, _FW), lambda j, i: (0, j))],
               out_shape=[_sds((t, 2 * D_FF), BF16), _sds((FFN_KERNEL, 2 * D_FF), F32), _sds((1, 2 * D_FF), F32)],
               scratch_shapes=[pltpu.VMEM((FFN_HALO + tm, _FW), F32), pltpu.VMEM((FFN_HALO + tm, _FW), F32)],
               compiler_params=_cp("parallel", "arbitrary"))(up, up, up, up, dact, cw, cw, cb, cb)


def ffn_mid_bwd2(dupc, cw, *, name):
    t = dupc.shape[0]
    tm = _pick(t, (256,))
    nsteps = t // tm
    last = t // FFN_HALO - 1

    def body(m_ref, h_ref, w_ref, o_ref, y_s):
        i = pl.program_id(0)
        y_s[0:tm, :] = m_ref[...].astype(F32)
        y_s[tm:tm + FFN_HALO, :] = jnp.where(i < nsteps - 1, h_ref[...].astype(F32), 0.0)
        acc = jnp.zeros((tm, _FW), F32)
        for k in range(FFN_KERNEL):
            acc = acc + w_ref[k:k + 1, :] * y_s[pl.ds(FFN_KERNEL - 1 - k, tm), :]
        o_ref[...] = acc.astype(o_ref.dtype)

    return _pc(body, name=name, grid=(nsteps, 2 * _FH),
               in_specs=[pl.BlockSpec((tm, _FW), lambda i, j: (i, j)),
                         pl.BlockSpec((FFN_HALO, _FW), lambda i, j: (jnp.minimum((i + 1) * (tm // FFN_HALO), last), j)),
                         pl.BlockSpec((FFN_KERNEL, _FW), lambda i, j: (0, j))],
               out_specs=pl.BlockSpec((tm, _FW), lambda i, j: (i, j)), out_shape=_sds((t, 2 * D_FF), BF16),
               scratch_shapes=[pltpu.VMEM((tm + FFN_HALO, _FW), F32)],
               compiler_params=_cp("parallel", "parallel"))(dupc, dupc, cw)


def _vec(v):
    return v.reshape(1, -1)


def _layer_consts(p):
    return dict(
        sg_bias=jnp.repeat(p['sg_b'].T, SB_HEAD_DIM, axis=1),
        sg_wt=jnp.swapaxes(p['sg_w'], 1, 2),
        gq=jnp.tile(p['q_norm_g'], SB_WIDTH // SB_HEAD_DIM).reshape(1, -1),
        gk=jnp.tile(p['k_norm_g'], SB_WIDTH // SB_HEAD_DIM).reshape(1, -1),
    )


def layer_fwd(x, p, after=()):
    c = _layer_consts(p)
    z, h = mm_norm_nn(x, _vec(p['ln1_g']), p['w_in'], name="in_proj", after=after)
    ga = mixa_fwd(z, _vec(p['sg_ln_g']), _vec(p['sg_ln_b']), p['sg_w'], c['sg_bias'], name="mixa_fwd")
    cb = mixb_fwd(z, p['cv_w'], _vec(p['cv_b']), _vec(p['cv_ln_g']), _vec(p['cv_ln_b']), name="mixb_fwd")
    q, k = attn_prep(z, c['gq'], c['gk'], name="attn_prep")
    ao = attn_fwd(q, k, z, name="attn_fwd")
    ya = mm_nn(ga, p['w_a_out'], name="a_out")
    yb = mm_nn(cb, p['w_b_out'], name="b_out")
    yc = mm_nn(ao, p['w_c_out'], name="c_out")
    merged = merge_fwd(z, ya, yb, yc, p['b_gate'], name="merge_fwd")
    x1 = mm_nn(merged, p['w_out'], res=x, out_dtype=F32, name="out_proj")
    up, h2 = mm_norm_nn(x1, _vec(p['ln2_g']), p['w_up'], name="up_proj")
    act = ffn_mid_fwd(up, p['ffn_conv_w'], _vec(p['ffn_conv_b']), name="ffn_mid_fwd")
    x2 = mm_nn(act, p['w_down'], res=x1, out_dtype=F32, name="down_proj")
    saved = dict(x=x, z=z, h=h, ga=ga, cb=cb, q=q, k=k, ao=ao, ya=ya, yb=yb, yc=yc, merged=merged, x1=x1, up=up,
                 h2=h2, act=act)
    return x2, saved


def layer_bwd(dx2, p, s, after=()):
    c = _layer_consts(p)
    g = {}
    g['w_down'] = mm_tn(s['act'], dx2, name="d_w_down", after=after)
    dact = mm_nt(dx2, p['w_down'], out_dtype=BF16, name="d_act")
    dupc, g['ffn_conv_w'], dcb = ffn_mid_bwd1(s['up'], dact, p['ffn_conv_w'], _vec(p['ffn_conv_b']), name="ffn_mid_bwd1")
    g['ffn_conv_b'] = dcb.reshape(-1)
    dup = ffn_mid_bwd2(dupc, p['ffn_conv_w'], name="ffn_mid_bwd2")
    g['w_up'] = mm_tn(s['h2'], dup, name="d_w_up")
    dh2 = mm_nt(dup, p['w_up'], out_dtype=F32, name="d_h2")
    dx1, dg2 = rms_bwd(dh2, s['x1'], _vec(p['ln2_g']), dx2, name="ln2_bwd")
    g['ln2_g'] = dg2.reshape(-1)
    g['w_out'] = mm_tn(s['merged'], dx1, name="d_w_out")
    dm = mm_nt(dx1, p['w_out'], out_dtype=BF16, name="d_merged")
    dya, dyb, dyc, dzg0, dzg1, dzg2, g['b_gate'] = merge_bwd(s['z'], s['ya'], s['yb'], s['yc'], p['b_gate'], dm,
                                                             name="merge_bwd")
    g['w_a_out'] = mm_tn(s['ga'], dya, name="d_w_a_out")
    g['w_b_out'] = mm_tn(s['cb'], dyb, name="d_w_b_out")
    g['w_c_out'] = mm_tn(s['ao'], dyc, name="d_w_c_out")
    dga = mm_nt(dya, p['w_a_out'], out_dtype=BF16, name="d_ga")
    dcb3 = mm_nt(dyb, p['w_b_out'], out_dtype=BF16, name="d_cb")
    dao = mm_nt(dyc, p['w_c_out'], out_dtype=BF16, name="d_ao")
    dza, g['sg_w'], dsgb, dlg, dlb = mixa_bwd(s['z'], dga, _vec(p['sg_ln_g']), _vec(p['sg_ln_b']), p['sg_w'],
                                               c['sg_wt'], c['sg_bias'], name="mixa_bwd")
    g['sg_b'] = dsgb[:, :SG_WIDTH // SB_HEAD_DIM].T
    g['sg_ln_g'] = dlg.reshape(-1)
    g['sg_ln_b'] = dlb.reshape(-1)
    dc1, g['cv_w'], dcvb, dcg, dcbb = mixb_bwd1(s['z'], dcb3, p['cv_w'], _vec(p['cv_b']), _vec(p['cv_ln_g']),
                                                _vec(p['cv_ln_b']), name="mixb_bwd1")
    g['cv_b'] = dcvb.reshape(-1)
    g['cv_ln_g'] = dcg.reshape(-1)
    g['cv_ln_b'] = dcbb.reshape(-1)
    dzb = mixb_bwd2(s['z'], dc1, p['cv_w'], name="mixb_bwd2")
    dq, dk, dv = attn_bwd(s['q'], s['k'], s['z'], dao, name="attn_bwd")
    dzc, dgq, dgk = attn_post_bwd(s['z'], dq, dk, dv, c['gq'], c['gk'], name="attn_post_bwd")
    g['q_norm_g'] = dgq[0, :SB_HEAD_DIM]
    g['k_norm_g'] = dgk[0, :SB_HEAD_DIM]
    dz = jnp.concatenate([dza, dzb, dzc, dzg0, dzg1, dzg2], axis=1)
    g['w_in'] = mm_tn(s['h'], dz, name="d_w_in")
    dh = mm_nt(dz, p['w_in'], out_dtype=F32, name="d_h")
    dx, dg1 = rms_bwd(dh, s['x'], _vec(p['ln1_g']), dx1, name="ln1_bwd")
    g['ln1_g'] = dg1.reshape(-1)
    return dx, g


def local_step(x, target, depth, get_layer, on_grads):
    saved, layers = [], []
    for l in range(depth):
        p, after = get_layer(l, x)
        x, s = layer_fwd(x, p, after)
        layers.append(p)
        saved.append(s)
    loss, dx = loss_head(x, target, name="loss_head")
    after = ()
    for l in reversed(range(depth)):
        dx, g = layer_bwd(dx, layers[l], saved[l], after)
        after = on_grads(l, g)
    return loss[0, 0], dx


def adamw(w, g, m, v, *, name):
    r, c = w.shape
    tr = _pick(r, (256, 704)) if r * c > 512 * 1024 else r

    def body(w_ref, g_ref, m_ref, v_ref, d_ref, mo_ref, vo_ref):
        gv = g_ref[...]
        mn = ADAM_B1 * m_ref[...] + (1.0 - ADAM_B1) * gv
        vn = ADAM_B2 * v_ref[...] + (1.0 - ADAM_B2) * (gv * gv)
        m_hat = mn / (1.0 - ADAM_B1 ** ADAM_STEP)
        v_hat = vn / (1.0 - ADAM_B2 ** ADAM_STEP)
        d_ref[...] = -ADAM_LR * (m_hat / (jnp.sqrt(v_hat) + ADAM_EPS) + ADAM_WD * w_ref[...])
        mo_ref[...] = mn
        vo_ref[...] = vn

    spec = pl.BlockSpec((tr, c), lambda i: (i, 0))
    out = _sds((r, c), F32)
    return _pc(body, name=name, grid=(r // tr,), in_specs=[spec] * 4, out_specs=[spec] * 3, out_shape=[out] * 3,
               compiler_params=_cp("parallel"))(w, g, m, v)


def _as3(a):
    return a if a.ndim == 3 else a.reshape((1,) + a.shape)


def add_half(g, recv, c_idx, *, name):
    s, rh, w = recv.shape
    tr = _pick(rh, (256, 352, 128))
    nb = rh // tr

    def body(c_ref, g_ref, r_ref, o_ref):
        o_ref[...] = (g_ref[...].astype(F32) + r_ref[...].astype(F32)).astype(o_ref.dtype)

    own = pl.BlockSpec((1, tr, w), lambda k, i, c_ref: (k, c_ref[0] * nb + i, 0))
    half = pl.BlockSpec((1, tr, w), lambda k, i, c_ref: (k, i, 0))
    gs = pltpu.PrefetchScalarGridSpec(num_scalar_prefetch=1, grid=(s, nb), in_specs=[own, half], out_specs=half)
    return _pc(body, name=name, grid_spec=gs, out_shape=_sds((s, rh, w), BF16),
               compiler_params=_cp("parallel", "parallel"))(c_idx, g, recv)


def sum_shard(p, recv, pos_idx, *, by_rows, name):
    _, rh, w = recv.shape
    tr = _pick(rh, (256, 352, 128))
    nb = rh // tr

    def body(pos_ref, p_ref, r_ref, o_ref):
        acc = p_ref[0].astype(F32)
        for j in range(N_CHIPS - 1):
            acc = acc + r_ref[j].astype(F32)
        o_ref[...] = acc

    if by_rows:
        own = pl.BlockSpec((1, tr, w), lambda i, pos_ref: (pos_ref[0], i, 0))
    else:
        own = pl.BlockSpec((1, tr, w), lambda i, pos_ref: (0, i, pos_ref[0]))
    gs = pltpu.PrefetchScalarGridSpec(num_scalar_prefetch=1, grid=(nb,),
                                      in_specs=[own, pl.BlockSpec((N_CHIPS - 1, tr, w), lambda i, pos_ref: (0, i, 0))],
                                      out_specs=pl.BlockSpec((tr, w), lambda i, pos_ref: (pos_ref[1] * nb + i, 0)))
    return _pc(body, name=name, grid_spec=gs, out_shape=_sds((2 * rh, w), F32),
               compiler_params=_cp("parallel"))(pos_idx, p, recv)


def sum_slots(slab, *, name):
    _, r, w = slab.shape
    tr = _pick(r, (512, 256, 8))

    def body(s_ref, o_ref):
        acc = s_ref[0]
        for j in range(1, N_DEV):
            acc = acc + s_ref[j]
        o_ref[...] = acc

    return _pc(body, name=name, grid=(r // tr,), in_specs=[pl.BlockSpec((N_DEV, tr, w), lambda i: (0, i, 0))],
               out_specs=pl.BlockSpec((tr, w), lambda i: (i, 0)), out_shape=_sds((r, w), F32),
               compiler_params=_cp("parallel"))(slab)


def _mesh_pos():
    x, y, c = lax.axis_index("x"), lax.axis_index("y"), lax.axis_index("c")
    others = [(1 - x, y), (x, 1 - y), (1 - x, 1 - y)]
    return x, y, c, others


def _rcopy(src, dst, ssem, rsem, k, dev):
    return pltpu.make_async_remote_copy(src_ref=src, dst_ref=dst, send_sem=ssem.at[k], recv_sem=rsem.at[k],
                                        device_id=dev, device_id_type=MESH)


def _comm_call(body, name, n_in, out_shape, n_local, n_remote):
    scratch = [pltpu.SemaphoreType.DMA((max(n_local, 1),)), pltpu.SemaphoreType.DMA((n_remote,)),
               pltpu.SemaphoreType.DMA((n_remote,))]
    return _pc(body, name=name, in_specs=[ANY] * n_in, out_specs=[ANY] * len(out_shape), out_shape=out_shape,
               scratch_shapes=scratch)


GATHERED = BIG + SMALL_COL
HBM_SPEC = pl.BlockSpec(memory_space=pltpu.HBM)
SEM_SPEC = pl.BlockSpec(memory_space=pltpu.SEMAPHORE)
TOKEN_SHAPE = (8, 128)


def place_block(w, layer, pos_idx, *, by_rows, dtype, name):
    _, r, c = w.shape
    tr = _pick(r, (512, 704, 256))

    def body(pos_ref, w_ref, o_ref):
        if by_rows:
            o_ref[0] = w_ref[0].astype(dtype)
        else:
            o_ref[...] = w_ref[0].astype(dtype)

    if by_rows:
        out_spec, shape = pl.BlockSpec((1, tr, c), lambda i, pos_ref: (pos_ref[0], i, 0)), (N_CHIPS, r, c)
    else:
        out_spec, shape = pl.BlockSpec((tr, c), lambda i, pos_ref: (i, pos_ref[0])), (r, N_CHIPS * c)
    gs = pltpu.PrefetchScalarGridSpec(num_scalar_prefetch=1, grid=(r // tr,),
                                      in_specs=[pl.BlockSpec((1, tr, c), lambda i, pos_ref: (layer, i, 0))],
                                      out_specs=out_spec)
    return _pc(body, name=name, grid_spec=gs, out_shape=_sds(shape, dtype), compiler_params=_cp("parallel"))(pos_idx, w)


def _gather_windows(bufs):
    def dwin(refs, i, k, h):
        if GATHERED[i] in BIG_ROW:
            _, r, _ = bufs[i].shape
            return refs[i].at[k] if h is None else refs[i].at[k, pl.ds(h * (r // 2), r // 2), :]
        r, cs = bufs[i].shape[0], bufs[i].shape[1] // N_CHIPS
        cols = pl.ds(pl.multiple_of(k * cs, 128), cs)
        return refs[i].at[:, cols] if h is None else refs[i].at[pl.ds(h * (r // 2), r // 2), cols]

    def swin(refs, i, h):
        x, y, _, _ = _mesh_pos()
        return dwin(refs, i, 2 * x + y, h)

    return dwin, swin


def _gather_send(ins, outs, ssem, rsem, dwin, swin, stride):
    x, y, c, others = _mesh_pos()
    sends = []
    for i, n in enumerate(GATHERED):
        h = c if n in BIG else None
        for j, chip in enumerate(others):
            cp = _rcopy(swin(ins, i, h), swin(outs, i, h), ssem, rsem, stride * i + j, (*chip, c))
            cp.start()
            sends.append(cp)
    return sends


def _gather_pass_on(outs, ssem, rsem, dwin, stride, first_off, pass_off):
    x, y, c, others = _mesh_pos()
    sib = (x, y, 1 - c)
    sends = []
    for j, chip in enumerate(others):
        kk = 2 * chip[0] + chip[1]
        for i, n in enumerate(GATHERED):
            got = dwin(outs, i, kk, c if n in BIG else None)
            if first_off is not None:
                _rcopy(got, got, ssem, rsem, stride * i + first_off + j, (*chip, c)).wait_recv()
            if n in BIG:
                fwd = _rcopy(got, got, ssem, rsem, stride * i + pass_off + j, sib)
                fwd.start()
                sends.append(fwd)
    for j, chip in enumerate(others):
        kk = 2 * chip[0] + chip[1]
        for i, n in enumerate(GATHERED):
            if n in BIG:
                got = dwin(outs, i, kk, 1 - c)
                _rcopy(got, got, ssem, rsem, stride * i + pass_off + j, sib).wait_recv()
    return sends


def _name_full(outs):
    return {n: (o.reshape(o.shape[0] * o.shape[1], o.shape[2]) if n in BIG_ROW else o) for n, o in zip(GATHERED, outs)}


def _comm_in_place(body, name, bufs, n_sems):
    nn = len(bufs)
    scratch = [pltpu.SemaphoreType.DMA((n_sems,)), pltpu.SemaphoreType.DMA((n_sems,))]
    return _pc(body, name=name, in_specs=[ANY] * nn, out_specs=[ANY] * nn, out_shape=[_sds(a.shape, a.dtype) for a in bufs],
               scratch_shapes=scratch, input_output_aliases={i: i for i in range(nn)})(*bufs)


def allgather_weights(bufs, *, name):
    nn = len(GATHERED)
    dwin, swin = _gather_windows(bufs)

    def body(*refs):
        ins, outs = refs[:nn], refs[nn:2 * nn]
        ssem, rsem = refs[2 * nn:]
        sends = _gather_send(ins, outs, ssem, rsem, dwin, swin, 6)
        sends += _gather_pass_on(outs, ssem, rsem, dwin, 6, 0, 3)
        for cp in sends:
            cp.wait_send()

    return _comm_in_place(body, name, bufs, 6 * nn)


def gather_start(bufs, after, *, name):
    nn = len(GATHERED)
    dwin, swin = _gather_windows(bufs)

    def body(*refs):
        ins = refs[:nn]
        ssem, rsem = refs[nn + len(after)], refs[nn + len(after) + 1]
        token = refs[-1]
        _gather_send(ins, ins, ssem, rsem, dwin, swin, 3)
        token[...] = jnp.zeros_like(token)

    hbm = lambda a: pltpu.with_memory_space_constraint(a, pltpu.HBM)
    out_shape = ([pltpu.SemaphoreType.DMA((3 * nn,)), pltpu.SemaphoreType.DMA((3 * nn,))]
                 + [pltpu.HBM(a.shape, a.dtype) for a in bufs] + [_sds(TOKEN_SHAPE, F32)])
    outs = _pc(body, name=name, in_specs=[HBM_SPEC] * nn + [ANY] * len(after),
               out_specs=[SEM_SPEC, SEM_SPEC] + [HBM_SPEC] * nn + [pl.BlockSpec(memory_space=pltpu.VMEM)],
               out_shape=out_shape, input_output_aliases={i: 2 + i for i in range(nn)},
               compiler_params=pltpu.CompilerParams(has_side_effects=pltpu.SideEffectType.DATAFLOW_SIDE_EFFECTING),
               )(*[hbm(a) for a in bufs], *after)
    return dict(ssem=outs[0], rsem=outs[1], bufs=outs[2:2 + nn], token=outs[-1])


def gather_wait(handle, after, *, name):
    nn = len(GATHERED)
    bufs = handle['bufs']
    dwin, swin = _gather_windows(bufs)

    def body(*refs):
        ins = refs[:nn]
        ssem, rsem = refs[nn], refs[nn + 1]
        x, y, c, others = _mesh_pos()
        for i, n in enumerate(GATHERED):
            h = c if n in BIG else None
            for j, chip in enumerate(others):
                kk = 2 * chip[0] + chip[1]
                cp = _rcopy(swin(ins, i, h), dwin(ins, i, kk, h), ssem, rsem, 3 * i + j, (*chip, c))
                cp.wait_send()
                cp.wait_recv()

    return _pc(body, name=name, in_specs=[HBM_SPEC] * nn + [SEM_SPEC, SEM_SPEC] + [ANY] * len(after),
               out_specs=[HBM_SPEC] * nn, out_shape=[pltpu.HBM(a.shape, a.dtype) for a in bufs],
               input_output_aliases={i: i for i in range(nn)},
               compiler_params=pltpu.CompilerParams(has_side_effects=pltpu.SideEffectType.DATAFLOW_SIDE_EFFECTING),
               )(*bufs, handle['ssem'], handle['rsem'], *after)


def gather_finish(bufs, *, name):
    nn = len(GATHERED)
    dwin, _ = _gather_windows(bufs)

    def body(*refs):
        outs = refs[nn:2 * nn]
        ssem, rsem = refs[2 * nn:]
        for cp in _gather_pass_on(outs, ssem, rsem, dwin, 3, None, 0):
            cp.wait_send()

    return _comm_in_place(body, name, bufs, 3 * nn)


def _grad_view(n, g):
    return g.reshape(N_CHIPS, g.shape[0] // N_CHIPS, g.shape[1]) if n in BIG_ROW else g.reshape((1,) + g.shape)


def exchange_halves(gv, *, name):
    nn = len(BIG)

    def body(*refs):
        ins, outs = refs[:nn], refs[nn:2 * nn]
        _, ssem, rsem = refs[2 * nn:]
        x, y, c, _ = _mesh_pos()
        cps = []
        for i in range(nn):
            rh = gv[i].shape[1] // 2
            cp = _rcopy(ins[i].at[:, pl.ds((1 - c) * rh, rh), :], outs[i], ssem, rsem, i, (x, y, 1 - c))
            cp.start()
            cps.append(cp)
        for cp in cps:
            cp.wait()

    out_shape = [_sds((a.shape[0], a.shape[1] // 2, a.shape[2]), a.dtype) for a in gv]
    return _comm_call(body, name, nn, out_shape, 0, nn)(*gv)


def _shard_shape(n, p):
    _, rh, w = p.shape
    return (rh, w) if n in BIG_ROW else (rh, w // N_CHIPS)


def _scatter_copies(pv, ins, outs, ssem, rsem):
    x, y, c, others = _mesh_pos()
    cps = []
    for i, n in enumerate(BIG):
        _, ws = _shard_shape(n, pv[i])
        for j, chip in enumerate(others):
            kk = 2 * chip[0] + chip[1]
            if n in BIG_ROW:
                src = ins[i].at[kk]
            else:
                src = ins[i].at[0, :, pl.ds(pl.multiple_of(kk * ws, 128), ws)]
            cps.append(_rcopy(src, outs[i].at[j], ssem, rsem, 3 * i + j, (*chip, c)))
    return cps


def _recv_shapes(pv):
    return [(N_CHIPS - 1,) + _shard_shape(n, p) for n, p in zip(BIG, pv)]


def scatter_partials(pv, *, name):
    nn = len(BIG)

    def body(*refs):
        ins, outs = refs[:nn], refs[nn:2 * nn]
        _, ssem, rsem = refs[2 * nn:]
        cps = _scatter_copies(pv, ins, outs, ssem, rsem)
        for cp in cps:
            cp.start()
        for cp in cps:
            cp.wait()

    out_shape = [_sds(s, p.dtype) for s, p in zip(_recv_shapes(pv), pv)]
    return pv, _comm_call(body, name, nn, out_shape, 0, 3 * nn)(*pv)


def scatter_start(pv, *, name):
    nn = len(BIG)

    def body(*refs):
        ins, lands = refs[:nn], refs[nn:2 * nn]
        ssem, rsem = refs[2 * nn], refs[2 * nn + 1]
        token = refs[-1]
        for cp in _scatter_copies(pv, ins, lands, ssem, rsem):
            cp.start()
        token[...] = jnp.zeros_like(token)

    hbm = lambda a: pltpu.with_memory_space_constraint(a, pltpu.HBM)
    lands = [hbm(lax.empty(s, p.dtype)) for s, p in zip(_recv_shapes(pv), pv)]
    out_shape = ([pltpu.SemaphoreType.DMA((3 * nn,)), pltpu.SemaphoreType.DMA((3 * nn,))]
                 + [pltpu.HBM(p.shape, p.dtype) for p in pv] + [pltpu.HBM(a.shape, a.dtype) for a in lands]
                 + [_sds(TOKEN_SHAPE, F32)])
    outs = _pc(body, name=name, in_specs=[HBM_SPEC] * (2 * nn),
               out_specs=[SEM_SPEC, SEM_SPEC] + [HBM_SPEC] * (2 * nn) + [pl.BlockSpec(memory_space=pltpu.VMEM)],
               out_shape=out_shape, input_output_aliases={i: 2 + i for i in range(2 * nn)},
               compiler_params=pltpu.CompilerParams(has_side_effects=pltpu.SideEffectType.DATAFLOW_SIDE_EFFECTING),
               )(*[hbm(p) for p in pv], *lands)
    return dict(ssem=outs[0], rsem=outs[1], pv=outs[2:2 + nn], lands=outs[2 + nn:2 + 2 * nn], token=outs[-1])


def scatter_wait(handle, after, *, name):
    nn = len(BIG)
    pv, lands = handle['pv'], handle['lands']

    def body(*refs):
        ins, zones = refs[:nn], refs[nn:2 * nn]
        ssem, rsem = refs[2 * nn], refs[2 * nn + 1]
        for cp in _scatter_copies(pv, ins, zones, ssem, rsem):
            cp.wait_send()
            cp.wait_recv()

    outs = _pc(body, name=name, in_specs=[HBM_SPEC] * (2 * nn) + [SEM_SPEC, SEM_SPEC] + [ANY] * len(after),
               out_specs=[HBM_SPEC] * (2 * nn),
               out_shape=[pltpu.HBM(p.shape, p.dtype) for p in pv] + [pltpu.HBM(a.shape, a.dtype) for a in lands],
               input_output_aliases={i: i for i in range(2 * nn)},
               compiler_params=pltpu.CompilerParams(has_side_effects=pltpu.SideEffectType.DATAFLOW_SIDE_EFFECTING),
               )(*pv, *lands, handle['ssem'], handle['rsem'], *after)
    return outs[:nn], outs[nn:]


def join_halves(rv, *, name):
    nn = len(BIG)

    def body(*refs):
        ins, outs = refs[:nn], refs[nn:2 * nn]
        _, ssem, rsem = refs[2 * nn:]
        x, y, c, _ = _mesh_pos()
        cps = []
        for i in range(nn):
            rh = rv[i].shape[0] // 2
            rows = pl.ds(c * rh, rh)
            cp = _rcopy(ins[i].at[rows, :], outs[i].at[rows, :], ssem, rsem, i, (x, y, 1 - c))
            cp.start()
            cps.append(cp)
        for i, cp in enumerate(cps):
            cp.wait_send()
            rh = rv[i].shape[0] // 2
            got = outs[i].at[pl.ds((1 - c) * rh, rh), :]
            _rcopy(got, got, ssem, rsem, i, (x, y, 1 - c)).wait_recv()

    out_shape = [_sds(a.shape, a.dtype) for a in rv]
    scratch = [pltpu.SemaphoreType.DMA((1,)), pltpu.SemaphoreType.DMA((nn,)), pltpu.SemaphoreType.DMA((nn,))]
    return _pc(body, name=name, in_specs=[ANY] * nn, out_specs=[ANY] * nn, out_shape=out_shape, scratch_shapes=scratch,
               input_output_aliases={i: i for i in range(nn)})(*rv)


def chip_partials(grads, c_idx):
    gv = [_grad_view(n, grads[n]) for n in BIG]
    recv = exchange_halves(gv, name="rs_exchange_halves")
    return [add_half(g, r, c_idx, name="rs_add_" + n) for n, g, r in zip(BIG, gv, recv)]


def reduce_shards(pv, got, pos_idx):
    rv = [sum_shard(p, r, pos_idx, by_rows=n in BIG_ROW, name="rs_sum_" + n) for n, p, r in zip(BIG, pv, got)]
    return dict(zip(BIG, join_halves(rv, name="rs_join_halves")))


def allgather_slab(slab, *, name):
    def body(in_ref, out_ref, ssem, rsem):
        x, y, c, others = _mesh_pos()
        me, sib = (x, y, c), (x, y, 1 - c)

        def slot(px, py, pc):
            return out_ref.at[4 * px + 2 * py + pc]

        x_ref = in_ref.at[4 * x + 2 * y + c]
        first = [_rcopy(x_ref, slot(*me), ssem, rsem, 0, sib)]
        first += [_rcopy(x_ref, slot(*me), ssem, rsem, 1 + j, (*chip, c)) for j, chip in enumerate(others)]
        for cp in first:
            cp.start()
        passed = [_rcopy(slot(*chip, c), slot(*chip, c), ssem, rsem, 4 + j, sib) for j, chip in enumerate(others)]
        for j, chip in enumerate(others):
            _rcopy(slot(*chip, c), slot(*chip, c), ssem, rsem, 1 + j, me).wait_recv()
            passed[j].start()
        _rcopy(slot(*sib), slot(*sib), ssem, rsem, 0, me).wait_recv()
        for j, chip in enumerate(others):
            _rcopy(slot(*chip, 1 - c), slot(*chip, 1 - c), ssem, rsem, 4 + j, me).wait_recv()
        for cp in first + passed:
            cp.wait_send()

    return _comm_in_place(body, name, [slab], 7)[0]


def _pad128(n):
    return -(-n // 128) * 128


def _pack_small(grads, shapes):
    parts = []
    for g in grads:
        for n in SMALL:
            v = g[n].astype(F32).reshape(-1)
            parts.append(jnp.pad(v, (0, _pad128(v.shape[0]) - v.shape[0])))
    flat = jnp.concatenate(parts)
    rows = -(-flat.shape[0] // (128 * 512)) * 512
    return jnp.pad(flat, (0, rows * 128 - flat.shape[0])).reshape(rows, 128)


def _unpack_small(slab, shapes, depth):
    flat = slab.reshape(-1)
    out = {n: [] for n in SMALL}
    off = 0
    for _ in range(depth):
        for n in SMALL:
            size = math.prod(shapes[n])
            out[n].append(flat[off:off + size].reshape(shapes[n]))
            off += _pad128(size)
    return {n: jnp.stack(v) for n, v in out.items()}


def _adamw_nd(w, g, m, v, name):
    shp = w.shape
    two = lambda a: a.reshape(-1, shp[-1])
    return tuple(o.reshape(shp) for o in adamw(two(w), two(g), two(m), two(v), name=name))


def kernel(x, ln1_g, w_in, b_gate, sg_ln_g, sg_ln_b, sg_w, sg_b, w_a_out, cv_w, cv_b, cv_ln_g, cv_ln_b, w_b_out, q_norm_g, k_norm_g, w_c_out, w_out, ln2_g, w_up, ffn_conv_w, ffn_conv_b, w_down, loss_target, m_ln1_g, m_w_in, m_b_gate, m_sg_ln_g, m_sg_ln_b, m_sg_w, m_sg_b, m_w_a_out, m_cv_w, m_cv_b, m_cv_ln_g, m_cv_ln_b, m_w_b_out, m_q_norm_g, m_k_norm_g, m_w_c_out, m_w_out, m_ln2_g, m_w_up, m_ffn_conv_w, m_ffn_conv_b, m_w_down, v_ln1_g, v_w_in, v_b_gate, v_sg_ln_g, v_sg_ln_b, v_sg_w, v_sg_b, v_w_a_out, v_cv_w, v_cv_b, v_cv_ln_g, v_cv_ln_b, v_w_b_out, v_q_norm_g, v_k_norm_g, v_w_c_out, v_w_out, v_ln2_g, v_w_up, v_ffn_conv_w, v_ffn_conv_b, v_w_down):
    w = dict(ln1_g=ln1_g, w_in=w_in, b_gate=b_gate, sg_ln_g=sg_ln_g, sg_ln_b=sg_ln_b, sg_w=sg_w, sg_b=sg_b,
             w_a_out=w_a_out, cv_w=cv_w, cv_b=cv_b, cv_ln_g=cv_ln_g, cv_ln_b=cv_ln_b, w_b_out=w_b_out,
             q_norm_g=q_norm_g, k_norm_g=k_norm_g, w_c_out=w_c_out, w_out=w_out, ln2_g=ln2_g, w_up=w_up,
             ffn_conv_w=ffn_conv_w, ffn_conv_b=ffn_conv_b, w_down=w_down)
    m = dict(ln1_g=m_ln1_g, w_in=m_w_in, b_gate=m_b_gate, sg_ln_g=m_sg_ln_g, sg_ln_b=m_sg_ln_b, sg_w=m_sg_w,
             sg_b=m_sg_b, w_a_out=m_w_a_out, cv_w=m_cv_w, cv_b=m_cv_b, cv_ln_g=m_cv_ln_g, cv_ln_b=m_cv_ln_b,
             w_b_out=m_w_b_out, q_norm_g=m_q_norm_g, k_norm_g=m_k_norm_g, w_c_out=m_w_c_out, w_out=m_w_out,
             ln2_g=m_ln2_g, w_up=m_w_up, ffn_conv_w=m_ffn_conv_w, ffn_conv_b=m_ffn_conv_b, w_down=m_w_down)
    v = dict(ln1_g=v_ln1_g, w_in=v_w_in, b_gate=v_b_gate, sg_ln_g=v_sg_ln_g, sg_ln_b=v_sg_ln_b, sg_w=v_sg_w,
             sg_b=v_sg_b, w_a_out=v_w_a_out, cv_w=v_cv_w, cv_b=v_cv_b, cv_ln_g=v_cv_ln_g, cv_ln_b=v_cv_ln_b,
             w_b_out=v_w_b_out, q_norm_g=v_q_norm_g, k_norm_g=v_k_norm_g, w_c_out=v_w_c_out, w_out=v_w_out,
             ln2_g=v_ln2_g, w_up=v_w_up, ffn_conv_w=v_ffn_conv_w, ffn_conv_b=v_ffn_conv_b, w_down=v_w_down)
    depth = ln1_g.shape[0]
    cx, cy, cc = lax.axis_index("x"), lax.axis_index("y"), lax.axis_index("c")
    me = 2 * cx + cy
    pos_idx = jnp.stack([me, cc]).astype(jnp.int32)
    c_idx = jnp.reshape(cc, (1,)).astype(jnp.int32)

    padded = {n: jnp.pad(w[n], ((0, 0), (0, -w[n].shape[1] % 8), (0, 0))) for n in SMALL_COL}

    def blocks(l):
        out = [place_block(w[n], l, pos_idx, by_rows=n in BIG_ROW, dtype=BF16, name="place_" + n) for n in BIG]
        return out + [place_block(padded[n], l, pos_idx, by_rows=False, dtype=F32, name="place_" + n) for n in SMALL_COL]

    full0 = allgather_weights(blocks(0), name="allgather_weights")
    gathers, prev = [], full0[:1]
    for l in range(1, depth):
        gathers.append(gather_start(blocks(l), prev, name="gather_start_%d" % l))
        prev = [gathers[-1]['token']]

    def get_layer(l, x_in):
        if l == 0:
            p, after = _name_full(full0), tuple(h['token'] for h in gathers)
        else:
            bufs = gather_wait(gathers[l - 1], [x_in], name="gather_wait_%d" % l)
            p, after = _name_full(gather_finish(bufs, name="gather_finish")), ()
        for n in SMALL:
            p[n] = p[n][:w[n].shape[1]] if n in SMALL_COL else w[n][l]
        return p, after

    grads, scatters = [None] * depth, [None] * depth

    def on_grads(l, g):
        grads[l] = g
        pv = chip_partials(g, c_idx)
        if l == 0:
            scatters[l] = scatter_partials(pv, name="rs_scatter_partials")
            return ()
        scatters[l] = scatter_start(pv, name="scatter_start_%d" % l)
        return (scatters[l]['token'],)

    loss, dx = local_step(x[0], loss_target[0], depth, get_layer, on_grads)
    loss = lax.psum(loss, ("x", "y", "c"))
    big = [None] * depth
    for l in range(depth):
        pv, got = scatters[l] if l == 0 else scatter_wait(scatters[l], [dx], name="scatter_wait_%d" % l)
        big[l] = reduce_shards(pv, got, pos_idx)
    full_shapes = {n: (w[n].shape[1], N_CHIPS * w[n].shape[2]) if n in SMALL_COL else w[n].shape[1:] for n in SMALL}
    mine = _pack_small(grads, full_shapes)
    slots = lax.dynamic_update_slice(lax.empty((N_DEV,) + mine.shape, F32), mine[None], (4 * cx + 2 * cy + cc, 0, 0))
    slab = sum_slots(allgather_slab(slots, name="allgather_small_grads"), name="sum_small_grads")
    small = _unpack_small(slab, full_shapes, depth)
    grad = {n: jnp.stack([b[n] for b in big]) for n in BIG}
    for n in SMALL:
        if n in SMALL_COL:
            cs = w[n].shape[-1]
            grad[n] = lax.dynamic_slice_in_dim(small[n], me * cs, cs, axis=small[n].ndim - 1)
        else:
            grad[n] = small[n]

    delta, new_m, new_v = {}, {}, {}
    for n in WEIGHTS:
        delta[n], new_m[n], new_v[n] = _adamw_nd(w[n], grad[n], m[n], v[n], "adamw_" + n)
    return (loss, dx[None], *[grad[n] for n in WEIGHTS], *[delta[n] for n in WEIGHTS],
            *[new_m[n] for n in WEIGHTS], *[new_v[n] for n in WEIGHTS])
```

```python
import functools
import math

import jax
import jax.numpy as jnp
from jax import lax
from jax.experimental import pallas as pl
from jax.experimental.pallas import tpu as pltpu

F32 = jnp.float32
BF16 = jnp.bfloat16
MESH = pl.DeviceIdType.MESH
ANY = pl.BlockSpec(memory_space=pl.ANY)

EPS = 1e-6
D_MODEL = 1024
DEPTH = 4
SG_WIDTH = 512
CHUNK = 128
CV_WIDTH = 512
CV_KERNEL = 31
SB_WIDTH = 512
SB_HEAD_DIM = 64
Q_BLOCK = 128
D_FF = 2816
FFN_KERNEL = 3
COL_B = 1024
COL_C = 2048
COL_G = 3584
IN_COLS = 6656
N_CHIPS = 4
N_DEV = 8
CV_HALO = 32
FFN_HALO = 16

ADAM_LR = 0.001
ADAM_B1 = 0.9
ADAM_B2 = 0.999
ADAM_EPS = 1e-08
ADAM_WD = 0.01
ADAM_STEP = 10

VMEM_LIMIT_BYTES = 56 * 1024 * 1024

NT_DIMS = (((1,), (1,)), ((), ()))
TN_DIMS = (((0,), (0,)), ((), ()))

WEIGHTS = ['ln1_g', 'w_in', 'b_gate', 'sg_ln_g', 'sg_ln_b', 'sg_w', 'sg_b', 'w_a_out', 'cv_w', 'cv_b',
           'cv_ln_g', 'cv_ln_b', 'w_b_out', 'q_norm_g', 'k_norm_g', 'w_c_out', 'w_out', 'ln2_g', 'w_up',
           'ffn_conv_w', 'ffn_conv_b', 'w_down']
BIG_COL = ['w_in', 'w_a_out', 'w_b_out', 'w_c_out', 'w_up']
BIG_ROW = ['w_out', 'w_down']
BIG = BIG_COL + BIG_ROW
SMALL_COL = ['b_gate', 'cv_w', 'ffn_conv_w']
SMALL = [n for n in WEIGHTS if n not in BIG]


def _pc(body, **kw):
    return pl.pallas_call(body, **kw)


def _cp(*sem):
    return pltpu.CompilerParams(dimension_semantics=sem, vmem_limit_bytes=VMEM_LIMIT_BYTES)


def _sds(shape, dtype):
    return jax.ShapeDtypeStruct(shape, dtype)


_GELU_C = math.sqrt(2.0 / math.pi)
_GELU_A = 0.044715


def _sigmoid(x):
    return jax.nn.sigmoid(x)


def _gelu(x):
    return 0.5 * x * (1.0 + jnp.tanh(_GELU_C * (x + _GELU_A * x * x * x)))


def _gelu_grad(x):
    t = jnp.tanh(_GELU_C * (x + _GELU_A * x * x * x))
    return 0.5 * (1.0 + t) + 0.5 * x * (1.0 - t * t) * _GELU_C * (1.0 + 3.0 * _GELU_A * x * x)


def _silu(x):
    return x * _sigmoid(x)


def _silu_grad(x):
    s = _sigmoid(x)
    return s * (1.0 + x * (1.0 - s))


def _ln_stats(x):
    mu = jnp.mean(x, axis=-1, keepdims=True)
    xc = x - mu
    r = lax.rsqrt(jnp.mean(xc * xc, axis=-1, keepdims=True) + EPS)
    return xc * r, r


def _ln_bwd(dy, xhat, r, g):
    dxh = dy * g
    return r * (dxh - jnp.mean(dxh, axis=-1, keepdims=True) - xhat * jnp.mean(dxh * xhat, axis=-1, keepdims=True))


def _split_dot(x, m):
    hi = x.astype(BF16)
    lo = (x - hi.astype(F32)).astype(BF16)
    return jnp.dot(hi, m, preferred_element_type=F32) + jnp.dot(lo, m, preferred_element_type=F32)


def _block_sums(x, m):
    return jnp.dot(x.astype(BF16), m, preferred_element_type=F32)


def _rowsum0(x):
    return jnp.sum(x, axis=0, keepdims=True)


def _pick(n, prefs):
    for p in prefs:
        if n % p == 0:
            return p
    return n


def mm_nn(a, w, *, name, res=None, out_dtype=BF16):
    t, k = a.shape
    n = w.shape[1]
    tm = _pick(t, (512, 256))
    tn = _pick(n, (1024, 512, 256))

    def body(*refs):
        if res is None:
            a_ref, w_ref, o_ref = refs
        else:
            a_ref, w_ref, r_ref, o_ref = refs
        acc = jnp.dot(a_ref[...], w_ref[...], preferred_element_type=F32)
        if res is not None:
            acc = acc + r_ref[...]
        o_ref[...] = acc.astype(o_ref.dtype)

    in_specs = [pl.BlockSpec((tm, k), lambda i, j: (i, 0)), pl.BlockSpec((k, tn), lambda i, j: (0, j))]
    args = [a, w]
    if res is not None:
        in_specs.append(pl.BlockSpec((tm, tn), lambda i, j: (i, j)))
        args.append(res)
    return _pc(body, name=name, grid=(t // tm, n // tn), in_specs=in_specs,
               out_specs=pl.BlockSpec((tm, tn), lambda i, j: (i, j)),
               out_shape=_sds((t, n), out_dtype), compiler_params=_cp("parallel", "parallel"))(*args)


def mm_norm_nn(x, g, w, *, name, after=()):
    t, k = x.shape
    n = w.shape[1]
    tm = _pick(t, (512, 256))
    tn = _pick(n, (1664, 1408, 512))

    def body(x_ref, g_ref, w_ref, *rest):
        z_ref, h_ref = rest[len(after):]
        xv = x_ref[...]
        r = lax.rsqrt(jnp.mean(xv * xv, axis=-1, keepdims=True) + EPS)
        h = (xv * r * g_ref[...]).astype(BF16)
        h_ref[...] = h
        for c in range(n // tn):
            cols = slice(c * tn, (c + 1) * tn)
            z_ref[:, cols] = jnp.dot(h, w_ref[:, cols], preferred_element_type=F32).astype(z_ref.dtype)

    return _pc(body, name=name, grid=(t // tm,),
               in_specs=[pl.BlockSpec((tm, k), lambda i: (i, 0)), pl.BlockSpec((1, k), lambda i: (0, 0)),
                         pl.BlockSpec((k, n), lambda i: (0, 0), pipeline_mode=pl.Buffered(1))]
               + [pl.BlockSpec(a.shape, lambda i: (0, 0)) for a in after],
               out_specs=[pl.BlockSpec((tm, n), lambda i: (i, 0)), pl.BlockSpec((tm, k), lambda i: (i, 0))],
               out_shape=[_sds((t, n), BF16), _sds((t, k), BF16)],
               compiler_params=_cp("parallel"))(x, g, w, *after)


def mm_nt(dy, w, *, name, out_dtype, after=()):
    t, n = dy.shape
    k = w.shape[0]
    tm = _pick(t, (512, 256))

    def body(dy_ref, w_ref, *rest):
        o_ref = rest[len(after)]
        o_ref[...] = lax.dot_general(dy_ref[...].astype(BF16), w_ref[...], NT_DIMS,
                                     preferred_element_type=F32).astype(o_ref.dtype)

    return _pc(body, name=name, grid=(t // tm,),
               in_specs=[pl.BlockSpec((tm, n), lambda i: (i, 0)),
                         pl.BlockSpec((k, n), lambda i: (0, 0), pipeline_mode=pl.Buffered(1))]
               + [pl.BlockSpec(tok.shape, lambda i: (0, 0)) for tok in after],
               out_specs=pl.BlockSpec((tm, k), lambda i: (i, 0)),
               out_shape=_sds((t, k), out_dtype), compiler_params=_cp("parallel"))(dy, w, *after)


def mm_tn(a, dy, *, name, out_dtype=BF16):
    t, k = a.shape
    n = dy.shape[1]
    tk = _pick(k, (1024, 1408, 512))
    tn = _pick(n, (512,) if dy.dtype == F32 else (1664, 1408, 1024, 512))

    def body(a_ref, dy_ref, o_ref):
        o_ref[...] = lax.dot_general(a_ref[...], dy_ref[...].astype(BF16), TN_DIMS,
                                     preferred_element_type=F32).astype(o_ref.dtype)

    return _pc(body, name=name, grid=(k // tk, n // tn),
               in_specs=[pl.BlockSpec((t, tk), lambda i, j: (0, i)), pl.BlockSpec((t, tn), lambda i, j: (0, j))],
               out_specs=pl.BlockSpec((tk, tn), lambda i, j: (i, j)),
               out_shape=_sds((k, n), out_dtype), compiler_params=_cp("parallel", "parallel"))(a, dy)


def rms_bwd(dh, x, g, dres, *, name):
    t, d = x.shape
    tm = _pick(t, (256,))

    def body(dh_ref, x_ref, g_ref, dres_ref, dx_ref, dg_ref):
        xv = x_ref[...]
        r = lax.rsqrt(jnp.mean(xv * xv, axis=-1, keepdims=True) + EPS)
        xh = xv * r
        dy = dh_ref[...].astype(F32)
        dxh = dy * g_ref[...]
        dx_ref[...] = dres_ref[...] + r * (dxh - xh * jnp.mean(dxh * xh, axis=-1, keepdims=True))

        @pl.when(pl.program_id(0) == 0)
        def _():
            dg_ref[...] = jnp.zeros_like(dg_ref)

        dg_ref[...] += _rowsum0(dy * xh)

    row = pl.BlockSpec((tm, d), lambda i: (i, 0))
    vec = pl.BlockSpec((1, d), lambda i: (0, 0))
    return _pc(body, name=name, grid=(t // tm,), in_specs=[row, row, vec, row], out_specs=[row, vec],
               out_shape=[_sds((t, d), F32), _sds((1, d), F32)], compiler_params=_cp("arbitrary"))(dh, x, g, dres)


def loss_head(y, target, *, name):
    t, d = y.shape
    tm = _pick(t, (256,))

    def body(y_ref, t_ref, loss_ref, dy_ref):
        e = y_ref[...] - t_ref[...]
        dy_ref[...] = e * (1.0 / d)

        @pl.when(pl.program_id(0) == 0)
        def _():
            loss_ref[...] = jnp.zeros_like(loss_ref)

        loss_ref[...] += _rowsum0(jnp.sum(e * e, axis=1, keepdims=True)) * (0.5 / d)

    row = pl.BlockSpec((tm, d), lambda i: (i, 0))
    return _pc(body, name=name, grid=(t // tm,), in_specs=[row, row],
               out_specs=[pl.BlockSpec((1, 1), lambda i: (0, 0)), row],
               out_shape=[_sds((1, 1), F32), _sds((t, d), F32)], compiler_params=_cp("arbitrary"))(y, target)


def _sg_masks():
    lane = lax.broadcasted_iota(jnp.int32, (CHUNK, CHUNK), 1)
    row = lax.broadcasted_iota(jnp.int32, (CHUNK, CHUNK), 0)
    return lane < 64, lane <= row, row <= lane


def _sg_gate(vn_chunk, w_ref, bias_ref, p, first_group, tril):
    wa = jnp.where(tril, w_ref[2 * p], 0.0).astype(BF16)
    wb = jnp.where(tril, w_ref[2 * p + 1], 0.0).astype(BF16)
    oa = jnp.dot(wa, vn_chunk, preferred_element_type=F32)
    ob = jnp.dot(wb, vn_chunk, preferred_element_type=F32)
    return jnp.where(first_group, oa, ob) + bias_ref[:, p * 128:(p + 1) * 128]


def mixa_fwd(z, ln_g, ln_b, sg_w, sg_bias, *, name):
    t = z.shape[0]
    tm = _pick(t, (256,))

    def body(z_ref, g_ref, b_ref, w_ref, bias_ref, o_ref):
        first_group, tril, _ = _sg_masks()
        zv = z_ref[...].astype(F32)
        u = _gelu(zv[:, :SG_WIDTH])
        v = _gelu(zv[:, SG_WIDTH:])
        vh, _ = _ln_stats(v)
        vn = (vh * g_ref[...] + b_ref[...]).astype(BF16)
        for c in range(tm // CHUNK):
            rows = slice(c * CHUNK, (c + 1) * CHUNK)
            for p in range(4):
                cols = slice(p * 128, (p + 1) * 128)
                o = _sg_gate(vn[rows, cols], w_ref, bias_ref, p, first_group, tril)
                o_ref[rows, cols] = (u[rows, cols] * o).astype(o_ref.dtype)

    vec = pl.BlockSpec((1, SG_WIDTH), lambda i: (0, 0))
    return _pc(body, name=name, grid=(t // tm,),
               in_specs=[pl.BlockSpec((tm, 2 * SG_WIDTH), lambda i: (i, 0)), vec, vec,
                         pl.BlockSpec((8, CHUNK, CHUNK), lambda i: (0, 0, 0)),
                         pl.BlockSpec((CHUNK, SG_WIDTH), lambda i: (0, 0))],
               out_specs=pl.BlockSpec((tm, SG_WIDTH), lambda i: (i, 0)),
               out_shape=_sds((t, SG_WIDTH), BF16), compiler_params=_cp("parallel"))(z, ln_g, ln_b, sg_w, sg_bias)


def mixa_bwd(z, dga, ln_g, ln_b, sg_w, sg_wt, sg_bias, *, name):
    t = z.shape[0]
    tm = _pick(t, (256,))
    nsteps = t // tm

    def body(z_ref, dga_ref, g_ref, b_ref, w_ref, wt_ref, bias_ref, dz_ref, dw_ref, dsgb_ref, dg_ref, db_ref, dvn_s,
             dbias_ref):
        i = pl.program_id(0)
        first_group, tril, triu = _sg_masks()

        @pl.when(i == 0)
        def _():
            dw_ref[...] = jnp.zeros_like(dw_ref)
            dbias_ref[...] = jnp.zeros_like(dbias_ref)
            dg_ref[...] = jnp.zeros_like(dg_ref)
            db_ref[...] = jnp.zeros_like(db_ref)

        zv = z_ref[...].astype(F32)
        zu = zv[:, :SG_WIDTH]
        zg = zv[:, SG_WIDTH:]
        u = _gelu(zu)
        v = _gelu(zg)
        vh, r = _ln_stats(v)
        vn = (vh * g_ref[...] + b_ref[...]).astype(BF16)
        dga_v = dga_ref[...].astype(F32)
        d_o = dga_v * u
        for c in range(tm // CHUNK):
            rows = slice(c * CHUNK, (c + 1) * CHUNK)
            dbias_ref[...] += d_o[rows, :]
            for p in range(4):
                cols = slice(p * 128, (p + 1) * 128)
                vp = vn[rows, cols]
                o = _sg_gate(vp, w_ref, bias_ref, p, first_group, tril)
                dz_ref[rows, cols] = (dga_v[rows, cols] * o * _gelu_grad(zu[rows, cols])).astype(dz_ref.dtype)
                dop = d_o[rows, cols]
                dop_a = jnp.where(first_group, dop, 0.0).astype(BF16)
                dop_b = jnp.where(first_group, 0.0, dop).astype(BF16)
                dw_ref[2 * p] += lax.dot_general(dop_a, vp, NT_DIMS, preferred_element_type=F32)
                dw_ref[2 * p + 1] += lax.dot_general(dop_b, vp, NT_DIMS, preferred_element_type=F32)
                wta = jnp.where(triu, wt_ref[2 * p], 0.0).astype(BF16)
                wtb = jnp.where(triu, wt_ref[2 * p + 1], 0.0).astype(BF16)
                dop16 = dop.astype(BF16)
                dvn_s[rows, cols] = jnp.where(first_group, jnp.dot(wta, dop16, preferred_element_type=F32),
                                              jnp.dot(wtb, dop16, preferred_element_type=F32))
        dvn = dvn_s[...]
        dg_ref[...] += _rowsum0(dvn * vh)
        db_ref[...] += _rowsum0(dvn)
        dv = _ln_bwd(dvn, vh, r, g_ref[...])
        dz_ref[:, SG_WIDTH:] = (dv * _gelu_grad(zg)).astype(dz_ref.dtype)

        @pl.when(i == nsteps - 1)
        def _():
            for gi in range(8):
                dw_ref[gi] = jnp.where(tril, dw_ref[gi], 0.0)
            r_id = lax.broadcasted_iota(jnp.int32, (SG_WIDTH, 128), 0) // SB_HEAD_DIM
            c_id = lax.broadcasted_iota(jnp.int32, (SG_WIDTH, 128), 1)
            dsgb_ref[...] = _split_dot(dbias_ref[...], (r_id == c_id).astype(BF16))

    vec = pl.BlockSpec((1, SG_WIDTH), lambda i: (0, 0))
    wspec = pl.BlockSpec((8, CHUNK, CHUNK), lambda i: (0, 0, 0))
    bspec = pl.BlockSpec((CHUNK, SG_WIDTH), lambda i: (0, 0))
    sgb = pl.BlockSpec((CHUNK, 128), lambda i: (0, 0))
    return _pc(body, name=name, grid=(nsteps,),
               in_specs=[pl.BlockSpec((tm, 2 * SG_WIDTH), lambda i: (i, 0)), pl.BlockSpec((tm, SG_WIDTH), lambda i: (i, 0)),
                         vec, vec, wspec, wspec, bspec],
               out_specs=[pl.BlockSpec((tm, 2 * SG_WIDTH), lambda i: (i, 0)), wspec, sgb, vec, vec],
               out_shape=[_sds((t, 2 * SG_WIDTH), BF16), _sds((8, CHUNK, CHUNK), F32), _sds((CHUNK, 128), F32),
                          _sds((1, SG_WIDTH), F32), _sds((1, SG_WIDTH), F32)],
               scratch_shapes=[pltpu.VMEM((tm, SG_WIDTH), F32), pltpu.VMEM((CHUNK, SG_WIDTH), F32)],
               compiler_params=_cp("arbitrary"))(z, dga, ln_g, ln_b, sg_w, sg_wt, sg_bias)


def _glu(zv):
    return zv[:, :CV_WIDTH] * _sigmoid(zv[:, CV_WIDTH:])


_SUB = 8


def _cv_phases(x_s, tm):
    rows = CV_HALO + tm - _SUB
    for r in range(1, _SUB):
        x_s[r, 0:rows, :] = x_s[0, pl.ds(r, rows), :]


def _cv_tap(x_s, o, tm):
    return x_s[o % _SUB, pl.ds(o - o % _SUB, tm), :]


def _cv_fill(zm_ref, zh_ref, x_s, i, tm):
    x_s[0, 0:CV_HALO, :] = jnp.where(i > 0, _glu(zh_ref[...].astype(F32)), 0.0)
    x_s[0, CV_HALO:CV_HALO + tm, :] = _glu(zm_ref[...].astype(F32))
    _cv_phases(x_s, tm)


def _cv_conv(x_s, w_ref, cb_ref, tm):
    acc = jnp.zeros((tm, CV_WIDTH), F32) + cb_ref[...]
    for k in range(CV_KERNEL):
        acc = acc + w_ref[k:k + 1, :] * _cv_tap(x_s, CV_HALO - (CV_KERNEL - 1) + k, tm)
    return acc


def _cv_specs(tm):
    zm = pl.BlockSpec((tm, 2 * CV_WIDTH), lambda i: (i, 1))
    zh = pl.BlockSpec((CV_HALO, 2 * CV_WIDTH), lambda i: (jnp.maximum(i * (tm // CV_HALO) - 1, 0), 1))
    w = pl.BlockSpec((CV_KERNEL, CV_WIDTH), lambda i: (0, 0))
    vec = pl.BlockSpec((1, CV_WIDTH), lambda i: (0, 0))
    return zm, zh, w, vec


def mixb_fwd(z, cv_w, cv_b, ln_g, ln_b, *, name):
    t = z.shape[0]
    tm = _pick(t, (256,))

    def body(zm_ref, zh_ref, w_ref, cb_ref, g_ref, b_ref, o_ref, x_s):
        _cv_fill(zm_ref, zh_ref, x_s, pl.program_id(0), tm)
        c1 = _cv_conv(x_s, w_ref, cb_ref, tm)
        ch, _ = _ln_stats(c1)
        o_ref[...] = _silu(ch * g_ref[...] + b_ref[...]).astype(o_ref.dtype)

    zm, zh, w, vec = _cv_specs(tm)
    return _pc(body, name=name, grid=(t // tm,), in_specs=[zm, zh, w, vec, vec, vec],
               out_specs=pl.BlockSpec((tm, CV_WIDTH), lambda i: (i, 0)), out_shape=_sds((t, CV_WIDTH), BF16),
               scratch_shapes=[pltpu.VMEM((_SUB, CV_HALO + tm, CV_WIDTH), F32)],
               compiler_params=_cp("parallel"))(z, z, cv_w, cv_b, ln_g, ln_b)


def mixb_bwd1(z, dc3, cv_w, cv_b, ln_g, ln_b, *, name):
    t = z.shape[0]
    tm = _pick(t, (256,))

    def body(zm_ref, zh_ref, dc3_ref, w_ref, cb_ref, g_ref, b_ref, dc1_ref, dw_ref, dcb_ref, dg_ref, db_ref, x_s):
        i = pl.program_id(0)

        @pl.when(i == 0)
        def _():
            dw_ref[...] = jnp.zeros_like(dw_ref)
            dcb_ref[...] = jnp.zeros_like(dcb_ref)
            dg_ref[...] = jnp.zeros_like(dg_ref)
            db_ref[...] = jnp.zeros_like(db_ref)

        _cv_fill(zm_ref, zh_ref, x_s, i, tm)
        c1 = _cv_conv(x_s, w_ref, cb_ref, tm)
        ch, r = _ln_stats(c1)
        c2 = ch * g_ref[...] + b_ref[...]
        dc2 = dc3_ref[...].astype(F32) * _silu_grad(c2)
        dg_ref[...] += _rowsum0(dc2 * ch)
        db_ref[...] += _rowsum0(dc2)
        dc1 = _ln_bwd(dc2, ch, r, g_ref[...])
        dc1_ref[...] = dc1
        dcb_ref[...] += _rowsum0(dc1)
        for k in range(CV_KERNEL):
            dw_ref[k:k + 1, :] += _rowsum0(dc1 * _cv_tap(x_s, CV_HALO - (CV_KERNEL - 1) + k, tm))

    zm, zh, w, vec = _cv_specs(tm)
    row = pl.BlockSpec((tm, CV_WIDTH), lambda i: (i, 0))
    return _pc(body, name=name, grid=(t // tm,), in_specs=[zm, zh, row, w, vec, vec, vec],
               out_specs=[row, w, vec, vec, vec],
               out_shape=[_sds((t, CV_WIDTH), F32), _sds((CV_KERNEL, CV_WIDTH), F32), _sds((1, CV_WIDTH), F32),
                          _sds((1, CV_WIDTH), F32), _sds((1, CV_WIDTH), F32)],
               scratch_shapes=[pltpu.VMEM((_SUB, CV_HALO + tm, CV_WIDTH), F32)],
               compiler_params=_cp("arbitrary"))(z, z, dc3, cv_w, cv_b, ln_g, ln_b)


def mixb_bwd2(z, dc1, cv_w, *, name):
    t = z.shape[0]
    tm = _pick(t, (256,))
    nsteps = t // tm

    def body(zm_ref, dm_ref, dh_ref, w_ref, dz_ref, y_s):
        i = pl.program_id(0)
        y_s[0, 0:tm, :] = dm_ref[...]
        y_s[0, tm:tm + CV_HALO, :] = jnp.where(i < nsteps - 1, dh_ref[...], 0.0)
        _cv_phases(y_s, tm)
        dc0 = jnp.zeros((tm, CV_WIDTH), F32)
        for k in range(CV_KERNEL):
            dc0 = dc0 + w_ref[k:k + 1, :] * _cv_tap(y_s, CV_KERNEL - 1 - k, tm)
        zv = zm_ref[...].astype(F32)
        p = zv[:, :CV_WIDTH]
        s = _sigmoid(zv[:, CV_WIDTH:])
        dz_ref[:, :CV_WIDTH] = (dc0 * s).astype(dz_ref.dtype)
        dz_ref[:, CV_WIDTH:] = (dc0 * p * s * (1.0 - s)).astype(dz_ref.dtype)

    last = t // CV_HALO - 1
    return _pc(body, name=name, grid=(nsteps,),
               in_specs=[pl.BlockSpec((tm, 2 * CV_WIDTH), lambda i: (i, 1)),
                         pl.BlockSpec((tm, CV_WIDTH), lambda i: (i, 0)),
                         pl.BlockSpec((CV_HALO, CV_WIDTH), lambda i: (jnp.minimum((i + 1) * (tm // CV_HALO), last), 0)),
                         pl.BlockSpec((CV_KERNEL, CV_WIDTH), lambda i: (0, 0))],
               out_specs=pl.BlockSpec((tm, 2 * CV_WIDTH), lambda i: (i, 0)),
               out_shape=_sds((t, 2 * CV_WIDTH), BF16),
               scratch_shapes=[pltpu.VMEM((_SUB, tm + CV_HALO, CV_WIDTH), F32)],
               compiler_params=_cp("parallel"))(z, dc1, dc1, cv_w)


def _group_ones():
    r = lax.broadcasted_iota(jnp.int32, (SB_WIDTH, SB_WIDTH), 0) // SB_HEAD_DIM
    c = lax.broadcasted_iota(jnp.int32, (SB_WIDTH, SB_WIDTH), 1) // SB_HEAD_DIM
    return (r == c).astype(BF16)


def attn_prep(z, gq, gk, *, name):
    t = z.shape[0]
    tm = _pick(t, (256,))
    scale = 1.0 / math.sqrt(SB_HEAD_DIM)

    def body(q_ref, k_ref, gq_ref, gk_ref, qo_ref, ko_ref):
        ones = _group_ones()
        for src, g_ref, dst, mul in ((q_ref, gq_ref, qo_ref, scale), (k_ref, gk_ref, ko_ref, 1.0)):
            v = src[...].astype(F32)
            r = lax.rsqrt(_split_dot(v * v, ones) * (1.0 / SB_HEAD_DIM) + EPS)
            dst[...] = ((v * r * g_ref[...]).astype(BF16).astype(F32) * mul).astype(dst.dtype)

    vec = pl.BlockSpec((1, SB_WIDTH), lambda i: (0, 0))
    row = pl.BlockSpec((tm, SB_WIDTH), lambda i: (i, 0))
    return _pc(body, name=name, grid=(t // tm,),
               in_specs=[pl.BlockSpec((tm, SB_WIDTH), lambda i: (i, COL_C // SB_WIDTH)),
                         pl.BlockSpec((tm, SB_WIDTH), lambda i: (i, COL_C // SB_WIDTH + 1)), vec, vec],
               out_specs=[row, row], out_shape=[_sds((t, SB_WIDTH), BF16), _sds((t, SB_WIDTH), BF16)],
               compiler_params=_cp("parallel"))(z, z, gq, gk)


_KB = Q_BLOCK
_PAIR = 2 * _KB


def _attn_tq(t):
    return _pick(t, (512, 256, 128))


def _attn_consts(tq):
    first_head = lax.broadcasted_iota(jnp.int32, (_KB, 128), 1) < SB_HEAD_DIM
    r2 = lax.broadcasted_iota(jnp.int32, (_PAIR, _PAIR), 0)
    c2 = lax.broadcasted_iota(jnp.int32, (_PAIR, _PAIR), 1)
    same = (r2 // _KB) == (c2 // _KB)
    m_suffix = (same & (r2 > c2)).astype(BF16)
    m_prefix = (same & (r2 < c2)).astype(BF16)
    row = lax.broadcasted_iota(jnp.int32, (tq, _PAIR), 0)
    col = lax.broadcasted_iota(jnp.int32, (tq, _PAIR), 1)
    return first_head, m_suffix, m_prefix, row, col & (_KB - 1), col < _KB


def _sb_logits(z, causal):
    sp = jnp.log(1.0 + jnp.exp(-jnp.abs(z)))
    g = jnp.minimum(z, 0.0) - sp
    l1m = g - z
    if causal is not None:
        l1m = jnp.where(causal, l1m, 0.0)
    return g, l1m


def _stack_heads(first_head, v):
    zero = jnp.zeros_like(v)
    return jnp.concatenate([jnp.where(first_head, v, zero), jnp.where(first_head, zero, v)], axis=0)


def _add_rows(x, upd, r0):
    return x + upd if r0 == 0 else jnp.concatenate([x[:r0], x[r0:] + upd], axis=0)


def _pair_sums(x):
    return jnp.sum(x[:, :_KB], axis=1, keepdims=True), jnp.sum(x[:, _KB:], axis=1, keepdims=True)


def _attn_specs(t, tq):
    qspec = pl.BlockSpec((tq, 128), lambda h, i: (i, h))
    kspec = pl.BlockSpec((t, 128), lambda h, i: (0, h))
    vspec = pl.BlockSpec((t, 128), lambda h, i: (0, (COL_C + 2 * SB_WIDTH) // 128 + h))
    return qspec, kspec, vspec


def attn_fwd(q, k, z, *, name):
    t = q.shape[0]
    tq = _attn_tq(t)
    nd = tq // _KB

    def body(q_ref, k_ref, v_ref, o_ref):
        qt = pl.program_id(1)
        first_head, m_suffix, _, row, key, is_first = _attn_consts(tq)
        qv = q_ref[...]

        def step(kb, state, causal, r0=0):
            acc, ca, cb = state
            off = pl.multiple_of(kb * _KB, _KB)
            kcat = _stack_heads(first_head, k_ref[pl.ds(off, _KB), :])
            vcat = _stack_heads(first_head, v_ref[pl.ds(off, _KB), :])
            zz = lax.dot_general(qv[r0:], kcat, NT_DIMS, preferred_element_type=F32)
            g, l1m = _sb_logits(zz, None if causal is None else causal[r0:])
            a = jnp.exp(g + _block_sums(l1m, m_suffix) + jnp.where(is_first[r0:], ca[r0:], cb[r0:]))
            if causal is not None:
                a = jnp.where(causal[r0:], a, 0.0)
            sa, sb = _pair_sums(l1m)
            pv = jnp.dot(a.astype(BF16), vcat, preferred_element_type=F32)
            return _add_rows(acc, pv, r0), _add_rows(ca, sa, r0), _add_rows(cb, sb, r0)

        c0 = jnp.zeros((tq, 1), F32)
        state = (jnp.zeros((tq, 128), F32), c0, c0)
        for d in reversed(range(nd)):
            state = step(qt * nd + d, state, key + d * _KB < row, d * _KB)
        state = lax.fori_loop(0, qt * nd, lambda s, st: step(qt * nd - 1 - s, st, None), state)
        o_ref[...] = state[0].astype(o_ref.dtype)

    qspec, kspec, vspec = _attn_specs(t, tq)
    return _pc(body, name=name, grid=(SB_WIDTH // 128, t // tq), in_specs=[qspec, kspec, vspec], out_specs=qspec,
               out_shape=_sds((t, SB_WIDTH), BF16), compiler_params=_cp("parallel", "arbitrary"))(q, k, z)


def attn_bwd(q, k, z, do, *, name):
    t = q.shape[0]
    tq = _attn_tq(t)
    nd = tq // _KB
    nk = t // _KB

    def body(q_ref, k_ref, v_ref, do_ref, dq_ref, dk_ref, dv_ref, e_s, sg_s):
        qt = pl.program_id(1)
        first_head, m_suffix, m_prefix, row, key, is_first = _attn_consts(tq)

        @pl.when(qt == 0)
        def _():
            dk_ref[...] = jnp.zeros_like(dk_ref)
            dv_ref[...] = jnp.zeros_like(dv_ref)

        qv = q_ref[...]
        dov = do_ref[...]

        def halves(x):
            return jnp.where(first_head, x[:_KB], x[_KB:])

        def sweep1(kb, state, causal, r0=0):
            ca, cb = state
            off = pl.multiple_of(kb * _KB, _KB)
            kcat = _stack_heads(first_head, k_ref[pl.ds(off, _KB), :])
            vcat = _stack_heads(first_head, v_ref[pl.ds(off, _KB), :])
            zz = lax.dot_general(qv[r0:], kcat, NT_DIMS, preferred_element_type=F32)
            g, l1m = _sb_logits(zz, None if causal is None else causal[r0:])
            a = jnp.exp(g + _block_sums(l1m, m_suffix) + jnp.where(is_first[r0:], ca[r0:], cb[r0:]))
            if causal is not None:
                a = jnp.where(causal[r0:], a, 0.0)
            da = lax.dot_general(dov[r0:], vcat, NT_DIMS, preferred_element_type=F32)
            e_s[kb, r0:, :] = a * da
            sg_s[kb, r0:, :] = jnp.exp(g).astype(BF16)
            dv_ref[pl.ds(off, _KB), :] += halves(lax.dot_general(a.astype(BF16), dov[r0:], TN_DIMS,
                                                                 preferred_element_type=F32))
            sa, sb = _pair_sums(l1m)
            return _add_rows(ca, sa, r0), _add_rows(cb, sb, r0)

        def sweep2(kb, state, causal, r0=0):
            dq, pa, pb = state
            off = pl.multiple_of(kb * _KB, _KB)
            kcat = _stack_heads(first_head, k_ref[pl.ds(off, _KB), :])
            e = e_s[kb, r0:, :]
            s = sg_s[kb, r0:, :].astype(F32)
            dz = e * (1.0 - s) - (jnp.where(is_first[r0:], pa[r0:], pb[r0:]) + _block_sums(e, m_prefix)) * s
            if causal is not None:
                dz = jnp.where(causal[r0:], dz, 0.0)
            dz = dz.astype(BF16)
            dk_ref[pl.ds(off, _KB), :] += halves(lax.dot_general(dz, qv[r0:], TN_DIMS, preferred_element_type=F32))
            sa, sb = _pair_sums(e)
            return (_add_rows(dq, jnp.dot(dz, kcat, preferred_element_type=F32), r0), _add_rows(pa, sa, r0),
                    _add_rows(pb, sb, r0))

        c0 = jnp.zeros((tq, 1), F32)
        st1 = (c0, c0)
        for d in reversed(range(nd)):
            st1 = sweep1(qt * nd + d, st1, key + d * _KB < row, d * _KB)
        lax.fori_loop(0, qt * nd, lambda s, st: sweep1(qt * nd - 1 - s, st, None), st1)
        st2 = lax.fori_loop(0, qt * nd, lambda s, st: sweep2(s, st, None), (jnp.zeros((tq, 128), F32), c0, c0))
        for d in range(nd):
            st2 = sweep2(qt * nd + d, st2, key + d * _KB < row, d * _KB)
        dq_ref[...] = st2[0]

    qspec, kspec, vspec = _attn_specs(t, tq)
    acc = pl.BlockSpec((t, 128), lambda h, i: (0, h))
    return _pc(body, name=name, grid=(SB_WIDTH // 128, t // tq), in_specs=[qspec, kspec, vspec, qspec],
               out_specs=[qspec, acc, acc],
               out_shape=[_sds((t, SB_WIDTH), F32), _sds((t, SB_WIDTH), F32), _sds((t, SB_WIDTH), F32)],
               scratch_shapes=[pltpu.VMEM((nk, tq, _PAIR), F32), pltpu.VMEM((nk, tq, _PAIR), BF16)],
               compiler_params=_cp("parallel", "arbitrary"))(q, k, z, do)


def attn_post_bwd(z, dq, dk, dv, gq, gk, *, name):
    t = z.shape[0]
    tm = _pick(t, (256,))
    scale = 1.0 / math.sqrt(SB_HEAD_DIM)

    def body(q_ref, k_ref, dq_ref, dk_ref, dv_ref, gq_ref, gk_ref, dz_ref, dgq_ref, dgk_ref):
        ones = _group_ones()

        @pl.when(pl.program_id(0) == 0)
        def _():
            dgq_ref[...] = jnp.zeros_like(dgq_ref)
            dgk_ref[...] = jnp.zeros_like(dgk_ref)

        for idx, (src, d_ref, g_ref, dg_ref, mul) in enumerate(
                ((q_ref, dq_ref, gq_ref, dgq_ref, scale), (k_ref, dk_ref, gk_ref, dgk_ref, 1.0))):
            v = src[...].astype(F32)
            r = lax.rsqrt(_split_dot(v * v, ones) * (1.0 / SB_HEAD_DIM) + EPS)
            vh = v * r
            dn = d_ref[...] * mul
            dxh = dn * g_ref[...]
            m = _split_dot(dxh * vh, ones) * (1.0 / SB_HEAD_DIM)
            dz_ref[:, idx * SB_WIDTH:(idx + 1) * SB_WIDTH] = (r * (dxh - vh * m)).astype(dz_ref.dtype)
            s = _rowsum0(dn * vh)
            f = jnp.broadcast_to(s[:, 0:128] + s[:, 128:256] + s[:, 256:384] + s[:, 384:512], dg_ref.shape)
            dg_ref[...] += f + pltpu.roll(f, 64, 1)
        dz_ref[:, 2 * SB_WIDTH:] = dv_ref[...].astype(dz_ref.dtype)

    vec = pl.BlockSpec((1, SB_WIDTH), lambda i: (0, 0))
    row = pl.BlockSpec((tm, SB_WIDTH), lambda i: (i, 0))
    fold = pl.BlockSpec((8, 128), lambda i: (0, 0))
    return _pc(body, name=name, grid=(t // tm,),
               in_specs=[pl.BlockSpec((tm, SB_WIDTH), lambda i: (i, COL_C // SB_WIDTH)),
                         pl.BlockSpec((tm, SB_WIDTH), lambda i: (i, COL_C // SB_WIDTH + 1)), row, row, row, vec, vec],
               out_specs=[pl.BlockSpec((tm, 3 * SB_WIDTH), lambda i: (i, 0)), fold, fold],
               out_shape=[_sds((t, 3 * SB_WIDTH), BF16), _sds((8, 128), F32), _sds((8, 128), F32)],
               compiler_params=_cp("arbitrary"))(z, z, dq, dk, dv, gq, gk)


_GW = 512


def merge_fwd(z, ya, yb, yc, b_gate, *, name):
    t = z.shape[0]
    tm = _pick(t, (512, 256))

    def body(za_ref, zb_ref, zc_ref, ya_ref, yb_ref, yc_ref, bg_ref, o_ref):
        acc = jnp.zeros((tm, _GW), F32)
        for b, (zr, yr) in enumerate(((za_ref, ya_ref), (zb_ref, yb_ref), (zc_ref, yc_ref))):
            acc = acc + _sigmoid(zr[...].astype(F32) + bg_ref[b:b + 1, :]) * yr[...].astype(F32)
        o_ref[...] = acc.astype(o_ref.dtype)

    def zspec(b):
        return pl.BlockSpec((tm, _GW), lambda i, j: (i, COL_G // _GW + 2 * b + j))

    yspec = pl.BlockSpec((tm, _GW), lambda i, j: (i, j))
    return _pc(body, name=name, grid=(t // tm, D_MODEL // _GW),
               in_specs=[zspec(0), zspec(1), zspec(2), yspec, yspec, yspec, pl.BlockSpec((3, _GW), lambda i, j: (0, j))],
               out_specs=yspec, out_shape=_sds((t, D_MODEL), BF16),
               compiler_params=_cp("parallel", "parallel"))(z, z, z, ya, yb, yc, b_gate)


def merge_bwd(z, ya, yb, yc, b_gate, dm, *, name):
    t = z.shape[0]
    tm = _pick(t, (512, 256))

    def body(za_ref, zb_ref, zc_ref, ya_ref, yb_ref, yc_ref, bg_ref, dm_ref,
             dya_ref, dyb_ref, dyc_ref, dza_ref, dzb_ref, dzc_ref, dbg_ref):
        @pl.when(pl.program_id(1) == 0)
        def _():
            dbg_ref[...] = jnp.zeros_like(dbg_ref)

        dmv = dm_ref[...].astype(F32)
        for b, (zr, yr, dyr, dzr) in enumerate(((za_ref, ya_ref, dya_ref, dza_ref), (zb_ref, yb_ref, dyb_ref, dzb_ref),
                                                (zc_ref, yc_ref, dyc_ref, dzc_ref))):
            s = _sigmoid(zr[...].astype(F32) + bg_ref[b:b + 1, :])
            dyr[...] = (dmv * s).astype(dyr.dtype)
            dg = dmv * yr[...].astype(F32) * s * (1.0 - s)
            dzr[...] = dg.astype(dzr.dtype)
            dbg_ref[b:b + 1, :] += _rowsum0(dg)

    def zspec(b):
        return pl.BlockSpec((tm, _GW), lambda j, i: (i, COL_G // _GW + 2 * b + j))

    yspec = pl.BlockSpec((tm, _GW), lambda j, i: (i, j))
    bspec = pl.BlockSpec((3, _GW), lambda j, i: (0, j))
    full = _sds((t, D_MODEL), BF16)
    return _pc(body, name=name, grid=(D_MODEL // _GW, t // tm),
               in_specs=[zspec(0), zspec(1), zspec(2), yspec, yspec, yspec, bspec, yspec],
               out_specs=[yspec] * 6 + [bspec], out_shape=[full] * 6 + [_sds((3, D_MODEL), F32)],
               compiler_params=_cp("parallel", "arbitrary"))(z, z, z, ya, yb, yc, b_gate, dm)


_FW = 1408
_FH = D_FF // _FW


def _ffn_fill(m_ref, h_ref, x_s, i, tm):
    x_s[0:FFN_HALO, :] = jnp.where(i > 0, h_ref[...].astype(F32), 0.0)
    x_s[FFN_HALO:FFN_HALO + tm, :] = m_ref[...].astype(F32)


def _ffn_taps(x_s, tm):
    return [x_s[pl.ds(FFN_HALO - (FFN_KERNEL - 1) + k, tm), :] for k in range(FFN_KERNEL)]


def _ffn_conv(x_s, w_ref, b_ref, tm, taps=None):
    taps = _ffn_taps(x_s, tm) if taps is None else taps
    acc = jnp.zeros((tm, _FW), F32) + b_ref[...]
    for k in range(FFN_KERNEL):
        acc = acc + w_ref[k:k + 1, :] * taps[k]
    return acc


def ffn_mid_fwd(up, cw, cb, *, name):
    t = up.shape[0]
    tm = _pick(t, (256,))

    def body(gm_ref, gh_ref, vm_ref, vh_ref, wg_ref, wv_ref, bg_ref, bv_ref, o_ref, xg_s, xv_s):
        i = pl.program_id(0)
        _ffn_fill(gm_ref, gh_ref, xg_s, i, tm)
        _ffn_fill(vm_ref, vh_ref, xv_s, i, tm)
        o_ref[...] = (_silu(_ffn_conv(xg_s, wg_ref, bg_ref, tm)) * _ffn_conv(xv_s, wv_ref, bv_ref, tm)).astype(o_ref.dtype)

    def main(off):
        return pl.BlockSpec((tm, _FW), lambda i, j: (i, j + off))

    def halo(off):
        return pl.BlockSpec((FFN_HALO, _FW), lambda i, j: (jnp.maximum(i * (tm // FFN_HALO) - 1, 0), j + off))

    def wspec(off):
        return pl.BlockSpec((FFN_KERNEL, _FW), lambda i, j: (0, j + off))

    def bspec(off):
        return pl.BlockSpec((1, _FW), lambda i, j: (0, j + off))

    return _pc(body, name=name, grid=(t // tm, _FH),
               in_specs=[main(0), halo(0), main(_FH), halo(_FH), wspec(0), wspec(_FH), bspec(0), bspec(_FH)],
               out_specs=pl.BlockSpec((tm, _FW), lambda i, j: (i, j)), out_shape=_sds((t, D_FF), BF16),
               scratch_shapes=[pltpu.VMEM((FFN_HALO + tm, _FW), F32), pltpu.VMEM((FFN_HALO + tm, _FW), F32)],
               compiler_params=_cp("parallel", "parallel"))(up, up, up, up, cw, cw, cb, cb)


def ffn_mid_bwd1(up, dact, cw, cb, *, name):
    t = up.shape[0]
    tm = _pick(t, (256,))

    def body(gm_ref, gh_ref, vm_ref, vh_ref, da_ref, wg_ref, wv_ref, bg_ref, bv_ref, d_ref, dw_ref, db_ref, xg_s, xv_s):
        j = pl.program_id(0)
        i = pl.program_id(1)

        @pl.when(i == 0)
        def _():
            dw_ref[...] = jnp.zeros_like(dw_ref)
            db_ref[...] = jnp.zeros_like(db_ref)

        _ffn_fill(gm_ref, gh_ref, xg_s, i, tm)
        _ffn_fill(vm_ref, vh_ref, xv_s, i, tm)
        gate_taps = _ffn_taps(xg_s, tm)
        gate = _ffn_conv(xg_s, wg_ref, bg_ref, tm, gate_taps)
        da = da_ref[...].astype(F32)

        def finish(d, taps):
            d_ref[...] = d.astype(d_ref.dtype)
            db_ref[...] += _rowsum0(d)
            for k in range(FFN_KERNEL):
                dw_ref[k:k + 1, :] += _rowsum0(d * taps[k])

        @pl.when(j < _FH)
        def _():
            finish(da * _ffn_conv(xv_s, wv_ref, bv_ref, tm) * _silu_grad(gate), gate_taps)

        @pl.when(j >= _FH)
        def _():
            finish(da * _silu(gate), _ffn_taps(xv_s, tm))

    def main(off):
        return pl.BlockSpec((tm, _FW), lambda j, i: (i, j % _FH + off))

    def halo(off):
        return pl.BlockSpec((FFN_HALO, _FW), lambda j, i: (jnp.maximum(i * (tm // FFN_HALO) - 1, 0), j % _FH + off))

    def wspec(off):
        return pl.BlockSpec((FFN_KERNEL, _FW), lambda j, i: (0, j % _FH + off))

    def bspec(off):
        return pl.BlockSpec((1, _FW), lambda j, i: (0, j % _FH + off))

    return _pc(body, name=name, grid=(2 * _FH, t // tm),
               in_specs=[main(0), halo(0), main(_FH), halo(_FH), pl.BlockSpec((tm, _FW), lambda j, i: (i, j % _FH)),
                         wspec(0), wspec(_FH), bspec(0), bspec(_FH)],
               out_specs=[pl.BlockSpec((tm, _FW), lambda j, i: (i, j)), pl.BlockSpec((FFN_KERNEL, _FW), lambda j, i: (0, j)),
                          pl.BlockSpec((1, _FW), lambda j, i: (0, j))],
               out_shape=[_sds((t, 2 * D_FF), BF16), _sds((FFN_KERNEL, 2 * D_FF), F32), _sds((1, 2 * D_FF), F32)],
               scratch_shapes=[pltpu.VMEM((FFN_HALO + tm, _FW), F32), pltpu.VMEM((FFN_HALO + tm, _FW), F32)],
               compiler_params=_cp("parallel", "arbitrary"))(up, up, up, up, dact, cw, cw, cb, cb)


def ffn_mid_bwd2(dupc, cw, *, name):
    t = dupc.shape[0]
    tm = _pick(t, (256,))
    nsteps = t // tm
    last = t // FFN_HALO - 1

    def body(m_ref, h_ref, w_ref, o_ref, y_s):
        i = pl.program_id(0)
        y_s[0:tm, :] = m_ref[...].astype(F32)
        y_s[tm:tm + FFN_HALO, :] = jnp.where(i < nsteps - 1, h_ref[...].astype(F32), 0.0)
        acc = jnp.zeros((tm, _FW), F32)
        for k in range(FFN_KERNEL):
            acc = acc + w_ref[k:k + 1, :] * y_s[pl.ds(FFN_KERNEL - 1 - k, tm), :]
        o_ref[...] = acc.astype(o_ref.dtype)

    return _pc(body, name=name, grid=(nsteps, 2 * _FH),
               in_specs=[pl.BlockSpec((tm, _FW), lambda i, j: (i, j)),
                         pl.BlockSpec((FFN_HALO, _FW), lambda i, j: (jnp.minimum((i + 1) * (tm // FFN_HALO), last), j)),
                         pl.BlockSpec((FFN_KERNEL, _FW), lambda i, j: (0, j))],
               out_specs=pl.BlockSpec((tm, _FW), lambda i, j: (i, j)), out_shape=_sds((t, 2 * D_FF), BF16),
               scratch_shapes=[pltpu.VMEM((tm + FFN_HALO, _FW), F32)],
               compiler_params=_cp("parallel", "parallel"))(dupc, dupc, cw)


def _vec(v):
    return v.reshape(1, -1)


def _layer_consts(p):
    return dict(
        sg_bias=jnp.repeat(p['sg_b'].T, SB_HEAD_DIM, axis=1),
        sg_wt=jnp.swapaxes(p['sg_w'], 1, 2),
        gq=jnp.tile(p['q_norm_g'], SB_WIDTH // SB_HEAD_DIM).reshape(1, -1),
        gk=jnp.tile(p['k_norm_g'], SB_WIDTH // SB_HEAD_DIM).reshape(1, -1),
    )


def layer_fwd(x, p, after=(), rest=None):
    c = _layer_consts(p)
    z, h = mm_norm_nn(x, _vec(p['ln1_g']), p['w_in'], name="in_proj", after=after)
    ga = mixa_fwd(z, _vec(p['sg_ln_g']), _vec(p['sg_ln_b']), p['sg_w'], c['sg_bias'], name="mixa_fwd")
    cb = mixb_fwd(z, p['cv_w'], _vec(p['cv_b']), _vec(p['cv_ln_g']), _vec(p['cv_ln_b']), name="mixb_fwd")
    q, k = attn_prep(z, c['gq'], c['gk'], name="attn_prep")
    ao = attn_fwd(q, k, z, name="attn_fwd")
    if rest is not None:
        p = {**p, **rest([ao])}
    ya = mm_nn(ga, p['w_a_out'], name="a_out")
    yb = mm_nn(cb, p['w_b_out'], name="b_out")
    yc = mm_nn(ao, p['w_c_out'], name="c_out")
    merged = merge_fwd(z, ya, yb, yc, p['b_gate'], name="merge_fwd")
    x1 = mm_nn(merged, p['w_out'], res=x, out_dtype=F32, name="out_proj")
    up, h2 = mm_norm_nn(x1, _vec(p['ln2_g']), p['w_up'], name="up_proj")
    act = ffn_mid_fwd(up, p['ffn_conv_w'], _vec(p['ffn_conv_b']), name="ffn_mid_fwd")
    x2 = mm_nn(act, p['w_down'], res=x1, out_dtype=F32, name="down_proj")
    saved = dict(x=x, z=z, h=h, ga=ga, cb=cb, q=q, k=k, ao=ao, ya=ya, yb=yb, yc=yc, merged=merged, x1=x1, up=up,
                 h2=h2, act=act)
    return x2, saved, p


GRAD_GROUPS = (('w_down', 'w_up'), ('w_out', 'w_a_out', 'w_b_out', 'w_c_out'), ('w_in',))


def layer_bwd(dx2, p, s, after=(), emit=None):
    c = _layer_consts(p)
    g = {}
    emit = emit or (lambda names, grads: ())
    g['w_down'] = mm_tn(s['act'], dx2, name="d_w_down")
    dact = mm_nt(dx2, p['w_down'], out_dtype=BF16, name="d_act", after=after)
    dupc, g['ffn_conv_w'], dcb = ffn_mid_bwd1(s['up'], dact, p['ffn_conv_w'], _vec(p['ffn_conv_b']), name="ffn_mid_bwd1")
    g['ffn_conv_b'] = dcb.reshape(-1)
    dup = ffn_mid_bwd2(dupc, p['ffn_conv_w'], name="ffn_mid_bwd2")
    g['w_up'] = mm_tn(s['h2'], dup, name="d_w_up")
    dh2 = mm_nt(dup, p['w_up'], out_dtype=F32, name="d_h2")
    dx1, dg2 = rms_bwd(dh2, s['x1'], _vec(p['ln2_g']), dx2, name="ln2_bwd")
    g['ln2_g'] = dg2.reshape(-1)
    g['w_out'] = mm_tn(s['merged'], dx1, name="d_w_out")
    dm = mm_nt(dx1, p['w_out'], out_dtype=BF16, name="d_merged", after=emit(GRAD_GROUPS[0], g))
    dya, dyb, dyc, dzg0, dzg1, dzg2, g['b_gate'] = merge_bwd(s['z'], s['ya'], s['yb'], s['yc'], p['b_gate'], dm,
                                                             name="merge_bwd")
    g['w_a_out'] = mm_tn(s['ga'], dya, name="d_w_a_out")
    g['w_b_out'] = mm_tn(s['cb'], dyb, name="d_w_b_out")
    g['w_c_out'] = mm_tn(s['ao'], dyc, name="d_w_c_out")
    dga = mm_nt(dya, p['w_a_out'], out_dtype=BF16, name="d_ga")
    dcb3 = mm_nt(dyb, p['w_b_out'], out_dtype=BF16, name="d_cb")
    dao = mm_nt(dyc, p['w_c_out'], out_dtype=BF16, name="d_ao")
    dza, g['sg_w'], dsgb, dlg, dlb = mixa_bwd(s['z'], dga, _vec(p['sg_ln_g']), _vec(p['sg_ln_b']), p['sg_w'],
                                               c['sg_wt'], c['sg_bias'], name="mixa_bwd")
    g['sg_b'] = dsgb[:, :SG_WIDTH // SB_HEAD_DIM].T
    g['sg_ln_g'] = dlg.reshape(-1)
    g['sg_ln_b'] = dlb.reshape(-1)
    dc1, g['cv_w'], dcvb, dcg, dcbb = mixb_bwd1(s['z'], dcb3, p['cv_w'], _vec(p['cv_b']), _vec(p['cv_ln_g']),
                                                _vec(p['cv_ln_b']), name="mixb_bwd1")
    g['cv_b'] = dcvb.reshape(-1)
    g['cv_ln_g'] = dcg.reshape(-1)
    g['cv_ln_b'] = dcbb.reshape(-1)
    dzb = mixb_bwd2(s['z'], dc1, p['cv_w'], name="mixb_bwd2")
    dq, dk, dv = attn_bwd(s['q'], s['k'], s['z'], dao, name="attn_bwd")
    dzc, dgq, dgk = attn_post_bwd(s['z'], dq, dk, dv, c['gq'], c['gk'], name="attn_post_bwd")
    g['q_norm_g'] = dgq[0, :SB_HEAD_DIM]
    g['k_norm_g'] = dgk[0, :SB_HEAD_DIM]
    dz = jnp.concatenate([dza, dzb, dzc, dzg0, dzg1, dzg2], axis=1)
    g['w_in'] = mm_tn(s['h'], dz, name="d_w_in")
    dh = mm_nt(dz, p['w_in'], out_dtype=F32, name="d_h", after=emit(GRAD_GROUPS[1], g))
    dx, dg1 = rms_bwd(dh, s['x'], _vec(p['ln1_g']), dx1, name="ln1_bwd")
    g['ln1_g'] = dg1.reshape(-1)
    return dx, g


def local_step(x, target, depth, get_layer, on_grads, emit=lambda l: None):
    saved, layers = [], []
    for l in range(depth):
        p, after, rest = get_layer(l, x)
        x, s, p = layer_fwd(x, p, after, rest)
        layers.append(p)
        saved.append(s)
    loss, dx = loss_head(x, target, name="loss_head")
    after = ()
    for l in reversed(range(depth)):
        dx, g = layer_bwd(dx, layers[l], saved[l], after, emit(l))
        after = on_grads(l, g)
    return loss[0, 0], dx


def adamw(w, g, m, v, *, name):
    r, c = w.shape
    tr = _pick(r, (256, 704)) if r * c > 512 * 1024 else r

    def body(w_ref, g_ref, m_ref, v_ref, d_ref, mo_ref, vo_ref):
        gv = g_ref[...]
        mn = ADAM_B1 * m_ref[...] + (1.0 - ADAM_B1) * gv
        vn = ADAM_B2 * v_ref[...] + (1.0 - ADAM_B2) * (gv * gv)
        m_hat = mn / (1.0 - ADAM_B1 ** ADAM_STEP)
        v_hat = vn / (1.0 - ADAM_B2 ** ADAM_STEP)
        d_ref[...] = -ADAM_LR * (m_hat / (jnp.sqrt(v_hat) + ADAM_EPS) + ADAM_WD * w_ref[...])
        mo_ref[...] = mn
        vo_ref[...] = vn

    spec = pl.BlockSpec((tr, c), lambda i: (i, 0))
    out = _sds((r, c), F32)
    return _pc(body, name=name, grid=(r // tr,), in_specs=[spec] * 4, out_specs=[spec] * 3, out_shape=[out] * 3,
               compiler_params=_cp("parallel"))(w, g, m, v)


def _as3(a):
    return a if a.ndim == 3 else a.reshape((1,) + a.shape)


def add_half(g, recv, c_idx, *, name):
    s, rh, w = recv.shape
    tr = _pick(rh, (256, 352, 128))
    nb = rh // tr

    def body(c_ref, g_ref, r_ref, o_ref):
        o_ref[...] = (g_ref[...].astype(F32) + r_ref[...].astype(F32)).astype(o_ref.dtype)

    own = pl.BlockSpec((1, tr, w), lambda k, i, c_ref: (k, c_ref[0] * nb + i, 0))
    half = pl.BlockSpec((1, tr, w), lambda k, i, c_ref: (k, i, 0))
    gs = pltpu.PrefetchScalarGridSpec(num_scalar_prefetch=1, grid=(s, nb), in_specs=[own, half], out_specs=half)
    return _pc(body, name=name, grid_spec=gs, out_shape=_sds((s, rh, w), BF16),
               compiler_params=_cp("parallel", "parallel"))(c_idx, g, recv)


def sum_shard(p, recv, pos_idx, *, by_rows, name):
    _, rh, w = recv.shape
    tr = _pick(rh, (256, 352, 128))
    nb = rh // tr

    def body(pos_ref, p_ref, r_ref, o_ref):
        acc = p_ref[0].astype(F32)
        for j in range(N_CHIPS - 1):
            acc = acc + r_ref[j].astype(F32)
        o_ref[...] = acc

    if by_rows:
        own = pl.BlockSpec((1, tr, w), lambda i, pos_ref: (pos_ref[0], i, 0))
    else:
        own = pl.BlockSpec((1, tr, w), lambda i, pos_ref: (0, i, pos_ref[0]))
    gs = pltpu.PrefetchScalarGridSpec(num_scalar_prefetch=1, grid=(nb,),
                                      in_specs=[own, pl.BlockSpec((N_CHIPS - 1, tr, w), lambda i, pos_ref: (0, i, 0))],
                                      out_specs=pl.BlockSpec((tr, w), lambda i, pos_ref: (pos_ref[1] * nb + i, 0)))
    return _pc(body, name=name, grid_spec=gs, out_shape=_sds((2 * rh, w), F32),
               compiler_params=_cp("parallel"))(pos_idx, p, recv)


def sum_slots(slab, *, name):
    _, r, w = slab.shape
    tr = _pick(r, (512, 256, 8))

    def body(s_ref, o_ref):
        acc = s_ref[0]
        for j in range(1, N_DEV):
            acc = acc + s_ref[j]
        o_ref[...] = acc

    return _pc(body, name=name, grid=(r // tr,), in_specs=[pl.BlockSpec((N_DEV, tr, w), lambda i: (0, i, 0))],
               out_specs=pl.BlockSpec((tr, w), lambda i: (i, 0)), out_shape=_sds((r, w), F32),
               compiler_params=_cp("parallel"))(slab)


def _mesh_pos():
    x, y, c = lax.axis_index("x"), lax.axis_index("y"), lax.axis_index("c")
    others = [(1 - x, y), (x, 1 - y), (1 - x, 1 - y)]
    return x, y, c, others


def _rcopy(src, dst, ssem, rsem, k, dev):
    return pltpu.make_async_remote_copy(src_ref=src, dst_ref=dst, send_sem=ssem.at[k], recv_sem=rsem.at[k],
                                        device_id=dev, device_id_type=MESH)


def _comm_call(body, name, n_in, out_shape, n_local, n_remote):
    scratch = [pltpu.SemaphoreType.DMA((max(n_local, 1),)), pltpu.SemaphoreType.DMA((n_remote,)),
               pltpu.SemaphoreType.DMA((n_remote,))]
    return _pc(body, name=name, in_specs=[ANY] * n_in, out_specs=[ANY] * len(out_shape), out_shape=out_shape,
               scratch_shapes=scratch)


GATHERED = BIG + SMALL_COL
HBM_SPEC = pl.BlockSpec(memory_space=pltpu.HBM)
SEM_SPEC = pl.BlockSpec(memory_space=pltpu.SEMAPHORE)
TOKEN_SHAPE = (8, 128)


def place_block(w, layer, pos_idx, *, by_rows, dtype, name):
    _, r, c = w.shape
    tr = _pick(r, (512, 704, 256))

    def body(pos_ref, w_ref, o_ref):
        if by_rows:
            o_ref[0] = w_ref[0].astype(dtype)
        else:
            o_ref[...] = w_ref[0].astype(dtype)

    if by_rows:
        out_spec, shape = pl.BlockSpec((1, tr, c), lambda i, pos_ref: (pos_ref[0], i, 0)), (N_CHIPS, r, c)
    else:
        out_spec, shape = pl.BlockSpec((tr, c), lambda i, pos_ref: (i, pos_ref[0])), (r, N_CHIPS * c)
    gs = pltpu.PrefetchScalarGridSpec(num_scalar_prefetch=1, grid=(r // tr,),
                                      in_specs=[pl.BlockSpec((1, tr, c), lambda i, pos_ref: (layer, i, 0))],
                                      out_specs=out_spec)
    return _pc(body, name=name, grid_spec=gs, out_shape=_sds(shape, dtype), compiler_params=_cp("parallel"))(pos_idx, w)


def _gather_windows(bufs):
    names, arrs = list(bufs), list(bufs.values())

    def dwin(refs, i, k, h):
        if names[i] in BIG_ROW:
            _, r, _ = arrs[i].shape
            return refs[i].at[k] if h is None else refs[i].at[k, pl.ds(h * (r // 2), r // 2), :]
        r, cs = arrs[i].shape[0], arrs[i].shape[1] // N_CHIPS
        cols = pl.ds(pl.multiple_of(k * cs, 128), cs)
        return refs[i].at[:, cols] if h is None else refs[i].at[pl.ds(h * (r // 2), r // 2), cols]

    def swin(refs, i, h):
        x, y, _, _ = _mesh_pos()
        return dwin(refs, i, 2 * x + y, h)

    return names, dwin, swin


def _gather_send(names, ins, outs, ssem, rsem, dwin, swin, stride):
    x, y, c, others = _mesh_pos()
    sends = []
    for i, n in enumerate(names):
        h = c if n in BIG else None
        for j, chip in enumerate(others):
            cp = _rcopy(swin(ins, i, h), swin(outs, i, h), ssem, rsem, stride * i + j, (*chip, c))
            cp.start()
            sends.append(cp)
    return sends


def _gather_pass_on(names, outs, ssem, rsem, dwin, stride, first_off, pass_off):
    x, y, c, others = _mesh_pos()
    sib = (x, y, 1 - c)
    sends = []
    for j, chip in enumerate(others):
        kk = 2 * chip[0] + chip[1]
        for i, n in enumerate(names):
            got = dwin(outs, i, kk, c if n in BIG else None)
            if first_off is not None:
                _rcopy(got, got, ssem, rsem, stride * i + first_off + j, (*chip, c)).wait_recv()
            if n in BIG:
                fwd = _rcopy(got, got, ssem, rsem, stride * i + pass_off + j, sib)
                fwd.start()
                sends.append(fwd)
    for j, chip in enumerate(others):
        kk = 2 * chip[0] + chip[1]
        for i, n in enumerate(names):
            if n in BIG:
                got = dwin(outs, i, kk, 1 - c)
                _rcopy(got, got, ssem, rsem, stride * i + pass_off + j, sib).wait_recv()
    return sends


def _as_weights(bufs):
    return {n: (o.reshape(o.shape[0] * o.shape[1], o.shape[2]) if n in BIG_ROW else o) for n, o in bufs.items()}


def _comm_in_place(body, name, bufs, n_sems):
    nn = len(bufs)
    arrs = list(bufs.values())
    scratch = [pltpu.SemaphoreType.DMA((n_sems,)), pltpu.SemaphoreType.DMA((n_sems,))]
    outs = _pc(body, name=name, in_specs=[ANY] * nn, out_specs=[ANY] * nn, out_shape=[_sds(a.shape, a.dtype) for a in arrs],
               scratch_shapes=scratch, input_output_aliases={i: i for i in range(nn)})(*arrs)
    return dict(zip(bufs, outs))


def allgather_weights(bufs, *, name):
    nn = len(bufs)
    names, dwin, swin = _gather_windows(bufs)

    def body(*refs):
        ins, outs = refs[:nn], refs[nn:2 * nn]
        ssem, rsem = refs[2 * nn:]
        sends = _gather_send(names, ins, outs, ssem, rsem, dwin, swin, 6)
        sends += _gather_pass_on(names, outs, ssem, rsem, dwin, 6, 0, 3)
        for cp in sends:
            cp.wait_send()

    return _comm_in_place(body, name, bufs, 6 * nn)


def _split_start(body, name, bufs, extra_in, n_sems):
    nn = len(bufs)
    arrs = [pltpu.with_memory_space_constraint(a, pltpu.HBM) for a in bufs.values()]
    out_shape = ([pltpu.SemaphoreType.DMA((n_sems,)), pltpu.SemaphoreType.DMA((n_sems,))]
                 + [pltpu.HBM(a.shape, a.dtype) for a in arrs] + [_sds(TOKEN_SHAPE, F32)])
    outs = _pc(body, name=name, in_specs=[HBM_SPEC] * nn + [ANY] * len(extra_in),
               out_specs=[SEM_SPEC, SEM_SPEC] + [HBM_SPEC] * nn + [pl.BlockSpec(memory_space=pltpu.VMEM)],
               out_shape=out_shape, input_output_aliases={i: 2 + i for i in range(nn)},
               compiler_params=pltpu.CompilerParams(has_side_effects=pltpu.SideEffectType.DATAFLOW_SIDE_EFFECTING),
               )(*arrs, *extra_in)
    return dict(ssem=outs[0], rsem=outs[1], bufs=dict(zip(bufs, outs[2:2 + nn])), token=outs[-1])


def _split_wait(body, name, handle, after):
    bufs = handle['bufs']
    nn = len(bufs)
    arrs = list(bufs.values())
    outs = _pc(body, name=name, in_specs=[HBM_SPEC] * nn + [SEM_SPEC, SEM_SPEC] + [ANY] * len(after),
               out_specs=[HBM_SPEC] * nn, out_shape=[pltpu.HBM(a.shape, a.dtype) for a in arrs],
               input_output_aliases={i: i for i in range(nn)},
               compiler_params=pltpu.CompilerParams(has_side_effects=pltpu.SideEffectType.DATAFLOW_SIDE_EFFECTING),
               )(*arrs, handle['ssem'], handle['rsem'], *after)
    return dict(zip(bufs, outs))


def gather_start(bufs, after, *, name):
    nn = len(bufs)
    names, dwin, swin = _gather_windows(bufs)

    def body(*refs):
        ins = refs[:nn]
        ssem, rsem = refs[nn + len(after)], refs[nn + len(after) + 1]
        _gather_send(names, ins, ins, ssem, rsem, dwin, swin, 3)
        refs[-1][...] = jnp.zeros(TOKEN_SHAPE, F32)

    return _split_start(body, name, bufs, after, 3 * nn)


def gather_wait(handle, after, *, name):
    nn = len(handle['bufs'])
    names, dwin, swin = _gather_windows(handle['bufs'])

    def body(*refs):
        ins = refs[:nn]
        ssem, rsem = refs[nn], refs[nn + 1]
        x, y, c, others = _mesh_pos()
        for i, n in enumerate(names):
            h = c if n in BIG else None
            for j, chip in enumerate(others):
                kk = 2 * chip[0] + chip[1]
                cp = _rcopy(swin(ins, i, h), dwin(ins, i, kk, h), ssem, rsem, 3 * i + j, (*chip, c))
                cp.wait_send()
                cp.wait_recv()

    return _split_wait(body, name, handle, after)


def gather_finish(bufs, *, name):
    nn = len(bufs)
    names, dwin, _ = _gather_windows(bufs)

    def body(*refs):
        outs = refs[nn:2 * nn]
        ssem, rsem = refs[2 * nn:]
        for cp in _gather_pass_on(names, outs, ssem, rsem, dwin, 3, None, 0):
            cp.wait_send()

    return _comm_in_place(body, name, bufs, 3 * nn)


def _grad_view(n, g):
    return g.reshape(N_CHIPS, g.shape[0] // N_CHIPS, g.shape[1]) if n in BIG_ROW else g.reshape((1,) + g.shape)


def exchange_halves(gv, *, name):
    nn = len(gv)
    arrs = list(gv.values())

    def body(*refs):
        ins, outs = refs[:nn], refs[nn:2 * nn]
        _, ssem, rsem = refs[2 * nn:]
        x, y, c, _ = _mesh_pos()
        cps = []
        for i in range(nn):
            rh = arrs[i].shape[1] // 2
            cp = _rcopy(ins[i].at[:, pl.ds((1 - c) * rh, rh), :], outs[i], ssem, rsem, i, (x, y, 1 - c))
            cp.start()
            cps.append(cp)
        for cp in cps:
            cp.wait()

    out_shape = [_sds((a.shape[0], a.shape[1] // 2, a.shape[2]), a.dtype) for a in arrs]
    return dict(zip(gv, _comm_call(body, name, nn, out_shape, 0, nn)(*arrs)))


def _shard_shape(n, p):
    _, rh, w = p.shape
    return (rh, w) if n in BIG_ROW else (rh, w // N_CHIPS)


def _scatter_copies(pv, ins, outs, ssem, rsem):
    x, y, c, others = _mesh_pos()
    cps = []
    for i, (n, p) in enumerate(pv.items()):
        _, ws = _shard_shape(n, p)
        for j, chip in enumerate(others):
            kk = 2 * chip[0] + chip[1]
            if n in BIG_ROW:
                src = ins[i].at[kk]
            else:
                src = ins[i].at[0, :, pl.ds(pl.multiple_of(kk * ws, 128), ws)]
            cps.append(_rcopy(src, outs[i].at[j], ssem, rsem, 3 * i + j, (*chip, c)))
    return cps


def _recv_shapes(pv):
    return [(N_CHIPS - 1,) + _shard_shape(n, p) for n, p in pv.items()]


def scatter_partials(pv, *, name):
    nn = len(pv)

    def body(*refs):
        ins, outs = refs[:nn], refs[nn:2 * nn]
        _, ssem, rsem = refs[2 * nn:]
        cps = _scatter_copies(pv, ins, outs, ssem, rsem)
        for cp in cps:
            cp.start()
        for cp in cps:
            cp.wait()

    out_shape = [_sds(s, p.dtype) for s, p in zip(_recv_shapes(pv), pv.values())]
    return pv, dict(zip(pv, _comm_call(body, name, nn, out_shape, 0, 3 * nn)(*pv.values())))


_RECV = "/recv"


def scatter_start(pv, *, name):
    nn = len(pv)

    def body(*refs):
        ins, lands = refs[:nn], refs[nn:2 * nn]
        ssem, rsem = refs[2 * nn], refs[2 * nn + 1]
        for cp in _scatter_copies(pv, ins, lands, ssem, rsem):
            cp.start()
        refs[-1][...] = jnp.zeros(TOKEN_SHAPE, F32)

    lands = {n + _RECV: lax.empty(s, p.dtype) for (n, p), s in zip(pv.items(), _recv_shapes(pv))}
    return _split_start(body, name, {**pv, **lands}, (), 3 * nn)


def scatter_wait(handle, after, *, name):
    nn = len(handle['bufs']) // 2
    pv = dict(list(handle['bufs'].items())[:nn])

    def body(*refs):
        ins, zones = refs[:nn], refs[nn:2 * nn]
        ssem, rsem = refs[2 * nn], refs[2 * nn + 1]
        for cp in _scatter_copies(pv, ins, zones, ssem, rsem):
            cp.wait_send()
            cp.wait_recv()

    outs = _split_wait(body, name, handle, after)
    return {n: outs[n] for n in pv}, {n: outs[n + _RECV] for n in pv}


def join_halves(rv, *, name):
    nn = len(rv)
    arrs = list(rv.values())

    def body(*refs):
        ins, outs = refs[:nn], refs[nn:2 * nn]
        ssem, rsem = refs[2 * nn:]
        x, y, c, _ = _mesh_pos()
        cps = []
        for i in range(nn):
            rh = arrs[i].shape[0] // 2
            rows = pl.ds(c * rh, rh)
            cp = _rcopy(ins[i].at[rows, :], outs[i].at[rows, :], ssem, rsem, i, (x, y, 1 - c))
            cp.start()
            cps.append(cp)
        for i, cp in enumerate(cps):
            cp.wait_send()
            rh = arrs[i].shape[0] // 2
            got = outs[i].at[pl.ds((1 - c) * rh, rh), :]
            _rcopy(got, got, ssem, rsem, i, (x, y, 1 - c)).wait_recv()

    return _comm_in_place(body, name, rv, nn)


def chip_partials(grads, names, c_idx):
    gv = {n: _grad_view(n, grads[n]) for n in names}
    recv = exchange_halves(gv, name="rs_exchange_halves")
    return {n: add_half(gv[n], recv[n], c_idx, name="rs_add_" + n) for n in names}


def reduce_shards(pv, got, pos_idx):
    rv = {n: sum_shard(pv[n], got[n], pos_idx, by_rows=n in BIG_ROW, name="rs_sum_" + n) for n in pv}
    return join_halves(rv, name="rs_join_halves")


def _slab_first(ref, ssem, rsem):
    x, y, c, others = _mesh_pos()
    mine = ref.at[4 * x + 2 * y + c]
    peers = [(x, y, 1 - c)] + [(*chip, c) for chip in others]
    out = []
    for k, p in enumerate(peers):
        got = ref.at[4 * p[0] + 2 * p[1] + p[2]]
        out.append((_rcopy(mine, mine, ssem, rsem, k, p), _rcopy(got, got, ssem, rsem, k, p)))
    return out


def slab_start(slab, *, name):
    def body(ref, ssem, rsem, thru, token):
        for cp, _ in _slab_first(ref, ssem, rsem):
            cp.start()
        token[...] = jnp.zeros(TOKEN_SHAPE, F32)

    return _split_start(body, name, {'slab': slab}, (), 4)


def slab_wait(handle, after, *, name):
    def body(ref, ssem, rsem, *rest):
        for sent, landed in _slab_first(ref, ssem, rsem):
            sent.wait_send()
            landed.wait_recv()

    return _split_wait(body, name, handle, after)['slab']


def slab_finish(slab, *, name):
    def body(in_ref, out_ref, ssem, rsem):
        x, y, c, others = _mesh_pos()
        sib = (x, y, 1 - c)
        sends = []
        for j, chip in enumerate(others):
            got = out_ref.at[4 * chip[0] + 2 * chip[1] + c]
            sends.append(_rcopy(got, got, ssem, rsem, j, sib))
            sends[-1].start()
        for j, chip in enumerate(others):
            got = out_ref.at[4 * chip[0] + 2 * chip[1] + 1 - c]
            _rcopy(got, got, ssem, rsem, j, sib).wait_recv()
        for cp in sends:
            cp.wait_send()

    return _comm_in_place(body, name, {'slab': slab}, 3)['slab']


def _pad128(n):
    return -(-n // 128) * 128


def _pack_small(grads, shapes):
    parts = []
    for g in grads:
        for n in SMALL:
            v = g[n].astype(F32).reshape(-1)
            parts.append(jnp.pad(v, (0, _pad128(v.shape[0]) - v.shape[0])))
    flat = jnp.concatenate(parts)
    rows = -(-flat.shape[0] // (128 * 512)) * 512
    return jnp.pad(flat, (0, rows * 128 - flat.shape[0])).reshape(rows, 128)


def _unpack_small(slab, shapes, depth):
    flat = slab.reshape(-1)
    out = {n: [] for n in SMALL}
    off = 0
    for _ in range(depth):
        for n in SMALL:
            size = math.prod(shapes[n])
            out[n].append(flat[off:off + size].reshape(shapes[n]))
            off += _pad128(size)
    return {n: jnp.stack(v) for n, v in out.items()}


def _adamw_nd(w, g, m, v, name):
    shp = w.shape
    two = lambda a: a.reshape(-1, shp[-1])
    return tuple(o.reshape(shp) for o in adamw(two(w), two(g), two(m), two(v), name=name))


def kernel(x, ln1_g, w_in, b_gate, sg_ln_g, sg_ln_b, sg_w, sg_b, w_a_out, cv_w, cv_b, cv_ln_g, cv_ln_b, w_b_out, q_norm_g, k_norm_g, w_c_out, w_out, ln2_g, w_up, ffn_conv_w, ffn_conv_b, w_down, loss_target, m_ln1_g, m_w_in, m_b_gate, m_sg_ln_g, m_sg_ln_b, m_sg_w, m_sg_b, m_w_a_out, m_cv_w, m_cv_b, m_cv_ln_g, m_cv_ln_b, m_w_b_out, m_q_norm_g, m_k_norm_g, m_w_c_out, m_w_out, m_ln2_g, m_w_up, m_ffn_conv_w, m_ffn_conv_b, m_w_down, v_ln1_g, v_w_in, v_b_gate, v_sg_ln_g, v_sg_ln_b, v_sg_w, v_sg_b, v_w_a_out, v_cv_w, v_cv_b, v_cv_ln_g, v_cv_ln_b, v_w_b_out, v_q_norm_g, v_k_norm_g, v_w_c_out, v_w_out, v_ln2_g, v_w_up, v_ffn_conv_w, v_ffn_conv_b, v_w_down):
    w = dict(ln1_g=ln1_g, w_in=w_in, b_gate=b_gate, sg_ln_g=sg_ln_g, sg_ln_b=sg_ln_b, sg_w=sg_w, sg_b=sg_b,
             w_a_out=w_a_out, cv_w=cv_w, cv_b=cv_b, cv_ln_g=cv_ln_g, cv_ln_b=cv_ln_b, w_b_out=w_b_out,
             q_norm_g=q_norm_g, k_norm_g=k_norm_g, w_c_out=w_c_out, w_out=w_out, ln2_g=ln2_g, w_up=w_up,
             ffn_conv_w=ffn_conv_w, ffn_conv_b=ffn_conv_b, w_down=w_down)
    m = dict(ln1_g=m_ln1_g, w_in=m_w_in, b_gate=m_b_gate, sg_ln_g=m_sg_ln_g, sg_ln_b=m_sg_ln_b, sg_w=m_sg_w,
             sg_b=m_sg_b, w_a_out=m_w_a_out, cv_w=m_cv_w, cv_b=m_cv_b, cv_ln_g=m_cv_ln_g, cv_ln_b=m_cv_ln_b,
             w_b_out=m_w_b_out, q_norm_g=m_q_norm_g, k_norm_g=m_k_norm_g, w_c_out=m_w_c_out, w_out=m_w_out,
             ln2_g=m_ln2_g, w_up=m_w_up, ffn_conv_w=m_ffn_conv_w, ffn_conv_b=m_ffn_conv_b, w_down=m_w_down)
    v = dict(ln1_g=v_ln1_g, w_in=v_w_in, b_gate=v_b_gate, sg_ln_g=v_sg_ln_g, sg_ln_b=v_sg_ln_b, sg_w=v_sg_w,
             sg_b=v_sg_b, w_a_out=v_w_a_out, cv_w=v_cv_w, cv_b=v_cv_b, cv_ln_g=v_cv_ln_g, cv_ln_b=v_cv_ln_b,
             w_b_out=v_w_b_out, q_norm_g=v_q_norm_g, k_norm_g=v_k_norm_g, w_c_out=v_w_c_out, w_out=v_w_out,
             ln2_g=v_ln2_g, w_up=v_w_up, ffn_conv_w=v_ffn_conv_w, ffn_conv_b=v_ffn_conv_b, w_down=v_w_down)
    depth = ln1_g.shape[0]
    cx, cy, cc = lax.axis_index("x"), lax.axis_index("y"), lax.axis_index("c")
    me = 2 * cx + cy
    pos_idx = jnp.stack([me, cc]).astype(jnp.int32)
    c_idx = jnp.reshape(cc, (1,)).astype(jnp.int32)

    padded = {n: jnp.pad(w[n], ((0, 0), (0, -w[n].shape[1] % 8), (0, 0))) for n in SMALL_COL}
    first, later = ['w_in'] + SMALL_COL, [n for n in BIG if n != 'w_in']

    def blocks(names, l):
        return {n: (place_block(w[n], l, pos_idx, by_rows=n in BIG_ROW, dtype=BF16, name="place_" + n) if n in BIG else
                    place_block(padded[n], l, pos_idx, by_rows=False, dtype=F32, name="place_" + n)) for n in names}

    full0 = allgather_weights(blocks(first, 0), name="allgather_weights")
    gathers = [gather_start(blocks(later, 0), [full0['w_in']], name="gather_start_0")]
    for l in range(1, depth):
        gathers.append(gather_start(blocks(GATHERED, l), [gathers[-1]['token']], name="gather_start_%d" % l))

    def arrived(l, after):
        bufs = gather_wait(gathers[l], after, name="gather_wait_%d" % l)
        return _as_weights(gather_finish(bufs, name="gather_finish_%d" % min(l, 1)))

    def get_layer(l, x_in):
        if l == 0:
            p, after, rest = _as_weights(full0), tuple(h['token'] for h in gathers), functools.partial(arrived, 0)
        else:
            p, after, rest = arrived(l, [x_in]), (), None
        for n in SMALL:
            p[n] = p[n][:w[n].shape[1]] if n in SMALL_COL else w[n][l]
        return p, after, rest

    grads, scatters = [None] * depth, []

    def on_grads(l, g):
        grads[l] = g
        if l == 0:
            scatters.append((0, scatter_partials(chip_partials(g, GRAD_GROUPS[-1], c_idx), name="rs_scatter_partials")))
            return ()
        scatters.append((l, scatter_start(chip_partials(g, BIG, c_idx), name="scatter_start_%d" % l)))
        return (scatters[-1][1]['token'],)

    def emit0(names, g):
        scatters.append((0, scatter_start(chip_partials(g, names, c_idx), name="scatter_start_0_" + names[0])))
        return (scatters[-1][1]['token'],)

    loss, dx = local_step(x[0], loss_target[0], depth, get_layer, on_grads, lambda l: emit0 if l == 0 else None)
    loss = lax.psum(loss, ("x", "y", "c"))
    full_shapes = {n: (w[n].shape[1], N_CHIPS * w[n].shape[2]) if n in SMALL_COL else w[n].shape[1:] for n in SMALL}
    mine = _pack_small(grads, full_shapes)
    slots = lax.dynamic_update_slice(lax.empty((N_DEV,) + mine.shape, F32), mine[None], (4 * cx + 2 * cy + cc, 0, 0))
    gathering = slab_start(slots, name="small_grads_start")
    big = [{} for _ in range(depth)]
    for i, (l, sc) in enumerate(scatters):
        pv, got = sc if isinstance(sc, tuple) else scatter_wait(sc, [dx, gathering['token']], name="scatter_wait_%d" % i)
        big[l].update(reduce_shards(pv, got, pos_idx))

    grad = {n: jnp.stack([b[n] for b in big]) for n in BIG}
    delta, new_m, new_v = {}, {}, {}
    for n in BIG:
        delta[n], new_m[n], new_v[n] = _adamw_nd(w[n], grad[n], m[n], v[n], "adamw_" + n)
    slots = slab_finish(slab_wait(gathering, [delta[n] for n in BIG], name="small_grads_wait"), name="small_grads_finish")
    small = _unpack_small(sum_slots(slots, name="sum_small_grads"), full_shapes, depth)
    for n in SMALL:
        if n in SMALL_COL:
            cs = w[n].shape[-1]
            grad[n] = lax.dynamic_slice_in_dim(small[n], me * cs, cs, axis=small[n].ndim - 1)
        else:
            grad[n] = small[n]
        delta[n], new_m[n], new_v[n] = _adamw_nd(w[n], grad[n], m[n], v[n], "adamw_" + n)
    return (loss, dx[None], *[grad[n] for n in WEIGHTS], *[delta[n] for n in WEIGHTS],
            *[new_m[n] for n in WEIGHTS], *[new_v[n] for n in WEIGHTS])
```

```python
import functools
import math

import jax
import jax.numpy as jnp
from jax import lax
from jax.experimental import pallas as pl
from jax.experimental.pallas import tpu as pltpu

F32 = jnp.float32
BF16 = jnp.bfloat16
MESH = pl.DeviceIdType.MESH
ANY = pl.BlockSpec(memory_space=pl.ANY)

EPS = 1e-6
D_MODEL = 1024
DEPTH = 4
SG_WIDTH = 512
CHUNK = 128
CV_WIDTH = 512
CV_KERNEL = 31
SB_WIDTH = 512
SB_HEAD_DIM = 64
Q_BLOCK = 128
D_FF = 2816
FFN_KERNEL = 3
COL_B = 1024
COL_C = 2048
COL_G = 3584
IN_COLS = 6656
N_CHIPS = 4
N_DEV = 8
CV_HALO = 32
FFN_HALO = 16

ADAM_LR = 0.001
ADAM_B1 = 0.9
ADAM_B2 = 0.999
ADAM_EPS = 1e-08
ADAM_WD = 0.01
ADAM_STEP = 10

VMEM_LIMIT_BYTES = 56 * 1024 * 1024

NT_DIMS = (((1,), (1,)), ((), ()))
TN_DIMS = (((0,), (0,)), ((), ()))

WEIGHTS = ['ln1_g', 'w_in', 'b_gate', 'sg_ln_g', 'sg_ln_b', 'sg_w', 'sg_b', 'w_a_out', 'cv_w', 'cv_b',
           'cv_ln_g', 'cv_ln_b', 'w_b_out', 'q_norm_g', 'k_norm_g', 'w_c_out', 'w_out', 'ln2_g', 'w_up',
           'ffn_conv_w', 'ffn_conv_b', 'w_down']
BIG_COL = ['w_in', 'w_a_out', 'w_b_out', 'w_c_out', 'w_up']
BIG_ROW = ['w_out', 'w_down']
BIG = BIG_COL + BIG_ROW
SMALL_COL = ['b_gate', 'cv_w', 'ffn_conv_w']
SMALL = [n for n in WEIGHTS if n not in BIG]


def _pc(body, **kw):
    return pl.pallas_call(body, **kw)


def _cp(*sem):
    return pltpu.CompilerParams(dimension_semantics=sem, vmem_limit_bytes=VMEM_LIMIT_BYTES)


def _sds(shape, dtype):
    return jax.ShapeDtypeStruct(shape, dtype)


_GELU_C = math.sqrt(2.0 / math.pi)
_GELU_A = 0.044715


def _sigmoid(x):
    return jax.nn.sigmoid(x)


def _gelu(x):
    return 0.5 * x * (1.0 + jnp.tanh(_GELU_C * (x + _GELU_A * x * x * x)))


def _gelu_grad(x):
    t = jnp.tanh(_GELU_C * (x + _GELU_A * x * x * x))
    return 0.5 * (1.0 + t) + 0.5 * x * (1.0 - t * t) * _GELU_C * (1.0 + 3.0 * _GELU_A * x * x)


def _silu(x):
    return x * _sigmoid(x)


def _silu_grad(x):
    s = _sigmoid(x)
    return s * (1.0 + x * (1.0 - s))


def _ln_stats(x):
    mu = jnp.mean(x, axis=-1, keepdims=True)
    xc = x - mu
    r = lax.rsqrt(jnp.mean(xc * xc, axis=-1, keepdims=True) + EPS)
    return xc * r, r


def _ln_bwd(dy, xhat, r, g):
    dxh = dy * g
    return r * (dxh - jnp.mean(dxh, axis=-1, keepdims=True) - xhat * jnp.mean(dxh * xhat, axis=-1, keepdims=True))


def _split_dot(x, m):
    hi = x.astype(BF16)
    lo = (x - hi.astype(F32)).astype(BF16)
    return jnp.dot(hi, m, preferred_element_type=F32) + jnp.dot(lo, m, preferred_element_type=F32)


def _block_sums(x, m):
    return jnp.dot(x.astype(BF16), m, preferred_element_type=F32)


def _rowsum0(x):
    return jnp.sum(x, axis=0, keepdims=True)


def _pick(n, prefs):
    for p in prefs:
        if n % p == 0:
            return p
    return n


def mm_nn(a, w, *, name, res=None, out_dtype=BF16):
    t, k = a.shape
    n = w.shape[1]
    tm = _pick(t, (512, 256))
    tn = _pick(n, (1024, 512, 256))

    def body(*refs):
        if res is None:
            a_ref, w_ref, o_ref = refs
        else:
            a_ref, w_ref, r_ref, o_ref = refs
        acc = jnp.dot(a_ref[...], w_ref[...], preferred_element_type=F32)
        if res is not None:
            acc = acc + r_ref[...]
        o_ref[...] = acc.astype(o_ref.dtype)

    in_specs = [pl.BlockSpec((tm, k), lambda i, j: (i, 0)), pl.BlockSpec((k, tn), lambda i, j: (0, j))]
    args = [a, w]
    if res is not None:
        in_specs.append(pl.BlockSpec((tm, tn), lambda i, j: (i, j)))
        args.append(res)
    return _pc(body, name=name, grid=(t // tm, n // tn), in_specs=in_specs,
               out_specs=pl.BlockSpec((tm, tn), lambda i, j: (i, j)),
               out_shape=_sds((t, n), out_dtype), compiler_params=_cp("parallel", "parallel"))(*args)


def mm_norm_nn(x, g, w, *, name, after=()):
    t, k = x.shape
    n = w.shape[1]
    tm = _pick(t, (512, 256))
    tn = _pick(n, (1664, 1408, 512))

    def body(x_ref, g_ref, w_ref, *rest):
        z_ref, h_ref = rest[len(after):]
        xv = x_ref[...]
        r = lax.rsqrt(jnp.mean(xv * xv, axis=-1, keepdims=True) + EPS)
        h = (xv * r * g_ref[...]).astype(BF16)
        h_ref[...] = h
        for c in range(n // tn):
            cols = slice(c * tn, (c + 1) * tn)
            z_ref[:, cols] = jnp.dot(h, w_ref[:, cols], preferred_element_type=F32).astype(z_ref.dtype)

    return _pc(body, name=name, grid=(t // tm,),
               in_specs=[pl.BlockSpec((tm, k), lambda i: (i, 0)), pl.BlockSpec((1, k), lambda i: (0, 0)),
                         pl.BlockSpec((k, n), lambda i: (0, 0), pipeline_mode=pl.Buffered(1))]
               + [pl.BlockSpec(a.shape, lambda i: (0, 0)) for a in after],
               out_specs=[pl.BlockSpec((tm, n), lambda i: (i, 0)), pl.BlockSpec((tm, k), lambda i: (i, 0))],
               out_shape=[_sds((t, n), BF16), _sds((t, k), BF16)],
               compiler_params=_cp("parallel"))(x, g, w, *after)


def mm_nt(dy, w, *, name, out_dtype, after=()):
    t, n = dy.shape
    k = w.shape[0]
    tm = _pick(t, (512, 256))

    def body(dy_ref, w_ref, *rest):
        o_ref = rest[len(after)]
        o_ref[...] = lax.dot_general(dy_ref[...].astype(BF16), w_ref[...], NT_DIMS,
                                     preferred_element_type=F32).astype(o_ref.dtype)

    return _pc(body, name=name, grid=(t // tm,),
               in_specs=[pl.BlockSpec((tm, n), lambda i: (i, 0)),
                         pl.BlockSpec((k, n), lambda i: (0, 0), pipeline_mode=pl.Buffered(1))]
               + [pl.BlockSpec(tok.shape, lambda i: (0, 0)) for tok in after],
               out_specs=pl.BlockSpec((tm, k), lambda i: (i, 0)),
               out_shape=_sds((t, k), out_dtype), compiler_params=_cp("parallel"))(dy, w, *after)


def mm_tn(a, dy, *, name, out_dtype=BF16):
    t, k = a.shape
    n = dy.shape[1]
    tk = _pick(k, (1024, 1408, 512))
    tn = _pick(n, (512,) if dy.dtype == F32 else (1664, 1408, 1024, 512))

    def body(a_ref, dy_ref, o_ref):
        o_ref[...] = lax.dot_general(a_ref[...], dy_ref[...].astype(BF16), TN_DIMS,
                                     preferred_element_type=F32).astype(o_ref.dtype)

    return _pc(body, name=name, grid=(k // tk, n // tn),
               in_specs=[pl.BlockSpec((t, tk), lambda i, j: (0, i)), pl.BlockSpec((t, tn), lambda i, j: (0, j))],
               out_specs=pl.BlockSpec((tk, tn), lambda i, j: (i, j)),
               out_shape=_sds((k, n), out_dtype), compiler_params=_cp("parallel", "parallel"))(a, dy)


def rms_bwd(dh, x, g, dres, *, name):
    t, d = x.shape
    tm = _pick(t, (256,))

    def body(dh_ref, x_ref, g_ref, dres_ref, dx_ref, dg_ref):
        xv = x_ref[...]
        r = lax.rsqrt(jnp.mean(xv * xv, axis=-1, keepdims=True) + EPS)
        xh = xv * r
        dy = dh_ref[...].astype(F32)
        dxh = dy * g_ref[...]
        dx_ref[...] = dres_ref[...] + r * (dxh - xh * jnp.mean(dxh * xh, axis=-1, keepdims=True))

        @pl.when(pl.program_id(0) == 0)
        def _():
            dg_ref[...] = jnp.zeros_like(dg_ref)

        dg_ref[...] += _rowsum0(dy * xh)

    row = pl.BlockSpec((tm, d), lambda i: (i, 0))
    vec = pl.BlockSpec((1, d), lambda i: (0, 0))
    return _pc(body, name=name, grid=(t // tm,), in_specs=[row, row, vec, row], out_specs=[row, vec],
               out_shape=[_sds((t, d), F32), _sds((1, d), F32)], compiler_params=_cp("arbitrary"))(dh, x, g, dres)


def loss_head(y, target, *, name):
    t, d = y.shape
    tm = _pick(t, (256,))

    def body(y_ref, t_ref, loss_ref, dy_ref):
        e = y_ref[...] - t_ref[...]
        dy_ref[...] = e * (1.0 / d)

        @pl.when(pl.program_id(0) == 0)
        def _():
            loss_ref[...] = jnp.zeros_like(loss_ref)

        loss_ref[...] += _rowsum0(jnp.sum(e * e, axis=1, keepdims=True)) * (0.5 / d)

    row = pl.BlockSpec((tm, d), lambda i: (i, 0))
    return _pc(body, name=name, grid=(t // tm,), in_specs=[row, row],
               out_specs=[pl.BlockSpec((1, 1), lambda i: (0, 0)), row],
               out_shape=[_sds((1, 1), F32), _sds((t, d), F32)], compiler_params=_cp("arbitrary"))(y, target)


def _sg_masks():
    lane = lax.broadcasted_iota(jnp.int32, (CHUNK, CHUNK), 1)
    row = lax.broadcasted_iota(jnp.int32, (CHUNK, CHUNK), 0)
    return lane < 64, lane <= row, row <= lane


def _sg_gate(vn_chunk, w_ref, bias_ref, p, first_group, tril):
    wa = jnp.where(tril, w_ref[2 * p], 0.0).astype(BF16)
    wb = jnp.where(tril, w_ref[2 * p + 1], 0.0).astype(BF16)
    oa = jnp.dot(wa, vn_chunk, preferred_element_type=F32)
    ob = jnp.dot(wb, vn_chunk, preferred_element_type=F32)
    return jnp.where(first_group, oa, ob) + bias_ref[:, p * 128:(p + 1) * 128]


def mixa_fwd(z, ln_g, ln_b, sg_w, sg_bias, *, name):
    t = z.shape[0]
    tm = _pick(t, (256,))

    def body(z_ref, g_ref, b_ref, w_ref, bias_ref, o_ref):
        first_group, tril, _ = _sg_masks()
        zv = z_ref[...].astype(F32)
        u = _gelu(zv[:, :SG_WIDTH])
        v = _gelu(zv[:, SG_WIDTH:])
        vh, _ = _ln_stats(v)
        vn = (vh * g_ref[...] + b_ref[...]).astype(BF16)
        for c in range(tm // CHUNK):
            rows = slice(c * CHUNK, (c + 1) * CHUNK)
            for p in range(4):
                cols = slice(p * 128, (p + 1) * 128)
                o = _sg_gate(vn[rows, cols], w_ref, bias_ref, p, first_group, tril)
                o_ref[rows, cols] = (u[rows, cols] * o).astype(o_ref.dtype)

    vec = pl.BlockSpec((1, SG_WIDTH), lambda i: (0, 0))
    return _pc(body, name=name, grid=(t // tm,),
               in_specs=[pl.BlockSpec((tm, 2 * SG_WIDTH), lambda i: (i, 0)), vec, vec,
                         pl.BlockSpec((8, CHUNK, CHUNK), lambda i: (0, 0, 0)),
                         pl.BlockSpec((CHUNK, SG_WIDTH), lambda i: (0, 0))],
               out_specs=pl.BlockSpec((tm, SG_WIDTH), lambda i: (i, 0)),
               out_shape=_sds((t, SG_WIDTH), BF16), compiler_params=_cp("parallel"))(z, ln_g, ln_b, sg_w, sg_bias)


def mixa_bwd(z, dga, ln_g, ln_b, sg_w, sg_wt, sg_bias, *, name):
    t = z.shape[0]
    tm = _pick(t, (256,))
    nsteps = t // tm

    def body(z_ref, dga_ref, g_ref, b_ref, w_ref, wt_ref, bias_ref, dz_ref, dw_ref, dsgb_ref, dg_ref, db_ref, dvn_s,
             dbias_ref):
        i = pl.program_id(0)
        first_group, tril, triu = _sg_masks()

        @pl.when(i == 0)
        def _():
            dw_ref[...] = jnp.zeros_like(dw_ref)
            dbias_ref[...] = jnp.zeros_like(dbias_ref)
            dg_ref[...] = jnp.zeros_like(dg_ref)
            db_ref[...] = jnp.zeros_like(db_ref)

        zv = z_ref[...].astype(F32)
        zu = zv[:, :SG_WIDTH]
        zg = zv[:, SG_WIDTH:]
        u = _gelu(zu)
        v = _gelu(zg)
        vh, r = _ln_stats(v)
        vn = (vh * g_ref[...] + b_ref[...]).astype(BF16)
        dga_v = dga_ref[...].astype(F32)
        d_o = dga_v * u
        for c in range(tm // CHUNK):
            rows = slice(c * CHUNK, (c + 1) * CHUNK)
            dbias_ref[...] += d_o[rows, :]
            for p in range(4):
                cols = slice(p * 128, (p + 1) * 128)
                vp = vn[rows, cols]
                o = _sg_gate(vp, w_ref, bias_ref, p, first_group, tril)
                dz_ref[rows, cols] = (dga_v[rows, cols] * o * _gelu_grad(zu[rows, cols])).astype(dz_ref.dtype)
                dop = d_o[rows, cols]
                dop_a = jnp.where(first_group, dop, 0.0).astype(BF16)
                dop_b = jnp.where(first_group, 0.0, dop).astype(BF16)
                dw_ref[2 * p] += lax.dot_general(dop_a, vp, NT_DIMS, preferred_element_type=F32)
                dw_ref[2 * p + 1] += lax.dot_general(dop_b, vp, NT_DIMS, preferred_element_type=F32)
                wta = jnp.where(triu, wt_ref[2 * p], 0.0).astype(BF16)
                wtb = jnp.where(triu, wt_ref[2 * p + 1], 0.0).astype(BF16)
                dop16 = dop.astype(BF16)
                dvn_s[rows, cols] = jnp.where(first_group, jnp.dot(wta, dop16, preferred_element_type=F32),
                                              jnp.dot(wtb, dop16, preferred_element_type=F32))
        dvn = dvn_s[...]
        dg_ref[...] += _rowsum0(dvn * vh)
        db_ref[...] += _rowsum0(dvn)
        dv = _ln_bwd(dvn, vh, r, g_ref[...])
        dz_ref[:, SG_WIDTH:] = (dv * _gelu_grad(zg)).astype(dz_ref.dtype)

        @pl.when(i == nsteps - 1)
        def _():
            for gi in range(8):
                dw_ref[gi] = jnp.where(tril, dw_ref[gi], 0.0)
            r_id = lax.broadcasted_iota(jnp.int32, (SG_WIDTH, 128), 0) // SB_HEAD_DIM
            c_id = lax.broadcasted_iota(jnp.int32, (SG_WIDTH, 128), 1)
            dsgb_ref[...] = _split_dot(dbias_ref[...], (r_id == c_id).astype(BF16))

    vec = pl.BlockSpec((1, SG_WIDTH), lambda i: (0, 0))
    wspec = pl.BlockSpec((8, CHUNK, CHUNK), lambda i: (0, 0, 0))
    bspec = pl.BlockSpec((CHUNK, SG_WIDTH), lambda i: (0, 0))
    sgb = pl.BlockSpec((CHUNK, 128), lambda i: (0, 0))
    return _pc(body, name=name, grid=(nsteps,),
               in_specs=[pl.BlockSpec((tm, 2 * SG_WIDTH), lambda i: (i, 0)), pl.BlockSpec((tm, SG_WIDTH), lambda i: (i, 0)),
                         vec, vec, wspec, wspec, bspec],
               out_specs=[pl.BlockSpec((tm, 2 * SG_WIDTH), lambda i: (i, 0)), wspec, sgb, vec, vec],
               out_shape=[_sds((t, 2 * SG_WIDTH), BF16), _sds((8, CHUNK, CHUNK), F32), _sds((CHUNK, 128), F32),
                          _sds((1, SG_WIDTH), F32), _sds((1, SG_WIDTH), F32)],
               scratch_shapes=[pltpu.VMEM((tm, SG_WIDTH), F32), pltpu.VMEM((CHUNK, SG_WIDTH), F32)],
               compiler_params=_cp("arbitrary"))(z, dga, ln_g, ln_b, sg_w, sg_wt, sg_bias)


def _glu(zv):
    return zv[:, :CV_WIDTH] * _sigmoid(zv[:, CV_WIDTH:])


_SUB = 8


def _cv_phases(x_s, tm):
    rows = CV_HALO + tm - _SUB
    for r in range(1, _SUB):
        x_s[r, 0:rows, :] = x_s[0, pl.ds(r, rows), :]


def _cv_tap(x_s, o, tm):
    return x_s[o % _SUB, pl.ds(o - o % _SUB, tm), :]


def _cv_fill(zm_ref, zh_ref, x_s, i, tm):
    x_s[0, 0:CV_HALO, :] = jnp.where(i > 0, _glu(zh_ref[...].astype(F32)), 0.0)
    x_s[0, CV_HALO:CV_HALO + tm, :] = _glu(zm_ref[...].astype(F32))
    _cv_phases(x_s, tm)


def _cv_conv(x_s, w_ref, cb_ref, tm):
    acc = jnp.zeros((tm, CV_WIDTH), F32) + cb_ref[...]
    for k in range(CV_KERNEL):
        acc = acc + w_ref[k:k + 1, :] * _cv_tap(x_s, CV_HALO - (CV_KERNEL - 1) + k, tm)
    return acc


def _cv_specs(tm):
    zm = pl.BlockSpec((tm, 2 * CV_WIDTH), lambda i: (i, 1))
    zh = pl.BlockSpec((CV_HALO, 2 * CV_WIDTH), lambda i: (jnp.maximum(i * (tm // CV_HALO) - 1, 0), 1))
    w = pl.BlockSpec((CV_KERNEL, CV_WIDTH), lambda i: (0, 0))
    vec = pl.BlockSpec((1, CV_WIDTH), lambda i: (0, 0))
    return zm, zh, w, vec


def mixb_fwd(z, cv_w, cv_b, ln_g, ln_b, *, name):
    t = z.shape[0]
    tm = _pick(t, (256,))

    def body(zm_ref, zh_ref, w_ref, cb_ref, g_ref, b_ref, o_ref, x_s):
        _cv_fill(zm_ref, zh_ref, x_s, pl.program_id(0), tm)
        c1 = _cv_conv(x_s, w_ref, cb_ref, tm)
        ch, _ = _ln_stats(c1)
        o_ref[...] = _silu(ch * g_ref[...] + b_ref[...]).astype(o_ref.dtype)

    zm, zh, w, vec = _cv_specs(tm)
    return _pc(body, name=name, grid=(t // tm,), in_specs=[zm, zh, w, vec, vec, vec],
               out_specs=pl.BlockSpec((tm, CV_WIDTH), lambda i: (i, 0)), out_shape=_sds((t, CV_WIDTH), BF16),
               scratch_shapes=[pltpu.VMEM((_SUB, CV_HALO + tm, CV_WIDTH), F32)],
               compiler_params=_cp("parallel"))(z, z, cv_w, cv_b, ln_g, ln_b)


def mixb_bwd1(z, dc3, cv_w, cv_b, ln_g, ln_b, *, name):
    t = z.shape[0]
    tm = _pick(t, (256,))

    def body(zm_ref, zh_ref, dc3_ref, w_ref, cb_ref, g_ref, b_ref, dc1_ref, dw_ref, dcb_ref, dg_ref, db_ref, x_s):
        i = pl.program_id(0)

        @pl.when(i == 0)
        def _():
            dw_ref[...] = jnp.zeros_like(dw_ref)
            dcb_ref[...] = jnp.zeros_like(dcb_ref)
            dg_ref[...] = jnp.zeros_like(dg_ref)
            db_ref[...] = jnp.zeros_like(db_ref)

        _cv_fill(zm_ref, zh_ref, x_s, i, tm)
        c1 = _cv_conv(x_s, w_ref, cb_ref, tm)
        ch, r = _ln_stats(c1)
        c2 = ch * g_ref[...] + b_ref[...]
        dc2 = dc3_ref[...].astype(F32) * _silu_grad(c2)
        dg_ref[...] += _rowsum0(dc2 * ch)
        db_ref[...] += _rowsum0(dc2)
        dc1 = _ln_bwd(dc2, ch, r, g_ref[...])
        dc1_ref[...] = dc1
        dcb_ref[...] += _rowsum0(dc1)
        for k in range(CV_KERNEL):
            dw_ref[k:k + 1, :] += _rowsum0(dc1 * _cv_tap(x_s, CV_HALO - (CV_KERNEL - 1) + k, tm))

    zm, zh, w, vec = _cv_specs(tm)
    row = pl.BlockSpec((tm, CV_WIDTH), lambda i: (i, 0))
    return _pc(body, name=name, grid=(t // tm,), in_specs=[zm, zh, row, w, vec, vec, vec],
               out_specs=[row, w, vec, vec, vec],
               out_shape=[_sds((t, CV_WIDTH), F32), _sds((CV_KERNEL, CV_WIDTH), F32), _sds((1, CV_WIDTH), F32),
                          _sds((1, CV_WIDTH), F32), _sds((1, CV_WIDTH), F32)],
               scratch_shapes=[pltpu.VMEM((_SUB, CV_HALO + tm, CV_WIDTH), F32)],
               compiler_params=_cp("arbitrary"))(z, z, dc3, cv_w, cv_b, ln_g, ln_b)


def mixb_bwd2(z, dc1, cv_w, *, name):
    t = z.shape[0]
    tm = _pick(t, (256,))
    nsteps = t // tm

    def body(zm_ref, dm_ref, dh_ref, w_ref, dz_ref, y_s):
        i = pl.program_id(0)
        y_s[0, 0:tm, :] = dm_ref[...]
        y_s[0, tm:tm + CV_HALO, :] = jnp.where(i < nsteps - 1, dh_ref[...], 0.0)
        _cv_phases(y_s, tm)
        dc0 = jnp.zeros((tm, CV_WIDTH), F32)
        for k in range(CV_KERNEL):
            dc0 = dc0 + w_ref[k:k + 1, :] * _cv_tap(y_s, CV_KERNEL - 1 - k, tm)
        zv = zm_ref[...].astype(F32)
        p = zv[:, :CV_WIDTH]
        s = _sigmoid(zv[:, CV_WIDTH:])
        dz_ref[:, :CV_WIDTH] = (dc0 * s).astype(dz_ref.dtype)
        dz_ref[:, CV_WIDTH:] = (dc0 * p * s * (1.0 - s)).astype(dz_ref.dtype)

    last = t // CV_HALO - 1
    return _pc(body, name=name, grid=(nsteps,),
               in_specs=[pl.BlockSpec((tm, 2 * CV_WIDTH), lambda i: (i, 1)),
                         pl.BlockSpec((tm, CV_WIDTH), lambda i: (i, 0)),
                         pl.BlockSpec((CV_HALO, CV_WIDTH), lambda i: (jnp.minimum((i + 1) * (tm // CV_HALO), last), 0)),
                         pl.BlockSpec((CV_KERNEL, CV_WIDTH), lambda i: (0, 0))],
               out_specs=pl.BlockSpec((tm, 2 * CV_WIDTH), lambda i: (i, 0)),
               out_shape=_sds((t, 2 * CV_WIDTH), BF16),
               scratch_shapes=[pltpu.VMEM((_SUB, tm + CV_HALO, CV_WIDTH), F32)],
               compiler_params=_cp("parallel"))(z, dc1, dc1, cv_w)


def _group_ones():
    r = lax.broadcasted_iota(jnp.int32, (SB_WIDTH, SB_WIDTH), 0) // SB_HEAD_DIM
    c = lax.broadcasted_iota(jnp.int32, (SB_WIDTH, SB_WIDTH), 1) // SB_HEAD_DIM
    return (r == c).astype(BF16)


def attn_prep(z, gq, gk, *, name):
    t = z.shape[0]
    tm = _pick(t, (256,))
    scale = 1.0 / math.sqrt(SB_HEAD_DIM)

    def body(q_ref, k_ref, gq_ref, gk_ref, qo_ref, ko_ref):
        ones = _group_ones()
        for src, g_ref, dst, mul in ((q_ref, gq_ref, qo_ref, scale), (k_ref, gk_ref, ko_ref, 1.0)):
            v = src[...].astype(F32)
            r = lax.rsqrt(_split_dot(v * v, ones) * (1.0 / SB_HEAD_DIM) + EPS)
            dst[...] = ((v * r * g_ref[...]).astype(BF16).astype(F32) * mul).astype(dst.dtype)

    vec = pl.BlockSpec((1, SB_WIDTH), lambda i: (0, 0))
    row = pl.BlockSpec((tm, SB_WIDTH), lambda i: (i, 0))
    return _pc(body, name=name, grid=(t // tm,),
               in_specs=[pl.BlockSpec((tm, SB_WIDTH), lambda i: (i, COL_C // SB_WIDTH)),
                         pl.BlockSpec((tm, SB_WIDTH), lambda i: (i, COL_C // SB_WIDTH + 1)), vec, vec],
               out_specs=[row, row], out_shape=[_sds((t, SB_WIDTH), BF16), _sds((t, SB_WIDTH), BF16)],
               compiler_params=_cp("parallel"))(z, z, gq, gk)


_KB = Q_BLOCK
_PAIR = 2 * _KB


def _attn_tq(t):
    return _pick(t, (512, 256, 128))


def _attn_consts(tq):
    first_head = lax.broadcasted_iota(jnp.int32, (_KB, 128), 1) < SB_HEAD_DIM
    r2 = lax.broadcasted_iota(jnp.int32, (_PAIR, _PAIR), 0)
    c2 = lax.broadcasted_iota(jnp.int32, (_PAIR, _PAIR), 1)
    same = (r2 // _KB) == (c2 // _KB)
    m_suffix = (same & (r2 > c2)).astype(BF16)
    m_prefix = (same & (r2 < c2)).astype(BF16)
    row = lax.broadcasted_iota(jnp.int32, (tq, _PAIR), 0)
    col = lax.broadcasted_iota(jnp.int32, (tq, _PAIR), 1)
    return first_head, m_suffix, m_prefix, row, col & (_KB - 1), col < _KB


def _sb_logits(z, causal):
    sp = jnp.log(1.0 + jnp.exp(-jnp.abs(z)))
    g = jnp.minimum(z, 0.0) - sp
    l1m = g - z
    if causal is not None:
        l1m = jnp.where(causal, l1m, 0.0)
    return g, l1m


def _stack_heads(first_head, v):
    zero = jnp.zeros_like(v)
    return jnp.concatenate([jnp.where(first_head, v, zero), jnp.where(first_head, zero, v)], axis=0)


def _add_rows(x, upd, r0):
    return x + upd if r0 == 0 else jnp.concatenate([x[:r0], x[r0:] + upd], axis=0)


def _pair_sums(x):
    return jnp.sum(x[:, :_KB], axis=1, keepdims=True), jnp.sum(x[:, _KB:], axis=1, keepdims=True)


def _attn_specs(t, tq):
    qspec = pl.BlockSpec((tq, 128), lambda h, i: (i, h))
    kspec = pl.BlockSpec((t, 128), lambda h, i: (0, h))
    vspec = pl.BlockSpec((t, 128), lambda h, i: (0, (COL_C + 2 * SB_WIDTH) // 128 + h))
    return qspec, kspec, vspec


def attn_fwd(q, k, z, *, name):
    t = q.shape[0]
    tq = _attn_tq(t)
    nd = tq // _KB
    assert nd % 2 == 0, "the key-block loop takes two blocks per pass"

    def body(q_ref, k_ref, v_ref, o_ref):
        qt = pl.program_id(1)
        first_head, m_suffix, _, row, key, is_first = _attn_consts(tq)
        qv = q_ref[...]

        def step(kb, state, causal, r0=0):
            acc, ca, cb = state
            off = pl.multiple_of(kb * _KB, _KB)
            kcat = _stack_heads(first_head, k_ref[pl.ds(off, _KB), :])
            vcat = _stack_heads(first_head, v_ref[pl.ds(off, _KB), :])
            zz = lax.dot_general(qv[r0:], kcat, NT_DIMS, preferred_element_type=F32)
            g, l1m = _sb_logits(zz, None if causal is None else causal[r0:])
            a = jnp.exp(g + _block_sums(l1m, m_suffix) + jnp.where(is_first[r0:], ca[r0:], cb[r0:]))
            if causal is not None:
                a = jnp.where(causal[r0:], a, 0.0)
            sa, sb = _pair_sums(l1m)
            pv = jnp.dot(a.astype(BF16), vcat, preferred_element_type=F32)
            return _add_rows(acc, pv, r0), _add_rows(ca, sa, r0), _add_rows(cb, sb, r0)

        c0 = jnp.zeros((tq, 1), F32)
        state = (jnp.zeros((tq, 128), F32), c0, c0)
        for d in reversed(range(nd)):
            state = step(qt * nd + d, state, key + d * _KB < row, d * _KB)
        def one_pass(s, st):
            for u in range(nd):
                st = step((qt - s) * nd - 1 - u, st, None)
            return st

        state = lax.fori_loop(0, qt, one_pass, state)
        o_ref[...] = state[0].astype(o_ref.dtype)

    qspec, kspec, vspec = _attn_specs(t, tq)
    return _pc(body, name=name, grid=(SB_WIDTH // 128, t // tq), in_specs=[qspec, kspec, vspec], out_specs=qspec,
               out_shape=_sds((t, SB_WIDTH), BF16), compiler_params=_cp("parallel", "arbitrary"))(q, k, z)


def attn_bwd(q, k, z, do, *, name):
    t = q.shape[0]
    tq = _attn_tq(t)
    nd = tq // _KB
    assert nd % 2 == 0, "the key-block loop takes two blocks per pass"
    nk = t // _KB

    def body(q_ref, k_ref, v_ref, do_ref, dq_ref, dk_ref, dv_ref, e_s, sg_s):
        qt = pl.program_id(1)
        first_head, m_suffix, m_prefix, row, key, is_first = _attn_consts(tq)

        @pl.when(qt == 0)
        def _():
            dk_ref[...] = jnp.zeros_like(dk_ref)
            dv_ref[...] = jnp.zeros_like(dv_ref)

        qv = q_ref[...]
        dov = do_ref[...]

        def halves(x):
            return jnp.where(first_head, x[:_KB], x[_KB:])

        def sweep1(kb, state, causal, r0=0):
            ca, cb = state
            off = pl.multiple_of(kb * _KB, _KB)
            kcat = _stack_heads(first_head, k_ref[pl.ds(off, _KB), :])
            vcat = _stack_heads(first_head, v_ref[pl.ds(off, _KB), :])
            zz = lax.dot_general(qv[r0:], kcat, NT_DIMS, preferred_element_type=F32)
            g, l1m = _sb_logits(zz, None if causal is None else causal[r0:])
            a = jnp.exp(g + _block_sums(l1m, m_suffix) + jnp.where(is_first[r0:], ca[r0:], cb[r0:]))
            if causal is not None:
                a = jnp.where(causal[r0:], a, 0.0)
            da = lax.dot_general(dov[r0:], vcat, NT_DIMS, preferred_element_type=F32)
            e_s[kb, r0:, :] = a * da
            sg_s[kb, r0:, :] = jnp.exp(g).astype(BF16)
            dv_ref[pl.ds(off, _KB), :] += halves(lax.dot_general(a.astype(BF16), dov[r0:], TN_DIMS,
                                                                 preferred_element_type=F32))
            sa, sb = _pair_sums(l1m)
            return _add_rows(ca, sa, r0), _add_rows(cb, sb, r0)

        def sweep2(kb, state, causal, r0=0):
            dq, pa, pb = state
            off = pl.multiple_of(kb * _KB, _KB)
            kcat = _stack_heads(first_head, k_ref[pl.ds(off, _KB), :])
            e = e_s[kb, r0:, :]
            s = sg_s[kb, r0:, :].astype(F32)
            dz = e * (1.0 - s) - (jnp.where(is_first[r0:], pa[r0:], pb[r0:]) + _block_sums(e, m_prefix)) * s
            if causal is not None:
                dz = jnp.where(causal[r0:], dz, 0.0)
            dz = dz.astype(BF16)
            dk_ref[pl.ds(off, _KB), :] += halves(lax.dot_general(dz, qv[r0:], TN_DIMS, preferred_element_type=F32))
            sa, sb = _pair_sums(e)
            return (_add_rows(dq, jnp.dot(dz, kcat, preferred_element_type=F32), r0), _add_rows(pa, sa, r0),
                    _add_rows(pb, sb, r0))

        c0 = jnp.zeros((tq, 1), F32)
        st1 = (c0, c0)
        for d in reversed(range(nd)):
            st1 = sweep1(qt * nd + d, st1, key + d * _KB < row, d * _KB)
        def pass1(s, st):
            for u in range(nd):
                st = sweep1((qt - s) * nd - 1 - u, st, None)
            return st

        def pass2(s, st):
            for u in range(nd):
                st = sweep2(s * nd + u, st, None)
            return st

        lax.fori_loop(0, qt, pass1, st1)
        st2 = lax.fori_loop(0, qt, pass2, (jnp.zeros((tq, 128), F32), c0, c0))
        for d in range(nd):
            st2 = sweep2(qt * nd + d, st2, key + d * _KB < row, d * _KB)
        dq_ref[...] = st2[0]

    qspec, kspec, vspec = _attn_specs(t, tq)
    acc = pl.BlockSpec((t, 128), lambda h, i: (0, h))
    return _pc(body, name=name, grid=(SB_WIDTH // 128, t // tq), in_specs=[qspec, kspec, vspec, qspec],
               out_specs=[qspec, acc, acc],
               out_shape=[_sds((t, SB_WIDTH), F32), _sds((t, SB_WIDTH), F32), _sds((t, SB_WIDTH), F32)],
               scratch_shapes=[pltpu.VMEM((nk, tq, _PAIR), F32), pltpu.VMEM((nk, tq, _PAIR), BF16)],
               compiler_params=_cp("parallel", "arbitrary"))(q, k, z, do)


def attn_post_bwd(z, dq, dk, dv, gq, gk, *, name):
    t = z.shape[0]
    tm = _pick(t, (256,))
    scale = 1.0 / math.sqrt(SB_HEAD_DIM)

    def body(q_ref, k_ref, dq_ref, dk_ref, dv_ref, gq_ref, gk_ref, dz_ref, dgq_ref, dgk_ref):
        ones = _group_ones()

        @pl.when(pl.program_id(0) == 0)
        def _():
            dgq_ref[...] = jnp.zeros_like(dgq_ref)
            dgk_ref[...] = jnp.zeros_like(dgk_ref)

        for idx, (src, d_ref, g_ref, dg_ref, mul) in enumerate(
                ((q_ref, dq_ref, gq_ref, dgq_ref, scale), (k_ref, dk_ref, gk_ref, dgk_ref, 1.0))):
            v = src[...].astype(F32)
            r = lax.rsqrt(_split_dot(v * v, ones) * (1.0 / SB_HEAD_DIM) + EPS)
            vh = v * r
            dn = d_ref[...] * mul
            dxh = dn * g_ref[...]
            m = _split_dot(dxh * vh, ones) * (1.0 / SB_HEAD_DIM)
            dz_ref[:, idx * SB_WIDTH:(idx + 1) * SB_WIDTH] = (r * (dxh - vh * m)).astype(dz_ref.dtype)
            s = _rowsum0(dn * vh)
            f = jnp.broadcast_to(s[:, 0:128] + s[:, 128:256] + s[:, 256:384] + s[:, 384:512], dg_ref.shape)
            dg_ref[...] += f + pltpu.roll(f, 64, 1)
        dz_ref[:, 2 * SB_WIDTH:] = dv_ref[...].astype(dz_ref.dtype)

    vec = pl.BlockSpec((1, SB_WIDTH), lambda i: (0, 0))
    row = pl.BlockSpec((tm, SB_WIDTH), lambda i: (i, 0))
    fold = pl.BlockSpec((8, 128), lambda i: (0, 0))
    return _pc(body, name=name, grid=(t // tm,),
               in_specs=[pl.BlockSpec((tm, SB_WIDTH), lambda i: (i, COL_C // SB_WIDTH)),
                         pl.BlockSpec((tm, SB_WIDTH), lambda i: (i, COL_C // SB_WIDTH + 1)), row, row, row, vec, vec],
               out_specs=[pl.BlockSpec((tm, 3 * SB_WIDTH), lambda i: (i, 0)), fold, fold],
               out_shape=[_sds((t, 3 * SB_WIDTH), BF16), _sds((8, 128), F32), _sds((8, 128), F32)],
               compiler_params=_cp("arbitrary"))(z, z, dq, dk, dv, gq, gk)


_GW = 512


def merge_fwd(z, ya, yb, yc, b_gate, *, name):
    t = z.shape[0]
    tm = _pick(t, (512, 256))

    def body(za_ref, zb_ref, zc_ref, ya_ref, yb_ref, yc_ref, bg_ref, o_ref):
        acc = jnp.zeros((tm, _GW), F32)
        for b, (zr, yr) in enumerate(((za_ref, ya_ref), (zb_ref, yb_ref), (zc_ref, yc_ref))):
            acc = acc + _sigmoid(zr[...].astype(F32) + bg_ref[b:b + 1, :]) * yr[...].astype(F32)
        o_ref[...] = acc.astype(o_ref.dtype)

    def zspec(b):
        return pl.BlockSpec((tm, _GW), lambda i, j: (i, COL_G // _GW + 2 * b + j))

    yspec = pl.BlockSpec((tm, _GW), lambda i, j: (i, j))
    return _pc(body, name=name, grid=(t // tm, D_MODEL // _GW),
               in_specs=[zspec(0), zspec(1), zspec(2), yspec, yspec, yspec, pl.BlockSpec((3, _GW), lambda i, j: (0, j))],
               out_specs=yspec, out_shape=_sds((t, D_MODEL), BF16),
               compiler_params=_cp("parallel", "parallel"))(z, z, z, ya, yb, yc, b_gate)


def merge_bwd(z, ya, yb, yc, b_gate, dm, *, name):
    t = z.shape[0]
    tm = _pick(t, (512, 256))

    def body(za_ref, zb_ref, zc_ref, ya_ref, yb_ref, yc_ref, bg_ref, dm_ref,
             dya_ref, dyb_ref, dyc_ref, dza_ref, dzb_ref, dzc_ref, dbg_ref):
        @pl.when(pl.program_id(1) == 0)
        def _():
            dbg_ref[...] = jnp.zeros_like(dbg_ref)

        dmv = dm_ref[...].astype(F32)
        for b, (zr, yr, dyr, dzr) in enumerate(((za_ref, ya_ref, dya_ref, dza_ref), (zb_ref, yb_ref, dyb_ref, dzb_ref),
                                                (zc_ref, yc_ref, dyc_ref, dzc_ref))):
            s = _sigmoid(zr[...].astype(F32) + bg_ref[b:b + 1, :])
            dyr[...] = (dmv * s).astype(dyr.dtype)
            dg = dmv * yr[...].astype(F32) * s * (1.0 - s)
            dzr[...] = dg.astype(dzr.dtype)
            dbg_ref[b:b + 1, :] += _rowsum0(dg)

    def zspec(b):
        return pl.BlockSpec((tm, _GW), lambda j, i: (i, COL_G // _GW + 2 * b + j))

    yspec = pl.BlockSpec((tm, _GW), lambda j, i: (i, j))
    bspec = pl.BlockSpec((3, _GW), lambda j, i: (0, j))
    full = _sds((t, D_MODEL), BF16)
    return _pc(body, name=name, grid=(D_MODEL // _GW, t // tm),
               in_specs=[zspec(0), zspec(1), zspec(2), yspec, yspec, yspec, bspec, yspec],
               out_specs=[yspec] * 6 + [bspec], out_shape=[full] * 6 + [_sds((3, D_MODEL), F32)],
               compiler_params=_cp("parallel", "arbitrary"))(z, z, z, ya, yb, yc, b_gate, dm)


_FW = 1408
_FH = D_FF // _FW


def _ffn_fill(m_ref, h_ref, x_s, i, tm):
    x_s[0:FFN_HALO, :] = jnp.where(i > 0, h_ref[...].astype(F32), 0.0)
    x_s[FFN_HALO:FFN_HALO + tm, :] = m_ref[...].astype(F32)


def _ffn_taps(x_s, tm):
    return [x_s[pl.ds(FFN_HALO - (FFN_KERNEL - 1) + k, tm), :] for k in range(FFN_KERNEL)]


def _ffn_conv(x_s, w_ref, b_ref, tm, taps=None):
    taps = _ffn_taps(x_s, tm) if taps is None else taps
    acc = jnp.zeros((tm, _FW), F32) + b_ref[...]
    for k in range(FFN_KERNEL):
        acc = acc + w_ref[k:k + 1, :] * taps[k]
    return acc


def ffn_mid_fwd(up, cw, cb, *, name):
    t = up.shape[0]
    tm = _pick(t, (256,))

    def body(gm_ref, gh_ref, vm_ref, vh_ref, wg_ref, wv_ref, bg_ref, bv_ref, o_ref, xg_s, xv_s):
        i = pl.program_id(0)
        _ffn_fill(gm_ref, gh_ref, xg_s, i, tm)
        _ffn_fill(vm_ref, vh_ref, xv_s, i, tm)
        o_ref[...] = (_silu(_ffn_conv(xg_s, wg_ref, bg_ref, tm)) * _ffn_conv(xv_s, wv_ref, bv_ref, tm)).astype(o_ref.dtype)

    def main(off):
        return pl.BlockSpec((tm, _FW), lambda i, j: (i, j + off))

    def halo(off):
        return pl.BlockSpec((FFN_HALO, _FW), lambda i, j: (jnp.maximum(i * (tm // FFN_HALO) - 1, 0), j + off))

    def wspec(off):
        return pl.BlockSpec((FFN_KERNEL, _FW), lambda i, j: (0, j + off))

    def bspec(off):
        return pl.BlockSpec((1, _FW), lambda i, j: (0, j + off))

    return _pc(body, name=name, grid=(t // tm, _FH),
               in_specs=[main(0), halo(0), main(_FH), halo(_FH), wspec(0), wspec(_FH), bspec(0), bspec(_FH)],
               out_specs=pl.BlockSpec((tm, _FW), lambda i, j: (i, j)), out_shape=_sds((t, D_FF), BF16),
               scratch_shapes=[pltpu.VMEM((FFN_HALO + tm, _FW), F32), pltpu.VMEM((FFN_HALO + tm, _FW), F32)],
               compiler_params=_cp("parallel", "parallel"))(up, up, up, up, cw, cw, cb, cb)


def ffn_mid_bwd1(up, dact, cw, cb, *, name):
    t = up.shape[0]
    tm = _pick(t, (256,))

    def body(gm_ref, gh_ref, vm_ref, vh_ref, da_ref, wg_ref, wv_ref, bg_ref, bv_ref, d_ref, dw_ref, db_ref, xg_s, xv_s):
        j = pl.program_id(0)
        i = pl.program_id(1)

        @pl.when(i == 0)
        def _():
            dw_ref[...] = jnp.zeros_like(dw_ref)
            db_ref[...] = jnp.zeros_like(db_ref)

        _ffn_fill(gm_ref, gh_ref, xg_s, i, tm)
        _ffn_fill(vm_ref, vh_ref, xv_s, i, tm)
        gate_taps = _ffn_taps(xg_s, tm)
        gate = _ffn_conv(xg_s, wg_ref, bg_ref, tm, gate_taps)
        da = da_ref[...].astype(F32)

        def finish(d, taps):
            d_ref[...] = d.astype(d_ref.dtype)
            db_ref[...] += _rowsum0(d)
            for k in range(FFN_KERNEL):
                dw_ref[k:k + 1, :] += _rowsum0(d * taps[k])

        @pl.when(j < _FH)
        def _():
            finish(da * _ffn_conv(xv_s, wv_ref, bv_ref, tm) * _silu_grad(gate), gate_taps)

        @pl.when(j >= _FH)
        def _():
            finish(da * _silu(gate), _ffn_taps(xv_s, tm))

    def main(off):
        return pl.BlockSpec((tm, _FW), lambda j, i: (i, j % _FH + off))

    def halo(off):
        return pl.BlockSpec((FFN_HALO, _FW), lambda j, i: (jnp.maximum(i * (tm // FFN_HALO) - 1, 0), j % _FH + off))

    def wspec(off):
        return pl.BlockSpec((FFN_KERNEL, _FW), lambda j, i: (0, j % _FH + off))

    def bspec(off):
        return pl.BlockSpec((1, _FW), lambda j, i: (0, j % _FH + off))

    return _pc(body, name=name, grid=(2 * _FH, t // tm),
               in_specs=[main(0), halo(0), main(_FH), halo(_FH), pl.BlockSpec((tm, _FW), lambda j, i: (i, j % _FH)),
                         wspec(0), wspec(_FH), bspec(0), bspec(_FH)],
               out_specs=[pl.BlockSpec((tm, _FW), lambda j, i: (i, j)), pl.BlockSpec((FFN_KERNEL, _FW), lambda j, i: (0, j)),
                          pl.BlockSpec((1, _FW), lambda j, i: (0, j))],
               out_shape=[_sds((t, 2 * D_FF), BF16), _sds((FFN_KERNEL, 2 * D_FF), F32), _sds((1, 2 * D_FF), F32)],
               scratch_shapes=[pltpu.VMEM((FFN_HALO + tm, _FW), F32), pltpu.VMEM((FFN_HALO + tm, _FW), F32)],
               compiler_params=_cp("parallel", "arbitrary"))(up, up, up, up, dact, cw, cw, cb, cb)


def ffn_mid_bwd2(dupc, cw, *, name):
    t = dupc.shape[0]
    tm = _pick(t, (256,))
    nsteps = t // tm
    last = t // FFN_HALO - 1

    def body(m_ref, h_ref, w_ref, o_ref, y_s):
        i = pl.program_id(0)
        y_s[0:tm, :] = m_ref[...].astype(F32)
        y_s[tm:tm + FFN_HALO, :] = jnp.where(i < nsteps - 1, h_ref[...].astype(F32), 0.0)
        acc = jnp.zeros((tm, _FW), F32)
        for k in range(FFN_KERNEL):
            acc = acc + w_ref[k:k + 1, :] * y_s[pl.ds(FFN_KERNEL - 1 - k, tm), :]
        o_ref[...] = acc.astype(o_ref.dtype)

    return _pc(body, name=name, grid=(nsteps, 2 * _FH),
               in_specs=[pl.BlockSpec((tm, _FW), lambda i, j: (i, j)),
                         pl.BlockSpec((FFN_HALO, _FW), lambda i, j: (jnp.minimum((i + 1) * (tm // FFN_HALO), last), j)),
                         pl.BlockSpec((FFN_KERNEL, _FW), lambda i, j: (0, j))],
               out_specs=pl.BlockSpec((tm, _FW), lambda i, j: (i, j)), out_shape=_sds((t, 2 * D_FF), BF16),
               scratch_shapes=[pltpu.VMEM((tm + FFN_HALO, _FW), F32)],
               compiler_params=_cp("parallel", "parallel"))(dupc, dupc, cw)


def _vec(v):
    return v.reshape(1, -1)


def _layer_consts(p):
    return dict(
        sg_bias=jnp.repeat(p['sg_b'].T, SB_HEAD_DIM, axis=1),
        sg_wt=jnp.swapaxes(p['sg_w'], 1, 2),
        gq=jnp.tile(p['q_norm_g'], SB_WIDTH // SB_HEAD_DIM).reshape(1, -1),
        gk=jnp.tile(p['k_norm_g'], SB_WIDTH // SB_HEAD_DIM).reshape(1, -1),
    )


def layer_fwd(x, p, after=(), rest=None):
    c = _layer_consts(p)
    z, h = mm_norm_nn(x, _vec(p['ln1_g']), p['w_in'], name="in_proj", after=after)
    ga = mixa_fwd(z, _vec(p['sg_ln_g']), _vec(p['sg_ln_b']), p['sg_w'], c['sg_bias'], name="mixa_fwd")
    cb = mixb_fwd(z, p['cv_w'], _vec(p['cv_b']), _vec(p['cv_ln_g']), _vec(p['cv_ln_b']), name="mixb_fwd")
    q, k = attn_prep(z, c['gq'], c['gk'], name="attn_prep")
    ao = attn_fwd(q, k, z, name="attn_fwd")
    if rest is not None:
        p = {**p, **rest([ao])}
    ya = mm_nn(ga, p['w_a_out'], name="a_out")
    yb = mm_nn(cb, p['w_b_out'], name="b_out")
    yc = mm_nn(ao, p['w_c_out'], name="c_out")
    merged = merge_fwd(z, ya, yb, yc, p['b_gate'], name="merge_fwd")
    x1 = mm_nn(merged, p['w_out'], res=x, out_dtype=F32, name="out_proj")
    up, h2 = mm_norm_nn(x1, _vec(p['ln2_g']), p['w_up'], name="up_proj")
    act = ffn_mid_fwd(up, p['ffn_conv_w'], _vec(p['ffn_conv_b']), name="ffn_mid_fwd")
    x2 = mm_nn(act, p['w_down'], res=x1, out_dtype=F32, name="down_proj")
    saved = dict(x=x, z=z, h=h, ga=ga, cb=cb, q=q, k=k, ao=ao, ya=ya, yb=yb, yc=yc, merged=merged, x1=x1, up=up,
                 h2=h2, act=act)
    return x2, saved, p


GRAD_GROUPS = (('w_down', 'w_up'), ('w_out', 'w_a_out', 'w_b_out', 'w_c_out'), ('w_in',))


def layer_bwd(dx2, p, s, after=(), emit=None):
    c = _layer_consts(p)
    g = {}
    emit = emit or (lambda names, grads: ())
    g['w_down'] = mm_tn(s['act'], dx2, name="d_w_down")
    dact = mm_nt(dx2, p['w_down'], out_dtype=BF16, name="d_act", after=after)
    dupc, g['ffn_conv_w'], dcb = ffn_mid_bwd1(s['up'], dact, p['ffn_conv_w'], _vec(p['ffn_conv_b']), name="ffn_mid_bwd1")
    g['ffn_conv_b'] = dcb.reshape(-1)
    dup = ffn_mid_bwd2(dupc, p['ffn_conv_w'], name="ffn_mid_bwd2")
    g['w_up'] = mm_tn(s['h2'], dup, name="d_w_up")
    dh2 = mm_nt(dup, p['w_up'], out_dtype=F32, name="d_h2")
    dx1, dg2 = rms_bwd(dh2, s['x1'], _vec(p['ln2_g']), dx2, name="ln2_bwd")
    g['ln2_g'] = dg2.reshape(-1)
    g['w_out'] = mm_tn(s['merged'], dx1, name="d_w_out")
    dm = mm_nt(dx1, p['w_out'], out_dtype=BF16, name="d_merged", after=emit(GRAD_GROUPS[0], g))
    dya, dyb, dyc, dzg0, dzg1, dzg2, g['b_gate'] = merge_bwd(s['z'], s['ya'], s['yb'], s['yc'], p['b_gate'], dm,
                                                             name="merge_bwd")
    g['w_a_out'] = mm_tn(s['ga'], dya, name="d_w_a_out")
    g['w_b_out'] = mm_tn(s['cb'], dyb, name="d_w_b_out")
    g['w_c_out'] = mm_tn(s['ao'], dyc, name="d_w_c_out")
    dga = mm_nt(dya, p['w_a_out'], out_dtype=BF16, name="d_ga")
    dcb3 = mm_nt(dyb, p['w_b_out'], out_dtype=BF16, name="d_cb")
    dao = mm_nt(dyc, p['w_c_out'], out_dtype=BF16, name="d_ao")
    dza, g['sg_w'], dsgb, dlg, dlb = mixa_bwd(s['z'], dga, _vec(p['sg_ln_g']), _vec(p['sg_ln_b']), p['sg_w'],
                                               c['sg_wt'], c['sg_bias'], name="mixa_bwd")
    g['sg_b'] = dsgb[:, :SG_WIDTH // SB_HEAD_DIM].T
    g['sg_ln_g'] = dlg.reshape(-1)
    g['sg_ln_b'] = dlb.reshape(-1)
    dc1, g['cv_w'], dcvb, dcg, dcbb = mixb_bwd1(s['z'], dcb3, p['cv_w'], _vec(p['cv_b']), _vec(p['cv_ln_g']),
                                                _vec(p['cv_ln_b']), name="mixb_bwd1")
    g['cv_b'] = dcvb.reshape(-1)
    g['cv_ln_g'] = dcg.reshape(-1)
    g['cv_ln_b'] = dcbb.reshape(-1)
    dzb = mixb_bwd2(s['z'], dc1, p['cv_w'], name="mixb_bwd2")
    dq, dk, dv = attn_bwd(s['q'], s['k'], s['z'], dao, name="attn_bwd")
    dzc, dgq, dgk = attn_post_bwd(s['z'], dq, dk, dv, c['gq'], c['gk'], name="attn_post_bwd")
    g['q_norm_g'] = dgq[0, :SB_HEAD_DIM]
    g['k_norm_g'] = dgk[0, :SB_HEAD_DIM]
    dz = jnp.concatenate([dza, dzb, dzc, dzg0, dzg1, dzg2], axis=1)
    g['w_in'] = mm_tn(s['h'], dz, name="d_w_in")
    dh = mm_nt(dz, p['w_in'], out_dtype=F32, name="d_h", after=emit(GRAD_GROUPS[1], g))
    dx, dg1 = rms_bwd(dh, s['x'], _vec(p['ln1_g']), dx1, name="ln1_bwd")
    g['ln1_g'] = dg1.reshape(-1)
    return dx, g


def local_step(x, target, depth, get_layer, on_grads, emit=lambda l: None):
    saved, layers = [], []
    for l in range(depth):
        p, after, rest = get_layer(l, x)
        x, s, p = layer_fwd(x, p, after, rest)
        layers.append(p)
        saved.append(s)
    loss, dx = loss_head(x, target, name="loss_head")
    after = ()
    for l in reversed(range(depth)):
        dx, g = layer_bwd(dx, layers[l], saved[l], after, emit(l))
        after = on_grads(l, g)
    return loss[0, 0], dx


def adamw(w, g, m, v, *, name):
    r, c = w.shape
    tr = _pick(r, (256, 704)) if r * c > 512 * 1024 else r

    def body(w_ref, g_ref, m_ref, v_ref, d_ref, mo_ref, vo_ref):
        gv = g_ref[...]
        mn = ADAM_B1 * m_ref[...] + (1.0 - ADAM_B1) * gv
        vn = ADAM_B2 * v_ref[...] + (1.0 - ADAM_B2) * (gv * gv)
        m_hat = mn / (1.0 - ADAM_B1 ** ADAM_STEP)
        v_hat = vn / (1.0 - ADAM_B2 ** ADAM_STEP)
        d_ref[...] = -ADAM_LR * (m_hat / (jnp.sqrt(v_hat) + ADAM_EPS) + ADAM_WD * w_ref[...])
        mo_ref[...] = mn
        vo_ref[...] = vn

    spec = pl.BlockSpec((tr, c), lambda i: (i, 0))
    out = _sds((r, c), F32)
    return _pc(body, name=name, grid=(r // tr,), in_specs=[spec] * 4, out_specs=[spec] * 3, out_shape=[out] * 3,
               compiler_params=_cp("parallel"))(w, g, m, v)


def _as3(a):
    return a if a.ndim == 3 else a.reshape((1,) + a.shape)


def add_half(g, recv, c_idx, *, name):
    s, rh, w = recv.shape
    tr = _pick(rh, (256, 352, 128))
    nb = rh // tr

    def body(c_ref, g_ref, r_ref, o_ref):
        o_ref[...] = (g_ref[...].astype(F32) + r_ref[...].astype(F32)).astype(o_ref.dtype)

    own = pl.BlockSpec((1, tr, w), lambda k, i, c_ref: (k, c_ref[0] * nb + i, 0))
    half = pl.BlockSpec((1, tr, w), lambda k, i, c_ref: (k, i, 0))
    gs = pltpu.PrefetchScalarGridSpec(num_scalar_prefetch=1, grid=(s, nb), in_specs=[own, half], out_specs=half)
    return _pc(body, name=name, grid_spec=gs, out_shape=_sds((s, rh, w), BF16),
               compiler_params=_cp("parallel", "parallel"))(c_idx, g, recv)


def sum_shard(p, recv, pos_idx, *, by_rows, name):
    _, rh, w = recv.shape
    tr = _pick(rh, (256, 352, 128))
    nb = rh // tr

    def body(pos_ref, p_ref, r_ref, o_ref):
        acc = p_ref[0].astype(F32)
        for j in range(N_CHIPS - 1):
            acc = acc + r_ref[j].astype(F32)
        o_ref[...] = acc

    if by_rows:
        own = pl.BlockSpec((1, tr, w), lambda i, pos_ref: (pos_ref[0], i, 0))
    else:
        own = pl.BlockSpec((1, tr, w), lambda i, pos_ref: (0, i, pos_ref[0]))
    gs = pltpu.PrefetchScalarGridSpec(num_scalar_prefetch=1, grid=(nb,),
                                      in_specs=[own, pl.BlockSpec((N_CHIPS - 1, tr, w), lambda i, pos_ref: (0, i, 0))],
                                      out_specs=pl.BlockSpec((tr, w), lambda i, pos_ref: (pos_ref[1] * nb + i, 0)))
    return _pc(body, name=name, grid_spec=gs, out_shape=_sds((2 * rh, w), F32),
               compiler_params=_cp("parallel"))(pos_idx, p, recv)


def sum_slots(slab, *, name):
    _, r, w = slab.shape
    tr = _pick(r, (512, 256, 8))

    def body(s_ref, o_ref):
        acc = s_ref[0]
        for j in range(1, N_DEV):
            acc = acc + s_ref[j]
        o_ref[...] = acc

    return _pc(body, name=name, grid=(r // tr,), in_specs=[pl.BlockSpec((N_DEV, tr, w), lambda i: (0, i, 0))],
               out_specs=pl.BlockSpec((tr, w), lambda i: (i, 0)), out_shape=_sds((r, w), F32),
               compiler_params=_cp("parallel"))(slab)


def _mesh_pos():
    x, y, c = lax.axis_index("x"), lax.axis_index("y"), lax.axis_index("c")
    others = [(1 - x, y), (x, 1 - y), (1 - x, 1 - y)]
    return x, y, c, others


def _rcopy(src, dst, ssem, rsem, k, dev):
    return pltpu.make_async_remote_copy(src_ref=src, dst_ref=dst, send_sem=ssem.at[k], recv_sem=rsem.at[k],
                                        device_id=dev, device_id_type=MESH)


def _comm_call(body, name, n_in, out_shape, n_local, n_remote):
    scratch = [pltpu.SemaphoreType.DMA((max(n_local, 1),)), pltpu.SemaphoreType.DMA((n_remote,)),
               pltpu.SemaphoreType.DMA((n_remote,))]
    return _pc(body, name=name, in_specs=[ANY] * n_in, out_specs=[ANY] * len(out_shape), out_shape=out_shape,
               scratch_shapes=scratch)


GATHERED = BIG + SMALL_COL
HBM_SPEC = pl.BlockSpec(memory_space=pltpu.HBM)
SEM_SPEC = pl.BlockSpec(memory_space=pltpu.SEMAPHORE)
TOKEN_SHAPE = (8, 128)


def place_block(w, layer, pos_idx, *, by_rows, dtype, name):
    _, r, c = w.shape
    tr = _pick(r, (512, 704, 256))

    def body(pos_ref, w_ref, o_ref):
        if by_rows:
            o_ref[0] = w_ref[0].astype(dtype)
        else:
            o_ref[...] = w_ref[0].astype(dtype)

    if by_rows:
        out_spec, shape = pl.BlockSpec((1, tr, c), lambda i, pos_ref: (pos_ref[0], i, 0)), (N_CHIPS, r, c)
    else:
        out_spec, shape = pl.BlockSpec((tr, c), lambda i, pos_ref: (i, pos_ref[0])), (r, N_CHIPS * c)
    gs = pltpu.PrefetchScalarGridSpec(num_scalar_prefetch=1, grid=(r // tr,),
                                      in_specs=[pl.BlockSpec((1, tr, c), lambda i, pos_ref: (layer, i, 0))],
                                      out_specs=out_spec)
    return _pc(body, name=name, grid_spec=gs, out_shape=_sds(shape, dtype), compiler_params=_cp("parallel"))(pos_idx, w)


def _gather_windows(bufs):
    names, arrs = list(bufs), list(bufs.values())

    def dwin(refs, i, k, h):
        if names[i] in BIG_ROW:
            _, r, _ = arrs[i].shape
            return refs[i].at[k] if h is None else refs[i].at[k, pl.ds(h * (r // 2), r // 2), :]
        r, cs = arrs[i].shape[0], arrs[i].shape[1] // N_CHIPS
        cols = pl.ds(pl.multiple_of(k * cs, 128), cs)
        return refs[i].at[:, cols] if h is None else refs[i].at[pl.ds(h * (r // 2), r // 2), cols]

    def swin(refs, i, h):
        x, y, _, _ = _mesh_pos()
        return dwin(refs, i, 2 * x + y, h)

    return names, dwin, swin


def _gather_send(names, ins, outs, ssem, rsem, dwin, swin, stride):
    x, y, c, others = _mesh_pos()
    sends = []
    for i, n in enumerate(names):
        h = c if n in BIG else None
        for j, chip in enumerate(others):
            cp = _rcopy(swin(ins, i, h), swin(outs, i, h), ssem, rsem, stride * i + j, (*chip, c))
            cp.start()
            sends.append(cp)
    return sends


def _gather_pass_on(names, outs, ssem, rsem, dwin, stride, first_off, pass_off):
    x, y, c, others = _mesh_pos()
    sib = (x, y, 1 - c)
    sends = []
    for j, chip in enumerate(others):
        kk = 2 * chip[0] + chip[1]
        for i, n in enumerate(names):
            got = dwin(outs, i, kk, c if n in BIG else None)
            if first_off is not None:
                _rcopy(got, got, ssem, rsem, stride * i + first_off + j, (*chip, c)).wait_recv()
            if n in BIG:
                fwd = _rcopy(got, got, ssem, rsem, stride * i + pass_off + j, sib)
                fwd.start()
                sends.append(fwd)
    for j, chip in enumerate(others):
        kk = 2 * chip[0] + chip[1]
        for i, n in enumerate(names):
            if n in BIG:
                got = dwin(outs, i, kk, 1 - c)
                _rcopy(got, got, ssem, rsem, stride * i + pass_off + j, sib).wait_recv()
    return sends


def _as_weights(bufs):
    return {n: (o.reshape(o.shape[0] * o.shape[1], o.shape[2]) if n in BIG_ROW else o) for n, o in bufs.items()}


def _comm_in_place(body, name, bufs, n_sems):
    nn = len(bufs)
    arrs = list(bufs.values())
    scratch = [pltpu.SemaphoreType.DMA((n_sems,)), pltpu.SemaphoreType.DMA((n_sems,))]
    outs = _pc(body, name=name, in_specs=[ANY] * nn, out_specs=[ANY] * nn, out_shape=[_sds(a.shape, a.dtype) for a in arrs],
               scratch_shapes=scratch, input_output_aliases={i: i for i in range(nn)})(*arrs)
    return dict(zip(bufs, outs))


def allgather_weights(bufs, *, name):
    nn = len(bufs)
    names, dwin, swin = _gather_windows(bufs)

    def body(*refs):
        ins, outs = refs[:nn], refs[nn:2 * nn]
        ssem, rsem = refs[2 * nn:]
        sends = _gather_send(names, ins, outs, ssem, rsem, dwin, swin, 6)
        sends += _gather_pass_on(names, outs, ssem, rsem, dwin, 6, 0, 3)
        for cp in sends:
            cp.wait_send()

    return _comm_in_place(body, name, bufs, 6 * nn)


def _split_start(body, name, bufs, extra_in, n_sems):
    nn = len(bufs)
    arrs = [pltpu.with_memory_space_constraint(a, pltpu.HBM) for a in bufs.values()]
    out_shape = ([pltpu.SemaphoreType.DMA((n_sems,)), pltpu.SemaphoreType.DMA((n_sems,))]
                 + [pltpu.HBM(a.shape, a.dtype) for a in arrs] + [_sds(TOKEN_SHAPE, F32)])
    outs = _pc(body, name=name, in_specs=[HBM_SPEC] * nn + [ANY] * len(extra_in),
               out_specs=[SEM_SPEC, SEM_SPEC] + [HBM_SPEC] * nn + [pl.BlockSpec(memory_space=pltpu.VMEM)],
               out_shape=out_shape, input_output_aliases={i: 2 + i for i in range(nn)},
               compiler_params=pltpu.CompilerParams(has_side_effects=pltpu.SideEffectType.DATAFLOW_SIDE_EFFECTING),
               )(*arrs, *extra_in)
    return dict(ssem=outs[0], rsem=outs[1], bufs=dict(zip(bufs, outs[2:2 + nn])), token=outs[-1])


def _split_wait(body, name, handle, after):
    bufs = handle['bufs']
    nn = len(bufs)
    arrs = list(bufs.values())
    outs = _pc(body, name=name, in_specs=[HBM_SPEC] * nn + [SEM_SPEC, SEM_SPEC] + [ANY] * len(after),
               out_specs=[HBM_SPEC] * nn, out_shape=[pltpu.HBM(a.shape, a.dtype) for a in arrs],
               input_output_aliases={i: i for i in range(nn)},
               compiler_params=pltpu.CompilerParams(has_side_effects=pltpu.SideEffectType.DATAFLOW_SIDE_EFFECTING),
               )(*arrs, handle['ssem'], handle['rsem'], *after)
    return dict(zip(bufs, outs))


def gather_start(bufs, after, *, name):
    nn = len(bufs)
    names, dwin, swin = _gather_windows(bufs)

    def body(*refs):
        ins = refs[:nn]
        ssem, rsem = refs[nn + len(after)], refs[nn + len(after) + 1]
        _gather_send(names, ins, ins, ssem, rsem, dwin, swin, 3)
        refs[-1][...] = jnp.zeros(TOKEN_SHAPE, F32)

    return _split_start(body, name, bufs, after, 3 * nn)


def gather_wait(handle, after, *, name):
    nn = len(handle['bufs'])
    names, dwin, swin = _gather_windows(handle['bufs'])

    def body(*refs):
        ins = refs[:nn]
        ssem, rsem = refs[nn], refs[nn + 1]
        x, y, c, others = _mesh_pos()
        for i, n in enumerate(names):
            h = c if n in BIG else None
            for j, chip in enumerate(others):
                kk = 2 * chip[0] + chip[1]
                cp = _rcopy(swin(ins, i, h), dwin(ins, i, kk, h), ssem, rsem, 3 * i + j, (*chip, c))
                cp.wait_send()
                cp.wait_recv()

    return _split_wait(body, name, handle, after)


def gather_finish(bufs, *, name):
    nn = len(bufs)
    names, dwin, _ = _gather_windows(bufs)

    def body(*refs):
        outs = refs[nn:2 * nn]
        ssem, rsem = refs[2 * nn:]
        for cp in _gather_pass_on(names, outs, ssem, rsem, dwin, 3, None, 0):
            cp.wait_send()

    return _comm_in_place(body, name, bufs, 3 * nn)


def _grad_view(n, g):
    return g.reshape(N_CHIPS, g.shape[0] // N_CHIPS, g.shape[1]) if n in BIG_ROW else g.reshape((1,) + g.shape)


def exchange_halves(gv, *, name):
    nn = len(gv)
    arrs = list(gv.values())

    def body(*refs):
        ins, outs = refs[:nn], refs[nn:2 * nn]
        _, ssem, rsem = refs[2 * nn:]
        x, y, c, _ = _mesh_pos()
        cps = []
        for i in range(nn):
            rh = arrs[i].shape[1] // 2
            cp = _rcopy(ins[i].at[:, pl.ds((1 - c) * rh, rh), :], outs[i], ssem, rsem, i, (x, y, 1 - c))
            cp.start()
            cps.append(cp)
        for cp in cps:
            cp.wait()

    out_shape = [_sds((a.shape[0], a.shape[1] // 2, a.shape[2]), a.dtype) for a in arrs]
    return dict(zip(gv, _comm_call(body, name, nn, out_shape, 0, nn)(*arrs)))


def _shard_shape(n, p):
    _, rh, w = p.shape
    return (rh, w) if n in BIG_ROW else (rh, w // N_CHIPS)


def _scatter_copies(pv, ins, outs, ssem, rsem):
    x, y, c, others = _mesh_pos()
    cps = []
    for i, (n, p) in enumerate(pv.items()):
        _, ws = _shard_shape(n, p)
        for j, chip in enumerate(others):
            kk = 2 * chip[0] + chip[1]
            if n in BIG_ROW:
                src = ins[i].at[kk]
            else:
                src = ins[i].at[0, :, pl.ds(pl.multiple_of(kk * ws, 128), ws)]
            cps.append(_rcopy(src, outs[i].at[j], ssem, rsem, 3 * i + j, (*chip, c)))
    return cps


def _recv_shapes(pv):
    return [(N_CHIPS - 1,) + _shard_shape(n, p) for n, p in pv.items()]


def scatter_partials(pv, *, name):
    nn = len(pv)

    def body(*refs):
        ins, outs = refs[:nn], refs[nn:2 * nn]
        _, ssem, rsem = refs[2 * nn:]
        cps = _scatter_copies(pv, ins, outs, ssem, rsem)
        for cp in cps:
            cp.start()
        for cp in cps:
            cp.wait()

    out_shape = [_sds(s, p.dtype) for s, p in zip(_recv_shapes(pv), pv.values())]
    return pv, dict(zip(pv, _comm_call(body, name, nn, out_shape, 0, 3 * nn)(*pv.values())))


_RECV = "/recv"


def scatter_start(pv, *, name):
    nn = len(pv)

    def body(*refs):
        ins, lands = refs[:nn], refs[nn:2 * nn]
        ssem, rsem = refs[2 * nn], refs[2 * nn + 1]
        for cp in _scatter_copies(pv, ins, lands, ssem, rsem):
            cp.start()
        refs[-1][...] = jnp.zeros(TOKEN_SHAPE, F32)

    lands = {n + _RECV: lax.empty(s, p.dtype) for (n, p), s in zip(pv.items(), _recv_shapes(pv))}
    return _split_start(body, name, {**pv, **lands}, (), 3 * nn)


def scatter_wait(handle, after, *, name):
    nn = len(handle['bufs']) // 2
    pv = dict(list(handle['bufs'].items())[:nn])

    def body(*refs):
        ins, zones = refs[:nn], refs[nn:2 * nn]
        ssem, rsem = refs[2 * nn], refs[2 * nn + 1]
        for cp in _scatter_copies(pv, ins, zones, ssem, rsem):
            cp.wait_send()
            cp.wait_recv()

    outs = _split_wait(body, name, handle, after)
    return {n: outs[n] for n in pv}, {n: outs[n + _RECV] for n in pv}


def join_halves(rv, *, name):
    nn = len(rv)
    arrs = list(rv.values())

    def body(*refs):
        ins, outs = refs[:nn], refs[nn:2 * nn]
        ssem, rsem = refs[2 * nn:]
        x, y, c, _ = _mesh_pos()
        cps = []
        for i in range(nn):
            rh = arrs[i].shape[0] // 2
            rows = pl.ds(c * rh, rh)
            cp = _rcopy(ins[i].at[rows, :], outs[i].at[rows, :], ssem, rsem, i, (x, y, 1 - c))
            cp.start()
            cps.append(cp)
        for i, cp in enumerate(cps):
            cp.wait_send()
            rh = arrs[i].shape[0] // 2
            got = outs[i].at[pl.ds((1 - c) * rh, rh), :]
            _rcopy(got, got, ssem, rsem, i, (x, y, 1 - c)).wait_recv()

    return _comm_in_place(body, name, rv, nn)


def chip_partials(grads, names, c_idx):
    gv = {n: _grad_view(n, grads[n]) for n in names}
    recv = exchange_halves(gv, name="rs_exchange_halves")
    return {n: add_half(gv[n], recv[n], c_idx, name="rs_add_" + n) for n in names}


def reduce_shards(pv, got, pos_idx):
    rv = {n: sum_shard(pv[n], got[n], pos_idx, by_rows=n in BIG_ROW, name="rs_sum_" + n) for n in pv}
    return join_halves(rv, name="rs_join_halves")


def _slab_first(ref, ssem, rsem):
    x, y, c, others = _mesh_pos()
    mine = ref.at[4 * x + 2 * y + c]
    peers = [(x, y, 1 - c)] + [(*chip, c) for chip in others]
    out = []
    for k, p in enumerate(peers):
        got = ref.at[4 * p[0] + 2 * p[1] + p[2]]
        out.append((_rcopy(mine, mine, ssem, rsem, k, p), _rcopy(got, got, ssem, rsem, k, p)))
    return out


def slab_start(slab, *, name):
    def body(ref, ssem, rsem, thru, token):
        for cp, _ in _slab_first(ref, ssem, rsem):
            cp.start()
        token[...] = jnp.zeros(TOKEN_SHAPE, F32)

    return _split_start(body, name, {'slab': slab}, (), 4)


def slab_wait(handle, after, *, name):
    def body(ref, ssem, rsem, *rest):
        for sent, landed in _slab_first(ref, ssem, rsem):
            sent.wait_send()
            landed.wait_recv()

    return _split_wait(body, name, handle, after)['slab']


def slab_finish(slab, *, name):
    def body(in_ref, out_ref, ssem, rsem):
        x, y, c, others = _mesh_pos()
        sib = (x, y, 1 - c)
        sends = []
        for j, chip in enumerate(others):
            got = out_ref.at[4 * chip[0] + 2 * chip[1] + c]
            sends.append(_rcopy(got, got, ssem, rsem, j, sib))
            sends[-1].start()
        for j, chip in enumerate(others):
            got = out_ref.at[4 * chip[0] + 2 * chip[1] + 1 - c]
            _rcopy(got, got, ssem, rsem, j, sib).wait_recv()
        for cp in sends:
            cp.wait_send()

    return _comm_in_place(body, name, {'slab': slab}, 3)['slab']


def _pad128(n):
    return -(-n // 128) * 128


def _pack_small(grads, shapes):
    parts = []
    for g in grads:
        for n in SMALL:
            v = g[n].astype(F32).reshape(-1)
            parts.append(jnp.pad(v, (0, _pad128(v.shape[0]) - v.shape[0])))
    flat = jnp.concatenate(parts)
    rows = -(-flat.shape[0] // (128 * 512)) * 512
    return jnp.pad(flat, (0, rows * 128 - flat.shape[0])).reshape(rows, 128)


def _unpack_small(slab, shapes, depth):
    flat = slab.reshape(-1)
    out = {n: [] for n in SMALL}
    off = 0
    for _ in range(depth):
        for n in SMALL:
            size = math.prod(shapes[n])
            out[n].append(flat[off:off + size].reshape(shapes[n]))
            off += _pad128(size)
    return {n: jnp.stack(v) for n, v in out.items()}


def _adamw_nd(w, g, m, v, name):
    shp = w.shape
    two = lambda a: a.reshape(-1, shp[-1])
    return tuple(o.reshape(shp) for o in adamw(two(w), two(g), two(m), two(v), name=name))


def kernel(x, ln1_g, w_in, b_gate, sg_ln_g, sg_ln_b, sg_w, sg_b, w_a_out, cv_w, cv_b, cv_ln_g, cv_ln_b, w_b_out, q_norm_g, k_norm_g, w_c_out, w_out, ln2_g, w_up, ffn_conv_w, ffn_conv_b, w_down, loss_target, m_ln1_g, m_w_in, m_b_gate, m_sg_ln_g, m_sg_ln_b, m_sg_w, m_sg_b, m_w_a_out, m_cv_w, m_cv_b, m_cv_ln_g, m_cv_ln_b, m_w_b_out, m_q_norm_g, m_k_norm_g, m_w_c_out, m_w_out, m_ln2_g, m_w_up, m_ffn_conv_w, m_ffn_conv_b, m_w_down, v_ln1_g, v_w_in, v_b_gate, v_sg_ln_g, v_sg_ln_b, v_sg_w, v_sg_b, v_w_a_out, v_cv_w, v_cv_b, v_cv_ln_g, v_cv_ln_b, v_w_b_out, v_q_norm_g, v_k_norm_g, v_w_c_out, v_w_out, v_ln2_g, v_w_up, v_ffn_conv_w, v_ffn_conv_b, v_w_down):
    w = dict(ln1_g=ln1_g, w_in=w_in, b_gate=b_gate, sg_ln_g=sg_ln_g, sg_ln_b=sg_ln_b, sg_w=sg_w, sg_b=sg_b,
             w_a_out=w_a_out, cv_w=cv_w, cv_b=cv_b, cv_ln_g=cv_ln_g, cv_ln_b=cv_ln_b, w_b_out=w_b_out,
             q_norm_g=q_norm_g, k_norm_g=k_norm_g, w_c_out=w_c_out, w_out=w_out, ln2_g=ln2_g, w_up=w_up,
             ffn_conv_w=ffn_conv_w, ffn_conv_b=ffn_conv_b, w_down=w_down)
    m = dict(ln1_g=m_ln1_g, w_in=m_w_in, b_gate=m_b_gate, sg_ln_g=m_sg_ln_g, sg_ln_b=m_sg_ln_b, sg_w=m_sg_w,
             sg_b=m_sg_b, w_a_out=m_w_a_out, cv_w=m_cv_w, cv_b=m_cv_b, cv_ln_g=m_cv_ln_g, cv_ln_b=m_cv_ln_b,
             w_b_out=m_w_b_out, q_norm_g=m_q_norm_g, k_norm_g=m_k_norm_g, w_c_out=m_w_c_out, w_out=m_w_out,
             ln2_g=m_ln2_g, w_up=m_w_up, ffn_conv_w=m_ffn_conv_w, ffn_conv_b=m_ffn_conv_b, w_down=m_w_down)
    v = dict(ln1_g=v_ln1_g, w_in=v_w_in, b_gate=v_b_gate, sg_ln_g=v_sg_ln_g, sg_ln_b=v_sg_ln_b, sg_w=v_sg_w,
             sg_b=v_sg_b, w_a_out=v_w_a_out, cv_w=v_cv_w, cv_b=v_cv_b, cv_ln_g=v_cv_ln_g, cv_ln_b=v_cv_ln_b,
             w_b_out=v_w_b_out, q_norm_g=v_q_norm_g, k_norm_g=v_k_norm_g, w_c_out=v_w_c_out, w_out=v_w_out,
             ln2_g=v_ln2_g, w_up=v_w_up, ffn_conv_w=v_ffn_conv_w, ffn_conv_b=v_ffn_conv_b, w_down=v_w_down)
    depth = ln1_g.shape[0]
    cx, cy, cc = lax.axis_index("x"), lax.axis_index("y"), lax.axis_index("c")
    me = 2 * cx + cy
    pos_idx = jnp.stack([me, cc]).astype(jnp.int32)
    c_idx = jnp.reshape(cc, (1,)).astype(jnp.int32)

    padded = {n: jnp.pad(w[n], ((0, 0), (0, -w[n].shape[1] % 8), (0, 0))) for n in SMALL_COL}
    first, later = ['w_in'] + SMALL_COL, [n for n in BIG if n != 'w_in']

    def blocks(names, l):
        return {n: (place_block(w[n], l, pos_idx, by_rows=n in BIG_ROW, dtype=BF16, name="place_" + n) if n in BIG else
                    place_block(padded[n], l, pos_idx, by_rows=False, dtype=F32, name="place_" + n)) for n in names}

    full0 = allgather_weights(blocks(first, 0), name="allgather_weights")
    gathers = [gather_start(blocks(later, 0), [full0['w_in']], name="gather_start_0")]
    for l in range(1, depth):
        gathers.append(gather_start(blocks(GATHERED, l), [gathers[-1]['token']], name="gather_start_%d" % l))

    def arrived(l, after):
        bufs = gather_wait(gathers[l], after, name="gather_wait_%d" % l)
        return _as_weights(gather_finish(bufs, name="gather_finish_%d" % min(l, 1)))

    def get_layer(l, x_in):
        if l == 0:
            p, after, rest = _as_weights(full0), tuple(h['token'] for h in gathers), functools.partial(arrived, 0)
        else:
            p, after, rest = arrived(l, [x_in]), (), None
        for n in SMALL:
            p[n] = p[n][:w[n].shape[1]] if n in SMALL_COL else w[n][l]
        return p, after, rest

    grads, scatters = [None] * depth, []

    def on_grads(l, g):
        grads[l] = g
        if l == 0:
            scatters.append((0, scatter_partials(chip_partials(g, GRAD_GROUPS[-1], c_idx), name="rs_scatter_partials")))
            return ()
        scatters.append((l, scatter_start(chip_partials(g, BIG, c_idx), name="scatter_start_%d" % l)))
        return (scatters[-1][1]['token'],)

    def emit0(names, g):
        scatters.append((0, scatter_start(chip_partials(g, names, c_idx), name="scatter_start_0_" + names[0])))
        return (scatters[-1][1]['token'],)

    loss, dx = local_step(x[0], loss_target[0], depth, get_layer, on_grads, lambda l: emit0 if l == 0 else None)
    loss = lax.psum(loss, ("x", "y", "c"))
    full_shapes = {n: (w[n].shape[1], N_CHIPS * w[n].shape[2]) if n in SMALL_COL else w[n].shape[1:] for n in SMALL}
    mine = _pack_small(grads, full_shapes)
    slots = lax.dynamic_update_slice(lax.empty((N_DEV,) + mine.shape, F32), mine[None], (4 * cx + 2 * cy + cc, 0, 0))
    gathering = slab_start(slots, name="small_grads_start")
    big = [{} for _ in range(depth)]
    for i, (l, sc) in enumerate(scatters):
        pv, got = sc if isinstance(sc, tuple) else scatter_wait(sc, [dx, gathering['token']], name="scatter_wait_%d" % i)
        big[l].update(reduce_shards(pv, got, pos_idx))

    grad = {n: jnp.stack([b[n] for b in big]) for n in BIG}
    delta, new_m, new_v = {}, {}, {}
    for n in BIG:
        delta[n], new_m[n], new_v[n] = _adamw_nd(w[n], grad[n], m[n], v[n], "adamw_" + n)
    slots = slab_finish(slab_wait(gathering, [delta[n] for n in BIG], name="small_grads_wait"), name="small_grads_finish")
    small = _unpack_small(sum_slots(slots, name="sum_small_grads"), full_shapes, depth)
    for n in SMALL:
        if n in SMALL_COL:
            cs = w[n].shape[-1]
            grad[n] = lax.dynamic_slice_in_dim(small[n], me * cs, cs, axis=small[n].ndim - 1)
        else:
            grad[n] = small[n]
        delta[n], new_m[n], new_v[n] = _adamw_nd(w[n], grad[n], m[n], v[n], "adamw_" + n)
    return (loss, dx[None], *[grad[n] for n in WEIGHTS], *[delta[n] for n in WEIGHTS],
            *[new_m[n] for n in WEIGHTS], *[new_v[n] for n in WEIGHTS])
```

```python
import functools
import math

import jax
import jax.numpy as jnp
from jax import lax
from jax.experimental import pallas as pl
from jax.experimental.pallas import tpu as pltpu

F32 = jnp.float32
BF16 = jnp.bfloat16
MESH = pl.DeviceIdType.MESH
ANY = pl.BlockSpec(memory_space=pl.ANY)

EPS = 1e-6
D_MODEL = 1024
DEPTH = 4
SG_WIDTH = 512
CHUNK = 128
CV_WIDTH = 512
CV_KERNEL = 31
SB_WIDTH = 512
SB_HEAD_DIM = 64
Q_BLOCK = 128
D_FF = 2816
FFN_KERNEL = 3
COL_B = 1024
COL_C = 2048
COL_G = 3584
IN_COLS = 6656
N_CHIPS = 4
N_DEV = 8
CV_HALO = 32
FFN_HALO = 16

ADAM_LR = 0.001
ADAM_B1 = 0.9
ADAM_B2 = 0.999
ADAM_EPS = 1e-08
ADAM_WD = 0.01
ADAM_STEP = 10

VMEM_LIMIT_BYTES = 56 * 1024 * 1024

NT_DIMS = (((1,), (1,)), ((), ()))
TN_DIMS = (((0,), (0,)), ((), ()))

WEIGHTS = ['ln1_g', 'w_in', 'b_gate', 'sg_ln_g', 'sg_ln_b', 'sg_w', 'sg_b', 'w_a_out', 'cv_w', 'cv_b',
           'cv_ln_g', 'cv_ln_b', 'w_b_out', 'q_norm_g', 'k_norm_g', 'w_c_out', 'w_out', 'ln2_g', 'w_up',
           'ffn_conv_w', 'ffn_conv_b', 'w_down']
BIG_COL = ['w_in', 'w_a_out', 'w_b_out', 'w_c_out', 'w_up']
BIG_ROW = ['w_out', 'w_down']
BIG = BIG_COL + BIG_ROW
SMALL_COL = ['b_gate', 'cv_w', 'ffn_conv_w']
SMALL = [n for n in WEIGHTS if n not in BIG]


def _pc(body, **kw):
    return pl.pallas_call(body, **kw)


def _cp(*sem):
    return pltpu.CompilerParams(dimension_semantics=sem, vmem_limit_bytes=VMEM_LIMIT_BYTES)


def _sds(shape, dtype):
    return jax.ShapeDtypeStruct(shape, dtype)


_GELU_C = math.sqrt(2.0 / math.pi)
_GELU_A = 0.044715


def _sigmoid(x):
    return jax.nn.sigmoid(x)


def _gelu(x):
    return 0.5 * x * (1.0 + jnp.tanh(_GELU_C * (x + _GELU_A * x * x * x)))


def _gelu_grad(x):
    t = jnp.tanh(_GELU_C * (x + _GELU_A * x * x * x))
    return 0.5 * (1.0 + t) + 0.5 * x * (1.0 - t * t) * _GELU_C * (1.0 + 3.0 * _GELU_A * x * x)


def _silu(x):
    return x * _sigmoid(x)


def _silu_grad(x):
    s = _sigmoid(x)
    return s * (1.0 + x * (1.0 - s))


def _ln_stats(x):
    mu = jnp.mean(x, axis=-1, keepdims=True)
    xc = x - mu
    r = lax.rsqrt(jnp.mean(xc * xc, axis=-1, keepdims=True) + EPS)
    return xc * r, r


def _ln_bwd(dy, xhat, r, g):
    dxh = dy * g
    return r * (dxh - jnp.mean(dxh, axis=-1, keepdims=True) - xhat * jnp.mean(dxh * xhat, axis=-1, keepdims=True))


def _split_dot(x, m):
    hi = x.astype(BF16)
    lo = (x - hi.astype(F32)).astype(BF16)
    return jnp.dot(hi, m, preferred_element_type=F32) + jnp.dot(lo, m, preferred_element_type=F32)


def _block_sums(x, m):
    return jnp.dot(x.astype(BF16), m, preferred_element_type=F32)


def _rowsum0(x):
    return jnp.sum(x, axis=0, keepdims=True)


def _pick(n, prefs):
    for p in prefs:
        if n % p == 0:
            return p
    return n


def mm_nn(a, w, *, name, res=None, out_dtype=BF16):
    t, k = a.shape
    n = w.shape[1]
    tm = _pick(t, (512, 256))
    tn = _pick(n, (1024, 512, 256))

    def body(*refs):
        if res is None:
            a_ref, w_ref, o_ref = refs
        else:
            a_ref, w_ref, r_ref, o_ref = refs
        acc = jnp.dot(a_ref[...], w_ref[...], preferred_element_type=F32)
        if res is not None:
            acc = acc + r_ref[...]
        o_ref[...] = acc.astype(o_ref.dtype)

    in_specs = [pl.BlockSpec((tm, k), lambda i, j: (i, 0)), pl.BlockSpec((k, tn), lambda i, j: (0, j))]
    args = [a, w]
    if res is not None:
        in_specs.append(pl.BlockSpec((tm, tn), lambda i, j: (i, j)))
        args.append(res)
    return _pc(body, name=name, grid=(t // tm, n // tn), in_specs=in_specs,
               out_specs=pl.BlockSpec((tm, tn), lambda i, j: (i, j)),
               out_shape=_sds((t, n), out_dtype), compiler_params=_cp("parallel", "parallel"))(*args)


def mm_norm_nn(x, g, w, *, name, after=()):
    t, k = x.shape
    n = w.shape[1]
    tm = _pick(t, (512, 256))
    tn = _pick(n, (1664, 1408, 512))

    def body(x_ref, g_ref, w_ref, *rest):
        z_ref, h_ref = rest[len(after):]
        xv = x_ref[...]
        r = lax.rsqrt(jnp.mean(xv * xv, axis=-1, keepdims=True) + EPS)
        h = (xv * r * g_ref[...]).astype(BF16)
        h_ref[...] = h
        for c in range(n // tn):
            cols = slice(c * tn, (c + 1) * tn)
            z_ref[:, cols] = jnp.dot(h, w_ref[:, cols], preferred_element_type=F32).astype(z_ref.dtype)

    return _pc(body, name=name, grid=(t // tm,),
               in_specs=[pl.BlockSpec((tm, k), lambda i: (i, 0)), pl.BlockSpec((1, k), lambda i: (0, 0)),
                         pl.BlockSpec((k, n), lambda i: (0, 0), pipeline_mode=pl.Buffered(1))]
               + [pl.BlockSpec(a.shape, lambda i: (0, 0)) for a in after],
               out_specs=[pl.BlockSpec((tm, n), lambda i: (i, 0)), pl.BlockSpec((tm, k), lambda i: (i, 0))],
               out_shape=[_sds((t, n), BF16), _sds((t, k), BF16)],
               compiler_params=_cp("parallel"))(x, g, w, *after)


def mm_nt(dy, w, *, name, out_dtype, after=()):
    t, n = dy.shape
    k = w.shape[0]
    tm = _pick(t, (512, 256))

    def body(dy_ref, w_ref, *rest):
        o_ref = rest[len(after)]
        o_ref[...] = lax.dot_general(dy_ref[...].astype(BF16), w_ref[...], NT_DIMS,
                                     preferred_element_type=F32).astype(o_ref.dtype)

    return _pc(body, name=name, grid=(t // tm,),
               in_specs=[pl.BlockSpec((tm, n), lambda i: (i, 0)),
                         pl.BlockSpec((k, n), lambda i: (0, 0), pipeline_mode=pl.Buffered(1))]
               + [pl.BlockSpec(tok.shape, lambda i: (0, 0)) for tok in after],
               out_specs=pl.BlockSpec((tm, k), lambda i: (i, 0)),
               out_shape=_sds((t, k), out_dtype), compiler_params=_cp("parallel"))(dy, w, *after)


def mm_tn(a, dy, *, name, out_dtype=BF16):
    t, k = a.shape
    n = dy.shape[1]
    tk = _pick(k, (1024, 1408, 512))
    tn = _pick(n, (512,) if dy.dtype == F32 else (1664, 1408, 1024, 512))

    def body(a_ref, dy_ref, o_ref):
        o_ref[...] = lax.dot_general(a_ref[...], dy_ref[...].astype(BF16), TN_DIMS,
                                     preferred_element_type=F32).astype(o_ref.dtype)

    return _pc(body, name=name, grid=(k // tk, n // tn),
               in_specs=[pl.BlockSpec((t, tk), lambda i, j: (0, i)), pl.BlockSpec((t, tn), lambda i, j: (0, j))],
               out_specs=pl.BlockSpec((tk, tn), lambda i, j: (i, j)),
               out_shape=_sds((k, n), out_dtype), compiler_params=_cp("parallel", "parallel"))(a, dy)


def rms_bwd(dh, x, g, dres, *, name):
    t, d = x.shape
    tm = _pick(t, (256,))

    def body(dh_ref, x_ref, g_ref, dres_ref, dx_ref, dg_ref):
        xv = x_ref[...]
        r = lax.rsqrt(jnp.mean(xv * xv, axis=-1, keepdims=True) + EPS)
        xh = xv * r
        dy = dh_ref[...].astype(F32)
        dxh = dy * g_ref[...]
        dx_ref[...] = dres_ref[...] + r * (dxh - xh * jnp.mean(dxh * xh, axis=-1, keepdims=True))

        @pl.when(pl.program_id(0) == 0)
        def _():
            dg_ref[...] = jnp.zeros_like(dg_ref)

        dg_ref[...] += _rowsum0(dy * xh)

    row = pl.BlockSpec((tm, d), lambda i: (i, 0))
    vec = pl.BlockSpec((1, d), lambda i: (0, 0))
    return _pc(body, name=name, grid=(t // tm,), in_specs=[row, row, vec, row], out_specs=[row, vec],
               out_shape=[_sds((t, d), F32), _sds((1, d), F32)], compiler_params=_cp("arbitrary"))(dh, x, g, dres)


def loss_head(y, target, *, name):
    t, d = y.shape
    tm = _pick(t, (256,))

    def body(y_ref, t_ref, loss_ref, dy_ref):
        e = y_ref[...] - t_ref[...]
        dy_ref[...] = e * (1.0 / d)

        @pl.when(pl.program_id(0) == 0)
        def _():
            loss_ref[...] = jnp.zeros_like(loss_ref)

        loss_ref[...] += _rowsum0(jnp.sum(e * e, axis=1, keepdims=True)) * (0.5 / d)

    row = pl.BlockSpec((tm, d), lambda i: (i, 0))
    return _pc(body, name=name, grid=(t // tm,), in_specs=[row, row],
               out_specs=[pl.BlockSpec((1, 1), lambda i: (0, 0)), row],
               out_shape=[_sds((1, 1), F32), _sds((t, d), F32)], compiler_params=_cp("arbitrary"))(y, target)


def _sg_masks():
    lane = lax.broadcasted_iota(jnp.int32, (CHUNK, CHUNK), 1)
    row = lax.broadcasted_iota(jnp.int32, (CHUNK, CHUNK), 0)
    return lane < 64, lane <= row, row <= lane


def _sg_gate(vn_chunk, w_ref, bias_ref, p, first_group, tril):
    wa = jnp.where(tril, w_ref[2 * p], 0.0).astype(BF16)
    wb = jnp.where(tril, w_ref[2 * p + 1], 0.0).astype(BF16)
    oa = jnp.dot(wa, vn_chunk, preferred_element_type=F32)
    ob = jnp.dot(wb, vn_chunk, preferred_element_type=F32)
    return jnp.where(first_group, oa, ob) + bias_ref[:, p * 128:(p + 1) * 128]


def mixa_fwd(z, ln_g, ln_b, sg_w, sg_bias, *, name):
    t = z.shape[0]
    tm = _pick(t, (256,))

    def body(z_ref, g_ref, b_ref, w_ref, bias_ref, o_ref):
        first_group, tril, _ = _sg_masks()
        zv = z_ref[...].astype(F32)
        u = _gelu(zv[:, :SG_WIDTH])
        v = _gelu(zv[:, SG_WIDTH:])
        vh, _ = _ln_stats(v)
        vn = (vh * g_ref[...] + b_ref[...]).astype(BF16)
        for c in range(tm // CHUNK):
            rows = slice(c * CHUNK, (c + 1) * CHUNK)
            for p in range(4):
                cols = slice(p * 128, (p + 1) * 128)
                o = _sg_gate(vn[rows, cols], w_ref, bias_ref, p, first_group, tril)
                o_ref[rows, cols] = (u[rows, cols] * o).astype(o_ref.dtype)

    vec = pl.BlockSpec((1, SG_WIDTH), lambda i: (0, 0))
    return _pc(body, name=name, grid=(t // tm,),
               in_specs=[pl.BlockSpec((tm, 2 * SG_WIDTH), lambda i: (i, 0)), vec, vec,
                         pl.BlockSpec((8, CHUNK, CHUNK), lambda i: (0, 0, 0)),
                         pl.BlockSpec((CHUNK, SG_WIDTH), lambda i: (0, 0))],
               out_specs=pl.BlockSpec((tm, SG_WIDTH), lambda i: (i, 0)),
               out_shape=_sds((t, SG_WIDTH), BF16), compiler_params=_cp("parallel"))(z, ln_g, ln_b, sg_w, sg_bias)


def mixa_bwd(z, dga, ln_g, ln_b, sg_w, sg_wt, sg_bias, *, name):
    t = z.shape[0]
    tm = _pick(t, (256,))
    nsteps = t // tm

    def body(z_ref, dga_ref, g_ref, b_ref, w_ref, wt_ref, bias_ref, dz_ref, dw_ref, dsgb_ref, dg_ref, db_ref, dvn_s,
             dbias_ref):
        i = pl.program_id(0)
        first_group, tril, triu = _sg_masks()

        @pl.when(i == 0)
        def _():
            dw_ref[...] = jnp.zeros_like(dw_ref)
            dbias_ref[...] = jnp.zeros_like(dbias_ref)
            dg_ref[...] = jnp.zeros_like(dg_ref)
            db_ref[...] = jnp.zeros_like(db_ref)

        zv = z_ref[...].astype(F32)
        zu = zv[:, :SG_WIDTH]
        zg = zv[:, SG_WIDTH:]
        u = _gelu(zu)
        v = _gelu(zg)
        vh, r = _ln_stats(v)
        vn = (vh * g_ref[...] + b_ref[...]).astype(BF16)
        dga_v = dga_ref[...].astype(F32)
        d_o = dga_v * u
        for c in range(tm // CHUNK):
            rows = slice(c * CHUNK, (c + 1) * CHUNK)
            dbias_ref[...] += d_o[rows, :]
            for p in range(4):
                cols = slice(p * 128, (p + 1) * 128)
                vp = vn[rows, cols]
                o = _sg_gate(vp, w_ref, bias_ref, p, first_group, tril)
                dz_ref[rows, cols] = (dga_v[rows, cols] * o * _gelu_grad(zu[rows, cols])).astype(dz_ref.dtype)
                dop = d_o[rows, cols]
                dop_a = jnp.where(first_group, dop, 0.0).astype(BF16)
                dop_b = jnp.where(first_group, 0.0, dop).astype(BF16)
                dw_ref[2 * p] += lax.dot_general(dop_a, vp, NT_DIMS, preferred_element_type=F32)
                dw_ref[2 * p + 1] += lax.dot_general(dop_b, vp, NT_DIMS, preferred_element_type=F32)
                wta = jnp.where(triu, wt_ref[2 * p], 0.0).astype(BF16)
                wtb = jnp.where(triu, wt_ref[2 * p + 1], 0.0).astype(BF16)
                dop16 = dop.astype(BF16)
                dvn_s[rows, cols] = jnp.where(first_group, jnp.dot(wta, dop16, preferred_element_type=F32),
                                              jnp.dot(wtb, dop16, preferred_element_type=F32))
        dvn = dvn_s[...]
        dg_ref[...] += _rowsum0(dvn * vh)
        db_ref[...] += _rowsum0(dvn)
        dv = _ln_bwd(dvn, vh, r, g_ref[...])
        dz_ref[:, SG_WIDTH:] = (dv * _gelu_grad(zg)).astype(dz_ref.dtype)

        @pl.when(i == nsteps - 1)
        def _():
            for gi in range(8):
                dw_ref[gi] = jnp.where(tril, dw_ref[gi], 0.0)
            r_id = lax.broadcasted_iota(jnp.int32, (SG_WIDTH, 128), 0) // SB_HEAD_DIM
            c_id = lax.broadcasted_iota(jnp.int32, (SG_WIDTH, 128), 1)
            dsgb_ref[...] = _split_dot(dbias_ref[...], (r_id == c_id).astype(BF16))

    vec = pl.BlockSpec((1, SG_WIDTH), lambda i: (0, 0))
    wspec = pl.BlockSpec((8, CHUNK, CHUNK), lambda i: (0, 0, 0))
    bspec = pl.BlockSpec((CHUNK, SG_WIDTH), lambda i: (0, 0))
    sgb = pl.BlockSpec((CHUNK, 128), lambda i: (0, 0))
    return _pc(body, name=name, grid=(nsteps,),
               in_specs=[pl.BlockSpec((tm, 2 * SG_WIDTH), lambda i: (i, 0)), pl.BlockSpec((tm, SG_WIDTH), lambda i: (i, 0)),
                         vec, vec, wspec, wspec, bspec],
               out_specs=[pl.BlockSpec((tm, 2 * SG_WIDTH), lambda i: (i, 0)), wspec, sgb, vec, vec],
               out_shape=[_sds((t, 2 * SG_WIDTH), BF16), _sds((8, CHUNK, CHUNK), F32), _sds((CHUNK, 128), F32),
                          _sds((1, SG_WIDTH), F32), _sds((1, SG_WIDTH), F32)],
               scratch_shapes=[pltpu.VMEM((tm, SG_WIDTH), F32), pltpu.VMEM((CHUNK, SG_WIDTH), F32)],
               compiler_params=_cp("arbitrary"))(z, dga, ln_g, ln_b, sg_w, sg_wt, sg_bias)


def _glu(zv):
    return zv[:, :CV_WIDTH] * _sigmoid(zv[:, CV_WIDTH:])


_SUB = 8


def _cv_phases(x_s, tm):
    rows = CV_HALO + tm - _SUB
    for r in range(1, _SUB):
        x_s[r, 0:rows, :] = x_s[0, pl.ds(r, rows), :]


def _cv_tap(x_s, o, tm):
    return x_s[o % _SUB, pl.ds(o - o % _SUB, tm), :]


def _cv_fill(zm_ref, zh_ref, x_s, i, tm):
    x_s[0, 0:CV_HALO, :] = jnp.where(i > 0, _glu(zh_ref[...].astype(F32)), 0.0)
    x_s[0, CV_HALO:CV_HALO + tm, :] = _glu(zm_ref[...].astype(F32))
    _cv_phases(x_s, tm)


def _cv_conv(x_s, w_ref, cb_ref, tm):
    acc = jnp.zeros((tm, CV_WIDTH), F32) + cb_ref[...]
    for k in range(CV_KERNEL):
        acc = acc + w_ref[k:k + 1, :] * _cv_tap(x_s, CV_HALO - (CV_KERNEL - 1) + k, tm)
    return acc


def _cv_specs(tm):
    zm = pl.BlockSpec((tm, 2 * CV_WIDTH), lambda i: (i, 1))
    zh = pl.BlockSpec((CV_HALO, 2 * CV_WIDTH), lambda i: (jnp.maximum(i * (tm // CV_HALO) - 1, 0), 1))
    w = pl.BlockSpec((CV_KERNEL, CV_WIDTH), lambda i: (0, 0))
    vec = pl.BlockSpec((1, CV_WIDTH), lambda i: (0, 0))
    return zm, zh, w, vec


def mixb_fwd(z, cv_w, cv_b, ln_g, ln_b, *, name):
    t = z.shape[0]
    tm = _pick(t, (256,))

    def body(zm_ref, zh_ref, w_ref, cb_ref, g_ref, b_ref, o_ref, x_s):
        _cv_fill(zm_ref, zh_ref, x_s, pl.program_id(0), tm)
        c1 = _cv_conv(x_s, w_ref, cb_ref, tm)
        ch, _ = _ln_stats(c1)
        o_ref[...] = _silu(ch * g_ref[...] + b_ref[...]).astype(o_ref.dtype)

    zm, zh, w, vec = _cv_specs(tm)
    return _pc(body, name=name, grid=(t // tm,), in_specs=[zm, zh, w, vec, vec, vec],
               out_specs=pl.BlockSpec((tm, CV_WIDTH), lambda i: (i, 0)), out_shape=_sds((t, CV_WIDTH), BF16),
               scratch_shapes=[pltpu.VMEM((_SUB, CV_HALO + tm, CV_WIDTH), F32)],
               compiler_params=_cp("parallel"))(z, z, cv_w, cv_b, ln_g, ln_b)


def mixb_bwd1(z, dc3, cv_w, cv_b, ln_g, ln_b, *, name):
    t = z.shape[0]
    tm = _pick(t, (256,))

    def body(zm_ref, zh_ref, dc3_ref, w_ref, cb_ref, g_ref, b_ref, dc1_ref, dw_ref, dcb_ref, dg_ref, db_ref, x_s):
        i = pl.program_id(0)

        @pl.when(i == 0)
        def _():
            dw_ref[...] = jnp.zeros_like(dw_ref)
            dcb_ref[...] = jnp.zeros_like(dcb_ref)
            dg_ref[...] = jnp.zeros_like(dg_ref)
            db_ref[...] = jnp.zeros_like(db_ref)

        _cv_fill(zm_ref, zh_ref, x_s, i, tm)
        c1 = _cv_conv(x_s, w_ref, cb_ref, tm)
        ch, r = _ln_stats(c1)
        c2 = ch * g_ref[...] + b_ref[...]
        dc2 = dc3_ref[...].astype(F32) * _silu_grad(c2)
        dg_ref[...] += _rowsum0(dc2 * ch)
        db_ref[...] += _rowsum0(dc2)
        dc1 = _ln_bwd(dc2, ch, r, g_ref[...])
        dc1_ref[...] = dc1
        dcb_ref[...] += _rowsum0(dc1)
        for k in range(CV_KERNEL):
            dw_ref[k:k + 1, :] += _rowsum0(dc1 * _cv_tap(x_s, CV_HALO - (CV_KERNEL - 1) + k, tm))

    zm, zh, w, vec = _cv_specs(tm)
    row = pl.BlockSpec((tm, CV_WIDTH), lambda i: (i, 0))
    return _pc(body, name=name, grid=(t // tm,), in_specs=[zm, zh, row, w, vec, vec, vec],
               out_specs=[row, w, vec, vec, vec],
               out_shape=[_sds((t, CV_WIDTH), F32), _sds((CV_KERNEL, CV_WIDTH), F32), _sds((1, CV_WIDTH), F32),
                          _sds((1, CV_WIDTH), F32), _sds((1, CV_WIDTH), F32)],
               scratch_shapes=[pltpu.VMEM((_SUB, CV_HALO + tm, CV_WIDTH), F32)],
               compiler_params=_cp("arbitrary"))(z, z, dc3, cv_w, cv_b, ln_g, ln_b)


def mixb_bwd2(z, dc1, cv_w, *, name):
    t = z.shape[0]
    tm = _pick(t, (256,))
    nsteps = t // tm

    def body(zm_ref, dm_ref, dh_ref, w_ref, dz_ref, y_s):
        i = pl.program_id(0)
        y_s[0, 0:tm, :] = dm_ref[...]
        y_s[0, tm:tm + CV_HALO, :] = jnp.where(i < nsteps - 1, dh_ref[...], 0.0)
        _cv_phases(y_s, tm)
        dc0 = jnp.zeros((tm, CV_WIDTH), F32)
        for k in range(CV_KERNEL):
            dc0 = dc0 + w_ref[k:k + 1, :] * _cv_tap(y_s, CV_KERNEL - 1 - k, tm)
        zv = zm_ref[...].astype(F32)
        p = zv[:, :CV_WIDTH]
        s = _sigmoid(zv[:, CV_WIDTH:])
        dz_ref[:, :CV_WIDTH] = (dc0 * s).astype(dz_ref.dtype)
        dz_ref[:, CV_WIDTH:] = (dc0 * p * s * (1.0 - s)).astype(dz_ref.dtype)

    last = t // CV_HALO - 1
    return _pc(body, name=name, grid=(nsteps,),
               in_specs=[pl.BlockSpec((tm, 2 * CV_WIDTH), lambda i: (i, 1)),
                         pl.BlockSpec((tm, CV_WIDTH), lambda i: (i, 0)),
                         pl.BlockSpec((CV_HALO, CV_WIDTH), lambda i: (jnp.minimum((i + 1) * (tm // CV_HALO), last), 0)),
                         pl.BlockSpec((CV_KERNEL, CV_WIDTH), lambda i: (0, 0))],
               out_specs=pl.BlockSpec((tm, 2 * CV_WIDTH), lambda i: (i, 0)),
               out_shape=_sds((t, 2 * CV_WIDTH), BF16),
               scratch_shapes=[pltpu.VMEM((_SUB, tm + CV_HALO, CV_WIDTH), F32)],
               compiler_params=_cp("parallel"))(z, dc1, dc1, cv_w)


def _group_ones():
    r = lax.broadcasted_iota(jnp.int32, (SB_WIDTH, SB_WIDTH), 0) // SB_HEAD_DIM
    c = lax.broadcasted_iota(jnp.int32, (SB_WIDTH, SB_WIDTH), 1) // SB_HEAD_DIM
    return (r == c).astype(BF16)


def attn_prep(z, gq, gk, *, name):
    t = z.shape[0]
    tm = _pick(t, (256,))
    scale = 1.0 / math.sqrt(SB_HEAD_DIM)

    def body(q_ref, k_ref, gq_ref, gk_ref, qo_ref, ko_ref):
        ones = _group_ones()
        for src, g_ref, dst, mul in ((q_ref, gq_ref, qo_ref, scale), (k_ref, gk_ref, ko_ref, 1.0)):
            v = src[...].astype(F32)
            r = lax.rsqrt(_split_dot(v * v, ones) * (1.0 / SB_HEAD_DIM) + EPS)
            dst[...] = ((v * r * g_ref[...]).astype(BF16).astype(F32) * mul).astype(dst.dtype)

    vec = pl.BlockSpec((1, SB_WIDTH), lambda i: (0, 0))
    row = pl.BlockSpec((tm, SB_WIDTH), lambda i: (i, 0))
    return _pc(body, name=name, grid=(t // tm,),
               in_specs=[pl.BlockSpec((tm, SB_WIDTH), lambda i: (i, COL_C // SB_WIDTH)),
                         pl.BlockSpec((tm, SB_WIDTH), lambda i: (i, COL_C // SB_WIDTH + 1)), vec, vec],
               out_specs=[row, row], out_shape=[_sds((t, SB_WIDTH), BF16), _sds((t, SB_WIDTH), BF16)],
               compiler_params=_cp("parallel"))(z, z, gq, gk)


_KB = Q_BLOCK
_PAIR = 2 * _KB


def _attn_tq(t):
    return _pick(t, (512, 256, 128))


def _attn_consts(tq):
    first_head = lax.broadcasted_iota(jnp.int32, (_KB, 128), 1) < SB_HEAD_DIM
    r2 = lax.broadcasted_iota(jnp.int32, (_PAIR, _PAIR), 0)
    c2 = lax.broadcasted_iota(jnp.int32, (_PAIR, _PAIR), 1)
    same = (r2 // _KB) == (c2 // _KB)
    m_suffix = (same & (r2 > c2)).astype(BF16)
    m_prefix = (same & (r2 < c2)).astype(BF16)
    row = lax.broadcasted_iota(jnp.int32, (tq, _PAIR), 0)
    col = lax.broadcasted_iota(jnp.int32, (tq, _PAIR), 1)
    return first_head, m_suffix, m_prefix, row, col & (_KB - 1), col < _KB


def _sb_logits(z, causal):
    sp = jnp.log(1.0 + jnp.exp(-jnp.abs(z)))
    g = jnp.minimum(z, 0.0) - sp
    l1m = g - z
    if causal is not None:
        l1m = jnp.where(causal, l1m, 0.0)
    return g, l1m


def _stack_heads(first_head, v):
    zero = jnp.zeros_like(v)
    return jnp.concatenate([jnp.where(first_head, v, zero), jnp.where(first_head, zero, v)], axis=0)


def _add_rows(x, upd, r0):
    return x + upd if r0 == 0 else jnp.concatenate([x[:r0], x[r0:] + upd], axis=0)


def _pair_sums(x):
    return jnp.sum(x[:, :_KB], axis=1, keepdims=True), jnp.sum(x[:, _KB:], axis=1, keepdims=True)


def _attn_specs(t, tq):
    qspec = pl.BlockSpec((tq, 128), lambda h, i: (i, h))
    kspec = pl.BlockSpec((t, 128), lambda h, i: (0, h))
    vspec = pl.BlockSpec((t, 128), lambda h, i: (0, (COL_C + 2 * SB_WIDTH) // 128 + h))
    return qspec, kspec, vspec


def attn_fwd(q, k, z, *, name):
    t = q.shape[0]
    tq = _attn_tq(t)
    nd = tq // _KB
    assert nd % 2 == 0, "the key-block loop takes two blocks per pass"

    def body(q_ref, k_ref, v_ref, o_ref):
        qt = pl.program_id(1)
        first_head, m_suffix, _, row, key, is_first = _attn_consts(tq)
        qv = q_ref[...]

        def step(kb, state, causal, r0=0):
            acc, ca, cb = state
            off = pl.multiple_of(kb * _KB, _KB)
            kcat = _stack_heads(first_head, k_ref[pl.ds(off, _KB), :])
            vcat = _stack_heads(first_head, v_ref[pl.ds(off, _KB), :])
            zz = lax.dot_general(qv[r0:], kcat, NT_DIMS, preferred_element_type=F32)
            g, l1m = _sb_logits(zz, None if causal is None else causal[r0:])
            a = jnp.exp(g + _block_sums(l1m, m_suffix) + jnp.where(is_first[r0:], ca[r0:], cb[r0:]))
            if causal is not None:
                a = jnp.where(causal[r0:], a, 0.0)
            sa, sb = _pair_sums(l1m)
            pv = jnp.dot(a.astype(BF16), vcat, preferred_element_type=F32)
            return _add_rows(acc, pv, r0), _add_rows(ca, sa, r0), _add_rows(cb, sb, r0)

        c0 = jnp.zeros((tq, 1), F32)
        state = (jnp.zeros((tq, 128), F32), c0, c0)
        for d in reversed(range(nd)):
            state = step(qt * nd + d, state, key + d * _KB < row, d * _KB)
        def one_pass(s, st):
            for u in range(nd):
                st = step((qt - s) * nd - 1 - u, st, None)
            return st

        state = lax.fori_loop(0, qt, one_pass, state)
        o_ref[...] = state[0].astype(o_ref.dtype)

    qspec, kspec, vspec = _attn_specs(t, tq)
    return _pc(body, name=name, grid=(SB_WIDTH // 128, t // tq), in_specs=[qspec, kspec, vspec], out_specs=qspec,
               out_shape=_sds((t, SB_WIDTH), BF16), compiler_params=_cp("parallel", "arbitrary"))(q, k, z)


def attn_bwd(q, k, z, do, *, name):
    t = q.shape[0]
    tq = _attn_tq(t)
    nd = tq // _KB
    assert nd % 2 == 0, "the key-block loop takes two blocks per pass"
    nk = t // _KB

    def body(q_ref, k_ref, v_ref, do_ref, dq_ref, dk_ref, dv_ref, e_s, sg_s):
        qt = pl.program_id(1)
        first_head, m_suffix, m_prefix, row, key, is_first = _attn_consts(tq)

        @pl.when(qt == 0)
        def _():
            dk_ref[...] = jnp.zeros_like(dk_ref)
            dv_ref[...] = jnp.zeros_like(dv_ref)

        qv = q_ref[...]
        dov = do_ref[...]

        def halves(x):
            return jnp.where(first_head, x[:_KB], x[_KB:])

        def sweep1(kb, state, causal, r0=0):
            ca, cb = state
            off = pl.multiple_of(kb * _KB, _KB)
            kcat = _stack_heads(first_head, k_ref[pl.ds(off, _KB), :])
            vcat = _stack_heads(first_head, v_ref[pl.ds(off, _KB), :])
            zz = lax.dot_general(qv[r0:], kcat, NT_DIMS, preferred_element_type=F32)
            g, l1m = _sb_logits(zz, None if causal is None else causal[r0:])
            a = jnp.exp(g + _block_sums(l1m, m_suffix) + jnp.where(is_first[r0:], ca[r0:], cb[r0:]))
            if causal is not None:
                a = jnp.where(causal[r0:], a, 0.0)
            da = lax.dot_general(dov[r0:], vcat, NT_DIMS, preferred_element_type=F32)
            e_s[kb, r0:, :] = a * da
            sg_s[kb, r0:, :] = jnp.exp(g).astype(BF16)
            dv_ref[pl.ds(off, _KB), :] += halves(lax.dot_general(a.astype(BF16), dov[r0:], TN_DIMS,
                                                                 preferred_element_type=F32))
            sa, sb = _pair_sums(l1m)
            return _add_rows(ca, sa, r0), _add_rows(cb, sb, r0)

        def sweep2(kb, state, causal, r0=0):
            dq, pa, pb = state
            off = pl.multiple_of(kb * _KB, _KB)
            kcat = _stack_heads(first_head, k_ref[pl.ds(off, _KB), :])
            e = e_s[kb, r0:, :]
            s = sg_s[kb, r0:, :].astype(F32)
            dz = e * (1.0 - s) - (jnp.where(is_first[r0:], pa[r0:], pb[r0:]) + _block_sums(e, m_prefix)) * s
            if causal is not None:
                dz = jnp.where(causal[r0:], dz, 0.0)
            dz = dz.astype(BF16)
            dk_ref[pl.ds(off, _KB), :] += halves(lax.dot_general(dz, qv[r0:], TN_DIMS, preferred_element_type=F32))
            sa, sb = _pair_sums(e)
            return (_add_rows(dq, jnp.dot(dz, kcat, preferred_element_type=F32), r0), _add_rows(pa, sa, r0),
                    _add_rows(pb, sb, r0))

        c0 = jnp.zeros((tq, 1), F32)
        st1 = (c0, c0)
        for d in reversed(range(nd)):
            st1 = sweep1(qt * nd + d, st1, key + d * _KB < row, d * _KB)
        def pass1(s, st):
            for u in range(nd):
                st = sweep1((qt - s) * nd - 1 - u, st, None)
            return st

        def pass2(s, st):
            for u in range(nd):
                st = sweep2(s * nd + u, st, None)
            return st

        lax.fori_loop(0, qt, pass1, st1)
        st2 = lax.fori_loop(0, qt, pass2, (jnp.zeros((tq, 128), F32), c0, c0))
        for d in range(nd):
            st2 = sweep2(qt * nd + d, st2, key + d * _KB < row, d * _KB)
        dq_ref[...] = st2[0]

    qspec, kspec, vspec = _attn_specs(t, tq)
    acc = pl.BlockSpec((t, 128), lambda h, i: (0, h))
    return _pc(body, name=name, grid=(SB_WIDTH // 128, t // tq), in_specs=[qspec, kspec, vspec, qspec],
               out_specs=[qspec, acc, acc],
               out_shape=[_sds((t, SB_WIDTH), F32), _sds((t, SB_WIDTH), F32), _sds((t, SB_WIDTH), F32)],
               scratch_shapes=[pltpu.VMEM((nk, tq, _PAIR), F32), pltpu.VMEM((nk, tq, _PAIR), BF16)],
               compiler_params=_cp("parallel", "arbitrary"))(q, k, z, do)


def attn_post_bwd(z, dq, dk, dv, gq, gk, *, name):
    t = z.shape[0]
    tm = _pick(t, (256,))
    scale = 1.0 / math.sqrt(SB_HEAD_DIM)

    def body(q_ref, k_ref, dq_ref, dk_ref, dv_ref, gq_ref, gk_ref, dz_ref, dgq_ref, dgk_ref):
        ones = _group_ones()

        @pl.when(pl.program_id(0) == 0)
        def _():
            dgq_ref[...] = jnp.zeros_like(dgq_ref)
            dgk_ref[...] = jnp.zeros_like(dgk_ref)

        for idx, (src, d_ref, g_ref, dg_ref, mul) in enumerate(
                ((q_ref, dq_ref, gq_ref, dgq_ref, scale), (k_ref, dk_ref, gk_ref, dgk_ref, 1.0))):
            v = src[...].astype(F32)
            r = lax.rsqrt(_split_dot(v * v, ones) * (1.0 / SB_HEAD_DIM) + EPS)
            vh = v * r
            dn = d_ref[...] * mul
            dxh = dn * g_ref[...]
            m = _split_dot(dxh * vh, ones) * (1.0 / SB_HEAD_DIM)
            dz_ref[:, idx * SB_WIDTH:(idx + 1) * SB_WIDTH] = (r * (dxh - vh * m)).astype(dz_ref.dtype)
            s = _rowsum0(dn * vh)
            f = jnp.broadcast_to(s[:, 0:128] + s[:, 128:256] + s[:, 256:384] + s[:, 384:512], dg_ref.shape)
            dg_ref[...] += f + pltpu.roll(f, 64, 1)
        dz_ref[:, 2 * SB_WIDTH:] = dv_ref[...].astype(dz_ref.dtype)

    vec = pl.BlockSpec((1, SB_WIDTH), lambda i: (0, 0))
    row = pl.BlockSpec((tm, SB_WIDTH), lambda i: (i, 0))
    fold = pl.BlockSpec((8, 128), lambda i: (0, 0))
    return _pc(body, name=name, grid=(t // tm,),
               in_specs=[pl.BlockSpec((tm, SB_WIDTH), lambda i: (i, COL_C // SB_WIDTH)),
                         pl.BlockSpec((tm, SB_WIDTH), lambda i: (i, COL_C // SB_WIDTH + 1)), row, row, row, vec, vec],
               out_specs=[pl.BlockSpec((tm, 3 * SB_WIDTH), lambda i: (i, 0)), fold, fold],
               out_shape=[_sds((t, 3 * SB_WIDTH), BF16), _sds((8, 128), F32), _sds((8, 128), F32)],
               compiler_params=_cp("arbitrary"))(z, z, dq, dk, dv, gq, gk)


_GW = 512


def merge_fwd(z, ya, yb, yc, b_gate, *, name):
    t = z.shape[0]
    tm = _pick(t, (512, 256))

    def body(za_ref, zb_ref, zc_ref, ya_ref, yb_ref, yc_ref, bg_ref, o_ref):
        acc = jnp.zeros((tm, _GW), F32)
        for b, (zr, yr) in enumerate(((za_ref, ya_ref), (zb_ref, yb_ref), (zc_ref, yc_ref))):
            acc = acc + _sigmoid(zr[...].astype(F32) + bg_ref[b:b + 1, :]) * yr[...].astype(F32)
        o_ref[...] = acc.astype(o_ref.dtype)

    def zspec(b):
        return pl.BlockSpec((tm, _GW), lambda i, j: (i, COL_G // _GW + 2 * b + j))

    yspec = pl.BlockSpec((tm, _GW), lambda i, j: (i, j))
    return _pc(body, name=name, grid=(t // tm, D_MODEL // _GW),
               in_specs=[zspec(0), zspec(1), zspec(2), yspec, yspec, yspec, pl.BlockSpec((3, _GW), lambda i, j: (0, j))],
               out_specs=yspec, out_shape=_sds((t, D_MODEL), BF16),
               compiler_params=_cp("parallel", "parallel"))(z, z, z, ya, yb, yc, b_gate)


def merge_bwd(z, ya, yb, yc, b_gate, dm, *, name):
    t = z.shape[0]
    tm = _pick(t, (512, 256))

    def body(za_ref, zb_ref, zc_ref, ya_ref, yb_ref, yc_ref, bg_ref, dm_ref,
             dya_ref, dyb_ref, dyc_ref, dza_ref, dzb_ref, dzc_ref, dbg_ref):
        @pl.when(pl.program_id(1) == 0)
        def _():
            dbg_ref[...] = jnp.zeros_like(dbg_ref)

        dmv = dm_ref[...].astype(F32)
        for b, (zr, yr, dyr, dzr) in enumerate(((za_ref, ya_ref, dya_ref, dza_ref), (zb_ref, yb_ref, dyb_ref, dzb_ref),
                                                (zc_ref, yc_ref, dyc_ref, dzc_ref))):
            s = _sigmoid(zr[...].astype(F32) + bg_ref[b:b + 1, :])
            dyr[...] = (dmv * s).astype(dyr.dtype)
            dg = dmv * yr[...].astype(F32) * s * (1.0 - s)
            dzr[...] = dg.astype(dzr.dtype)
            dbg_ref[b:b + 1, :] += _rowsum0(dg)

    def zspec(b):
        return pl.BlockSpec((tm, _GW), lambda j, i: (i, COL_G // _GW + 2 * b + j))

    yspec = pl.BlockSpec((tm, _GW), lambda j, i: (i, j))
    bspec = pl.BlockSpec((3, _GW), lambda j, i: (0, j))
    full = _sds((t, D_MODEL), BF16)
    return _pc(body, name=name, grid=(D_MODEL // _GW, t // tm),
               in_specs=[zspec(0), zspec(1), zspec(2), yspec, yspec, yspec, bspec, yspec],
               out_specs=[yspec] * 6 + [bspec], out_shape=[full] * 6 + [_sds((3, D_MODEL), F32)],
               compiler_params=_cp("parallel", "arbitrary"))(z, z, z, ya, yb, yc, b_gate, dm)


_FW = 1408
_FH = D_FF // _FW


def _row_shifts(window, tm, offsets):
    r = lax.broadcasted_iota(jnp.int32, (tm, window.shape[0]), 0)
    c = lax.broadcasted_iota(jnp.int32, (tm, window.shape[0]), 1)
    return [jnp.dot((c == r + o).astype(BF16), window, preferred_element_type=F32) for o in offsets]


def _ffn_taps(m_ref, h_ref, i, tm):
    main, halo = m_ref[...], h_ref[...]
    window = jnp.concatenate([jnp.where(i > 0, halo, jnp.zeros_like(halo)), main], axis=0)
    return _row_shifts(window, tm, [FFN_HALO - 2, FFN_HALO - 1]) + [main.astype(F32)]


def _ffn_conv(taps, w_ref, b_ref):
    acc = jnp.zeros_like(taps[0]) + b_ref[...]
    for k in range(FFN_KERNEL):
        acc = acc + w_ref[k:k + 1, :] * taps[k]
    return acc


def ffn_mid_fwd(up, cw, cb, *, name):
    t = up.shape[0]
    tm = _pick(t, (256,))

    def body(gm_ref, gh_ref, vm_ref, vh_ref, wg_ref, wv_ref, bg_ref, bv_ref, o_ref):
        i = pl.program_id(0)
        gate = _ffn_conv(_ffn_taps(gm_ref, gh_ref, i, tm), wg_ref, bg_ref)
        val = _ffn_conv(_ffn_taps(vm_ref, vh_ref, i, tm), wv_ref, bv_ref)
        o_ref[...] = (_silu(gate) * val).astype(o_ref.dtype)

    def main(off):
        return pl.BlockSpec((tm, _FW), lambda i, j: (i, j + off))

    def halo(off):
        return pl.BlockSpec((FFN_HALO, _FW), lambda i, j: (jnp.maximum(i * (tm // FFN_HALO) - 1, 0), j + off))

    def wspec(off):
        return pl.BlockSpec((FFN_KERNEL, _FW), lambda i, j: (0, j + off))

    def bspec(off):
        return pl.BlockSpec((1, _FW), lambda i, j: (0, j + off))

    return _pc(body, name=name, grid=(t // tm, _FH),
               in_specs=[main(0), halo(0), main(_FH), halo(_FH), wspec(0), wspec(_FH), bspec(0), bspec(_FH)],
               out_specs=pl.BlockSpec((tm, _FW), lambda i, j: (i, j)), out_shape=_sds((t, D_FF), BF16),
               compiler_params=_cp("parallel", "parallel"))(up, up, up, up, cw, cw, cb, cb)


def ffn_mid_bwd1(up, dact, cw, cb, *, name):
    t = up.shape[0]
    tm = _pick(t, (256,))

    def body(gm_ref, gh_ref, vm_ref, vh_ref, da_ref, wg_ref, wv_ref, bg_ref, bv_ref, d_ref, dw_ref, db_ref):
        j = pl.program_id(0)
        i = pl.program_id(1)

        @pl.when(i == 0)
        def _():
            dw_ref[...] = jnp.zeros_like(dw_ref)
            db_ref[...] = jnp.zeros_like(db_ref)

        gate_taps = _ffn_taps(gm_ref, gh_ref, i, tm)
        val_taps = _ffn_taps(vm_ref, vh_ref, i, tm)
        gate = _ffn_conv(gate_taps, wg_ref, bg_ref)
        da = da_ref[...].astype(F32)

        def finish(d, taps):
            d_ref[...] = d.astype(d_ref.dtype)
            db_ref[...] += _rowsum0(d)
            for k in range(FFN_KERNEL):
                dw_ref[k:k + 1, :] += _rowsum0(d * taps[k])

        @pl.when(j < _FH)
        def _():
            finish(da * _ffn_conv(val_taps, wv_ref, bv_ref) * _silu_grad(gate), gate_taps)

        @pl.when(j >= _FH)
        def _():
            finish(da * _silu(gate), val_taps)

    def main(off):
        return pl.BlockSpec((tm, _FW), lambda j, i: (i, j % _FH + off))

    def halo(off):
        return pl.BlockSpec((FFN_HALO, _FW), lambda j, i: (jnp.maximum(i * (tm // FFN_HALO) - 1, 0), j % _FH + off))

    def wspec(off):
        return pl.BlockSpec((FFN_KERNEL, _FW), lambda j, i: (0, j % _FH + off))

    def bspec(off):
        return pl.BlockSpec((1, _FW), lambda j, i: (0, j % _FH + off))

    return _pc(body, name=name, grid=(2 * _FH, t // tm),
               in_specs=[main(0), halo(0), main(_FH), halo(_FH), pl.BlockSpec((tm, _FW), lambda j, i: (i, j % _FH)),
                         wspec(0), wspec(_FH), bspec(0), bspec(_FH)],
               out_specs=[pl.BlockSpec((tm, _FW), lambda j, i: (i, j)), pl.BlockSpec((FFN_KERNEL, _FW), lambda j, i: (0, j)),
                          pl.BlockSpec((1, _FW), lambda j, i: (0, j))],
               out_shape=[_sds((t, 2 * D_FF), BF16), _sds((FFN_KERNEL, 2 * D_FF), F32), _sds((1, 2 * D_FF), F32)],
               compiler_params=_cp("parallel", "arbitrary"))(up, up, up, up, dact, cw, cw, cb, cb)


def ffn_mid_bwd2(dupc, cw, *, name):
    t = dupc.shape[0]
    tm = _pick(t, (256,))
    nsteps = t // tm
    last = t // FFN_HALO - 1

    def body(m_ref, h_ref, w_ref, o_ref):
        i = pl.program_id(0)
        main, halo = m_ref[...], h_ref[...]
        window = jnp.concatenate([main, jnp.where(i < nsteps - 1, halo, jnp.zeros_like(halo))], axis=0)
        taps = _row_shifts(window, tm, [2, 1]) + [main.astype(F32)]
        acc = jnp.zeros((tm, _FW), F32)
        for k in range(FFN_KERNEL):
            acc = acc + w_ref[k:k + 1, :] * taps[k]
        o_ref[...] = acc.astype(o_ref.dtype)

    return _pc(body, name=name, grid=(nsteps, 2 * _FH),
               in_specs=[pl.BlockSpec((tm, _FW), lambda i, j: (i, j)),
                         pl.BlockSpec((FFN_HALO, _FW), lambda i, j: (jnp.minimum((i + 1) * (tm // FFN_HALO), last), j)),
                         pl.BlockSpec((FFN_KERNEL, _FW), lambda i, j: (0, j))],
               out_specs=pl.BlockSpec((tm, _FW), lambda i, j: (i, j)), out_shape=_sds((t, 2 * D_FF), BF16),
               compiler_params=_cp("parallel", "parallel"))(dupc, dupc, cw)


def _vec(v):
    return v.reshape(1, -1)


def _layer_consts(p):
    return dict(
        sg_bias=jnp.repeat(p['sg_b'].T, SB_HEAD_DIM, axis=1),
        sg_wt=jnp.swapaxes(p['sg_w'], 1, 2),
        gq=jnp.tile(p['q_norm_g'], SB_WIDTH // SB_HEAD_DIM).reshape(1, -1),
        gk=jnp.tile(p['k_norm_g'], SB_WIDTH // SB_HEAD_DIM).reshape(1, -1),
    )


def layer_fwd(x, p, after=(), rest=None):
    c = _layer_consts(p)
    z, h = mm_norm_nn(x, _vec(p['ln1_g']), p['w_in'], name="in_proj", after=after)
    ga = mixa_fwd(z, _vec(p['sg_ln_g']), _vec(p['sg_ln_b']), p['sg_w'], c['sg_bias'], name="mixa_fwd")
    cb = mixb_fwd(z, p['cv_w'], _vec(p['cv_b']), _vec(p['cv_ln_g']), _vec(p['cv_ln_b']), name="mixb_fwd")
    q, k = attn_prep(z, c['gq'], c['gk'], name="attn_prep")
    ao = attn_fwd(q, k, z, name="attn_fwd")
    if rest is not None:
        p = {**p, **rest([ao])}
    ya = mm_nn(ga, p['w_a_out'], name="a_out")
    yb = mm_nn(cb, p['w_b_out'], name="b_out")
    yc = mm_nn(ao, p['w_c_out'], name="c_out")
    merged = merge_fwd(z, ya, yb, yc, p['b_gate'], name="merge_fwd")
    x1 = mm_nn(merged, p['w_out'], res=x, out_dtype=F32, name="out_proj")
    up, h2 = mm_norm_nn(x1, _vec(p['ln2_g']), p['w_up'], name="up_proj")
    act = ffn_mid_fwd(up, p['ffn_conv_w'], _vec(p['ffn_conv_b']), name="ffn_mid_fwd")
    x2 = mm_nn(act, p['w_down'], res=x1, out_dtype=F32, name="down_proj")
    saved = dict(x=x, z=z, h=h, ga=ga, cb=cb, q=q, k=k, ao=ao, ya=ya, yb=yb, yc=yc, merged=merged, x1=x1, up=up,
                 h2=h2, act=act)
    return x2, saved, p


GRAD_GROUPS = (('w_down', 'w_up'), ('w_out', 'w_a_out', 'w_b_out', 'w_c_out'), ('w_in',))


def layer_bwd(dx2, p, s, after=(), emit=None):
    c = _layer_consts(p)
    g = {}
    emit = emit or (lambda names, grads: ())
    g['w_down'] = mm_tn(s['act'], dx2, name="d_w_down")
    dact = mm_nt(dx2, p['w_down'], out_dtype=BF16, name="d_act", after=after)
    dupc, g['ffn_conv_w'], dcb = ffn_mid_bwd1(s['up'], dact, p['ffn_conv_w'], _vec(p['ffn_conv_b']), name="ffn_mid_bwd1")
    g['ffn_conv_b'] = dcb.reshape(-1)
    dup = ffn_mid_bwd2(dupc, p['ffn_conv_w'], name="ffn_mid_bwd2")
    g['w_up'] = mm_tn(s['h2'], dup, name="d_w_up")
    dh2 = mm_nt(dup, p['w_up'], out_dtype=F32, name="d_h2")
    dx1, dg2 = rms_bwd(dh2, s['x1'], _vec(p['ln2_g']), dx2, name="ln2_bwd")
    g['ln2_g'] = dg2.reshape(-1)
    g['w_out'] = mm_tn(s['merged'], dx1, name="d_w_out")
    dm = mm_nt(dx1, p['w_out'], out_dtype=BF16, name="d_merged", after=emit(GRAD_GROUPS[0], g))
    dya, dyb, dyc, dzg0, dzg1, dzg2, g['b_gate'] = merge_bwd(s['z'], s['ya'], s['yb'], s['yc'], p['b_gate'], dm,
                                                             name="merge_bwd")
    g['w_a_out'] = mm_tn(s['ga'], dya, name="d_w_a_out")
    g['w_b_out'] = mm_tn(s['cb'], dyb, name="d_w_b_out")
    g['w_c_out'] = mm_tn(s['ao'], dyc, name="d_w_c_out")
    dga = mm_nt(dya, p['w_a_out'], out_dtype=BF16, name="d_ga")
    dcb3 = mm_nt(dyb, p['w_b_out'], out_dtype=BF16, name="d_cb")
    dao = mm_nt(dyc, p['w_c_out'], out_dtype=BF16, name="d_ao")
    dza, g['sg_w'], dsgb, dlg, dlb = mixa_bwd(s['z'], dga, _vec(p['sg_ln_g']), _vec(p['sg_ln_b']), p['sg_w'],
                                               c['sg_wt'], c['sg_bias'], name="mixa_bwd")
    g['sg_b'] = dsgb[:, :SG_WIDTH // SB_HEAD_DIM].T
    g['sg_ln_g'] = dlg.reshape(-1)
    g['sg_ln_b'] = dlb.reshape(-1)
    dc1, g['cv_w'], dcvb, dcg, dcbb = mixb_bwd1(s['z'], dcb3, p['cv_w'], _vec(p['cv_b']), _vec(p['cv_ln_g']),
                                                _vec(p['cv_ln_b']), name="mixb_bwd1")
    g['cv_b'] = dcvb.reshape(-1)
    g['cv_ln_g'] = dcg.reshape(-1)
    g['cv_ln_b'] = dcbb.reshape(-1)
    dzb = mixb_bwd2(s['z'], dc1, p['cv_w'], name="mixb_bwd2")
    dq, dk, dv = attn_bwd(s['q'], s['k'], s['z'], dao, name="attn_bwd")
    dzc, dgq, dgk = attn_post_bwd(s['z'], dq, dk, dv, c['gq'], c['gk'], name="attn_post_bwd")
    g['q_norm_g'] = dgq[0, :SB_HEAD_DIM]
    g['k_norm_g'] = dgk[0, :SB_HEAD_DIM]
    dz = jnp.concatenate([dza, dzb, dzc, dzg0, dzg1, dzg2], axis=1)
    g['w_in'] = mm_tn(s['h'], dz, name="d_w_in")
    dh = mm_nt(dz, p['w_in'], out_dtype=F32, name="d_h", after=emit(GRAD_GROUPS[1], g))
    dx, dg1 = rms_bwd(dh, s['x'], _vec(p['ln1_g']), dx1, name="ln1_bwd")
    g['ln1_g'] = dg1.reshape(-1)
    return dx, g


def local_step(x, target, depth, get_layer, on_grads, emit=lambda l: None):
    saved, layers = [], []
    for l in range(depth):
        p, after, rest = get_layer(l, x)
        x, s, p = layer_fwd(x, p, after, rest)
        layers.append(p)
        saved.append(s)
    loss, dx = loss_head(x, target, name="loss_head")
    after = ()
    for l in reversed(range(depth)):
        dx, g = layer_bwd(dx, layers[l], saved[l], after, emit(l))
        after = on_grads(l, g)
    return loss[0, 0], dx


def adamw(w, g, m, v, *, name):
    r, c = w.shape
    tr = _pick(r, (256, 704)) if r * c > 512 * 1024 else r

    def body(w_ref, g_ref, m_ref, v_ref, d_ref, mo_ref, vo_ref):
        gv = g_ref[...]
        mn = ADAM_B1 * m_ref[...] + (1.0 - ADAM_B1) * gv
        vn = ADAM_B2 * v_ref[...] + (1.0 - ADAM_B2) * (gv * gv)
        m_hat = mn / (1.0 - ADAM_B1 ** ADAM_STEP)
        v_hat = vn / (1.0 - ADAM_B2 ** ADAM_STEP)
        d_ref[...] = -ADAM_LR * (m_hat / (jnp.sqrt(v_hat) + ADAM_EPS) + ADAM_WD * w_ref[...])
        mo_ref[...] = mn
        vo_ref[...] = vn

    spec = pl.BlockSpec((tr, c), lambda i: (i, 0))
    out = _sds((r, c), F32)
    return _pc(body, name=name, grid=(r // tr,), in_specs=[spec] * 4, out_specs=[spec] * 3, out_shape=[out] * 3,
               compiler_params=_cp("parallel"))(w, g, m, v)


def _as3(a):
    return a if a.ndim == 3 else a.reshape((1,) + a.shape)


def add_half(g, recv, c_idx, *, name):
    s, rh, w = recv.shape
    tr = _pick(rh, (256, 352, 128))
    nb = rh // tr

    def body(c_ref, g_ref, r_ref, o_ref):
        o_ref[...] = (g_ref[...].astype(F32) + r_ref[...].astype(F32)).astype(o_ref.dtype)

    own = pl.BlockSpec((1, tr, w), lambda k, i, c_ref: (k, c_ref[0] * nb + i, 0))
    half = pl.BlockSpec((1, tr, w), lambda k, i, c_ref: (k, i, 0))
    gs = pltpu.PrefetchScalarGridSpec(num_scalar_prefetch=1, grid=(s, nb), in_specs=[own, half], out_specs=half)
    return _pc(body, name=name, grid_spec=gs, out_shape=_sds((s, rh, w), BF16),
               compiler_params=_cp("parallel", "parallel"))(c_idx, g, recv)


def sum_shard(p, recv, pos_idx, stacked, layer, *, by_rows, name):
    _, rh, w = recv.shape
    tr = _pick(rh, (256, 352, 128))
    nb = rh // tr

    def body(pos_ref, p_ref, r_ref, s_ref, o_ref):
        acc = p_ref[0].astype(F32)
        for j in range(N_CHIPS - 1):
            acc = acc + r_ref[j].astype(F32)
        o_ref[0] = acc

    if by_rows:
        own = pl.BlockSpec((1, tr, w), lambda i, pos_ref: (pos_ref[0], i, 0))
    else:
        own = pl.BlockSpec((1, tr, w), lambda i, pos_ref: (0, i, pos_ref[0]))
    gs = pltpu.PrefetchScalarGridSpec(num_scalar_prefetch=1, grid=(nb,),
                                      in_specs=[own, pl.BlockSpec((N_CHIPS - 1, tr, w), lambda i, pos_ref: (0, i, 0)), ANY],
                                      out_specs=pl.BlockSpec((1, tr, w), lambda i, pos_ref: (layer, pos_ref[1] * nb + i, 0)))
    return _pc(body, name=name, grid_spec=gs, out_shape=_sds(stacked.shape, F32), input_output_aliases={3: 0},
               compiler_params=_cp("parallel"))(pos_idx, p, recv, stacked)


def sum_slots(slab, *, name):
    _, r, w = slab.shape
    tr = _pick(r, (512, 256, 8))

    def body(s_ref, o_ref):
        acc = s_ref[0]
        for j in range(1, N_DEV):
            acc = acc + s_ref[j]
        o_ref[...] = acc

    return _pc(body, name=name, grid=(r // tr,), in_specs=[pl.BlockSpec((N_DEV, tr, w), lambda i: (0, i, 0))],
               out_specs=pl.BlockSpec((tr, w), lambda i: (i, 0)), out_shape=_sds((r, w), F32),
               compiler_params=_cp("parallel"))(slab)


def _mesh_pos():
    x, y, c = lax.axis_index("x"), lax.axis_index("y"), lax.axis_index("c")
    others = [(1 - x, y), (x, 1 - y), (1 - x, 1 - y)]
    return x, y, c, others


def _rcopy(src, dst, ssem, rsem, k, dev):
    return pltpu.make_async_remote_copy(src_ref=src, dst_ref=dst, send_sem=ssem.at[k], recv_sem=rsem.at[k],
                                        device_id=dev, device_id_type=MESH)


def _comm_call(body, name, n_in, out_shape, n_local, n_remote):
    scratch = [pltpu.SemaphoreType.DMA((max(n_local, 1),)), pltpu.SemaphoreType.DMA((n_remote,)),
               pltpu.SemaphoreType.DMA((n_remote,))]
    return _pc(body, name=name, in_specs=[ANY] * n_in, out_specs=[ANY] * len(out_shape), out_shape=out_shape,
               scratch_shapes=scratch)


GATHERED = BIG + SMALL_COL
HBM_SPEC = pl.BlockSpec(memory_space=pltpu.HBM)
SEM_SPEC = pl.BlockSpec(memory_space=pltpu.SEMAPHORE)
TOKEN_SHAPE = (8, 128)


def place_block(w, layer, pos_idx, *, by_rows, dtype, name):
    _, r, c = w.shape
    tr = _pick(r, (512, 704, 256))

    def body(pos_ref, w_ref, o_ref):
        if by_rows:
            o_ref[0] = w_ref[0].astype(dtype)
        else:
            o_ref[...] = w_ref[0].astype(dtype)

    if by_rows:
        out_spec, shape = pl.BlockSpec((1, tr, c), lambda i, pos_ref: (pos_ref[0], i, 0)), (N_CHIPS, r, c)
    else:
        out_spec, shape = pl.BlockSpec((tr, c), lambda i, pos_ref: (i, pos_ref[0])), (r, N_CHIPS * c)
    gs = pltpu.PrefetchScalarGridSpec(num_scalar_prefetch=1, grid=(r // tr,),
                                      in_specs=[pl.BlockSpec((1, tr, c), lambda i, pos_ref: (layer, i, 0))],
                                      out_specs=out_spec)
    return _pc(body, name=name, grid_spec=gs, out_shape=_sds(shape, dtype), compiler_params=_cp("parallel"))(pos_idx, w)


def _gather_windows(bufs):
    names, arrs = list(bufs), list(bufs.values())

    def dwin(refs, i, k, h):
        if names[i] in BIG_ROW:
            _, r, _ = arrs[i].shape
            return refs[i].at[k] if h is None else refs[i].at[k, pl.ds(h * (r // 2), r // 2), :]
        r, cs = arrs[i].shape[0], arrs[i].shape[1] // N_CHIPS
        cols = pl.ds(pl.multiple_of(k * cs, 128), cs)
        return refs[i].at[:, cols] if h is None else refs[i].at[pl.ds(h * (r // 2), r // 2), cols]

    def swin(refs, i, h):
        x, y, _, _ = _mesh_pos()
        return dwin(refs, i, 2 * x + y, h)

    return names, dwin, swin


def _gather_send(names, ins, outs, ssem, rsem, dwin, swin, stride):
    x, y, c, others = _mesh_pos()
    sends = []
    for i, n in enumerate(names):
        h = c if n in BIG else None
        for j, chip in enumerate(others):
            cp = _rcopy(swin(ins, i, h), swin(outs, i, h), ssem, rsem, stride * i + j, (*chip, c))
            cp.start()
            sends.append(cp)
    return sends


def _gather_pass_on(names, outs, ssem, rsem, dwin, stride, first_off, pass_off):
    x, y, c, others = _mesh_pos()
    sib = (x, y, 1 - c)
    sends = []
    for j, chip in enumerate(others):
        kk = 2 * chip[0] + chip[1]
        for i, n in enumerate(names):
            got = dwin(outs, i, kk, c if n in BIG else None)
            if first_off is not None:
                _rcopy(got, got, ssem, rsem, stride * i + first_off + j, (*chip, c)).wait_recv()
            if n in BIG:
                fwd = _rcopy(got, got, ssem, rsem, stride * i + pass_off + j, sib)
                fwd.start()
                sends.append(fwd)
    for j, chip in enumerate(others):
        kk = 2 * chip[0] + chip[1]
        for i, n in enumerate(names):
            if n in BIG:
                got = dwin(outs, i, kk, 1 - c)
                _rcopy(got, got, ssem, rsem, stride * i + pass_off + j, sib).wait_recv()
    return sends


def _as_weights(bufs):
    return {n: (o.reshape(o.shape[0] * o.shape[1], o.shape[2]) if n in BIG_ROW else o) for n, o in bufs.items()}


def _comm_in_place(body, name, bufs, n_sems):
    nn = len(bufs)
    arrs = list(bufs.values())
    scratch = [pltpu.SemaphoreType.DMA((n_sems,)), pltpu.SemaphoreType.DMA((n_sems,))]
    outs = _pc(body, name=name, in_specs=[ANY] * nn, out_specs=[ANY] * nn, out_shape=[_sds(a.shape, a.dtype) for a in arrs],
               scratch_shapes=scratch, input_output_aliases={i: i for i in range(nn)})(*arrs)
    return dict(zip(bufs, outs))


def allgather_weights(bufs, *, name):
    nn = len(bufs)
    names, dwin, swin = _gather_windows(bufs)

    def body(*refs):
        ins, outs = refs[:nn], refs[nn:2 * nn]
        ssem, rsem = refs[2 * nn:]
        sends = _gather_send(names, ins, outs, ssem, rsem, dwin, swin, 6)
        sends += _gather_pass_on(names, outs, ssem, rsem, dwin, 6, 0, 3)
        for cp in sends:
            cp.wait_send()

    return _comm_in_place(body, name, bufs, 6 * nn)


def _split_start(body, name, bufs, extra_in, n_sems):
    nn = len(bufs)
    arrs = [pltpu.with_memory_space_constraint(a, pltpu.HBM) for a in bufs.values()]
    out_shape = ([pltpu.SemaphoreType.DMA((n_sems,)), pltpu.SemaphoreType.DMA((n_sems,))]
                 + [pltpu.HBM(a.shape, a.dtype) for a in arrs] + [_sds(TOKEN_SHAPE, F32)])
    outs = _pc(body, name=name, in_specs=[HBM_SPEC] * nn + [ANY] * len(extra_in),
               out_specs=[SEM_SPEC, SEM_SPEC] + [HBM_SPEC] * nn + [pl.BlockSpec(memory_space=pltpu.VMEM)],
               out_shape=out_shape, input_output_aliases={i: 2 + i for i in range(nn)},
               compiler_params=pltpu.CompilerParams(has_side_effects=pltpu.SideEffectType.DATAFLOW_SIDE_EFFECTING),
               )(*arrs, *extra_in)
    return dict(ssem=outs[0], rsem=outs[1], bufs=dict(zip(bufs, outs[2:2 + nn])), token=outs[-1])


def _split_wait(body, name, handle, after):
    bufs = handle['bufs']
    nn = len(bufs)
    arrs = list(bufs.values())
    outs = _pc(body, name=name, in_specs=[HBM_SPEC] * nn + [SEM_SPEC, SEM_SPEC] + [ANY] * len(after),
               out_specs=[HBM_SPEC] * nn, out_shape=[pltpu.HBM(a.shape, a.dtype) for a in arrs],
               input_output_aliases={i: i for i in range(nn)},
               compiler_params=pltpu.CompilerParams(has_side_effects=pltpu.SideEffectType.DATAFLOW_SIDE_EFFECTING),
               )(*arrs, handle['ssem'], handle['rsem'], *after)
    return dict(zip(bufs, outs))


def gather_start(bufs, after, *, name):
    nn = len(bufs)
    names, dwin, swin = _gather_windows(bufs)

    def body(*refs):
        ins = refs[:nn]
        ssem, rsem = refs[nn + len(after)], refs[nn + len(after) + 1]
        _gather_send(names, ins, ins, ssem, rsem, dwin, swin, 3)
        refs[-1][...] = jnp.zeros(TOKEN_SHAPE, F32)

    return _split_start(body, name, bufs, after, 3 * nn)


def gather_wait(handle, after, *, name):
    nn = len(handle['bufs'])
    names, dwin, swin = _gather_windows(handle['bufs'])

    def body(*refs):
        ins = refs[:nn]
        ssem, rsem = refs[nn], refs[nn + 1]
        x, y, c, others = _mesh_pos()
        for i, n in enumerate(names):
            h = c if n in BIG else None
            for j, chip in enumerate(others):
                kk = 2 * chip[0] + chip[1]
                cp = _rcopy(swin(ins, i, h), dwin(ins, i, kk, h), ssem, rsem, 3 * i + j, (*chip, c))
                cp.wait_send()
                cp.wait_recv()

    return _split_wait(body, name, handle, after)


def gather_finish(bufs, *, name):
    nn = len(bufs)
    names, dwin, _ = _gather_windows(bufs)

    def body(*refs):
        outs = refs[nn:2 * nn]
        ssem, rsem = refs[2 * nn:]
        for cp in _gather_pass_on(names, outs, ssem, rsem, dwin, 3, None, 0):
            cp.wait_send()

    return _comm_in_place(body, name, bufs, 3 * nn)


def _grad_view(n, g):
    return g.reshape(N_CHIPS, g.shape[0] // N_CHIPS, g.shape[1]) if n in BIG_ROW else g.reshape((1,) + g.shape)


def exchange_halves(gv, *, name):
    nn = len(gv)
    arrs = list(gv.values())

    def body(*refs):
        ins, outs = refs[:nn], refs[nn:2 * nn]
        _, ssem, rsem = refs[2 * nn:]
        x, y, c, _ = _mesh_pos()
        cps = []
        for i in range(nn):
            rh = arrs[i].shape[1] // 2
            cp = _rcopy(ins[i].at[:, pl.ds((1 - c) * rh, rh), :], outs[i], ssem, rsem, i, (x, y, 1 - c))
            cp.start()
            cps.append(cp)
        for cp in cps:
            cp.wait()

    out_shape = [_sds((a.shape[0], a.shape[1] // 2, a.shape[2]), a.dtype) for a in arrs]
    return dict(zip(gv, _comm_call(body, name, nn, out_shape, 0, nn)(*arrs)))


def _shard_shape(n, p):
    _, rh, w = p.shape
    return (rh, w) if n in BIG_ROW else (rh, w // N_CHIPS)


def _scatter_copies(pv, ins, outs, ssem, rsem):
    x, y, c, others = _mesh_pos()
    cps = []
    for i, (n, p) in enumerate(pv.items()):
        _, ws = _shard_shape(n, p)
        for j, chip in enumerate(others):
            kk = 2 * chip[0] + chip[1]
            if n in BIG_ROW:
                src = ins[i].at[kk]
            else:
                src = ins[i].at[0, :, pl.ds(pl.multiple_of(kk * ws, 128), ws)]
            cps.append(_rcopy(src, outs[i].at[j], ssem, rsem, 3 * i + j, (*chip, c)))
    return cps


def _recv_shapes(pv):
    return [(N_CHIPS - 1,) + _shard_shape(n, p) for n, p in pv.items()]


def scatter_partials(pv, *, name):
    nn = len(pv)

    def body(*refs):
        ins, outs = refs[:nn], refs[nn:2 * nn]
        _, ssem, rsem = refs[2 * nn:]
        cps = _scatter_copies(pv, ins, outs, ssem, rsem)
        for cp in cps:
            cp.start()
        for cp in cps:
            cp.wait()

    out_shape = [_sds(s, p.dtype) for s, p in zip(_recv_shapes(pv), pv.values())]
    return pv, dict(zip(pv, _comm_call(body, name, nn, out_shape, 0, 3 * nn)(*pv.values())))


_RECV = "/recv"


def scatter_start(pv, *, name):
    nn = len(pv)

    def body(*refs):
        ins, lands = refs[:nn], refs[nn:2 * nn]
        ssem, rsem = refs[2 * nn], refs[2 * nn + 1]
        for cp in _scatter_copies(pv, ins, lands, ssem, rsem):
            cp.start()
        refs[-1][...] = jnp.zeros(TOKEN_SHAPE, F32)

    lands = {n + _RECV: lax.empty(s, p.dtype) for (n, p), s in zip(pv.items(), _recv_shapes(pv))}
    return _split_start(body, name, {**pv, **lands}, (), 3 * nn)


def scatter_wait(handle, after, *, name):
    nn = len(handle['bufs']) // 2
    pv = dict(list(handle['bufs'].items())[:nn])

    def body(*refs):
        ins, zones = refs[:nn], refs[nn:2 * nn]
        ssem, rsem = refs[2 * nn], refs[2 * nn + 1]
        for cp in _scatter_copies(pv, ins, zones, ssem, rsem):
            cp.wait_send()
            cp.wait_recv()

    outs = _split_wait(body, name, handle, after)
    return {n: outs[n] for n in pv}, {n: outs[n + _RECV] for n in pv}


def join_halves(rv, layer, *, name):
    nn = len(rv)
    arrs = list(rv.values())

    def body(*refs):
        ins, outs = refs[:nn], refs[nn:2 * nn]
        ssem, rsem = refs[2 * nn:]
        x, y, c, _ = _mesh_pos()
        cps = []
        for i in range(nn):
            rh = arrs[i].shape[1] // 2
            rows = pl.ds(c * rh, rh)
            cp = _rcopy(ins[i].at[layer, rows, :], outs[i].at[layer, rows, :], ssem, rsem, i, (x, y, 1 - c))
            cp.start()
            cps.append(cp)
        for i, cp in enumerate(cps):
            cp.wait_send()
            rh = arrs[i].shape[1] // 2
            got = outs[i].at[layer, pl.ds((1 - c) * rh, rh), :]
            _rcopy(got, got, ssem, rsem, i, (x, y, 1 - c)).wait_recv()

    return _comm_in_place(body, name, rv, nn)


def chip_partials(grads, names, c_idx):
    gv = {n: _grad_view(n, grads[n]) for n in names}
    recv = exchange_halves(gv, name="rs_exchange_halves")
    return {n: add_half(gv[n], recv[n], c_idx, name="rs_add_" + n) for n in names}


def reduce_shards(pv, got, pos_idx, stacked, layer):
    rv = {n: sum_shard(pv[n], got[n], pos_idx, stacked[n], layer, by_rows=n in BIG_ROW, name="rs_sum_" + n) for n in pv}
    return join_halves(rv, layer, name="rs_join_halves")


def _slab_first(ref, ssem, rsem):
    x, y, c, others = _mesh_pos()
    mine = ref.at[4 * x + 2 * y + c]
    peers = [(x, y, 1 - c)] + [(*chip, c) for chip in others]
    out = []
    for k, p in enumerate(peers):
        got = ref.at[4 * p[0] + 2 * p[1] + p[2]]
        out.append((_rcopy(mine, mine, ssem, rsem, k, p), _rcopy(got, got, ssem, rsem, k, p)))
    return out


def slab_start(slab, *, name):
    def body(ref, ssem, rsem, thru, token):
        for cp, _ in _slab_first(ref, ssem, rsem):
            cp.start()
        token[...] = jnp.zeros(TOKEN_SHAPE, F32)

    return _split_start(body, name, {'slab': slab}, (), 4)


def slab_wait(handle, after, *, name):
    def body(ref, ssem, rsem, *rest):
        for sent, landed in _slab_first(ref, ssem, rsem):
            sent.wait_send()
            landed.wait_recv()

    return _split_wait(body, name, handle, after)['slab']


def slab_finish(slab, *, name):
    def body(in_ref, out_ref, ssem, rsem):
        x, y, c, others = _mesh_pos()
        sib = (x, y, 1 - c)
        sends = []
        for j, chip in enumerate(others):
            got = out_ref.at[4 * chip[0] + 2 * chip[1] + c]
            sends.append(_rcopy(got, got, ssem, rsem, j, sib))
            sends[-1].start()
        for j, chip in enumerate(others):
            got = out_ref.at[4 * chip[0] + 2 * chip[1] + 1 - c]
            _rcopy(got, got, ssem, rsem, j, sib).wait_recv()
        for cp in sends:
            cp.wait_send()

    return _comm_in_place(body, name, {'slab': slab}, 3)['slab']


def _pad128(n):
    return -(-n // 128) * 128


def _pack_small(grads, shapes):
    parts = []
    for g in grads:
        for n in SMALL:
            v = g[n].astype(F32).reshape(-1)
            parts.append(jnp.pad(v, (0, _pad128(v.shape[0]) - v.shape[0])))
    flat = jnp.concatenate(parts)
    rows = -(-flat.shape[0] // (128 * 512)) * 512
    return jnp.pad(flat, (0, rows * 128 - flat.shape[0])).reshape(rows, 128)


def _unpack_small(slab, shapes, depth):
    flat = slab.reshape(-1)
    out = {n: [] for n in SMALL}
    off = 0
    for _ in range(depth):
        for n in SMALL:
            size = math.prod(shapes[n])
            out[n].append(flat[off:off + size].reshape(shapes[n]))
            off += _pad128(size)
    return {n: jnp.stack(v) for n, v in out.items()}


def _adamw_nd(w, g, m, v, name):
    shp = w.shape
    two = lambda a: a.reshape(-1, shp[-1])
    return tuple(o.reshape(shp) for o in adamw(two(w), two(g), two(m), two(v), name=name))


def kernel(x, ln1_g, w_in, b_gate, sg_ln_g, sg_ln_b, sg_w, sg_b, w_a_out, cv_w, cv_b, cv_ln_g, cv_ln_b, w_b_out, q_norm_g, k_norm_g, w_c_out, w_out, ln2_g, w_up, ffn_conv_w, ffn_conv_b, w_down, loss_target, m_ln1_g, m_w_in, m_b_gate, m_sg_ln_g, m_sg_ln_b, m_sg_w, m_sg_b, m_w_a_out, m_cv_w, m_cv_b, m_cv_ln_g, m_cv_ln_b, m_w_b_out, m_q_norm_g, m_k_norm_g, m_w_c_out, m_w_out, m_ln2_g, m_w_up, m_ffn_conv_w, m_ffn_conv_b, m_w_down, v_ln1_g, v_w_in, v_b_gate, v_sg_ln_g, v_sg_ln_b, v_sg_w, v_sg_b, v_w_a_out, v_cv_w, v_cv_b, v_cv_ln_g, v_cv_ln_b, v_w_b_out, v_q_norm_g, v_k_norm_g, v_w_c_out, v_w_out, v_ln2_g, v_w_up, v_ffn_conv_w, v_ffn_conv_b, v_w_down):
    w = dict(ln1_g=ln1_g, w_in=w_in, b_gate=b_gate, sg_ln_g=sg_ln_g, sg_ln_b=sg_ln_b, sg_w=sg_w, sg_b=sg_b,
             w_a_out=w_a_out, cv_w=cv_w, cv_b=cv_b, cv_ln_g=cv_ln_g, cv_ln_b=cv_ln_b, w_b_out=w_b_out,
             q_norm_g=q_norm_g, k_norm_g=k_norm_g, w_c_out=w_c_out, w_out=w_out, ln2_g=ln2_g, w_up=w_up,
             ffn_conv_w=ffn_conv_w, ffn_conv_b=ffn_conv_b, w_down=w_down)
    m = dict(ln1_g=m_ln1_g, w_in=m_w_in, b_gate=m_b_gate, sg_ln_g=m_sg_ln_g, sg_ln_b=m_sg_ln_b, sg_w=m_sg_w,
             sg_b=m_sg_b, w_a_out=m_w_a_out, cv_w=m_cv_w, cv_b=m_cv_b, cv_ln_g=m_cv_ln_g, cv_ln_b=m_cv_ln_b,
             w_b_out=m_w_b_out, q_norm_g=m_q_norm_g, k_norm_g=m_k_norm_g, w_c_out=m_w_c_out, w_out=m_w_out,
             ln2_g=m_ln2_g, w_up=m_w_up, ffn_conv_w=m_ffn_conv_w, ffn_conv_b=m_ffn_conv_b, w_down=m_w_down)
    v = dict(ln1_g=v_ln1_g, w_in=v_w_in, b_gate=v_b_gate, sg_ln_g=v_sg_ln_g, sg_ln_b=v_sg_ln_b, sg_w=v_sg_w,
             sg_b=v_sg_b, w_a_out=v_w_a_out, cv_w=v_cv_w, cv_b=v_cv_b, cv_ln_g=v_cv_ln_g, cv_ln_b=v_cv_ln_b,
             w_b_out=v_w_b_out, q_norm_g=v_q_norm_g, k_norm_g=v_k_norm_g, w_c_out=v_w_c_out, w_out=v_w_out,
             ln2_g=v_ln2_g, w_up=v_w_up, ffn_conv_w=v_ffn_conv_w, ffn_conv_b=v_ffn_conv_b, w_down=v_w_down)
    depth = ln1_g.shape[0]
    cx, cy, cc = lax.axis_index("x"), lax.axis_index("y"), lax.axis_index("c")
    me = 2 * cx + cy
    pos_idx = jnp.stack([me, cc]).astype(jnp.int32)
    c_idx = jnp.reshape(cc, (1,)).astype(jnp.int32)

    padded = {n: jnp.pad(w[n], ((0, 0), (0, -w[n].shape[1] % 8), (0, 0))) for n in SMALL_COL}
    first, later = ['w_in'] + SMALL_COL, [n for n in BIG if n != 'w_in']

    def blocks(names, l):
        return {n: (place_block(w[n], l, pos_idx, by_rows=n in BIG_ROW, dtype=BF16, name="place_" + n) if n in BIG else
                    place_block(padded[n], l, pos_idx, by_rows=False, dtype=F32, name="place_" + n)) for n in names}

    full0 = allgather_weights(blocks(first, 0), name="allgather_weights")
    gathers = [gather_start(blocks(later, 0), [full0['w_in']], name="gather_start_0")]
    for l in range(1, depth):
        gathers.append(gather_start(blocks(GATHERED, l), [gathers[-1]['token']], name="gather_start_%d" % l))

    def arrived(l, after):
        bufs = gather_wait(gathers[l], after, name="gather_wait_%d" % l)
        return _as_weights(gather_finish(bufs, name="gather_finish_%d" % min(l, 1)))

    def get_layer(l, x_in):
        if l == 0:
            p, after, rest = _as_weights(full0), tuple(h['token'] for h in gathers), functools.partial(arrived, 0)
        else:
            p, after, rest = arrived(l, [x_in]), (), None
        for n in SMALL:
            p[n] = p[n][:w[n].shape[1]] if n in SMALL_COL else w[n][l]
        return p, after, rest

    grads, scatters = [None] * depth, []

    def on_grads(l, g):
        grads[l] = g
        if l == 0:
            scatters.append((0, scatter_partials(chip_partials(g, GRAD_GROUPS[-1], c_idx), name="rs_scatter_partials")))
            return ()
        scatters.append((l, scatter_start(chip_partials(g, BIG, c_idx), name="scatter_start_%d" % l)))
        return (scatters[-1][1]['token'],)

    def emit0(names, g):
        scatters.append((0, scatter_start(chip_partials(g, names, c_idx), name="scatter_start_0_" + names[0])))
        return (scatters[-1][1]['token'],)

    loss, dx = local_step(x[0], loss_target[0], depth, get_layer, on_grads, lambda l: emit0 if l == 0 else None)
    loss = lax.psum(loss, ("x", "y", "c"))
    full_shapes = {n: (w[n].shape[1], N_CHIPS * w[n].shape[2]) if n in SMALL_COL else w[n].shape[1:] for n in SMALL}
    mine = _pack_small(grads, full_shapes)
    slots = lax.dynamic_update_slice(lax.empty((N_DEV,) + mine.shape, F32), mine[None], (4 * cx + 2 * cy + cc, 0, 0))
    gathering = slab_start(slots, name="small_grads_start")
    grad = {n: lax.empty(w[n].shape, F32) for n in BIG}
    for i, (l, sc) in enumerate(scatters):
        pv, got = sc if isinstance(sc, tuple) else scatter_wait(sc, [dx, gathering['token']], name="scatter_wait_%d" % i)
        grad.update(reduce_shards(pv, got, pos_idx, grad, l))

    delta, new_m, new_v = {}, {}, {}
    for n in BIG:
        delta[n], new_m[n], new_v[n] = _adamw_nd(w[n], grad[n], m[n], v[n], "adamw_" + n)
    slots = slab_finish(slab_wait(gathering, [delta[n] for n in BIG], name="small_grads_wait"), name="small_grads_finish")
    small = _unpack_small(sum_slots(slots, name="sum_small_grads"), full_shapes, depth)
    for n in SMALL:
        if n in SMALL_COL:
            cs = w[n].shape[-1]
            grad[n] = lax.dynamic_slice_in_dim(small[n], me * cs, cs, axis=small[n].ndim - 1)
        else:
            grad[n] = small[n]
        delta[n], new_m[n], new_v[n] = _adamw_nd(w[n], grad[n], m[n], v[n], "adamw_" + n)
    return (loss, dx[None], *[grad[n] for n in WEIGHTS], *[delta[n] for n in WEIGHTS],
            *[new_m[n] for n in WEIGHTS], *[new_v[n] for n in WEIGHTS])
```

```python
import functools
import math

import jax
import jax.numpy as jnp
from jax import lax
from jax.experimental import pallas as pl
from jax.experimental.pallas import tpu as pltpu

F32 = jnp.float32
BF16 = jnp.bfloat16
MESH = pl.DeviceIdType.MESH
ANY = pl.BlockSpec(memory_space=pl.ANY)

EPS = 1e-6
D_MODEL = 1024
DEPTH = 4
SG_WIDTH = 512
CHUNK = 128
CV_WIDTH = 512
CV_KERNEL = 31
SB_WIDTH = 512
SB_HEAD_DIM = 64
Q_BLOCK = 128
D_FF = 2816
FFN_KERNEL = 3
COL_B = 1024
COL_C = 2048
COL_G = 3584
IN_COLS = 6656
N_CHIPS = 4
N_DEV = 8
CV_HALO = 32
FFN_HALO = 16

ADAM_LR = 0.001
ADAM_B1 = 0.9
ADAM_B2 = 0.999
ADAM_EPS = 1e-08
ADAM_WD = 0.01
ADAM_STEP = 10

VMEM_LIMIT_BYTES = 56 * 1024 * 1024

NT_DIMS = (((1,), (1,)), ((), ()))
TN_DIMS = (((0,), (0,)), ((), ()))

WEIGHTS = ['ln1_g', 'w_in', 'b_gate', 'sg_ln_g', 'sg_ln_b', 'sg_w', 'sg_b', 'w_a_out', 'cv_w', 'cv_b',
           'cv_ln_g', 'cv_ln_b', 'w_b_out', 'q_norm_g', 'k_norm_g', 'w_c_out', 'w_out', 'ln2_g', 'w_up',
           'ffn_conv_w', 'ffn_conv_b', 'w_down']
BIG_COL = ['w_in', 'w_a_out', 'w_b_out', 'w_c_out', 'w_up']
BIG_ROW = ['w_out', 'w_down']
BIG = BIG_COL + BIG_ROW
SMALL_COL = ['b_gate', 'cv_w', 'ffn_conv_w']
SMALL = [n for n in WEIGHTS if n not in BIG]


def _pc(body, **kw):
    return pl.pallas_call(body, **kw)


def _cp(*sem):
    return pltpu.CompilerParams(dimension_semantics=sem, vmem_limit_bytes=VMEM_LIMIT_BYTES)


def _sds(shape, dtype):
    return jax.ShapeDtypeStruct(shape, dtype)


_GELU_C = math.sqrt(2.0 / math.pi)
_GELU_A = 0.044715


def _sigmoid(x):
    return jax.nn.sigmoid(x)


def _gelu(x):
    return 0.5 * x * (1.0 + jnp.tanh(_GELU_C * (x + _GELU_A * x * x * x)))


def _gelu_grad(x):
    t = jnp.tanh(_GELU_C * (x + _GELU_A * x * x * x))
    return 0.5 * (1.0 + t) + 0.5 * x * (1.0 - t * t) * _GELU_C * (1.0 + 3.0 * _GELU_A * x * x)


def _silu(x):
    return x * _sigmoid(x)


def _silu_grad(x):
    s = _sigmoid(x)
    return s * (1.0 + x * (1.0 - s))


def _ln_stats(x):
    mu = jnp.mean(x, axis=-1, keepdims=True)
    xc = x - mu
    r = lax.rsqrt(jnp.mean(xc * xc, axis=-1, keepdims=True) + EPS)
    return xc * r, r


def _ln_bwd(dy, xhat, r, g):
    dxh = dy * g
    return r * (dxh - jnp.mean(dxh, axis=-1, keepdims=True) - xhat * jnp.mean(dxh * xhat, axis=-1, keepdims=True))


def _split_dot(x, m):
    hi = x.astype(BF16)
    lo = (x - hi.astype(F32)).astype(BF16)
    return jnp.dot(hi, m, preferred_element_type=F32) + jnp.dot(lo, m, preferred_element_type=F32)


def _block_sums(x, m):
    return jnp.dot(x.astype(BF16), m, preferred_element_type=F32)


def _rowsum0(x):
    return jnp.sum(x, axis=0, keepdims=True)


def _pick(n, prefs):
    for p in prefs:
        if n % p == 0:
            return p
    return n


def mm_nn(a, w, *, name, res=None, out_dtype=BF16):
    t, k = a.shape
    n = w.shape[1]
    tm = _pick(t, (512, 256))
    tn = _pick(n, (1024, 512, 256))

    def body(*refs):
        if res is None:
            a_ref, w_ref, o_ref = refs
        else:
            a_ref, w_ref, r_ref, o_ref = refs
        acc = jnp.dot(a_ref[...], w_ref[...], preferred_element_type=F32)
        if res is not None:
            acc = acc + r_ref[...]
        o_ref[...] = acc.astype(o_ref.dtype)

    in_specs = [pl.BlockSpec((tm, k), lambda i, j: (i, 0)), pl.BlockSpec((k, tn), lambda i, j: (0, j))]
    args = [a, w]
    if res is not None:
        in_specs.append(pl.BlockSpec((tm, tn), lambda i, j: (i, j)))
        args.append(res)
    return _pc(body, name=name, grid=(t // tm, n // tn), in_specs=in_specs,
               out_specs=pl.BlockSpec((tm, tn), lambda i, j: (i, j)),
               out_shape=_sds((t, n), out_dtype), compiler_params=_cp("parallel", "parallel"))(*args)


def mm_norm_nn(x, g, w, *, name, after=()):
    t, k = x.shape
    n = w.shape[1]
    tm = _pick(t, (512, 256))
    tn = _pick(n, (1664, 1408, 512))

    def body(x_ref, g_ref, w_ref, *rest):
        z_ref, h_ref = rest[len(after):]
        xv = x_ref[...]
        r = lax.rsqrt(jnp.mean(xv * xv, axis=-1, keepdims=True) + EPS)
        h = (xv * r * g_ref[...]).astype(BF16)
        h_ref[...] = h
        for c in range(n // tn):
            cols = slice(c * tn, (c + 1) * tn)
            z_ref[:, cols] = jnp.dot(h, w_ref[:, cols], preferred_element_type=F32).astype(z_ref.dtype)

    return _pc(body, name=name, grid=(t // tm,),
               in_specs=[pl.BlockSpec((tm, k), lambda i: (i, 0)), pl.BlockSpec((1, k), lambda i: (0, 0)),
                         pl.BlockSpec((k, n), lambda i: (0, 0), pipeline_mode=pl.Buffered(1))]
               + [pl.BlockSpec(a.shape, lambda i: (0, 0)) for a in after],
               out_specs=[pl.BlockSpec((tm, n), lambda i: (i, 0)), pl.BlockSpec((tm, k), lambda i: (i, 0))],
               out_shape=[_sds((t, n), BF16), _sds((t, k), BF16)],
               compiler_params=_cp("parallel"))(x, g, w, *after)


def mm_nt(dy, w, *, name, out_dtype, after=()):
    t, n = dy.shape
    k = w.shape[0]
    tm = _pick(t, (512, 256))

    def body(dy_ref, w_ref, *rest):
        o_ref = rest[len(after)]
        o_ref[...] = lax.dot_general(dy_ref[...].astype(BF16), w_ref[...], NT_DIMS,
                                     preferred_element_type=F32).astype(o_ref.dtype)

    return _pc(body, name=name, grid=(t // tm,),
               in_specs=[pl.BlockSpec((tm, n), lambda i: (i, 0)),
                         pl.BlockSpec((k, n), lambda i: (0, 0), pipeline_mode=pl.Buffered(1))]
               + [pl.BlockSpec(tok.shape, lambda i: (0, 0)) for tok in after],
               out_specs=pl.BlockSpec((tm, k), lambda i: (i, 0)),
               out_shape=_sds((t, k), out_dtype), compiler_params=_cp("parallel"))(dy, w, *after)


def mm_tn(a, dy, *, name, out_dtype=BF16):
    t, k = a.shape
    n = dy.shape[1]
    tk = _pick(k, (1024, 1408, 512))
    tn = _pick(n, (512,) if dy.dtype == F32 else (1664, 1408, 1024, 512))

    def body(a_ref, dy_ref, o_ref):
        o_ref[...] = lax.dot_general(a_ref[...], dy_ref[...].astype(BF16), TN_DIMS,
                                     preferred_element_type=F32).astype(o_ref.dtype)

    return _pc(body, name=name, grid=(k // tk, n // tn),
               in_specs=[pl.BlockSpec((t, tk), lambda i, j: (0, i)), pl.BlockSpec((t, tn), lambda i, j: (0, j))],
               out_specs=pl.BlockSpec((tk, tn), lambda i, j: (i, j)),
               out_shape=_sds((k, n), out_dtype), compiler_params=_cp("parallel", "parallel"))(a, dy)


def mm_branches(kind, xs, ys, *, name):
    nb = len(xs)
    t = xs[0].shape[0]
    if kind == "tn":
        k, n = xs[0].shape[1], ys[0].shape[1]
        tn = _pick(n, (512,))
        grid = (n // tn,)
        x_spec = pl.BlockSpec((t, k), lambda j: (0, 0), pipeline_mode=pl.Buffered(1))
        y_spec = pl.BlockSpec((t, tn), lambda j: (0, j))
        o_spec, o_shape = pl.BlockSpec((k, tn), lambda j: (0, j)), (k, n)
        dims = TN_DIMS
    else:
        tm = _pick(t, (512, 256))
        grid = (t // tm,)
        x_spec = pl.BlockSpec((tm, xs[0].shape[1]), lambda i: (i, 0))
        y_spec = pl.BlockSpec(ys[0].shape, lambda i: (0, 0), pipeline_mode=pl.Buffered(1))
        width = ys[0].shape[1] if kind == "nn" else ys[0].shape[0]
        o_spec, o_shape = pl.BlockSpec((tm, width), lambda i: (i, 0)), (t, width)
        dims = (((1,), (0,)), ((), ())) if kind == "nn" else NT_DIMS

    def body(*refs):
        for q in range(nb):
            refs[2 * nb + q][...] = lax.dot_general(refs[q][...], refs[nb + q][...], dims,
                                                    preferred_element_type=F32).astype(BF16)

    return _pc(body, name=name, grid=grid, in_specs=[x_spec] * nb + [y_spec] * nb, out_specs=[o_spec] * nb,
               out_shape=[_sds(o_shape, BF16)] * nb, compiler_params=_cp("parallel"))(*xs, *ys)


def rms_bwd(dh, x, g, dres, *, name):
    t, d = x.shape
    tm = _pick(t, (256,))

    def body(dh_ref, x_ref, g_ref, dres_ref, dx_ref, dg_ref):
        xv = x_ref[...]
        r = lax.rsqrt(jnp.mean(xv * xv, axis=-1, keepdims=True) + EPS)
        xh = xv * r
        dy = dh_ref[...].astype(F32)
        dxh = dy * g_ref[...]
        dx_ref[...] = dres_ref[...] + r * (dxh - xh * jnp.mean(dxh * xh, axis=-1, keepdims=True))

        @pl.when(pl.program_id(0) == 0)
        def _():
            dg_ref[...] = jnp.zeros_like(dg_ref)

        dg_ref[...] += _rowsum0(dy * xh)

    row = pl.BlockSpec((tm, d), lambda i: (i, 0))
    vec = pl.BlockSpec((1, d), lambda i: (0, 0))
    return _pc(body, name=name, grid=(t // tm,), in_specs=[row, row, vec, row], out_specs=[row, vec],
               out_shape=[_sds((t, d), F32), _sds((1, d), F32)], compiler_params=_cp("arbitrary"))(dh, x, g, dres)


def loss_head(y, target, *, name):
    t, d = y.shape
    tm = _pick(t, (256,))

    def body(y_ref, t_ref, loss_ref, dy_ref):
        e = y_ref[...] - t_ref[...]
        dy_ref[...] = e * (1.0 / d)

        @pl.when(pl.program_id(0) == 0)
        def _():
            loss_ref[...] = jnp.zeros_like(loss_ref)

        loss_ref[...] += _rowsum0(jnp.sum(e * e, axis=1, keepdims=True)) * (0.5 / d)

    row = pl.BlockSpec((tm, d), lambda i: (i, 0))
    return _pc(body, name=name, grid=(t // tm,), in_specs=[row, row],
               out_specs=[pl.BlockSpec((1, 1), lambda i: (0, 0)), row],
               out_shape=[_sds((1, 1), F32), _sds((t, d), F32)], compiler_params=_cp("arbitrary"))(y, target)


def _sg_masks():
    lane = lax.broadcasted_iota(jnp.int32, (CHUNK, CHUNK), 1)
    row = lax.broadcasted_iota(jnp.int32, (CHUNK, CHUNK), 0)
    return lane < 64, lane <= row, row <= lane


def _sg_gate(vn_chunk, w_ref, bias_ref, p, first_group, tril):
    wa = jnp.where(tril, w_ref[2 * p], 0.0).astype(BF16)
    wb = jnp.where(tril, w_ref[2 * p + 1], 0.0).astype(BF16)
    oa = jnp.dot(wa, vn_chunk, preferred_element_type=F32)
    ob = jnp.dot(wb, vn_chunk, preferred_element_type=F32)
    return jnp.where(first_group, oa, ob) + bias_ref[:, p * 128:(p + 1) * 128]


def mixa_fwd(z, ln_g, ln_b, sg_w, sg_bias, *, name):
    t = z.shape[0]
    tm = _pick(t, (256,))

    def body(z_ref, g_ref, b_ref, w_ref, bias_ref, o_ref):
        first_group, tril, _ = _sg_masks()
        zv = z_ref[...].astype(F32)
        u = _gelu(zv[:, :SG_WIDTH])
        v = _gelu(zv[:, SG_WIDTH:])
        vh, _ = _ln_stats(v)
        vn = (vh * g_ref[...] + b_ref[...]).astype(BF16)
        for c in range(tm // CHUNK):
            rows = slice(c * CHUNK, (c + 1) * CHUNK)
            for p in range(4):
                cols = slice(p * 128, (p + 1) * 128)
                o = _sg_gate(vn[rows, cols], w_ref, bias_ref, p, first_group, tril)
                o_ref[rows, cols] = (u[rows, cols] * o).astype(o_ref.dtype)

    vec = pl.BlockSpec((1, SG_WIDTH), lambda i: (0, 0))
    return _pc(body, name=name, grid=(t // tm,),
               in_specs=[pl.BlockSpec((tm, 2 * SG_WIDTH), lambda i: (i, 0)), vec, vec,
                         pl.BlockSpec((8, CHUNK, CHUNK), lambda i: (0, 0, 0)),
                         pl.BlockSpec((CHUNK, SG_WIDTH), lambda i: (0, 0))],
               out_specs=pl.BlockSpec((tm, SG_WIDTH), lambda i: (i, 0)),
               out_shape=_sds((t, SG_WIDTH), BF16), compiler_params=_cp("parallel"))(z, ln_g, ln_b, sg_w, sg_bias)


def mixa_bwd(z, dga, ln_g, ln_b, sg_w, sg_wt, sg_bias, *, name):
    t = z.shape[0]
    tm = _pick(t, (256,))
    nsteps = t // tm

    def body(z_ref, dga_ref, g_ref, b_ref, w_ref, wt_ref, bias_ref, dz_ref, dw_ref, dsgb_ref, dg_ref, db_ref, dvn_s,
             dbias_ref):
        i = pl.program_id(0)
        first_group, tril, triu = _sg_masks()

        @pl.when(i == 0)
        def _():
            dw_ref[...] = jnp.zeros_like(dw_ref)
            dbias_ref[...] = jnp.zeros_like(dbias_ref)
            dg_ref[...] = jnp.zeros_like(dg_ref)
            db_ref[...] = jnp.zeros_like(db_ref)

        zv = z_ref[...].astype(F32)
        zu = zv[:, :SG_WIDTH]
        zg = zv[:, SG_WIDTH:]
        u = _gelu(zu)
        v = _gelu(zg)
        vh, r = _ln_stats(v)
        vn = (vh * g_ref[...] + b_ref[...]).astype(BF16)
        dga_v = dga_ref[...].astype(F32)
        d_o = dga_v * u
        for c in range(tm // CHUNK):
            rows = slice(c * CHUNK, (c + 1) * CHUNK)
            dbias_ref[...] += d_o[rows, :]
            for p in range(4):
                cols = slice(p * 128, (p + 1) * 128)
                vp = vn[rows, cols]
                o = _sg_gate(vp, w_ref, bias_ref, p, first_group, tril)
                dz_ref[rows, cols] = (dga_v[rows, cols] * o * _gelu_grad(zu[rows, cols])).astype(dz_ref.dtype)
                dop = d_o[rows, cols]
                dop_a = jnp.where(first_group, dop, 0.0).astype(BF16)
                dop_b = jnp.where(first_group, 0.0, dop).astype(BF16)
                dw_ref[2 * p] += lax.dot_general(dop_a, vp, NT_DIMS, preferred_element_type=F32)
                dw_ref[2 * p + 1] += lax.dot_general(dop_b, vp, NT_DIMS, preferred_element_type=F32)
                wta = jnp.where(triu, wt_ref[2 * p], 0.0).astype(BF16)
                wtb = jnp.where(triu, wt_ref[2 * p + 1], 0.0).astype(BF16)
                dop16 = dop.astype(BF16)
                dvn_s[rows, cols] = jnp.where(first_group, jnp.dot(wta, dop16, preferred_element_type=F32),
                                              jnp.dot(wtb, dop16, preferred_element_type=F32))
        dvn = dvn_s[...]
        dg_ref[...] += _rowsum0(dvn * vh)
        db_ref[...] += _rowsum0(dvn)
        dv = _ln_bwd(dvn, vh, r, g_ref[...])
        dz_ref[:, SG_WIDTH:] = (dv * _gelu_grad(zg)).astype(dz_ref.dtype)

        @pl.when(i == nsteps - 1)
        def _():
            for gi in range(8):
                dw_ref[gi] = jnp.where(tril, dw_ref[gi], 0.0)
            r_id = lax.broadcasted_iota(jnp.int32, (SG_WIDTH, 128), 0) // SB_HEAD_DIM
            c_id = lax.broadcasted_iota(jnp.int32, (SG_WIDTH, 128), 1)
            dsgb_ref[...] = _split_dot(dbias_ref[...], (r_id == c_id).astype(BF16))

    vec = pl.BlockSpec((1, SG_WIDTH), lambda i: (0, 0))
    wspec = pl.BlockSpec((8, CHUNK, CHUNK), lambda i: (0, 0, 0))
    bspec = pl.BlockSpec((CHUNK, SG_WIDTH), lambda i: (0, 0))
    sgb = pl.BlockSpec((CHUNK, 128), lambda i: (0, 0))
    return _pc(body, name=name, grid=(nsteps,),
               in_specs=[pl.BlockSpec((tm, 2 * SG_WIDTH), lambda i: (i, 0)), pl.BlockSpec((tm, SG_WIDTH), lambda i: (i, 0)),
                         vec, vec, wspec, wspec, bspec],
               out_specs=[pl.BlockSpec((tm, 2 * SG_WIDTH), lambda i: (i, 0)), wspec, sgb, vec, vec],
               out_shape=[_sds((t, 2 * SG_WIDTH), BF16), _sds((8, CHUNK, CHUNK), F32), _sds((CHUNK, 128), F32),
                          _sds((1, SG_WIDTH), F32), _sds((1, SG_WIDTH), F32)],
               scratch_shapes=[pltpu.VMEM((tm, SG_WIDTH), F32), pltpu.VMEM((CHUNK, SG_WIDTH), F32)],
               compiler_params=_cp("arbitrary"))(z, dga, ln_g, ln_b, sg_w, sg_wt, sg_bias)


def _glu(zv):
    return zv[:, :CV_WIDTH] * _sigmoid(zv[:, CV_WIDTH:])


_SUB = 8


def _cv_phases(x_s, tm):
    rows = CV_HALO + tm - _SUB
    for r in range(1, _SUB):
        x_s[r, 0:rows, :] = x_s[0, pl.ds(r, rows), :]


def _cv_tap(x_s, o, tm):
    return x_s[o % _SUB, pl.ds(o - o % _SUB, tm), :]


def _cv_fill(zm_ref, zh_ref, x_s, i, tm):
    x_s[0, 0:CV_HALO, :] = jnp.where(i > 0, _glu(zh_ref[...].astype(F32)), 0.0)
    x_s[0, CV_HALO:CV_HALO + tm, :] = _glu(zm_ref[...].astype(F32))
    _cv_phases(x_s, tm)


def _cv_conv(x_s, w_ref, cb_ref, tm):
    acc = jnp.zeros((tm, CV_WIDTH), F32) + cb_ref[...]
    for k in range(CV_KERNEL):
        acc = acc + w_ref[k:k + 1, :] * _cv_tap(x_s, CV_HALO - (CV_KERNEL - 1) + k, tm)
    return acc


def _cv_specs(tm):
    zm = pl.BlockSpec((tm, 2 * CV_WIDTH), lambda i: (i, 1))
    zh = pl.BlockSpec((CV_HALO, 2 * CV_WIDTH), lambda i: (jnp.maximum(i * (tm // CV_HALO) - 1, 0), 1))
    w = pl.BlockSpec((CV_KERNEL, CV_WIDTH), lambda i: (0, 0))
    vec = pl.BlockSpec((1, CV_WIDTH), lambda i: (0, 0))
    return zm, zh, w, vec


def mixb_fwd(z, cv_w, cv_b, ln_g, ln_b, *, name):
    t = z.shape[0]
    tm = _pick(t, (256,))

    def body(zm_ref, zh_ref, w_ref, cb_ref, g_ref, b_ref, o_ref, x_s):
        _cv_fill(zm_ref, zh_ref, x_s, pl.program_id(0), tm)
        c1 = _cv_conv(x_s, w_ref, cb_ref, tm)
        ch, _ = _ln_stats(c1)
        o_ref[...] = _silu(ch * g_ref[...] + b_ref[...]).astype(o_ref.dtype)

    zm, zh, w, vec = _cv_specs(tm)
    return _pc(body, name=name, grid=(t // tm,), in_specs=[zm, zh, w, vec, vec, vec],
               out_specs=pl.BlockSpec((tm, CV_WIDTH), lambda i: (i, 0)), out_shape=_sds((t, CV_WIDTH), BF16),
               scratch_shapes=[pltpu.VMEM((_SUB, CV_HALO + tm, CV_WIDTH), F32)],
               compiler_params=_cp("parallel"))(z, z, cv_w, cv_b, ln_g, ln_b)


def mixb_bwd1(z, dc3, cv_w, cv_b, ln_g, ln_b, *, name):
    t = z.shape[0]
    tm = _pick(t, (256,))

    def body(zm_ref, zh_ref, dc3_ref, w_ref, cb_ref, g_ref, b_ref, dc1_ref, dw_ref, dcb_ref, dg_ref, db_ref, x_s):
        i = pl.program_id(0)

        @pl.when(i == 0)
        def _():
            dw_ref[...] = jnp.zeros_like(dw_ref)
            dcb_ref[...] = jnp.zeros_like(dcb_ref)
            dg_ref[...] = jnp.zeros_like(dg_ref)
            db_ref[...] = jnp.zeros_like(db_ref)

        _cv_fill(zm_ref, zh_ref, x_s, i, tm)
        c1 = _cv_conv(x_s, w_ref, cb_ref, tm)
        ch, r = _ln_stats(c1)
        c2 = ch * g_ref[...] + b_ref[...]
        dc2 = dc3_ref[...].astype(F32) * _silu_grad(c2)
        dg_ref[...] += _rowsum0(dc2 * ch)
        db_ref[...] += _rowsum0(dc2)
        dc1 = _ln_bwd(dc2, ch, r, g_ref[...])
        dc1_ref[...] = dc1
        dcb_ref[...] += _rowsum0(dc1)
        for k in range(CV_KERNEL):
            dw_ref[k:k + 1, :] += _rowsum0(dc1 * _cv_tap(x_s, CV_HALO - (CV_KERNEL - 1) + k, tm))

    zm, zh, w, vec = _cv_specs(tm)
    row = pl.BlockSpec((tm, CV_WIDTH), lambda i: (i, 0))
    return _pc(body, name=name, grid=(t // tm,), in_specs=[zm, zh, row, w, vec, vec, vec],
               out_specs=[row, w, vec, vec, vec],
               out_shape=[_sds((t, CV_WIDTH), F32), _sds((CV_KERNEL, CV_WIDTH), F32), _sds((1, CV_WIDTH), F32),
                          _sds((1, CV_WIDTH), F32), _sds((1, CV_WIDTH), F32)],
               scratch_shapes=[pltpu.VMEM((_SUB, CV_HALO + tm, CV_WIDTH), F32)],
               compiler_params=_cp("arbitrary"))(z, z, dc3, cv_w, cv_b, ln_g, ln_b)


def mixb_bwd2(z, dc1, cv_w, *, name):
    t = z.shape[0]
    tm = _pick(t, (256,))
    nsteps = t // tm

    def body(zm_ref, dm_ref, dh_ref, w_ref, dz_ref, y_s):
        i = pl.program_id(0)
        y_s[0, 0:tm, :] = dm_ref[...]
        y_s[0, tm:tm + CV_HALO, :] = jnp.where(i < nsteps - 1, dh_ref[...], 0.0)
        _cv_phases(y_s, tm)
        dc0 = jnp.zeros((tm, CV_WIDTH), F32)
        for k in range(CV_KERNEL):
            dc0 = dc0 + w_ref[k:k + 1, :] * _cv_tap(y_s, CV_KERNEL - 1 - k, tm)
        zv = zm_ref[...].astype(F32)
        p = zv[:, :CV_WIDTH]
        s = _sigmoid(zv[:, CV_WIDTH:])
        dz_ref[:, :CV_WIDTH] = (dc0 * s).astype(dz_ref.dtype)
        dz_ref[:, CV_WIDTH:] = (dc0 * p * s * (1.0 - s)).astype(dz_ref.dtype)

    last = t // CV_HALO - 1
    return _pc(body, name=name, grid=(nsteps,),
               in_specs=[pl.BlockSpec((tm, 2 * CV_WIDTH), lambda i: (i, 1)),
                         pl.BlockSpec((tm, CV_WIDTH), lambda i: (i, 0)),
                         pl.BlockSpec((CV_HALO, CV_WIDTH), lambda i: (jnp.minimum((i + 1) * (tm // CV_HALO), last), 0)),
                         pl.BlockSpec((CV_KERNEL, CV_WIDTH), lambda i: (0, 0))],
               out_specs=pl.BlockSpec((tm, 2 * CV_WIDTH), lambda i: (i, 0)),
               out_shape=_sds((t, 2 * CV_WIDTH), BF16),
               scratch_shapes=[pltpu.VMEM((_SUB, tm + CV_HALO, CV_WIDTH), F32)],
               compiler_params=_cp("parallel"))(z, dc1, dc1, cv_w)


def _group_ones():
    r = lax.broadcasted_iota(jnp.int32, (SB_WIDTH, SB_WIDTH), 0) // SB_HEAD_DIM
    c = lax.broadcasted_iota(jnp.int32, (SB_WIDTH, SB_WIDTH), 1) // SB_HEAD_DIM
    return (r == c).astype(BF16)


def attn_prep(z, gq, gk, *, name):
    t = z.shape[0]
    tm = _pick(t, (256,))
    scale = 1.0 / math.sqrt(SB_HEAD_DIM)

    def body(q_ref, k_ref, gq_ref, gk_ref, qo_ref, ko_ref):
        ones = _group_ones()
        for src, g_ref, dst, mul in ((q_ref, gq_ref, qo_ref, scale), (k_ref, gk_ref, ko_ref, 1.0)):
            v = src[...].astype(F32)
            r = lax.rsqrt(_split_dot(v * v, ones) * (1.0 / SB_HEAD_DIM) + EPS)
            dst[...] = ((v * r * g_ref[...]).astype(BF16).astype(F32) * mul).astype(dst.dtype)

    vec = pl.BlockSpec((1, SB_WIDTH), lambda i: (0, 0))
    row = pl.BlockSpec((tm, SB_WIDTH), lambda i: (i, 0))
    return _pc(body, name=name, grid=(t // tm,),
               in_specs=[pl.BlockSpec((tm, SB_WIDTH), lambda i: (i, COL_C // SB_WIDTH)),
                         pl.BlockSpec((tm, SB_WIDTH), lambda i: (i, COL_C // SB_WIDTH + 1)), vec, vec],
               out_specs=[row, row], out_shape=[_sds((t, SB_WIDTH), BF16), _sds((t, SB_WIDTH), BF16)],
               compiler_params=_cp("parallel"))(z, z, gq, gk)


_KB = Q_BLOCK
_PAIR = 2 * _KB


def _attn_tq(t):
    return _pick(t, (512, 256, 128))


def _attn_consts(tq):
    first_head = lax.broadcasted_iota(jnp.int32, (_KB, 128), 1) < SB_HEAD_DIM
    r2 = lax.broadcasted_iota(jnp.int32, (_PAIR, _PAIR), 0)
    c2 = lax.broadcasted_iota(jnp.int32, (_PAIR, _PAIR), 1)
    same = (r2 // _KB) == (c2 // _KB)
    m_suffix = (same & (r2 > c2)).astype(BF16)
    m_prefix = (same & (r2 < c2)).astype(BF16)
    row = lax.broadcasted_iota(jnp.int32, (tq, _PAIR), 0)
    col = lax.broadcasted_iota(jnp.int32, (tq, _PAIR), 1)
    return first_head, m_suffix, m_prefix, row, col & (_KB - 1), col < _KB


def _sb_logits(z, causal):
    sp = jnp.log(1.0 + jnp.exp(-jnp.abs(z)))
    g = jnp.minimum(z, 0.0) - sp
    l1m = g - z
    if causal is not None:
        l1m = jnp.where(causal, l1m, 0.0)
    return g, l1m


def _stack_heads(first_head, v):
    zero = jnp.zeros_like(v)
    return jnp.concatenate([jnp.where(first_head, v, zero), jnp.where(first_head, zero, v)], axis=0)


def _add_rows(x, upd, r0):
    return x + upd if r0 == 0 else jnp.concatenate([x[:r0], x[r0:] + upd], axis=0)


def _pair_sums(x):
    return jnp.sum(x[:, :_KB], axis=1, keepdims=True), jnp.sum(x[:, _KB:], axis=1, keepdims=True)


def _attn_specs(t, tq):
    qspec = pl.BlockSpec((tq, 128), lambda h, i: (i, h))
    kspec = pl.BlockSpec((t, 128), lambda h, i: (0, h))
    vspec = pl.BlockSpec((t, 128), lambda h, i: (0, (COL_C + 2 * SB_WIDTH) // 128 + h))
    return qspec, kspec, vspec


def attn_fwd(q, k, z, *, name):
    t = q.shape[0]
    tq = _attn_tq(t)
    nd = tq // _KB
    assert nd % 2 == 0, "the key-block loop takes two blocks per pass"

    def body(q_ref, k_ref, v_ref, o_ref):
        qt = pl.program_id(1)
        first_head, m_suffix, _, row, key, is_first = _attn_consts(tq)
        qv = q_ref[...]

        def step(kb, state, causal, r0=0):
            acc, ca, cb = state
            off = pl.multiple_of(kb * _KB, _KB)
            kcat = _stack_heads(first_head, k_ref[pl.ds(off, _KB), :])
            vcat = _stack_heads(first_head, v_ref[pl.ds(off, _KB), :])
            zz = lax.dot_general(qv[r0:], kcat, NT_DIMS, preferred_element_type=F32)
            g, l1m = _sb_logits(zz, None if causal is None else causal[r0:])
            a = jnp.exp(g + _block_sums(l1m, m_suffix) + jnp.where(is_first[r0:], ca[r0:], cb[r0:]))
            if causal is not None:
                a = jnp.where(causal[r0:], a, 0.0)
            sa, sb = _pair_sums(l1m)
            pv = jnp.dot(a.astype(BF16), vcat, preferred_element_type=F32)
            return _add_rows(acc, pv, r0), _add_rows(ca, sa, r0), _add_rows(cb, sb, r0)

        c0 = jnp.zeros((tq, 1), F32)
        state = (jnp.zeros((tq, 128), F32), c0, c0)
        for d in reversed(range(nd)):
            state = step(qt * nd + d, state, key + d * _KB < row, d * _KB)
        def one_pass(s, st):
            for u in range(nd):
                st = step((qt - s) * nd - 1 - u, st, None)
            return st

        state = lax.fori_loop(0, qt, one_pass, state)
        o_ref[...] = state[0].astype(o_ref.dtype)

    qspec, kspec, vspec = _attn_specs(t, tq)
    return _pc(body, name=name, grid=(SB_WIDTH // 128, t // tq), in_specs=[qspec, kspec, vspec], out_specs=qspec,
               out_shape=_sds((t, SB_WIDTH), BF16), compiler_params=_cp("parallel", "arbitrary"))(q, k, z)


def attn_bwd(q, k, z, do, *, name):
    t = q.shape[0]
    tq = _attn_tq(t)
    nd = tq // _KB
    assert nd % 2 == 0, "the key-block loop takes two blocks per pass"
    nk = t // _KB

    def body(q_ref, k_ref, v_ref, do_ref, dq_ref, dk_ref, dv_ref, e_s, sg_s):
        qt = pl.program_id(1)
        first_head, m_suffix, m_prefix, row, key, is_first = _attn_consts(tq)

        @pl.when(qt == 0)
        def _():
            dk_ref[...] = jnp.zeros_like(dk_ref)
            dv_ref[...] = jnp.zeros_like(dv_ref)

        qv = q_ref[...]
        dov = do_ref[...]

        def halves(x):
            return jnp.where(first_head, x[:_KB], x[_KB:])

        def sweep1(kb, state, causal, r0=0):
            ca, cb = state
            off = pl.multiple_of(kb * _KB, _KB)
            kcat = _stack_heads(first_head, k_ref[pl.ds(off, _KB), :])
            vcat = _stack_heads(first_head, v_ref[pl.ds(off, _KB), :])
            zz = lax.dot_general(qv[r0:], kcat, NT_DIMS, preferred_element_type=F32)
            g, l1m = _sb_logits(zz, None if causal is None else causal[r0:])
            a = jnp.exp(g + _block_sums(l1m, m_suffix) + jnp.where(is_first[r0:], ca[r0:], cb[r0:]))
            if causal is not None:
                a = jnp.where(causal[r0:], a, 0.0)
            da = lax.dot_general(dov[r0:], vcat, NT_DIMS, preferred_element_type=F32)
            e_s[kb, r0:, :] = a * da
            sg_s[kb, r0:, :] = jnp.exp(g).astype(BF16)
            dv_ref[pl.ds(off, _KB), :] += halves(lax.dot_general(a.astype(BF16), dov[r0:], TN_DIMS,
                                                                 preferred_element_type=F32))
            sa, sb = _pair_sums(l1m)
            return _add_rows(ca, sa, r0), _add_rows(cb, sb, r0)

        def sweep2(kb, state, causal, r0=0):
            dq, pa, pb = state
            off = pl.multiple_of(kb * _KB, _KB)
            kcat = _stack_heads(first_head, k_ref[pl.ds(off, _KB), :])
            e = e_s[kb, r0:, :]
            s = sg_s[kb, r0:, :].astype(F32)
            dz = e * (1.0 - s) - (jnp.where(is_first[r0:], pa[r0:], pb[r0:]) + _block_sums(e, m_prefix)) * s
            if causal is not None:
                dz = jnp.where(causal[r0:], dz, 0.0)
            dz = dz.astype(BF16)
            dk_ref[pl.ds(off, _KB), :] += halves(lax.dot_general(dz, qv[r0:], TN_DIMS, preferred_element_type=F32))
            sa, sb = _pair_sums(e)
            return (_add_rows(dq, jnp.dot(dz, kcat, preferred_element_type=F32), r0), _add_rows(pa, sa, r0),
                    _add_rows(pb, sb, r0))

        c0 = jnp.zeros((tq, 1), F32)
        st1 = (c0, c0)
        for d in reversed(range(nd)):
            st1 = sweep1(qt * nd + d, st1, key + d * _KB < row, d * _KB)
        def pass1(s, st):
            for u in range(nd):
                st = sweep1((qt - s) * nd - 1 - u, st, None)
            return st

        def pass2(s, st):
            for u in range(nd):
                st = sweep2(s * nd + u, st, None)
            return st

        lax.fori_loop(0, qt, pass1, st1)
        st2 = lax.fori_loop(0, qt, pass2, (jnp.zeros((tq, 128), F32), c0, c0))
        for d in range(nd):
            st2 = sweep2(qt * nd + d, st2, key + d * _KB < row, d * _KB)
        dq_ref[...] = st2[0]

    qspec, kspec, vspec = _attn_specs(t, tq)
    acc = pl.BlockSpec((t, 128), lambda h, i: (0, h))
    return _pc(body, name=name, grid=(SB_WIDTH // 128, t // tq), in_specs=[qspec, kspec, vspec, qspec],
               out_specs=[qspec, acc, acc],
               out_shape=[_sds((t, SB_WIDTH), F32), _sds((t, SB_WIDTH), F32), _sds((t, SB_WIDTH), F32)],
               scratch_shapes=[pltpu.VMEM((nk, tq, _PAIR), F32), pltpu.VMEM((nk, tq, _PAIR), BF16)],
               compiler_params=_cp("parallel", "arbitrary"))(q, k, z, do)


def attn_post_bwd(z, dq, dk, dv, gq, gk, *, name):
    t = z.shape[0]
    tm = _pick(t, (256,))
    scale = 1.0 / math.sqrt(SB_HEAD_DIM)

    def body(q_ref, k_ref, dq_ref, dk_ref, dv_ref, gq_ref, gk_ref, dz_ref, dgq_ref, dgk_ref):
        ones = _group_ones()

        @pl.when(pl.program_id(0) == 0)
        def _():
            dgq_ref[...] = jnp.zeros_like(dgq_ref)
            dgk_ref[...] = jnp.zeros_like(dgk_ref)

        for idx, (src, d_ref, g_ref, dg_ref, mul) in enumerate(
                ((q_ref, dq_ref, gq_ref, dgq_ref, scale), (k_ref, dk_ref, gk_ref, dgk_ref, 1.0))):
            v = src[...].astype(F32)
            r = lax.rsqrt(_split_dot(v * v, ones) * (1.0 / SB_HEAD_DIM) + EPS)
            vh = v * r
            dn = d_ref[...] * mul
            dxh = dn * g_ref[...]
            m = _split_dot(dxh * vh, ones) * (1.0 / SB_HEAD_DIM)
            dz_ref[:, idx * SB_WIDTH:(idx + 1) * SB_WIDTH] = (r * (dxh - vh * m)).astype(dz_ref.dtype)
            s = _rowsum0(dn * vh)
            f = jnp.broadcast_to(s[:, 0:128] + s[:, 128:256] + s[:, 256:384] + s[:, 384:512], dg_ref.shape)
            dg_ref[...] += f + pltpu.roll(f, 64, 1)
        dz_ref[:, 2 * SB_WIDTH:] = dv_ref[...].astype(dz_ref.dtype)

    vec = pl.BlockSpec((1, SB_WIDTH), lambda i: (0, 0))
    row = pl.BlockSpec((tm, SB_WIDTH), lambda i: (i, 0))
    fold = pl.BlockSpec((8, 128), lambda i: (0, 0))
    return _pc(body, name=name, grid=(t // tm,),
               in_specs=[pl.BlockSpec((tm, SB_WIDTH), lambda i: (i, COL_C // SB_WIDTH)),
                         pl.BlockSpec((tm, SB_WIDTH), lambda i: (i, COL_C // SB_WIDTH + 1)), row, row, row, vec, vec],
               out_specs=[pl.BlockSpec((tm, 3 * SB_WIDTH), lambda i: (i, 0)), fold, fold],
               out_shape=[_sds((t, 3 * SB_WIDTH), BF16), _sds((8, 128), F32), _sds((8, 128), F32)],
               compiler_params=_cp("arbitrary"))(z, z, dq, dk, dv, gq, gk)


_GW = 512


def merge_fwd(z, ya, yb, yc, b_gate, *, name):
    t = z.shape[0]
    tm = _pick(t, (512, 256))

    def body(za_ref, zb_ref, zc_ref, ya_ref, yb_ref, yc_ref, bg_ref, o_ref):
        acc = jnp.zeros((tm, _GW), F32)
        for b, (zr, yr) in enumerate(((za_ref, ya_ref), (zb_ref, yb_ref), (zc_ref, yc_ref))):
            acc = acc + _sigmoid(zr[...].astype(F32) + bg_ref[b:b + 1, :]) * yr[...].astype(F32)
        o_ref[...] = acc.astype(o_ref.dtype)

    def zspec(b):
        return pl.BlockSpec((tm, _GW), lambda i, j: (i, COL_G // _GW + 2 * b + j))

    yspec = pl.BlockSpec((tm, _GW), lambda i, j: (i, j))
    return _pc(body, name=name, grid=(t // tm, D_MODEL // _GW),
               in_specs=[zspec(0), zspec(1), zspec(2), yspec, yspec, yspec, pl.BlockSpec((3, _GW), lambda i, j: (0, j))],
               out_specs=yspec, out_shape=_sds((t, D_MODEL), BF16),
               compiler_params=_cp("parallel", "parallel"))(z, z, z, ya, yb, yc, b_gate)


def merge_bwd(z, ya, yb, yc, b_gate, dm, *, name):
    t = z.shape[0]
    tm = _pick(t, (512, 256))

    def body(za_ref, zb_ref, zc_ref, ya_ref, yb_ref, yc_ref, bg_ref, dm_ref,
             dya_ref, dyb_ref, dyc_ref, dza_ref, dzb_ref, dzc_ref, dbg_ref):
        @pl.when(pl.program_id(1) == 0)
        def _():
            dbg_ref[...] = jnp.zeros_like(dbg_ref)

        dmv = dm_ref[...].astype(F32)
        for b, (zr, yr, dyr, dzr) in enumerate(((za_ref, ya_ref, dya_ref, dza_ref), (zb_ref, yb_ref, dyb_ref, dzb_ref),
                                                (zc_ref, yc_ref, dyc_ref, dzc_ref))):
            s = _sigmoid(zr[...].astype(F32) + bg_ref[b:b + 1, :])
            dyr[...] = (dmv * s).astype(dyr.dtype)
            dg = dmv * yr[...].astype(F32) * s * (1.0 - s)
            dzr[...] = dg.astype(dzr.dtype)
            dbg_ref[b:b + 1, :] += _rowsum0(dg)

    def zspec(b):
        return pl.BlockSpec((tm, _GW), lambda j, i: (i, COL_G // _GW + 2 * b + j))

    yspec = pl.BlockSpec((tm, _GW), lambda j, i: (i, j))
    bspec = pl.BlockSpec((3, _GW), lambda j, i: (0, j))
    full = _sds((t, D_MODEL), BF16)
    return _pc(body, name=name, grid=(D_MODEL // _GW, t // tm),
               in_specs=[zspec(0), zspec(1), zspec(2), yspec, yspec, yspec, bspec, yspec],
               out_specs=[yspec] * 6 + [bspec], out_shape=[full] * 6 + [_sds((3, D_MODEL), F32)],
               compiler_params=_cp("parallel", "arbitrary"))(z, z, z, ya, yb, yc, b_gate, dm)


_FW = 1408
_FH = D_FF // _FW


def _row_shifts(window, tm, offsets):
    r = lax.broadcasted_iota(jnp.int32, (tm, window.shape[0]), 0)
    c = lax.broadcasted_iota(jnp.int32, (tm, window.shape[0]), 1)
    return [jnp.dot((c == r + o).astype(BF16), window, preferred_element_type=F32) for o in offsets]


def _ffn_taps(m_ref, h_ref, i, tm):
    main, halo = m_ref[...], h_ref[...]
    window = jnp.concatenate([jnp.where(i > 0, halo, jnp.zeros_like(halo)), main], axis=0)
    return _row_shifts(window, tm, [FFN_HALO - 2, FFN_HALO - 1]) + [main.astype(F32)]


def _ffn_taps_vpu(m_ref, h_ref, x_s, i, tm):
    x_s[0:FFN_HALO, :] = jnp.where(i > 0, h_ref[...].astype(F32), 0.0)
    x_s[FFN_HALO:FFN_HALO + tm, :] = m_ref[...].astype(F32)
    return [x_s[pl.ds(FFN_HALO - (FFN_KERNEL - 1) + k, tm), :] for k in range(FFN_KERNEL)]


def _ffn_conv(taps, w_ref, b_ref):
    acc = jnp.zeros_like(taps[0]) + b_ref[...]
    for k in range(FFN_KERNEL):
        acc = acc + w_ref[k:k + 1, :] * taps[k]
    return acc


def ffn_mid_fwd(up, cw, cb, *, name):
    t = up.shape[0]
    tm = _pick(t, (256,))

    def body(gm_ref, gh_ref, vm_ref, vh_ref, wg_ref, wv_ref, bg_ref, bv_ref, o_ref):
        i = pl.program_id(0)
        gate = _ffn_conv(_ffn_taps(gm_ref, gh_ref, i, tm), wg_ref, bg_ref)
        val = _ffn_conv(_ffn_taps(vm_ref, vh_ref, i, tm), wv_ref, bv_ref)
        o_ref[...] = (_silu(gate) * val).astype(o_ref.dtype)

    def main(off):
        return pl.BlockSpec((tm, _FW), lambda i, j: (i, j + off))

    def halo(off):
        return pl.BlockSpec((FFN_HALO, _FW), lambda i, j: (jnp.maximum(i * (tm // FFN_HALO) - 1, 0), j + off))

    def wspec(off):
        return pl.BlockSpec((FFN_KERNEL, _FW), lambda i, j: (0, j + off))

    def bspec(off):
        return pl.BlockSpec((1, _FW), lambda i, j: (0, j + off))

    return _pc(body, name=name, grid=(t // tm, _FH),
               in_specs=[main(0), halo(0), main(_FH), halo(_FH), wspec(0), wspec(_FH), bspec(0), bspec(_FH)],
               out_specs=pl.BlockSpec((tm, _FW), lambda i, j: (i, j)), out_shape=_sds((t, D_FF), BF16),
               compiler_params=_cp("parallel", "parallel"))(up, up, up, up, cw, cw, cb, cb)


def ffn_mid_bwd1(up, dact, cw, cb, *, name):
    t = up.shape[0]
    tm = _pick(t, (256,))

    def body(gm_ref, gh_ref, vm_ref, vh_ref, da_ref, wg_ref, wv_ref, bg_ref, bv_ref, d_ref, dw_ref, db_ref, xg_s, xv_s):
        j = pl.program_id(0)
        i = pl.program_id(1)

        @pl.when(i == 0)
        def _():
            dw_ref[...] = jnp.zeros_like(dw_ref)
            db_ref[...] = jnp.zeros_like(db_ref)

        gate_taps = _ffn_taps_vpu(gm_ref, gh_ref, xg_s, i, tm)
        val_taps = _ffn_taps_vpu(vm_ref, vh_ref, xv_s, i, tm)
        gate = _ffn_conv(gate_taps, wg_ref, bg_ref)
        da = da_ref[...].astype(F32)

        def finish(d, taps):
            d_ref[...] = d.astype(d_ref.dtype)
            db_ref[...] += _rowsum0(d)
            for k in range(FFN_KERNEL):
                dw_ref[k:k + 1, :] += _rowsum0(d * taps[k])

        @pl.when(j < _FH)
        def _():
            finish(da * _ffn_conv(val_taps, wv_ref, bv_ref) * _silu_grad(gate), gate_taps)

        @pl.when(j >= _FH)
        def _():
            finish(da * _silu(gate), val_taps)

    def main(off):
        return pl.BlockSpec((tm, _FW), lambda j, i: (i, j % _FH + off))

    def halo(off):
        return pl.BlockSpec((FFN_HALO, _FW), lambda j, i: (jnp.maximum(i * (tm // FFN_HALO) - 1, 0), j % _FH + off))

    def wspec(off):
        return pl.BlockSpec((FFN_KERNEL, _FW), lambda j, i: (0, j % _FH + off))

    def bspec(off):
        return pl.BlockSpec((1, _FW), lambda j, i: (0, j % _FH + off))

    return _pc(body, name=name, grid=(2 * _FH, t // tm),
               in_specs=[main(0), halo(0), main(_FH), halo(_FH), pl.BlockSpec((tm, _FW), lambda j, i: (i, j % _FH)),
                         wspec(0), wspec(_FH), bspec(0), bspec(_FH)],
               out_specs=[pl.BlockSpec((tm, _FW), lambda j, i: (i, j)), pl.BlockSpec((FFN_KERNEL, _FW), lambda j, i: (0, j)),
                          pl.BlockSpec((1, _FW), lambda j, i: (0, j))],
               out_shape=[_sds((t, 2 * D_FF), BF16), _sds((FFN_KERNEL, 2 * D_FF), F32), _sds((1, 2 * D_FF), F32)],
               scratch_shapes=[pltpu.VMEM((FFN_HALO + tm, _FW), F32), pltpu.VMEM((FFN_HALO + tm, _FW), F32)],
               compiler_params=_cp("parallel", "arbitrary"))(up, up, up, up, dact, cw, cw, cb, cb)


def ffn_mid_bwd2(dupc, cw, *, name):
    t = dupc.shape[0]
    tm = _pick(t, (256,))
    nsteps = t // tm
    last = t // FFN_HALO - 1

    def body(m_ref, h_ref, w_ref, o_ref):
        i = pl.program_id(0)
        main, halo = m_ref[...], h_ref[...]
        window = jnp.concatenate([main, jnp.where(i < nsteps - 1, halo, jnp.zeros_like(halo))], axis=0)
        taps = _row_shifts(window, tm, [2, 1]) + [main.astype(F32)]
        acc = jnp.zeros((tm, _FW), F32)
        for k in range(FFN_KERNEL):
            acc = acc + w_ref[k:k + 1, :] * taps[k]
        o_ref[...] = acc.astype(o_ref.dtype)

    return _pc(body, name=name, grid=(nsteps, 2 * _FH),
               in_specs=[pl.BlockSpec((tm, _FW), lambda i, j: (i, j)),
                         pl.BlockSpec((FFN_HALO, _FW), lambda i, j: (jnp.minimum((i + 1) * (tm // FFN_HALO), last), j)),
                         pl.BlockSpec((FFN_KERNEL, _FW), lambda i, j: (0, j))],
               out_specs=pl.BlockSpec((tm, _FW), lambda i, j: (i, j)), out_shape=_sds((t, 2 * D_FF), BF16),
               compiler_params=_cp("parallel", "parallel"))(dupc, dupc, cw)


def _vec(v):
    return v.reshape(1, -1)


def _layer_consts(p):
    return dict(
        sg_bias=jnp.repeat(p['sg_b'].T, SB_HEAD_DIM, axis=1),
        sg_wt=jnp.swapaxes(p['sg_w'], 1, 2),
        gq=jnp.tile(p['q_norm_g'], SB_WIDTH // SB_HEAD_DIM).reshape(1, -1),
        gk=jnp.tile(p['k_norm_g'], SB_WIDTH // SB_HEAD_DIM).reshape(1, -1),
    )


def layer_fwd(x, p, after=(), rest=None):
    c = _layer_consts(p)
    z, h = mm_norm_nn(x, _vec(p['ln1_g']), p['w_in'], name="in_proj", after=after)
    ga = mixa_fwd(z, _vec(p['sg_ln_g']), _vec(p['sg_ln_b']), p['sg_w'], c['sg_bias'], name="mixa_fwd")
    cb = mixb_fwd(z, p['cv_w'], _vec(p['cv_b']), _vec(p['cv_ln_g']), _vec(p['cv_ln_b']), name="mixb_fwd")
    q, k = attn_prep(z, c['gq'], c['gk'], name="attn_prep")
    ao = attn_fwd(q, k, z, name="attn_fwd")
    if rest is not None:
        p = {**p, **rest([ao])}
    ya, yb, yc = mm_branches("nn", [ga, cb, ao], [p['w_a_out'], p['w_b_out'], p['w_c_out']], name="abc_out")
    merged = merge_fwd(z, ya, yb, yc, p['b_gate'], name="merge_fwd")
    x1 = mm_nn(merged, p['w_out'], res=x, out_dtype=F32, name="out_proj")
    up, h2 = mm_norm_nn(x1, _vec(p['ln2_g']), p['w_up'], name="up_proj")
    act = ffn_mid_fwd(up, p['ffn_conv_w'], _vec(p['ffn_conv_b']), name="ffn_mid_fwd")
    x2 = mm_nn(act, p['w_down'], res=x1, out_dtype=F32, name="down_proj")
    saved = dict(x=x, z=z, h=h, ga=ga, cb=cb, q=q, k=k, ao=ao, ya=ya, yb=yb, yc=yc, merged=merged, x1=x1, up=up,
                 h2=h2, act=act)
    return x2, saved, p


GRAD_GROUPS = (('w_down', 'w_up'), ('w_out', 'w_a_out', 'w_b_out', 'w_c_out'), ('w_in',))


def layer_bwd(dx2, p, s, after=(), emit=None):
    c = _layer_consts(p)
    g = {}
    emit = emit or (lambda names, grads: ())
    g['w_down'] = mm_tn(s['act'], dx2, name="d_w_down")
    dact = mm_nt(dx2, p['w_down'], out_dtype=BF16, name="d_act", after=after)
    dupc, g['ffn_conv_w'], dcb = ffn_mid_bwd1(s['up'], dact, p['ffn_conv_w'], _vec(p['ffn_conv_b']), name="ffn_mid_bwd1")
    g['ffn_conv_b'] = dcb.reshape(-1)
    dup = ffn_mid_bwd2(dupc, p['ffn_conv_w'], name="ffn_mid_bwd2")
    g['w_up'] = mm_tn(s['h2'], dup, name="d_w_up")
    dh2 = mm_nt(dup, p['w_up'], out_dtype=F32, name="d_h2")
    dx1, dg2 = rms_bwd(dh2, s['x1'], _vec(p['ln2_g']), dx2, name="ln2_bwd")
    g['ln2_g'] = dg2.reshape(-1)
    g['w_out'] = mm_tn(s['merged'], dx1, name="d_w_out")
    dm = mm_nt(dx1, p['w_out'], out_dtype=BF16, name="d_merged", after=emit(GRAD_GROUPS[0], g))
    dya, dyb, dyc, dzg0, dzg1, dzg2, g['b_gate'] = merge_bwd(s['z'], s['ya'], s['yb'], s['yc'], p['b_gate'], dm,
                                                             name="merge_bwd")
    g['w_a_out'], g['w_b_out'], g['w_c_out'] = mm_branches("tn", [s['ga'], s['cb'], s['ao']], [dya, dyb, dyc],
                                                           name="d_w_abc_out")
    dga, dcb3, dao = mm_branches("nt", [dya, dyb, dyc], [p['w_a_out'], p['w_b_out'], p['w_c_out']], name="d_abc")
    dza, g['sg_w'], dsgb, dlg, dlb = mixa_bwd(s['z'], dga, _vec(p['sg_ln_g']), _vec(p['sg_ln_b']), p['sg_w'],
                                               c['sg_wt'], c['sg_bias'], name="mixa_bwd")
    g['sg_b'] = dsgb[:, :SG_WIDTH // SB_HEAD_DIM].T
    g['sg_ln_g'] = dlg.reshape(-1)
    g['sg_ln_b'] = dlb.reshape(-1)
    dc1, g['cv_w'], dcvb, dcg, dcbb = mixb_bwd1(s['z'], dcb3, p['cv_w'], _vec(p['cv_b']), _vec(p['cv_ln_g']),
                                                _vec(p['cv_ln_b']), name="mixb_bwd1")
    g['cv_b'] = dcvb.reshape(-1)
    g['cv_ln_g'] = dcg.reshape(-1)
    g['cv_ln_b'] = dcbb.reshape(-1)
    dzb = mixb_bwd2(s['z'], dc1, p['cv_w'], name="mixb_bwd2")
    dq, dk, dv = attn_bwd(s['q'], s['k'], s['z'], dao, name="attn_bwd")
    dzc, dgq, dgk = attn_post_bwd(s['z'], dq, dk, dv, c['gq'], c['gk'], name="attn_post_bwd")
    g['q_norm_g'] = dgq[0, :SB_HEAD_DIM]
    g['k_norm_g'] = dgk[0, :SB_HEAD_DIM]
    dz = jnp.concatenate([dza, dzb, dzc, dzg0, dzg1, dzg2], axis=1)
    g['w_in'] = mm_tn(s['h'], dz, name="d_w_in")
    dh = mm_nt(dz, p['w_in'], out_dtype=F32, name="d_h", after=emit(GRAD_GROUPS[1], g))
    dx, dg1 = rms_bwd(dh, s['x'], _vec(p['ln1_g']), dx1, name="ln1_bwd")
    g['ln1_g'] = dg1.reshape(-1)
    return dx, g


def local_step(x, target, depth, get_layer, on_grads, emit=lambda l: None):
    saved, layers = [], []
    for l in range(depth):
        p, after, rest = get_layer(l, x)
        x, s, p = layer_fwd(x, p, after, rest)
        layers.append(p)
        saved.append(s)
    loss, dx = loss_head(x, target, name="loss_head")
    after = ()
    for l in reversed(range(depth)):
        dx, g = layer_bwd(dx, layers[l], saved[l], after, emit(l))
        after = on_grads(l, g)
    return loss[0, 0], dx


def adamw(w, g, m, v, *, name):
    r, c = w.shape
    tr = _pick(r, (256, 704)) if r * c > 512 * 1024 else r

    def body(w_ref, g_ref, m_ref, v_ref, d_ref, mo_ref, vo_ref):
        gv = g_ref[...]
        mn = ADAM_B1 * m_ref[...] + (1.0 - ADAM_B1) * gv
        vn = ADAM_B2 * v_ref[...] + (1.0 - ADAM_B2) * (gv * gv)
        m_hat = mn / (1.0 - ADAM_B1 ** ADAM_STEP)
        v_hat = vn / (1.0 - ADAM_B2 ** ADAM_STEP)
        d_ref[...] = -ADAM_LR * (m_hat / (jnp.sqrt(v_hat) + ADAM_EPS) + ADAM_WD * w_ref[...])
        mo_ref[...] = mn
        vo_ref[...] = vn

    spec = pl.BlockSpec((tr, c), lambda i: (i, 0))
    out = _sds((r, c), F32)
    return _pc(body, name=name, grid=(r // tr,), in_specs=[spec] * 4, out_specs=[spec] * 3, out_shape=[out] * 3,
               compiler_params=_cp("parallel"))(w, g, m, v)


def _as3(a):
    return a if a.ndim == 3 else a.reshape((1,) + a.shape)


def add_half(g, recv, c_idx, *, name):
    s, rh, w = recv.shape
    tr = _pick(rh, (256, 352, 128))
    nb = rh // tr

    def body(c_ref, g_ref, r_ref, o_ref):
        o_ref[...] = (g_ref[...].astype(F32) + r_ref[...].astype(F32)).astype(o_ref.dtype)

    own = pl.BlockSpec((1, tr, w), lambda k, i, c_ref: (k, c_ref[0] * nb + i, 0))
    half = pl.BlockSpec((1, tr, w), lambda k, i, c_ref: (k, i, 0))
    gs = pltpu.PrefetchScalarGridSpec(num_scalar_prefetch=1, grid=(s, nb), in_specs=[own, half], out_specs=half)
    return _pc(body, name=name, grid_spec=gs, out_shape=_sds((s, rh, w), BF16),
               compiler_params=_cp("parallel", "parallel"))(c_idx, g, recv)


def sum_shard(p, recv, pos_idx, stacked, layer, *, by_rows, name):
    _, rh, w = recv.shape
    tr = _pick(rh, (256, 352, 128))
    nb = rh // tr

    def body(pos_ref, p_ref, r_ref, s_ref, o_ref):
        acc = p_ref[0].astype(F32)
        for j in range(N_CHIPS - 1):
            acc = acc + r_ref[j].astype(F32)
        o_ref[0] = acc

    if by_rows:
        own = pl.BlockSpec((1, tr, w), lambda i, pos_ref: (pos_ref[0], i, 0))
    else:
        own = pl.BlockSpec((1, tr, w), lambda i, pos_ref: (0, i, pos_ref[0]))
    gs = pltpu.PrefetchScalarGridSpec(num_scalar_prefetch=1, grid=(nb,),
                                      in_specs=[own, pl.BlockSpec((N_CHIPS - 1, tr, w), lambda i, pos_ref: (0, i, 0)), ANY],
                                      out_specs=pl.BlockSpec((1, tr, w), lambda i, pos_ref: (layer, pos_ref[1] * nb + i, 0)))
    return _pc(body, name=name, grid_spec=gs, out_shape=_sds(stacked.shape, F32), input_output_aliases={3: 0},
               compiler_params=_cp("parallel"))(pos_idx, p, recv, stacked)


def sum_slots(slab, *, name):
    _, r, w = slab.shape
    tr = _pick(r, (512, 256, 8))

    def body(s_ref, o_ref):
        acc = s_ref[0]
        for j in range(1, N_DEV):
            acc = acc + s_ref[j]
        o_ref[...] = acc

    return _pc(body, name=name, grid=(r // tr,), in_specs=[pl.BlockSpec((N_DEV, tr, w), lambda i: (0, i, 0))],
               out_specs=pl.BlockSpec((tr, w), lambda i: (i, 0)), out_shape=_sds((r, w), F32),
               compiler_params=_cp("parallel"))(slab)


def _mesh_pos():
    x, y, c = lax.axis_index("x"), lax.axis_index("y"), lax.axis_index("c")
    others = [(1 - x, y), (x, 1 - y), (1 - x, 1 - y)]
    return x, y, c, others


def _rcopy(src, dst, ssem, rsem, k, dev):
    return pltpu.make_async_remote_copy(src_ref=src, dst_ref=dst, send_sem=ssem.at[k], recv_sem=rsem.at[k],
                                        device_id=dev, device_id_type=MESH)


def _comm_call(body, name, n_in, out_shape, n_local, n_remote):
    scratch = [pltpu.SemaphoreType.DMA((max(n_local, 1),)), pltpu.SemaphoreType.DMA((n_remote,)),
               pltpu.SemaphoreType.DMA((n_remote,))]
    return _pc(body, name=name, in_specs=[ANY] * n_in, out_specs=[ANY] * len(out_shape), out_shape=out_shape,
               scratch_shapes=scratch)


GATHERED = BIG + SMALL_COL
HBM_SPEC = pl.BlockSpec(memory_space=pltpu.HBM)
SEM_SPEC = pl.BlockSpec(memory_space=pltpu.SEMAPHORE)
TOKEN_SHAPE = (8, 128)


def place_block(w, layer, pos_idx, *, by_rows, dtype, name):
    _, r, c = w.shape
    tr = _pick(r, (512, 704, 256))

    def body(pos_ref, w_ref, o_ref):
        if by_rows:
            o_ref[0] = w_ref[0].astype(dtype)
        else:
            o_ref[...] = w_ref[0].astype(dtype)

    if by_rows:
        out_spec, shape = pl.BlockSpec((1, tr, c), lambda i, pos_ref: (pos_ref[0], i, 0)), (N_CHIPS, r, c)
    else:
        out_spec, shape = pl.BlockSpec((tr, c), lambda i, pos_ref: (i, pos_ref[0])), (r, N_CHIPS * c)
    gs = pltpu.PrefetchScalarGridSpec(num_scalar_prefetch=1, grid=(r // tr,),
                                      in_specs=[pl.BlockSpec((1, tr, c), lambda i, pos_ref: (layer, i, 0))],
                                      out_specs=out_spec)
    return _pc(body, name=name, grid_spec=gs, out_shape=_sds(shape, dtype), compiler_params=_cp("parallel"))(pos_idx, w)


def _gather_windows(bufs):
    names, arrs = list(bufs), list(bufs.values())

    def dwin(refs, i, k, h):
        if names[i] in BIG_ROW:
            _, r, _ = arrs[i].shape
            return refs[i].at[k] if h is None else refs[i].at[k, pl.ds(h * (r // 2), r // 2), :]
        r, cs = arrs[i].shape[0], arrs[i].shape[1] // N_CHIPS
        cols = pl.ds(pl.multiple_of(k * cs, 128), cs)
        return refs[i].at[:, cols] if h is None else refs[i].at[pl.ds(h * (r // 2), r // 2), cols]

    def swin(refs, i, h):
        x, y, _, _ = _mesh_pos()
        return dwin(refs, i, 2 * x + y, h)

    return names, dwin, swin


def _gather_send(names, ins, outs, ssem, rsem, dwin, swin, stride):
    x, y, c, others = _mesh_pos()
    sends = []
    for i, n in enumerate(names):
        h = c if n in BIG else None
        for j, chip in enumerate(others):
            cp = _rcopy(swin(ins, i, h), swin(outs, i, h), ssem, rsem, stride * i + j, (*chip, c))
            cp.start()
            sends.append(cp)
    return sends


def _gather_pass_on(names, outs, ssem, rsem, dwin, stride, first_off, pass_off):
    x, y, c, others = _mesh_pos()
    sib = (x, y, 1 - c)
    sends = []
    for j, chip in enumerate(others):
        kk = 2 * chip[0] + chip[1]
        for i, n in enumerate(names):
            got = dwin(outs, i, kk, c if n in BIG else None)
            if first_off is not None:
                _rcopy(got, got, ssem, rsem, stride * i + first_off + j, (*chip, c)).wait_recv()
            if n in BIG:
                fwd = _rcopy(got, got, ssem, rsem, stride * i + pass_off + j, sib)
                fwd.start()
                sends.append(fwd)
    for j, chip in enumerate(others):
        kk = 2 * chip[0] + chip[1]
        for i, n in enumerate(names):
            if n in BIG:
                got = dwin(outs, i, kk, 1 - c)
                _rcopy(got, got, ssem, rsem, stride * i + pass_off + j, sib).wait_recv()
    return sends


def _as_weights(bufs):
    return {n: (o.reshape(o.shape[0] * o.shape[1], o.shape[2]) if n in BIG_ROW else o) for n, o in bufs.items()}


def _comm_in_place(body, name, bufs, n_sems):
    nn = len(bufs)
    arrs = list(bufs.values())
    scratch = [pltpu.SemaphoreType.DMA((n_sems,)), pltpu.SemaphoreType.DMA((n_sems,))]
    outs = _pc(body, name=name, in_specs=[ANY] * nn, out_specs=[ANY] * nn, out_shape=[_sds(a.shape, a.dtype) for a in arrs],
               scratch_shapes=scratch, input_output_aliases={i: i for i in range(nn)})(*arrs)
    return dict(zip(bufs, outs))


def allgather_weights(bufs, *, name):
    nn = len(bufs)
    names, dwin, swin = _gather_windows(bufs)

    def body(*refs):
        ins, outs = refs[:nn], refs[nn:2 * nn]
        ssem, rsem = refs[2 * nn:]
        sends = _gather_send(names, ins, outs, ssem, rsem, dwin, swin, 6)
        sends += _gather_pass_on(names, outs, ssem, rsem, dwin, 6, 0, 3)
        for cp in sends:
            cp.wait_send()

    return _comm_in_place(body, name, bufs, 6 * nn)


def _split_start(body, name, bufs, extra_in, n_sems):
    nn = len(bufs)
    arrs = [pltpu.with_memory_space_constraint(a, pltpu.HBM) for a in bufs.values()]
    out_shape = ([pltpu.SemaphoreType.DMA((n_sems,)), pltpu.SemaphoreType.DMA((n_sems,))]
                 + [pltpu.HBM(a.shape, a.dtype) for a in arrs] + [_sds(TOKEN_SHAPE, F32)])
    outs = _pc(body, name=name, in_specs=[HBM_SPEC] * nn + [ANY] * len(extra_in),
               out_specs=[SEM_SPEC, SEM_SPEC] + [HBM_SPEC] * nn + [pl.BlockSpec(memory_space=pltpu.VMEM)],
               out_shape=out_shape, input_output_aliases={i: 2 + i for i in range(nn)},
               compiler_params=pltpu.CompilerParams(has_side_effects=pltpu.SideEffectType.DATAFLOW_SIDE_EFFECTING),
               )(*arrs, *extra_in)
    return dict(ssem=outs[0], rsem=outs[1], bufs=dict(zip(bufs, outs[2:2 + nn])), token=outs[-1])


def _split_wait(body, name, handle, after):
    bufs = handle['bufs']
    nn = len(bufs)
    arrs = list(bufs.values())
    outs = _pc(body, name=name, in_specs=[HBM_SPEC] * nn + [SEM_SPEC, SEM_SPEC] + [ANY] * len(after),
               out_specs=[HBM_SPEC] * nn, out_shape=[pltpu.HBM(a.shape, a.dtype) for a in arrs],
               input_output_aliases={i: i for i in range(nn)},
               compiler_params=pltpu.CompilerParams(has_side_effects=pltpu.SideEffectType.DATAFLOW_SIDE_EFFECTING),
               )(*arrs, handle['ssem'], handle['rsem'], *after)
    return dict(zip(bufs, outs))


def gather_start(bufs, after, *, name):
    nn = len(bufs)
    names, dwin, swin = _gather_windows(bufs)

    def body(*refs):
        ins = refs[:nn]
        ssem, rsem = refs[nn + len(after)], refs[nn + len(after) + 1]
        _gather_send(names, ins, ins, ssem, rsem, dwin, swin, 3)
        refs[-1][...] = jnp.zeros(TOKEN_SHAPE, F32)

    return _split_start(body, name, bufs, after, 3 * nn)


def gather_wait(handle, after, *, name):
    nn = len(handle['bufs'])
    names, dwin, swin = _gather_windows(handle['bufs'])

    def body(*refs):
        ins = refs[:nn]
        ssem, rsem = refs[nn], refs[nn + 1]
        x, y, c, others = _mesh_pos()
        for i, n in enumerate(names):
            h = c if n in BIG else None
            for j, chip in enumerate(others):
                kk = 2 * chip[0] + chip[1]
                cp = _rcopy(swin(ins, i, h), dwin(ins, i, kk, h), ssem, rsem, 3 * i + j, (*chip, c))
                cp.wait_send()
                cp.wait_recv()

    return _split_wait(body, name, handle, after)


def gather_finish(bufs, *, name):
    nn = len(bufs)
    names, dwin, _ = _gather_windows(bufs)

    def body(*refs):
        outs = refs[nn:2 * nn]
        ssem, rsem = refs[2 * nn:]
        for cp in _gather_pass_on(names, outs, ssem, rsem, dwin, 3, None, 0):
            cp.wait_send()

    return _comm_in_place(body, name, bufs, 3 * nn)


def _grad_view(n, g):
    return g.reshape(N_CHIPS, g.shape[0] // N_CHIPS, g.shape[1]) if n in BIG_ROW else g.reshape((1,) + g.shape)


def exchange_halves(gv, *, name):
    nn = len(gv)
    arrs = list(gv.values())

    def body(*refs):
        ins, outs = refs[:nn], refs[nn:2 * nn]
        _, ssem, rsem = refs[2 * nn:]
        x, y, c, _ = _mesh_pos()
        cps = []
        for i in range(nn):
            rh = arrs[i].shape[1] // 2
            cp = _rcopy(ins[i].at[:, pl.ds((1 - c) * rh, rh), :], outs[i], ssem, rsem, i, (x, y, 1 - c))
            cp.start()
            cps.append(cp)
        for cp in cps:
            cp.wait()

    out_shape = [_sds((a.shape[0], a.shape[1] // 2, a.shape[2]), a.dtype) for a in arrs]
    return dict(zip(gv, _comm_call(body, name, nn, out_shape, 0, nn)(*arrs)))


def _shard_shape(n, p):
    _, rh, w = p.shape
    return (rh, w) if n in BIG_ROW else (rh, w // N_CHIPS)


def _scatter_copies(pv, ins, outs, ssem, rsem):
    x, y, c, others = _mesh_pos()
    cps = []
    for i, (n, p) in enumerate(pv.items()):
        _, ws = _shard_shape(n, p)
        for j, chip in enumerate(others):
            kk = 2 * chip[0] + chip[1]
            if n in BIG_ROW:
                src = ins[i].at[kk]
            else:
                src = ins[i].at[0, :, pl.ds(pl.multiple_of(kk * ws, 128), ws)]
            cps.append(_rcopy(src, outs[i].at[j], ssem, rsem, 3 * i + j, (*chip, c)))
    return cps


def _recv_shapes(pv):
    return [(N_CHIPS - 1,) + _shard_shape(n, p) for n, p in pv.items()]


def scatter_partials(pv, *, name):
    nn = len(pv)

    def body(*refs):
        ins, outs = refs[:nn], refs[nn:2 * nn]
        _, ssem, rsem = refs[2 * nn:]
        cps = _scatter_copies(pv, ins, outs, ssem, rsem)
        for cp in cps:
            cp.start()
        for cp in cps:
            cp.wait()

    out_shape = [_sds(s, p.dtype) for s, p in zip(_recv_shapes(pv), pv.values())]
    return pv, dict(zip(pv, _comm_call(body, name, nn, out_shape, 0, 3 * nn)(*pv.values())))


_RECV = "/recv"


def scatter_start(pv, *, name):
    nn = len(pv)

    def body(*refs):
        ins, lands = refs[:nn], refs[nn:2 * nn]
        ssem, rsem = refs[2 * nn], refs[2 * nn + 1]
        for cp in _scatter_copies(pv, ins, lands, ssem, rsem):
            cp.start()
        refs[-1][...] = jnp.zeros(TOKEN_SHAPE, F32)

    lands = {n + _RECV: lax.empty(s, p.dtype) for (n, p), s in zip(pv.items(), _recv_shapes(pv))}
    return _split_start(body, name, {**pv, **lands}, (), 3 * nn)


def scatter_wait(handle, after, *, name):
    nn = len(handle['bufs']) // 2
    pv = dict(list(handle['bufs'].items())[:nn])

    def body(*refs):
        ins, zones = refs[:nn], refs[nn:2 * nn]
        ssem, rsem = refs[2 * nn], refs[2 * nn + 1]
        for cp in _scatter_copies(pv, ins, zones, ssem, rsem):
            cp.wait_send()
            cp.wait_recv()

    outs = _split_wait(body, name, handle, after)
    return {n: outs[n] for n in pv}, {n: outs[n + _RECV] for n in pv}


def join_halves(rv, layer, *, name):
    nn = len(rv)
    arrs = list(rv.values())

    def body(*refs):
        ins, outs = refs[:nn], refs[nn:2 * nn]
        ssem, rsem = refs[2 * nn:]
        x, y, c, _ = _mesh_pos()
        cps = []
        for i in range(nn):
            rh = arrs[i].shape[1] // 2
            rows = pl.ds(c * rh, rh)
            cp = _rcopy(ins[i].at[layer, rows, :], outs[i].at[layer, rows, :], ssem, rsem, i, (x, y, 1 - c))
            cp.start()
            cps.append(cp)
        for i, cp in enumerate(cps):
            cp.wait_send()
            rh = arrs[i].shape[1] // 2
            got = outs[i].at[layer, pl.ds((1 - c) * rh, rh), :]
            _rcopy(got, got, ssem, rsem, i, (x, y, 1 - c)).wait_recv()

    return _comm_in_place(body, name, rv, nn)


def chip_partials(grads, names, c_idx):
    gv = {n: _grad_view(n, grads[n]) for n in names}
    recv = exchange_halves(gv, name="rs_exchange_halves")
    return {n: add_half(gv[n], recv[n], c_idx, name="rs_add_" + n) for n in names}


def reduce_shards(pv, got, pos_idx, stacked, layer):
    rv = {n: sum_shard(pv[n], got[n], pos_idx, stacked[n], layer, by_rows=n in BIG_ROW, name="rs_sum_" + n) for n in pv}
    return join_halves(rv, layer, name="rs_join_halves")


def _slab_first(ref, ssem, rsem):
    x, y, c, others = _mesh_pos()
    mine = ref.at[4 * x + 2 * y + c]
    peers = [(x, y, 1 - c)] + [(*chip, c) for chip in others]
    out = []
    for k, p in enumerate(peers):
        got = ref.at[4 * p[0] + 2 * p[1] + p[2]]
        out.append((_rcopy(mine, mine, ssem, rsem, k, p), _rcopy(got, got, ssem, rsem, k, p)))
    return out


def slab_start(slab, *, name):
    def body(ref, ssem, rsem, thru, token):
        for cp, _ in _slab_first(ref, ssem, rsem):
            cp.start()
        token[...] = jnp.zeros(TOKEN_SHAPE, F32)

    return _split_start(body, name, {'slab': slab}, (), 4)


def slab_wait(handle, after, *, name):
    def body(ref, ssem, rsem, *rest):
        for sent, landed in _slab_first(ref, ssem, rsem):
            sent.wait_send()
            landed.wait_recv()

    return _split_wait(body, name, handle, after)['slab']


def slab_finish(slab, *, name):
    def body(in_ref, out_ref, ssem, rsem):
        x, y, c, others = _mesh_pos()
        sib = (x, y, 1 - c)
        sends = []
        for j, chip in enumerate(others):
            got = out_ref.at[4 * chip[0] + 2 * chip[1] + c]
            sends.append(_rcopy(got, got, ssem, rsem, j, sib))
            sends[-1].start()
        for j, chip in enumerate(others):
            got = out_ref.at[4 * chip[0] + 2 * chip[1] + 1 - c]
            _rcopy(got, got, ssem, rsem, j, sib).wait_recv()
        for cp in sends:
            cp.wait_send()

    return _comm_in_place(body, name, {'slab': slab}, 3)['slab']


def _pad128(n):
    return -(-n // 128) * 128


def _pack_small(grads, shapes):
    parts = []
    for g in grads:
        for n in SMALL:
            v = g[n].astype(F32).reshape(-1)
            parts.append(jnp.pad(v, (0, _pad128(v.shape[0]) - v.shape[0])))
    flat = jnp.concatenate(parts)
    rows = -(-flat.shape[0] // (128 * 512)) * 512
    return jnp.pad(flat, (0, rows * 128 - flat.shape[0])).reshape(rows, 128)


def _unpack_small(slab, shapes, depth):
    flat = slab.reshape(-1)
    out = {n: [] for n in SMALL}
    off = 0
    for _ in range(depth):
        for n in SMALL:
            size = math.prod(shapes[n])
            out[n].append(flat[off:off + size].reshape(shapes[n]))
            off += _pad128(size)
    return {n: jnp.stack(v) for n, v in out.items()}


def _adamw_nd(w, g, m, v, name):
    shp = w.shape
    two = lambda a: a.reshape(-1, shp[-1])
    return tuple(o.reshape(shp) for o in adamw(two(w), two(g), two(m), two(v), name=name))


def kernel(x, ln1_g, w_in, b_gate, sg_ln_g, sg_ln_b, sg_w, sg_b, w_a_out, cv_w, cv_b, cv_ln_g, cv_ln_b, w_b_out, q_norm_g, k_norm_g, w_c_out, w_out, ln2_g, w_up, ffn_conv_w, ffn_conv_b, w_down, loss_target, m_ln1_g, m_w_in, m_b_gate, m_sg_ln_g, m_sg_ln_b, m_sg_w, m_sg_b, m_w_a_out, m_cv_w, m_cv_b, m_cv_ln_g, m_cv_ln_b, m_w_b_out, m_q_norm_g, m_k_norm_g, m_w_c_out, m_w_out, m_ln2_g, m_w_up, m_ffn_conv_w, m_ffn_conv_b, m_w_down, v_ln1_g, v_w_in, v_b_gate, v_sg_ln_g, v_sg_ln_b, v_sg_w, v_sg_b, v_w_a_out, v_cv_w, v_cv_b, v_cv_ln_g, v_cv_ln_b, v_w_b_out, v_q_norm_g, v_k_norm_g, v_w_c_out, v_w_out, v_ln2_g, v_w_up, v_ffn_conv_w, v_ffn_conv_b, v_w_down):
    w = dict(ln1_g=ln1_g, w_in=w_in, b_gate=b_gate, sg_ln_g=sg_ln_g, sg_ln_b=sg_ln_b, sg_w=sg_w, sg_b=sg_b,
             w_a_out=w_a_out, cv_w=cv_w, cv_b=cv_b, cv_ln_g=cv_ln_g, cv_ln_b=cv_ln_b, w_b_out=w_b_out,
             q_norm_g=q_norm_g, k_norm_g=k_norm_g, w_c_out=w_c_out, w_out=w_out, ln2_g=ln2_g, w_up=w_up,
             ffn_conv_w=ffn_conv_w, ffn_conv_b=ffn_conv_b, w_down=w_down)
    m = dict(ln1_g=m_ln1_g, w_in=m_w_in, b_gate=m_b_gate, sg_ln_g=m_sg_ln_g, sg_ln_b=m_sg_ln_b, sg_w=m_sg_w,
             sg_b=m_sg_b, w_a_out=m_w_a_out, cv_w=m_cv_w, cv_b=m_cv_b, cv_ln_g=m_cv_ln_g, cv_ln_b=m_cv_ln_b,
             w_b_out=m_w_b_out, q_norm_g=m_q_norm_g, k_norm_g=m_k_norm_g, w_c_out=m_w_c_out, w_out=m_w_out,
             ln2_g=m_ln2_g, w_up=m_w_up, ffn_conv_w=m_ffn_conv_w, ffn_conv_b=m_ffn_conv_b, w_down=m_w_down)
    v = dict(ln1_g=v_ln1_g, w_in=v_w_in, b_gate=v_b_gate, sg_ln_g=v_sg_ln_g, sg_ln_b=v_sg_ln_b, sg_w=v_sg_w,
             sg_b=v_sg_b, w_a_out=v_w_a_out, cv_w=v_cv_w, cv_b=v_cv_b, cv_ln_g=v_cv_ln_g, cv_ln_b=v_cv_ln_b,
             w_b_out=v_w_b_out, q_norm_g=v_q_norm_g, k_norm_g=v_k_norm_g, w_c_out=v_w_c_out, w_out=v_w_out,
             ln2_g=v_ln2_g, w_up=v_w_up, ffn_conv_w=v_ffn_conv_w, ffn_conv_b=v_ffn_conv_b, w_down=v_w_down)
    depth = ln1_g.shape[0]
    cx, cy, cc = lax.axis_index("x"), lax.axis_index("y"), lax.axis_index("c")
    me = 2 * cx + cy
    pos_idx = jnp.stack([me, cc]).astype(jnp.int32)
    c_idx = jnp.reshape(cc, (1,)).astype(jnp.int32)

    padded = {n: jnp.pad(w[n], ((0, 0), (0, -w[n].shape[1] % 8), (0, 0))) for n in SMALL_COL}
    first, later = ['w_in'] + SMALL_COL, [n for n in BIG if n != 'w_in']

    def blocks(names, l):
        return {n: (place_block(w[n], l, pos_idx, by_rows=n in BIG_ROW, dtype=BF16, name="place_" + n) if n in BIG else
                    place_block(padded[n], l, pos_idx, by_rows=False, dtype=F32, name="place_" + n)) for n in names}

    full0 = allgather_weights(blocks(first, 0), name="allgather_weights")
    gathers = [gather_start(blocks(later, 0), [full0['w_in']], name="gather_start_0")]
    for l in range(1, depth):
        gathers.append(gather_start(blocks(GATHERED, l), [gathers[-1]['token']], name="gather_start_%d" % l))

    def arrived(l, after):
        bufs = gather_wait(gathers[l], after, name="gather_wait_%d" % l)
        return _as_weights(gather_finish(bufs, name="gather_finish_%d" % min(l, 1)))

    def get_layer(l, x_in):
        if l == 0:
            p, after, rest = _as_weights(full0), tuple(h['token'] for h in gathers), functools.partial(arrived, 0)
        else:
            p, after, rest = arrived(l, [x_in]), (), None
        for n in SMALL:
            p[n] = p[n][:w[n].shape[1]] if n in SMALL_COL else w[n][l]
        return p, after, rest

    grads, scatters = [None] * depth, []

    def on_grads(l, g):
        grads[l] = g
        if l == 0:
            scatters.append((0, scatter_partials(chip_partials(g, GRAD_GROUPS[-1], c_idx), name="rs_scatter_partials")))
            return ()
        scatters.append((l, scatter_start(chip_partials(g, BIG, c_idx), name="scatter_start_%d" % l)))
        return (scatters[-1][1]['token'],)

    def emit0(names, g):
        scatters.append((0, scatter_start(chip_partials(g, names, c_idx), name="scatter_start_0_" + names[0])))
        return (scatters[-1][1]['token'],)

    loss, dx = local_step(x[0], loss_target[0], depth, get_layer, on_grads, lambda l: emit0 if l == 0 else None)
    loss = lax.psum(loss, ("x", "y", "c"))
    full_shapes = {n: (w[n].shape[1], N_CHIPS * w[n].shape[2]) if n in SMALL_COL else w[n].shape[1:] for n in SMALL}
    mine = _pack_small(grads, full_shapes)
    slots = lax.dynamic_update_slice(lax.empty((N_DEV,) + mine.shape, F32), mine[None], (4 * cx + 2 * cy + cc, 0, 0))
    gathering = slab_start(slots, name="small_grads_start")
    grad = {n: lax.empty(w[n].shape, F32) for n in BIG}
    for i, (l, sc) in enumerate(scatters):
        pv, got = sc if isinstance(sc, tuple) else scatter_wait(sc, [dx, gathering['token']], name="scatter_wait_%d" % i)
        grad.update(reduce_shards(pv, got, pos_idx, grad, l))

    delta, new_m, new_v = {}, {}, {}
    for n in BIG:
        delta[n], new_m[n], new_v[n] = _adamw_nd(w[n], grad[n], m[n], v[n], "adamw_" + n)
    slots = slab_finish(slab_wait(gathering, [delta[n] for n in BIG], name="small_grads_wait"), name="small_grads_finish")
    small = _unpack_small(sum_slots(slots, name="sum_small_grads"), full_shapes, depth)
    for n in SMALL:
        if n in SMALL_COL:
            cs = w[n].shape[-1]
            grad[n] = lax.dynamic_slice_in_dim(small[n], me * cs, cs, axis=small[n].ndim - 1)
        else:
            grad[n] = small[n]
        delta[n], new_m[n], new_v[n] = _adamw_nd(w[n], grad[n], m[n], v[n], "adamw_" + n)
    return (loss, dx[None], *[grad[n] for n in WEIGHTS], *[delta[n] for n in WEIGHTS],
            *[new_m[n] for n in WEIGHTS], *[new_v[n] for n in WEIGHTS])
```

```python
import functools
import math

import jax
import jax.numpy as jnp
from jax import lax
from jax.experimental import pallas as pl
from jax.experimental.pallas import tpu as pltpu

F32 = jnp.float32
BF16 = jnp.bfloat16
MESH = pl.DeviceIdType.MESH
ANY = pl.BlockSpec(memory_space=pl.ANY)

EPS = 1e-6
D_MODEL = 1024
DEPTH = 4
SG_WIDTH = 512
CHUNK = 128
CV_WIDTH = 512
CV_KERNEL = 31
SB_WIDTH = 512
SB_HEAD_DIM = 64
Q_BLOCK = 128
D_FF = 2816
FFN_KERNEL = 3
COL_B = 1024
COL_C = 2048
COL_G = 3584
IN_COLS = 6656
N_CHIPS = 4
N_DEV = 8
CV_HALO = 32
FFN_HALO = 16

ADAM_LR = 0.001
ADAM_B1 = 0.9
ADAM_B2 = 0.999
ADAM_EPS = 1e-08
ADAM_WD = 0.01
ADAM_STEP = 10

VMEM_LIMIT_BYTES = 56 * 1024 * 1024

NT_DIMS = (((1,), (1,)), ((), ()))
TN_DIMS = (((0,), (0,)), ((), ()))

WEIGHTS = ['ln1_g', 'w_in', 'b_gate', 'sg_ln_g', 'sg_ln_b', 'sg_w', 'sg_b', 'w_a_out', 'cv_w', 'cv_b',
           'cv_ln_g', 'cv_ln_b', 'w_b_out', 'q_norm_g', 'k_norm_g', 'w_c_out', 'w_out', 'ln2_g', 'w_up',
           'ffn_conv_w', 'ffn_conv_b', 'w_down']
BIG_COL = ['w_in', 'w_a_out', 'w_b_out', 'w_c_out', 'w_up']
BIG_ROW = ['w_out', 'w_down']
BIG = BIG_COL + BIG_ROW
SMALL_COL = ['b_gate', 'cv_w', 'ffn_conv_w']
SMALL = [n for n in WEIGHTS if n not in BIG]


def _pc(body, **kw):
    return pl.pallas_call(body, **kw)


def _cp(*sem):
    return pltpu.CompilerParams(dimension_semantics=sem, vmem_limit_bytes=VMEM_LIMIT_BYTES)


def _sds(shape, dtype):
    return jax.ShapeDtypeStruct(shape, dtype)


_GELU_C = math.sqrt(2.0 / math.pi)
_GELU_A = 0.044715


def _sigmoid(x):
    return jax.nn.sigmoid(x)


def _gelu(x):
    return 0.5 * x * (1.0 + jnp.tanh(_GELU_C * (x + _GELU_A * x * x * x)))


def _gelu_grad(x):
    t = jnp.tanh(_GELU_C * (x + _GELU_A * x * x * x))
    return 0.5 * (1.0 + t) + 0.5 * x * (1.0 - t * t) * _GELU_C * (1.0 + 3.0 * _GELU_A * x * x)


def _silu(x):
    return x * _sigmoid(x)


def _silu_grad(x):
    s = _sigmoid(x)
    return s * (1.0 + x * (1.0 - s))


def _ln_stats(x):
    mu = jnp.mean(x, axis=-1, keepdims=True)
    xc = x - mu
    r = lax.rsqrt(jnp.mean(xc * xc, axis=-1, keepdims=True) + EPS)
    return xc * r, r


def _ln_bwd(dy, xhat, r, g):
    dxh = dy * g
    return r * (dxh - jnp.mean(dxh, axis=-1, keepdims=True) - xhat * jnp.mean(dxh * xhat, axis=-1, keepdims=True))


def _split_dot(x, m):
    hi = x.astype(BF16)
    lo = (x - hi.astype(F32)).astype(BF16)
    return jnp.dot(hi, m, preferred_element_type=F32) + jnp.dot(lo, m, preferred_element_type=F32)


def _block_sums(x, m):
    return jnp.dot(x.astype(BF16), m, preferred_element_type=F32)


def _rowsum0(x):
    return jnp.sum(x, axis=0, keepdims=True)


def _pick(n, prefs):
    for p in prefs:
        if n % p == 0:
            return p
    return n


def mm_nn(a, w, *, name, res=None, out_dtype=BF16):
    t, k = a.shape
    n = w.shape[1]
    tm = _pick(t, (512, 256))
    tn = _pick(n, (1024, 512, 256))

    def body(*refs):
        if res is None:
            a_ref, w_ref, o_ref = refs
        else:
            a_ref, w_ref, r_ref, o_ref = refs
        acc = jnp.dot(a_ref[...], w_ref[...], preferred_element_type=F32)
        if res is not None:
            acc = acc + r_ref[...]
        o_ref[...] = acc.astype(o_ref.dtype)

    in_specs = [pl.BlockSpec((tm, k), lambda i, j: (i, 0)), pl.BlockSpec((k, tn), lambda i, j: (0, j))]
    args = [a, w]
    if res is not None:
        in_specs.append(pl.BlockSpec((tm, tn), lambda i, j: (i, j)))
        args.append(res)
    return _pc(body, name=name, grid=(t // tm, n // tn), in_specs=in_specs,
               out_specs=pl.BlockSpec((tm, tn), lambda i, j: (i, j)),
               out_shape=_sds((t, n), out_dtype), compiler_params=_cp("parallel", "parallel"))(*args)


def mm_norm_nn(x, g, w, *, name, after=()):
    t, k = x.shape
    n = w.shape[1]
    tm = _pick(t, (512, 256))
    tn = _pick(n, (1664, 1408, 512))

    def body(x_ref, g_ref, w_ref, *rest):
        z_ref, h_ref = rest[len(after):]
        xv = x_ref[...]
        r = lax.rsqrt(jnp.mean(xv * xv, axis=-1, keepdims=True) + EPS)
        h = (xv * r * g_ref[...]).astype(BF16)
        h_ref[...] = h
        for c in range(n // tn):
            cols = slice(c * tn, (c + 1) * tn)
            z_ref[:, cols] = jnp.dot(h, w_ref[:, cols], preferred_element_type=F32).astype(z_ref.dtype)

    return _pc(body, name=name, grid=(t // tm,),
               in_specs=[pl.BlockSpec((tm, k), lambda i: (i, 0)), pl.BlockSpec((1, k), lambda i: (0, 0)),
                         pl.BlockSpec((k, n), lambda i: (0, 0), pipeline_mode=pl.Buffered(1))]
               + [pl.BlockSpec(a.shape, lambda i: (0, 0)) for a in after],
               out_specs=[pl.BlockSpec((tm, n), lambda i: (i, 0)), pl.BlockSpec((tm, k), lambda i: (i, 0))],
               out_shape=[_sds((t, n), BF16), _sds((t, k), BF16)],
               compiler_params=_cp("parallel"))(x, g, w, *after)


def mm_nt(dy, w, *, name, out_dtype, after=()):
    t, n = dy.shape
    k = w.shape[0]
    tm = _pick(t, (512, 256))

    def body(dy_ref, w_ref, *rest):
        o_ref = rest[len(after)]
        o_ref[...] = lax.dot_general(dy_ref[...].astype(BF16), w_ref[...], NT_DIMS,
                                     preferred_element_type=F32).astype(o_ref.dtype)

    return _pc(body, name=name, grid=(t // tm,),
               in_specs=[pl.BlockSpec((tm, n), lambda i: (i, 0)),
                         pl.BlockSpec((k, n), lambda i: (0, 0), pipeline_mode=pl.Buffered(1))]
               + [pl.BlockSpec(tok.shape, lambda i: (0, 0)) for tok in after],
               out_specs=pl.BlockSpec((tm, k), lambda i: (i, 0)),
               out_shape=_sds((t, k), out_dtype), compiler_params=_cp("parallel"))(dy, w, *after)


def mm_tn(a, dy, *, name, out_dtype=BF16):
    t, k = a.shape
    n = dy.shape[1]
    tk = _pick(k, (1024, 1408, 512))
    tn = _pick(n, (512,) if dy.dtype == F32 else (1664, 1408, 1024, 512))

    def body(a_ref, dy_ref, o_ref):
        o_ref[...] = lax.dot_general(a_ref[...], dy_ref[...].astype(BF16), TN_DIMS,
                                     preferred_element_type=F32).astype(o_ref.dtype)

    return _pc(body, name=name, grid=(k // tk, n // tn),
               in_specs=[pl.BlockSpec((t, tk), lambda i, j: (0, i)), pl.BlockSpec((t, tn), lambda i, j: (0, j))],
               out_specs=pl.BlockSpec((tk, tn), lambda i, j: (i, j)),
               out_shape=_sds((k, n), out_dtype), compiler_params=_cp("parallel", "parallel"))(a, dy)


def mm_branches(kind, xs, ys, *, name):
    nb = len(xs)
    t = xs[0].shape[0]
    if kind == "tn":
        k, n = xs[0].shape[1], ys[0].shape[1]
        tn = _pick(n, (512,))
        grid = (n // tn,)
        x_spec = pl.BlockSpec((t, k), lambda j: (0, 0), pipeline_mode=pl.Buffered(1))
        y_spec = pl.BlockSpec((t, tn), lambda j: (0, j))
        o_spec, o_shape = pl.BlockSpec((k, tn), lambda j: (0, j)), (k, n)
        dims = TN_DIMS
    else:
        tm = _pick(t, (512, 256))
        grid = (t // tm,)
        x_spec = pl.BlockSpec((tm, xs[0].shape[1]), lambda i: (i, 0))
        y_spec = pl.BlockSpec(ys[0].shape, lambda i: (0, 0), pipeline_mode=pl.Buffered(1))
        width = ys[0].shape[1] if kind == "nn" else ys[0].shape[0]
        o_spec, o_shape = pl.BlockSpec((tm, width), lambda i: (i, 0)), (t, width)
        dims = (((1,), (0,)), ((), ())) if kind == "nn" else NT_DIMS

    def body(*refs):
        for q in range(nb):
            refs[2 * nb + q][...] = lax.dot_general(refs[q][...], refs[nb + q][...], dims,
                                                    preferred_element_type=F32).astype(BF16)

    return _pc(body, name=name, grid=grid, in_specs=[x_spec] * nb + [y_spec] * nb, out_specs=[o_spec] * nb,
               out_shape=[_sds(o_shape, BF16)] * nb, compiler_params=_cp("parallel"))(*xs, *ys)


def rms_bwd(dh, x, g, dres, *, name):
    t, d = x.shape
    tm = _pick(t, (256,))

    def body(dh_ref, x_ref, g_ref, dres_ref, dx_ref, dg_ref):
        xv = x_ref[...]
        r = lax.rsqrt(jnp.mean(xv * xv, axis=-1, keepdims=True) + EPS)
        xh = xv * r
        dy = dh_ref[...].astype(F32)
        dxh = dy * g_ref[...]
        dx_ref[...] = dres_ref[...] + r * (dxh - xh * jnp.mean(dxh * xh, axis=-1, keepdims=True))

        @pl.when(pl.program_id(0) == 0)
        def _():
            dg_ref[...] = jnp.zeros_like(dg_ref)

        dg_ref[...] += _rowsum0(dy * xh)

    row = pl.BlockSpec((tm, d), lambda i: (i, 0))
    vec = pl.BlockSpec((1, d), lambda i: (0, 0))
    return _pc(body, name=name, grid=(t // tm,), in_specs=[row, row, vec, row], out_specs=[row, vec],
               out_shape=[_sds((t, d), F32), _sds((1, d), F32)], compiler_params=_cp("arbitrary"))(dh, x, g, dres)


def loss_head(y, target, *, name):
    t, d = y.shape
    tm = _pick(t, (256,))

    def body(y_ref, t_ref, loss_ref, dy_ref):
        e = y_ref[...] - t_ref[...]
        dy_ref[...] = e * (1.0 / d)

        @pl.when(pl.program_id(0) == 0)
        def _():
            loss_ref[...] = jnp.zeros_like(loss_ref)

        loss_ref[...] += _rowsum0(jnp.sum(e * e, axis=1, keepdims=True)) * (0.5 / d)

    row = pl.BlockSpec((tm, d), lambda i: (i, 0))
    return _pc(body, name=name, grid=(t // tm,), in_specs=[row, row],
               out_specs=[pl.BlockSpec((1, 1), lambda i: (0, 0)), row],
               out_shape=[_sds((1, 1), F32), _sds((t, d), F32)], compiler_params=_cp("arbitrary"))(y, target)


def _sg_masks():
    lane = lax.broadcasted_iota(jnp.int32, (CHUNK, CHUNK), 1)
    row = lax.broadcasted_iota(jnp.int32, (CHUNK, CHUNK), 0)
    return lane < 64, lane <= row, row <= lane


def _sg_gate(vn_chunk, w_ref, bias_ref, p, first_group, tril):
    wa = jnp.where(tril, w_ref[2 * p], 0.0).astype(BF16)
    wb = jnp.where(tril, w_ref[2 * p + 1], 0.0).astype(BF16)
    oa = jnp.dot(wa, vn_chunk, preferred_element_type=F32)
    ob = jnp.dot(wb, vn_chunk, preferred_element_type=F32)
    return jnp.where(first_group, oa, ob) + bias_ref[:, p * 128:(p + 1) * 128]


def mixa_fwd(z, ln_g, ln_b, sg_w, sg_bias, *, name):
    t = z.shape[0]
    tm = _pick(t, (256,))

    def body(z_ref, g_ref, b_ref, w_ref, bias_ref, o_ref):
        first_group, tril, _ = _sg_masks()
        zv = z_ref[...].astype(F32)
        u = _gelu(zv[:, :SG_WIDTH])
        v = _gelu(zv[:, SG_WIDTH:])
        vh, _ = _ln_stats(v)
        vn = (vh * g_ref[...] + b_ref[...]).astype(BF16)
        for c in range(tm // CHUNK):
            rows = slice(c * CHUNK, (c + 1) * CHUNK)
            for p in range(4):
                cols = slice(p * 128, (p + 1) * 128)
                o = _sg_gate(vn[rows, cols], w_ref, bias_ref, p, first_group, tril)
                o_ref[rows, cols] = (u[rows, cols] * o).astype(o_ref.dtype)

    vec = pl.BlockSpec((1, SG_WIDTH), lambda i: (0, 0))
    return _pc(body, name=name, grid=(t // tm,),
               in_specs=[pl.BlockSpec((tm, 2 * SG_WIDTH), lambda i: (i, 0)), vec, vec,
                         pl.BlockSpec((8, CHUNK, CHUNK), lambda i: (0, 0, 0)),
                         pl.BlockSpec((CHUNK, SG_WIDTH), lambda i: (0, 0))],
               out_specs=pl.BlockSpec((tm, SG_WIDTH), lambda i: (i, 0)),
               out_shape=_sds((t, SG_WIDTH), BF16), compiler_params=_cp("parallel"))(z, ln_g, ln_b, sg_w, sg_bias)


def mixa_bwd(z, dga, ln_g, ln_b, sg_w, sg_wt, sg_bias, *, name):
    t = z.shape[0]
    tm = _pick(t, (256,))
    nsteps = t // tm

    def body(z_ref, dga_ref, g_ref, b_ref, w_ref, wt_ref, bias_ref, dz_ref, dw_ref, dsgb_ref, dg_ref, db_ref, dvn_s,
             dbias_ref):
        i = pl.program_id(0)
        first_group, tril, triu = _sg_masks()

        @pl.when(i == 0)
        def _():
            dw_ref[...] = jnp.zeros_like(dw_ref)
            dbias_ref[...] = jnp.zeros_like(dbias_ref)
            dg_ref[...] = jnp.zeros_like(dg_ref)
            db_ref[...] = jnp.zeros_like(db_ref)

        zv = z_ref[...].astype(F32)
        zu = zv[:, :SG_WIDTH]
        zg = zv[:, SG_WIDTH:]
        u = _gelu(zu)
        v = _gelu(zg)
        vh, r = _ln_stats(v)
        vn = (vh * g_ref[...] + b_ref[...]).astype(BF16)
        dga_v = dga_ref[...].astype(F32)
        d_o = dga_v * u
        for c in range(tm // CHUNK):
            rows = slice(c * CHUNK, (c + 1) * CHUNK)
            dbias_ref[...] += d_o[rows, :]
            for p in range(4):
                cols = slice(p * 128, (p + 1) * 128)
                vp = vn[rows, cols]
                o = _sg_gate(vp, w_ref, bias_ref, p, first_group, tril)
                dz_ref[rows, cols] = (dga_v[rows, cols] * o * _gelu_grad(zu[rows, cols])).astype(dz_ref.dtype)
                dop = d_o[rows, cols]
                dop_a = jnp.where(first_group, dop, 0.0).astype(BF16)
                dop_b = jnp.where(first_group, 0.0, dop).astype(BF16)
                dw_ref[2 * p] += lax.dot_general(dop_a, vp, NT_DIMS, preferred_element_type=F32)
                dw_ref[2 * p + 1] += lax.dot_general(dop_b, vp, NT_DIMS, preferred_element_type=F32)
                wta = jnp.where(triu, wt_ref[2 * p], 0.0).astype(BF16)
                wtb = jnp.where(triu, wt_ref[2 * p + 1], 0.0).astype(BF16)
                dop16 = dop.astype(BF16)
                dvn_s[rows, cols] = jnp.where(first_group, jnp.dot(wta, dop16, preferred_element_type=F32),
                                              jnp.dot(wtb, dop16, preferred_element_type=F32))
        dvn = dvn_s[...]
        dg_ref[...] += _rowsum0(dvn * vh)
        db_ref[...] += _rowsum0(dvn)
        dv = _ln_bwd(dvn, vh, r, g_ref[...])
        dz_ref[:, SG_WIDTH:] = (dv * _gelu_grad(zg)).astype(dz_ref.dtype)

        @pl.when(i == nsteps - 1)
        def _():
            for gi in range(8):
                dw_ref[gi] = jnp.where(tril, dw_ref[gi], 0.0)
            r_id = lax.broadcasted_iota(jnp.int32, (SG_WIDTH, 128), 0) // SB_HEAD_DIM
            c_id = lax.broadcasted_iota(jnp.int32, (SG_WIDTH, 128), 1)
            dsgb_ref[...] = _split_dot(dbias_ref[...], (r_id == c_id).astype(BF16))

    vec = pl.BlockSpec((1, SG_WIDTH), lambda i: (0, 0))
    wspec = pl.BlockSpec((8, CHUNK, CHUNK), lambda i: (0, 0, 0))
    bspec = pl.BlockSpec((CHUNK, SG_WIDTH), lambda i: (0, 0))
    sgb = pl.BlockSpec((CHUNK, 128), lambda i: (0, 0))
    return _pc(body, name=name, grid=(nsteps,),
               in_specs=[pl.BlockSpec((tm, 2 * SG_WIDTH), lambda i: (i, 0)), pl.BlockSpec((tm, SG_WIDTH), lambda i: (i, 0)),
                         vec, vec, wspec, wspec, bspec],
               out_specs=[pl.BlockSpec((tm, 2 * SG_WIDTH), lambda i: (i, 0)), wspec, sgb, vec, vec],
               out_shape=[_sds((t, 2 * SG_WIDTH), BF16), _sds((8, CHUNK, CHUNK), F32), _sds((CHUNK, 128), F32),
                          _sds((1, SG_WIDTH), F32), _sds((1, SG_WIDTH), F32)],
               scratch_shapes=[pltpu.VMEM((tm, SG_WIDTH), F32), pltpu.VMEM((CHUNK, SG_WIDTH), F32)],
               compiler_params=_cp("arbitrary"))(z, dga, ln_g, ln_b, sg_w, sg_wt, sg_bias)


def _glu(zv):
    return zv[:, :CV_WIDTH] * _sigmoid(zv[:, CV_WIDTH:])


_SUB = 8


def _cv_phases(x_s, tm):
    rows = CV_HALO + tm - _SUB
    for r in range(1, _SUB):
        x_s[r, 0:rows, :] = x_s[0, pl.ds(r, rows), :]


def _cv_tap(x_s, o, tm):
    return x_s[o % _SUB, pl.ds(o - o % _SUB, tm), :]


def _cv_fill(zm_ref, zh_ref, x_s, i, tm):
    x_s[0, 0:CV_HALO, :] = jnp.where(i > 0, _glu(zh_ref[...].astype(F32)), 0.0)
    x_s[0, CV_HALO:CV_HALO + tm, :] = _glu(zm_ref[...].astype(F32))
    _cv_phases(x_s, tm)


def _cv_conv(x_s, w_ref, cb_ref, tm):
    acc = jnp.zeros((tm, CV_WIDTH), F32) + cb_ref[...]
    for k in range(CV_KERNEL):
        acc = acc + w_ref[k:k + 1, :] * _cv_tap(x_s, CV_HALO - (CV_KERNEL - 1) + k, tm)
    return acc


def _cv_specs(tm):
    zm = pl.BlockSpec((tm, 2 * CV_WIDTH), lambda i: (i, 1))
    zh = pl.BlockSpec((CV_HALO, 2 * CV_WIDTH), lambda i: (jnp.maximum(i * (tm // CV_HALO) - 1, 0), 1))
    w = pl.BlockSpec((CV_KERNEL, CV_WIDTH), lambda i: (0, 0))
    vec = pl.BlockSpec((1, CV_WIDTH), lambda i: (0, 0))
    return zm, zh, w, vec


def mixb_fwd(z, cv_w, cv_b, ln_g, ln_b, *, name):
    t = z.shape[0]
    tm = _pick(t, (256,))

    def body(zm_ref, zh_ref, w_ref, cb_ref, g_ref, b_ref, o_ref, x_s):
        _cv_fill(zm_ref, zh_ref, x_s, pl.program_id(0), tm)
        c1 = _cv_conv(x_s, w_ref, cb_ref, tm)
        ch, _ = _ln_stats(c1)
        o_ref[...] = _silu(ch * g_ref[...] + b_ref[...]).astype(o_ref.dtype)

    zm, zh, w, vec = _cv_specs(tm)
    return _pc(body, name=name, grid=(t // tm,), in_specs=[zm, zh, w, vec, vec, vec],
               out_specs=pl.BlockSpec((tm, CV_WIDTH), lambda i: (i, 0)), out_shape=_sds((t, CV_WIDTH), BF16),
               scratch_shapes=[pltpu.VMEM((_SUB, CV_HALO + tm, CV_WIDTH), F32)],
               compiler_params=_cp("parallel"))(z, z, cv_w, cv_b, ln_g, ln_b)


def mixb_bwd1(z, dc3, cv_w, cv_b, ln_g, ln_b, *, name):
    t = z.shape[0]
    tm = _pick(t, (256,))

    def body(zm_ref, zh_ref, dc3_ref, w_ref, cb_ref, g_ref, b_ref, dc1_ref, dw_ref, dcb_ref, dg_ref, db_ref, x_s):
        i = pl.program_id(0)

        @pl.when(i == 0)
        def _():
            dw_ref[...] = jnp.zeros_like(dw_ref)
            dcb_ref[...] = jnp.zeros_like(dcb_ref)
            dg_ref[...] = jnp.zeros_like(dg_ref)
            db_ref[...] = jnp.zeros_like(db_ref)

        _cv_fill(zm_ref, zh_ref, x_s, i, tm)
        c1 = _cv_conv(x_s, w_ref, cb_ref, tm)
        ch, r = _ln_stats(c1)
        c2 = ch * g_ref[...] + b_ref[...]
        dc2 = dc3_ref[...].astype(F32) * _silu_grad(c2)
        dg_ref[...] += _rowsum0(dc2 * ch)
        db_ref[...] += _rowsum0(dc2)
        dc1 = _ln_bwd(dc2, ch, r, g_ref[...])
        dc1_ref[...] = dc1
        dcb_ref[...] += _rowsum0(dc1)
        for k in range(CV_KERNEL):
            dw_ref[k:k + 1, :] += _rowsum0(dc1 * _cv_tap(x_s, CV_HALO - (CV_KERNEL - 1) + k, tm))

    zm, zh, w, vec = _cv_specs(tm)
    row = pl.BlockSpec((tm, CV_WIDTH), lambda i: (i, 0))
    return _pc(body, name=name, grid=(t // tm,), in_specs=[zm, zh, row, w, vec, vec, vec],
               out_specs=[row, w, vec, vec, vec],
               out_shape=[_sds((t, CV_WIDTH), F32), _sds((CV_KERNEL, CV_WIDTH), F32), _sds((1, CV_WIDTH), F32),
                          _sds((1, CV_WIDTH), F32), _sds((1, CV_WIDTH), F32)],
               scratch_shapes=[pltpu.VMEM((_SUB, CV_HALO + tm, CV_WIDTH), F32)],
               compiler_params=_cp("arbitrary"))(z, z, dc3, cv_w, cv_b, ln_g, ln_b)


def mixb_bwd2(z, dc1, cv_w, *, name):
    t = z.shape[0]
    tm = _pick(t, (256,))
    nsteps = t // tm

    def body(zm_ref, dm_ref, dh_ref, w_ref, dz_ref, y_s):
        i = pl.program_id(0)
        y_s[0, 0:tm, :] = dm_ref[...]
        y_s[0, tm:tm + CV_HALO, :] = jnp.where(i < nsteps - 1, dh_ref[...], 0.0)
        _cv_phases(y_s, tm)
        dc0 = jnp.zeros((tm, CV_WIDTH), F32)
        for k in range(CV_KERNEL):
            dc0 = dc0 + w_ref[k:k + 1, :] * _cv_tap(y_s, CV_KERNEL - 1 - k, tm)
        zv = zm_ref[...].astype(F32)
        p = zv[:, :CV_WIDTH]
        s = _sigmoid(zv[:, CV_WIDTH:])
        dz_ref[:, :CV_WIDTH] = (dc0 * s).astype(dz_ref.dtype)
        dz_ref[:, CV_WIDTH:] = (dc0 * p * s * (1.0 - s)).astype(dz_ref.dtype)

    last = t // CV_HALO - 1
    return _pc(body, name=name, grid=(nsteps,),
               in_specs=[pl.BlockSpec((tm, 2 * CV_WIDTH), lambda i: (i, 1)),
                         pl.BlockSpec((tm, CV_WIDTH), lambda i: (i, 0)),
                         pl.BlockSpec((CV_HALO, CV_WIDTH), lambda i: (jnp.minimum((i + 1) * (tm // CV_HALO), last), 0)),
                         pl.BlockSpec((CV_KERNEL, CV_WIDTH), lambda i: (0, 0))],
               out_specs=pl.BlockSpec((tm, 2 * CV_WIDTH), lambda i: (i, 0)),
               out_shape=_sds((t, 2 * CV_WIDTH), BF16),
               scratch_shapes=[pltpu.VMEM((_SUB, tm + CV_HALO, CV_WIDTH), F32)],
               compiler_params=_cp("parallel"))(z, dc1, dc1, cv_w)


def _group_ones():
    r = lax.broadcasted_iota(jnp.int32, (SB_WIDTH, SB_WIDTH), 0) // SB_HEAD_DIM
    c = lax.broadcasted_iota(jnp.int32, (SB_WIDTH, SB_WIDTH), 1) // SB_HEAD_DIM
    return (r == c).astype(BF16)


def attn_prep(z, gq, gk, *, name):
    t = z.shape[0]
    tm = _pick(t, (256,))
    scale = 1.0 / math.sqrt(SB_HEAD_DIM)

    def body(q_ref, k_ref, gq_ref, gk_ref, qo_ref, ko_ref):
        ones = _group_ones()
        for src, g_ref, dst, mul in ((q_ref, gq_ref, qo_ref, scale), (k_ref, gk_ref, ko_ref, 1.0)):
            v = src[...].astype(F32)
            r = lax.rsqrt(_split_dot(v * v, ones) * (1.0 / SB_HEAD_DIM) + EPS)
            dst[...] = ((v * r * g_ref[...]).astype(BF16).astype(F32) * mul).astype(dst.dtype)

    vec = pl.BlockSpec((1, SB_WIDTH), lambda i: (0, 0))
    row = pl.BlockSpec((tm, SB_WIDTH), lambda i: (i, 0))
    return _pc(body, name=name, grid=(t // tm,),
               in_specs=[pl.BlockSpec((tm, SB_WIDTH), lambda i: (i, COL_C // SB_WIDTH)),
                         pl.BlockSpec((tm, SB_WIDTH), lambda i: (i, COL_C // SB_WIDTH + 1)), vec, vec],
               out_specs=[row, row], out_shape=[_sds((t, SB_WIDTH), BF16), _sds((t, SB_WIDTH), BF16)],
               compiler_params=_cp("parallel"))(z, z, gq, gk)


_KB = Q_BLOCK
_PAIR = 2 * _KB


def _attn_tq(t):
    return _pick(t, (512, 256, 128))


def _attn_consts(tq):
    first_head = lax.broadcasted_iota(jnp.int32, (_KB, 128), 1) < SB_HEAD_DIM
    r2 = lax.broadcasted_iota(jnp.int32, (_PAIR, _PAIR), 0)
    c2 = lax.broadcasted_iota(jnp.int32, (_PAIR, _PAIR), 1)
    same = (r2 // _KB) == (c2 // _KB)
    m_suffix = (same & (r2 > c2)).astype(BF16)
    m_prefix = (same & (r2 < c2)).astype(BF16)
    row = lax.broadcasted_iota(jnp.int32, (tq, _PAIR), 0)
    col = lax.broadcasted_iota(jnp.int32, (tq, _PAIR), 1)
    return first_head, m_suffix, m_prefix, row, col & (_KB - 1), col < _KB


def _sb_logits(z, causal):
    sp = jnp.log(1.0 + jnp.exp(-jnp.abs(z)))
    g = jnp.minimum(z, 0.0) - sp
    l1m = g - z
    if causal is not None:
        l1m = jnp.where(causal, l1m, 0.0)
    return g, l1m


def _stack_heads(first_head, v):
    zero = jnp.zeros_like(v)
    return jnp.concatenate([jnp.where(first_head, v, zero), jnp.where(first_head, zero, v)], axis=0)


def _add_rows(x, upd, r0):
    return x + upd if r0 == 0 else jnp.concatenate([x[:r0], x[r0:] + upd], axis=0)


def _pair_sums(x):
    return jnp.sum(x[:, :_KB], axis=1, keepdims=True), jnp.sum(x[:, _KB:], axis=1, keepdims=True)


def _attn_specs(t, tq):
    qspec = pl.BlockSpec((tq, 128), lambda h, i: (i, h))
    kspec = pl.BlockSpec((t, 128), lambda h, i: (0, h))
    vspec = pl.BlockSpec((t, 128), lambda h, i: (0, (COL_C + 2 * SB_WIDTH) // 128 + h))
    return qspec, kspec, vspec


def attn_fwd(q, k, z, *, name):
    t = q.shape[0]
    tq = _attn_tq(t)
    nd = tq // _KB
    assert nd % 2 == 0, "the key-block loop takes two blocks per pass"

    def body(q_ref, k_ref, v_ref, o_ref):
        qt = pl.program_id(1)
        first_head, m_suffix, _, row, key, is_first = _attn_consts(tq)
        qv = q_ref[...]

        def step(kb, state, causal, r0=0):
            acc, ca, cb = state
            off = pl.multiple_of(kb * _KB, _KB)
            kcat = _stack_heads(first_head, k_ref[pl.ds(off, _KB), :])
            vcat = _stack_heads(first_head, v_ref[pl.ds(off, _KB), :])
            zz = lax.dot_general(qv[r0:], kcat, NT_DIMS, preferred_element_type=F32)
            g, l1m = _sb_logits(zz, None if causal is None else causal[r0:])
            a = jnp.exp(g + _block_sums(l1m, m_suffix) + jnp.where(is_first[r0:], ca[r0:], cb[r0:]))
            if causal is not None:
                a = jnp.where(causal[r0:], a, 0.0)
            sa, sb = _pair_sums(l1m)
            pv = jnp.dot(a.astype(BF16), vcat, preferred_element_type=F32)
            return _add_rows(acc, pv, r0), _add_rows(ca, sa, r0), _add_rows(cb, sb, r0)

        c0 = jnp.zeros((tq, 1), F32)
        state = (jnp.zeros((tq, 128), F32), c0, c0)
        for d in reversed(range(nd)):
            state = step(qt * nd + d, state, key + d * _KB < row, d * _KB)
        def one_pass(s, st):
            for u in range(nd):
                st = step((qt - s) * nd - 1 - u, st, None)
            return st

        state = lax.fori_loop(0, qt, one_pass, state)
        o_ref[...] = state[0].astype(o_ref.dtype)

    qspec, kspec, vspec = _attn_specs(t, tq)
    return _pc(body, name=name, grid=(SB_WIDTH // 128, t // tq), in_specs=[qspec, kspec, vspec], out_specs=qspec,
               out_shape=_sds((t, SB_WIDTH), BF16), compiler_params=_cp("parallel", "arbitrary"))(q, k, z)


def attn_bwd(q, k, z, do, *, name):
    t = q.shape[0]
    tq = _attn_tq(t)
    nd = tq // _KB
    assert nd % 2 == 0, "the key-block loop takes two blocks per pass"
    nk = t // _KB

    def body(q_ref, k_ref, v_ref, do_ref, dq_ref, dk_ref, dv_ref, e_s, sg_s):
        qt = pl.program_id(1)
        first_head, m_suffix, m_prefix, row, key, is_first = _attn_consts(tq)

        @pl.when(qt == 0)
        def _():
            dk_ref[...] = jnp.zeros_like(dk_ref)
            dv_ref[...] = jnp.zeros_like(dv_ref)

        qv = q_ref[...]
        dov = do_ref[...]

        def halves(x):
            return jnp.where(first_head, x[:_KB], x[_KB:])

        def sweep1(kb, state, causal, r0=0):
            ca, cb = state
            off = pl.multiple_of(kb * _KB, _KB)
            kcat = _stack_heads(first_head, k_ref[pl.ds(off, _KB), :])
            vcat = _stack_heads(first_head, v_ref[pl.ds(off, _KB), :])
            zz = lax.dot_general(qv[r0:], kcat, NT_DIMS, preferred_element_type=F32)
            g, l1m = _sb_logits(zz, None if causal is None else causal[r0:])
            a = jnp.exp(g + _block_sums(l1m, m_suffix) + jnp.where(is_first[r0:], ca[r0:], cb[r0:]))
            if causal is not None:
                a = jnp.where(causal[r0:], a, 0.0)
            da = lax.dot_general(dov[r0:], vcat, NT_DIMS, preferred_element_type=F32)
            e_s[kb, r0:, :] = a * da
            sg_s[kb, r0:, :] = jnp.exp(g).astype(BF16)
            dv_ref[pl.ds(off, _KB), :] += halves(lax.dot_general(a.astype(BF16), dov[r0:], TN_DIMS,
                                                                 preferred_element_type=F32))
            sa, sb = _pair_sums(l1m)
            return _add_rows(ca, sa, r0), _add_rows(cb, sb, r0)

        def sweep2(kb, state, causal, r0=0):
            dq, pa, pb = state
            off = pl.multiple_of(kb * _KB, _KB)
            kcat = _stack_heads(first_head, k_ref[pl.ds(off, _KB), :])
            e = e_s[kb, r0:, :]
            s = sg_s[kb, r0:, :].astype(F32)
            dz = e * (1.0 - s) - (jnp.where(is_first[r0:], pa[r0:], pb[r0:]) + _block_sums(e, m_prefix)) * s
            if causal is not None:
                dz = jnp.where(causal[r0:], dz, 0.0)
            dz = dz.astype(BF16)
            dk_ref[pl.ds(off, _KB), :] += halves(lax.dot_general(dz, qv[r0:], TN_DIMS, preferred_element_type=F32))
            sa, sb = _pair_sums(e)
            return (_add_rows(dq, jnp.dot(dz, kcat, preferred_element_type=F32), r0), _add_rows(pa, sa, r0),
                    _add_rows(pb, sb, r0))

        c0 = jnp.zeros((tq, 1), F32)
        st1 = (c0, c0)
        for d in reversed(range(nd)):
            st1 = sweep1(qt * nd + d, st1, key + d * _KB < row, d * _KB)
        def pass1(s, st):
            for u in range(nd):
                st = sweep1((qt - s) * nd - 1 - u, st, None)
            return st

        def pass2(s, st):
            for u in range(nd):
                st = sweep2(s * nd + u, st, None)
            return st

        lax.fori_loop(0, qt, pass1, st1)
        st2 = lax.fori_loop(0, qt, pass2, (jnp.zeros((tq, 128), F32), c0, c0))
        for d in range(nd):
            st2 = sweep2(qt * nd + d, st2, key + d * _KB < row, d * _KB)
        dq_ref[...] = st2[0]

    qspec, kspec, vspec = _attn_specs(t, tq)
    acc = pl.BlockSpec((t, 128), lambda h, i: (0, h))
    return _pc(body, name=name, grid=(SB_WIDTH // 128, t // tq), in_specs=[qspec, kspec, vspec, qspec],
               out_specs=[qspec, acc, acc],
               out_shape=[_sds((t, SB_WIDTH), F32), _sds((t, SB_WIDTH), F32), _sds((t, SB_WIDTH), F32)],
               scratch_shapes=[pltpu.VMEM((nk, tq, _PAIR), F32), pltpu.VMEM((nk, tq, _PAIR), BF16)],
               compiler_params=_cp("parallel", "arbitrary"))(q, k, z, do)


def attn_post_bwd(z, dq, dk, dv, gq, gk, *, name):
    t = z.shape[0]
    tm = _pick(t, (256,))
    scale = 1.0 / math.sqrt(SB_HEAD_DIM)

    def body(q_ref, k_ref, dq_ref, dk_ref, dv_ref, gq_ref, gk_ref, dz_ref, dgq_ref, dgk_ref):
        ones = _group_ones()

        @pl.when(pl.program_id(0) == 0)
        def _():
            dgq_ref[...] = jnp.zeros_like(dgq_ref)
            dgk_ref[...] = jnp.zeros_like(dgk_ref)

        for idx, (src, d_ref, g_ref, dg_ref, mul) in enumerate(
                ((q_ref, dq_ref, gq_ref, dgq_ref, scale), (k_ref, dk_ref, gk_ref, dgk_ref, 1.0))):
            v = src[...].astype(F32)
            r = lax.rsqrt(_split_dot(v * v, ones) * (1.0 / SB_HEAD_DIM) + EPS)
            vh = v * r
            dn = d_ref[...] * mul
            dxh = dn * g_ref[...]
            m = _split_dot(dxh * vh, ones) * (1.0 / SB_HEAD_DIM)
            dz_ref[:, idx * SB_WIDTH:(idx + 1) * SB_WIDTH] = (r * (dxh - vh * m)).astype(dz_ref.dtype)
            s = _rowsum0(dn * vh)
            f = jnp.broadcast_to(s[:, 0:128] + s[:, 128:256] + s[:, 256:384] + s[:, 384:512], dg_ref.shape)
            dg_ref[...] += f + pltpu.roll(f, 64, 1)
        dz_ref[:, 2 * SB_WIDTH:] = dv_ref[...].astype(dz_ref.dtype)

    vec = pl.BlockSpec((1, SB_WIDTH), lambda i: (0, 0))
    row = pl.BlockSpec((tm, SB_WIDTH), lambda i: (i, 0))
    fold = pl.BlockSpec((8, 128), lambda i: (0, 0))
    return _pc(body, name=name, grid=(t // tm,),
               in_specs=[pl.BlockSpec((tm, SB_WIDTH), lambda i: (i, COL_C // SB_WIDTH)),
                         pl.BlockSpec((tm, SB_WIDTH), lambda i: (i, COL_C // SB_WIDTH + 1)), row, row, row, vec, vec],
               out_specs=[pl.BlockSpec((tm, 3 * SB_WIDTH), lambda i: (i, 0)), fold, fold],
               out_shape=[_sds((t, 3 * SB_WIDTH), BF16), _sds((8, 128), F32), _sds((8, 128), F32)],
               compiler_params=_cp("arbitrary"))(z, z, dq, dk, dv, gq, gk)


_GW = 512


def merge_fwd(z, ya, yb, yc, b_gate, *, name):
    t = z.shape[0]
    tm = _pick(t, (512, 256))

    def body(za_ref, zb_ref, zc_ref, ya_ref, yb_ref, yc_ref, bg_ref, o_ref):
        acc = jnp.zeros((tm, _GW), F32)
        for b, (zr, yr) in enumerate(((za_ref, ya_ref), (zb_ref, yb_ref), (zc_ref, yc_ref))):
            acc = acc + _sigmoid(zr[...].astype(F32) + bg_ref[b:b + 1, :]) * yr[...].astype(F32)
        o_ref[...] = acc.astype(o_ref.dtype)

    def zspec(b):
        return pl.BlockSpec((tm, _GW), lambda i, j: (i, COL_G // _GW + 2 * b + j))

    yspec = pl.BlockSpec((tm, _GW), lambda i, j: (i, j))
    return _pc(body, name=name, grid=(t // tm, D_MODEL // _GW),
               in_specs=[zspec(0), zspec(1), zspec(2), yspec, yspec, yspec, pl.BlockSpec((3, _GW), lambda i, j: (0, j))],
               out_specs=yspec, out_shape=_sds((t, D_MODEL), BF16),
               compiler_params=_cp("parallel", "parallel"))(z, z, z, ya, yb, yc, b_gate)


def merge_bwd(z, ya, yb, yc, b_gate, dm, *, name):
    t = z.shape[0]
    tm = _pick(t, (512, 256))

    def body(za_ref, zb_ref, zc_ref, ya_ref, yb_ref, yc_ref, bg_ref, dm_ref,
             dya_ref, dyb_ref, dyc_ref, dza_ref, dzb_ref, dzc_ref, dbg_ref):
        @pl.when(pl.program_id(1) == 0)
        def _():
            dbg_ref[...] = jnp.zeros_like(dbg_ref)

        dmv = dm_ref[...].astype(F32)
        for b, (zr, yr, dyr, dzr) in enumerate(((za_ref, ya_ref, dya_ref, dza_ref), (zb_ref, yb_ref, dyb_ref, dzb_ref),
                                                (zc_ref, yc_ref, dyc_ref, dzc_ref))):
            s = _sigmoid(zr[...].astype(F32) + bg_ref[b:b + 1, :])
            dyr[...] = (dmv * s).astype(dyr.dtype)
            dg = dmv * yr[...].astype(F32) * s * (1.0 - s)
            dzr[...] = dg.astype(dzr.dtype)
            dbg_ref[b:b + 1, :] += _rowsum0(dg)

    def zspec(b):
        return pl.BlockSpec((tm, _GW), lambda j, i: (i, COL_G // _GW + 2 * b + j))

    yspec = pl.BlockSpec((tm, _GW), lambda j, i: (i, j))
    bspec = pl.BlockSpec((3, _GW), lambda j, i: (0, j))
    full = _sds((t, D_MODEL), BF16)
    return _pc(body, name=name, grid=(D_MODEL // _GW, t // tm),
               in_specs=[zspec(0), zspec(1), zspec(2), yspec, yspec, yspec, bspec, yspec],
               out_specs=[yspec] * 6 + [bspec], out_shape=[full] * 6 + [_sds((3, D_MODEL), F32)],
               compiler_params=_cp("parallel", "arbitrary"))(z, z, z, ya, yb, yc, b_gate, dm)


_FW = 1408
_FH = D_FF // _FW


def _row_shifts(window, tm, offsets):
    r = lax.broadcasted_iota(jnp.int32, (tm, window.shape[0]), 0)
    c = lax.broadcasted_iota(jnp.int32, (tm, window.shape[0]), 1)
    return [jnp.dot((c == r + o).astype(BF16), window, preferred_element_type=F32) for o in offsets]


def _ffn_taps(m_ref, h_ref, i, tm):
    main, halo = m_ref[...], h_ref[...]
    window = jnp.concatenate([jnp.where(i > 0, halo, jnp.zeros_like(halo)), main], axis=0)
    return _row_shifts(window, tm, [FFN_HALO - 2, FFN_HALO - 1]) + [main.astype(F32)]


def _ffn_taps_vpu(m_ref, h_ref, x_s, i, tm):
    x_s[0:FFN_HALO, :] = jnp.where(i > 0, h_ref[...].astype(F32), 0.0)
    x_s[FFN_HALO:FFN_HALO + tm, :] = m_ref[...].astype(F32)
    return [x_s[pl.ds(FFN_HALO - (FFN_KERNEL - 1) + k, tm), :] for k in range(FFN_KERNEL)]


def _ffn_conv(taps, w_ref, b_ref):
    acc = jnp.zeros_like(taps[0]) + b_ref[...]
    for k in range(FFN_KERNEL):
        acc = acc + w_ref[k:k + 1, :] * taps[k]
    return acc


def ffn_mid_fwd(up, cw, cb, *, name):
    t = up.shape[0]
    tm = _pick(t, (256,))

    def body(gm_ref, gh_ref, vm_ref, vh_ref, wg_ref, wv_ref, bg_ref, bv_ref, o_ref):
        i = pl.program_id(0)
        gate = _ffn_conv(_ffn_taps(gm_ref, gh_ref, i, tm), wg_ref, bg_ref)
        val = _ffn_conv(_ffn_taps(vm_ref, vh_ref, i, tm), wv_ref, bv_ref)
        o_ref[...] = (_silu(gate) * val).astype(o_ref.dtype)

    def main(off):
        return pl.BlockSpec((tm, _FW), lambda i, j: (i, j + off))

    def halo(off):
        return pl.BlockSpec((FFN_HALO, _FW), lambda i, j: (jnp.maximum(i * (tm // FFN_HALO) - 1, 0), j + off))

    def wspec(off):
        return pl.BlockSpec((FFN_KERNEL, _FW), lambda i, j: (0, j + off))

    def bspec(off):
        return pl.BlockSpec((1, _FW), lambda i, j: (0, j + off))

    return _pc(body, name=name, grid=(t // tm, _FH),
               in_specs=[main(0), halo(0), main(_FH), halo(_FH), wspec(0), wspec(_FH), bspec(0), bspec(_FH)],
               out_specs=pl.BlockSpec((tm, _FW), lambda i, j: (i, j)), out_shape=_sds((t, D_FF), BF16),
               compiler_params=_cp("parallel", "parallel"))(up, up, up, up, cw, cw, cb, cb)


def ffn_mid_bwd1(up, dact, cw, cb, *, name):
    t = up.shape[0]
    tm = _pick(t, (256,))

    def body(gm_ref, gh_ref, vm_ref, vh_ref, da_ref, wg_ref, wv_ref, bg_ref, bv_ref, d_ref, dw_ref, db_ref, xg_s, xv_s):
        j = pl.program_id(0)
        i = pl.program_id(1)

        @pl.when(i == 0)
        def _():
            dw_ref[...] = jnp.zeros_like(dw_ref)
            db_ref[...] = jnp.zeros_like(db_ref)

        gate_taps = _ffn_taps_vpu(gm_ref, gh_ref, xg_s, i, tm)
        val_taps = _ffn_taps_vpu(vm_ref, vh_ref, xv_s, i, tm)
        gate = _ffn_conv(gate_taps, wg_ref, bg_ref)
        da = da_ref[...].astype(F32)

        def finish(d, taps):
            d_ref[...] = d.astype(d_ref.dtype)
            db_ref[...] += _rowsum0(d)
            for k in range(FFN_KERNEL):
                dw_ref[k:k + 1, :] += _rowsum0(d * taps[k])

        @pl.when(j < _FH)
        def _():
            finish(da * _ffn_conv(val_taps, wv_ref, bv_ref) * _silu_grad(gate), gate_taps)

        @pl.when(j >= _FH)
        def _():
            finish(da * _silu(gate), val_taps)

    def main(off):
        return pl.BlockSpec((tm, _FW), lambda j, i: (i, j % _FH + off))

    def halo(off):
        return pl.BlockSpec((FFN_HALO, _FW), lambda j, i: (jnp.maximum(i * (tm // FFN_HALO) - 1, 0), j % _FH + off))

    def wspec(off):
        return pl.BlockSpec((FFN_KERNEL, _FW), lambda j, i: (0, j % _FH + off))

    def bspec(off):
        return pl.BlockSpec((1, _FW), lambda j, i: (0, j % _FH + off))

    return _pc(body, name=name, grid=(2 * _FH, t // tm),
               in_specs=[main(0), halo(0), main(_FH), halo(_FH), pl.BlockSpec((tm, _FW), lambda j, i: (i, j % _FH)),
                         wspec(0), wspec(_FH), bspec(0), bspec(_FH)],
               out_specs=[pl.BlockSpec((tm, _FW), lambda j, i: (i, j)), pl.BlockSpec((FFN_KERNEL, _FW), lambda j, i: (0, j)),
                          pl.BlockSpec((1, _FW), lambda j, i: (0, j))],
               out_shape=[_sds((t, 2 * D_FF), BF16), _sds((FFN_KERNEL, 2 * D_FF), F32), _sds((1, 2 * D_FF), F32)],
               scratch_shapes=[pltpu.VMEM((FFN_HALO + tm, _FW), F32), pltpu.VMEM((FFN_HALO + tm, _FW), F32)],
               compiler_params=_cp("parallel", "arbitrary"))(up, up, up, up, dact, cw, cw, cb, cb)


def ffn_mid_bwd2(dupc, cw, *, name):
    t = dupc.shape[0]
    tm = _pick(t, (256,))
    nsteps = t // tm
    last = t // FFN_HALO - 1

    def body(m_ref, h_ref, w_ref, o_ref):
        i = pl.program_id(0)
        main, halo = m_ref[...], h_ref[...]
        window = jnp.concatenate([main, jnp.where(i < nsteps - 1, halo, jnp.zeros_like(halo))], axis=0)
        taps = _row_shifts(window, tm, [2, 1]) + [main.astype(F32)]
        acc = jnp.zeros((tm, _FW), F32)
        for k in range(FFN_KERNEL):
            acc = acc + w_ref[k:k + 1, :] * taps[k]
        o_ref[...] = acc.astype(o_ref.dtype)

    return _pc(body, name=name, grid=(nsteps, 2 * _FH),
               in_specs=[pl.BlockSpec((tm, _FW), lambda i, j: (i, j)),
                         pl.BlockSpec((FFN_HALO, _FW), lambda i, j: (jnp.minimum((i + 1) * (tm // FFN_HALO), last), j)),
                         pl.BlockSpec((FFN_KERNEL, _FW), lambda i, j: (0, j))],
               out_specs=pl.BlockSpec((tm, _FW), lambda i, j: (i, j)), out_shape=_sds((t, 2 * D_FF), BF16),
               compiler_params=_cp("parallel", "parallel"))(dupc, dupc, cw)


def _vec(v):
    return v.reshape(1, -1)


def _layer_consts(p):
    return dict(
        sg_bias=jnp.repeat(p['sg_b'].T, SB_HEAD_DIM, axis=1),
        sg_wt=jnp.swapaxes(p['sg_w'], 1, 2),
        gq=jnp.tile(p['q_norm_g'], SB_WIDTH // SB_HEAD_DIM).reshape(1, -1),
        gk=jnp.tile(p['k_norm_g'], SB_WIDTH // SB_HEAD_DIM).reshape(1, -1),
    )


def layer_fwd(x, p, after=(), rest=None):
    c = _layer_consts(p)
    z, h = mm_norm_nn(x, _vec(p['ln1_g']), p['w_in'], name="in_proj", after=after)
    ga = mixa_fwd(z, _vec(p['sg_ln_g']), _vec(p['sg_ln_b']), p['sg_w'], c['sg_bias'], name="mixa_fwd")
    cb = mixb_fwd(z, p['cv_w'], _vec(p['cv_b']), _vec(p['cv_ln_g']), _vec(p['cv_ln_b']), name="mixb_fwd")
    q, k = attn_prep(z, c['gq'], c['gk'], name="attn_prep")
    ao = attn_fwd(q, k, z, name="attn_fwd")
    if rest is not None:
        p = {**p, **rest([ao])}
    ya, yb, yc = mm_branches("nn", [ga, cb, ao], [p['w_a_out'], p['w_b_out'], p['w_c_out']], name="abc_out")
    merged = merge_fwd(z, ya, yb, yc, p['b_gate'], name="merge_fwd")
    x1 = mm_nn(merged, p['w_out'], res=x, out_dtype=F32, name="out_proj")
    up, h2 = mm_norm_nn(x1, _vec(p['ln2_g']), p['w_up'], name="up_proj")
    act = ffn_mid_fwd(up, p['ffn_conv_w'], _vec(p['ffn_conv_b']), name="ffn_mid_fwd")
    x2 = mm_nn(act, p['w_down'], res=x1, out_dtype=F32, name="down_proj")
    saved = dict(x=x, z=z, h=h, ga=ga, cb=cb, q=q, k=k, ao=ao, ya=ya, yb=yb, yc=yc, merged=merged, x1=x1, up=up,
                 h2=h2, act=act)
    return x2, saved, p


GRAD_GROUPS = (('w_down', 'w_up'), ('w_out', 'w_a_out', 'w_b_out', 'w_c_out'), ('w_in',))


def layer_bwd(dx2, p, s, after=(), emit=None):
    c = _layer_consts(p)
    g = {}
    emit = emit or (lambda names, grads: ())
    g['w_down'] = mm_tn(s['act'], dx2, name="d_w_down")
    dact = mm_nt(dx2, p['w_down'], out_dtype=BF16, name="d_act", after=after)
    dupc, g['ffn_conv_w'], dcb = ffn_mid_bwd1(s['up'], dact, p['ffn_conv_w'], _vec(p['ffn_conv_b']), name="ffn_mid_bwd1")
    g['ffn_conv_b'] = dcb.reshape(-1)
    dup = ffn_mid_bwd2(dupc, p['ffn_conv_w'], name="ffn_mid_bwd2")
    g['w_up'] = mm_tn(s['h2'], dup, name="d_w_up")
    dh2 = mm_nt(dup, p['w_up'], out_dtype=F32, name="d_h2")
    dx1, dg2 = rms_bwd(dh2, s['x1'], _vec(p['ln2_g']), dx2, name="ln2_bwd")
    g['ln2_g'] = dg2.reshape(-1)
    g['w_out'] = mm_tn(s['merged'], dx1, name="d_w_out")
    dm = mm_nt(dx1, p['w_out'], out_dtype=BF16, name="d_merged", after=emit(GRAD_GROUPS[0], g))
    dya, dyb, dyc, dzg0, dzg1, dzg2, g['b_gate'] = merge_bwd(s['z'], s['ya'], s['yb'], s['yc'], p['b_gate'], dm,
                                                             name="merge_bwd")
    g['w_a_out'], g['w_b_out'], g['w_c_out'] = mm_branches("tn", [s['ga'], s['cb'], s['ao']], [dya, dyb, dyc],
                                                           name="d_w_abc_out")
    dga, dcb3, dao = mm_branches("nt", [dya, dyb, dyc], [p['w_a_out'], p['w_b_out'], p['w_c_out']], name="d_abc")
    dza, g['sg_w'], dsgb, dlg, dlb = mixa_bwd(s['z'], dga, _vec(p['sg_ln_g']), _vec(p['sg_ln_b']), p['sg_w'],
                                               c['sg_wt'], c['sg_bias'], name="mixa_bwd")
    g['sg_b'] = dsgb[:, :SG_WIDTH // SB_HEAD_DIM].T
    g['sg_ln_g'] = dlg.reshape(-1)
    g['sg_ln_b'] = dlb.reshape(-1)
    dc1, g['cv_w'], dcvb, dcg, dcbb = mixb_bwd1(s['z'], dcb3, p['cv_w'], _vec(p['cv_b']), _vec(p['cv_ln_g']),
                                                _vec(p['cv_ln_b']), name="mixb_bwd1")
    g['cv_b'] = dcvb.reshape(-1)
    g['cv_ln_g'] = dcg.reshape(-1)
    g['cv_ln_b'] = dcbb.reshape(-1)
    dzb = mixb_bwd2(s['z'], dc1, p['cv_w'], name="mixb_bwd2")
    dq, dk, dv = attn_bwd(s['q'], s['k'], s['z'], dao, name="attn_bwd")
    dzc, dgq, dgk = attn_post_bwd(s['z'], dq, dk, dv, c['gq'], c['gk'], name="attn_post_bwd")
    g['q_norm_g'] = dgq[0, :SB_HEAD_DIM]
    g['k_norm_g'] = dgk[0, :SB_HEAD_DIM]
    dz = jnp.concatenate([dza, dzb, dzc, dzg0, dzg1, dzg2], axis=1)
    g['w_in'] = mm_tn(s['h'], dz, name="d_w_in")
    dh = mm_nt(dz, p['w_in'], out_dtype=F32, name="d_h", after=emit(GRAD_GROUPS[1], g))
    dx, dg1 = rms_bwd(dh, s['x'], _vec(p['ln1_g']), dx1, name="ln1_bwd")
    g['ln1_g'] = dg1.reshape(-1)
    return dx, g


def local_step(x, target, depth, get_layer, on_grads, emit=lambda l: None):
    saved, layers = [], []
    for l in range(depth):
        p, after, rest = get_layer(l, x)
        x, s, p = layer_fwd(x, p, after, rest)
        layers.append(p)
        saved.append(s)
    loss, dx = loss_head(x, target, name="loss_head")
    after = ()
    for l in reversed(range(depth)):
        dx, g = layer_bwd(dx, layers[l], saved[l], after, emit(l))
        after = on_grads(l, g)
    return loss[0, 0], dx


def adamw(w, g, m, v, *, name):
    r, c = w.shape
    tr = _pick(r, (256, 704)) if r * c > 512 * 1024 else r

    def body(w_ref, g_ref, m_ref, v_ref, d_ref, mo_ref, vo_ref):
        gv = g_ref[...]
        mn = ADAM_B1 * m_ref[...] + (1.0 - ADAM_B1) * gv
        vn = ADAM_B2 * v_ref[...] + (1.0 - ADAM_B2) * (gv * gv)
        m_hat = mn / (1.0 - ADAM_B1 ** ADAM_STEP)
        v_hat = vn / (1.0 - ADAM_B2 ** ADAM_STEP)
        d_ref[...] = -ADAM_LR * (m_hat / (jnp.sqrt(v_hat) + ADAM_EPS) + ADAM_WD * w_ref[...])
        mo_ref[...] = mn
        vo_ref[...] = vn

    spec = pl.BlockSpec((tr, c), lambda i: (i, 0))
    out = _sds((r, c), F32)
    return _pc(body, name=name, grid=(r // tr,), in_specs=[spec] * 4, out_specs=[spec] * 3, out_shape=[out] * 3,
               compiler_params=_cp("parallel"))(w, g, m, v)


def _as3(a):
    return a if a.ndim == 3 else a.reshape((1,) + a.shape)


def add_half(g, recv, c_idx, *, name):
    s, rh, w = recv.shape
    tr = _pick(rh, (256, 352, 128))
    nb = rh // tr

    def body(c_ref, g_ref, r_ref, o_ref):
        o_ref[...] = (g_ref[...].astype(F32) + r_ref[...].astype(F32)).astype(o_ref.dtype)

    own = pl.BlockSpec((1, tr, w), lambda k, i, c_ref: (k, c_ref[0] * nb + i, 0))
    half = pl.BlockSpec((1, tr, w), lambda k, i, c_ref: (k, i, 0))
    gs = pltpu.PrefetchScalarGridSpec(num_scalar_prefetch=1, grid=(s, nb), in_specs=[own, half], out_specs=half)
    return _pc(body, name=name, grid_spec=gs, out_shape=_sds((s, rh, w), BF16),
               compiler_params=_cp("parallel", "parallel"))(c_idx, g, recv)


def sum_shard(p, recv, pos_idx, stacked, layer, *, by_rows, name):
    _, rh, w = recv.shape
    tr = _pick(rh, (256, 352, 128))
    nb = rh // tr

    def body(pos_ref, p_ref, r_ref, s_ref, o_ref):
        acc = p_ref[0].astype(F32)
        for j in range(N_CHIPS - 1):
            acc = acc + r_ref[j].astype(F32)
        o_ref[0] = acc

    if by_rows:
        own = pl.BlockSpec((1, tr, w), lambda i, pos_ref: (pos_ref[0], i, 0))
    else:
        own = pl.BlockSpec((1, tr, w), lambda i, pos_ref: (0, i, pos_ref[0]))
    gs = pltpu.PrefetchScalarGridSpec(num_scalar_prefetch=1, grid=(nb,),
                                      in_specs=[own, pl.BlockSpec((N_CHIPS - 1, tr, w), lambda i, pos_ref: (0, i, 0)), ANY],
                                      out_specs=pl.BlockSpec((1, tr, w), lambda i, pos_ref: (layer, pos_ref[1] * nb + i, 0)))
    return _pc(body, name=name, grid_spec=gs, out_shape=_sds(stacked.shape, F32), input_output_aliases={3: 0},
               compiler_params=_cp("parallel"))(pos_idx, p, recv, stacked)


def sum_slots(slab, *, name):
    _, r, w = slab.shape
    tr = _pick(r, (512, 256, 8))

    def body(s_ref, o_ref):
        acc = s_ref[0]
        for j in range(1, N_DEV):
            acc = acc + s_ref[j]
        o_ref[...] = acc

    return _pc(body, name=name, grid=(r // tr,), in_specs=[pl.BlockSpec((N_DEV, tr, w), lambda i: (0, i, 0))],
               out_specs=pl.BlockSpec((tr, w), lambda i: (i, 0)), out_shape=_sds((r, w), F32),
               compiler_params=_cp("parallel"))(slab)


def _mesh_pos():
    x, y, c = lax.axis_index("x"), lax.axis_index("y"), lax.axis_index("c")
    others = [(1 - x, y), (x, 1 - y), (1 - x, 1 - y)]
    return x, y, c, others


def _rcopy(src, dst, ssem, rsem, k, dev):
    return pltpu.make_async_remote_copy(src_ref=src, dst_ref=dst, send_sem=ssem.at[k], recv_sem=rsem.at[k],
                                        device_id=dev, device_id_type=MESH)


def _comm_call(body, name, n_in, out_shape, n_local, n_remote):
    scratch = [pltpu.SemaphoreType.DMA((max(n_local, 1),)), pltpu.SemaphoreType.DMA((n_remote,)),
               pltpu.SemaphoreType.DMA((n_remote,))]
    return _pc(body, name=name, in_specs=[ANY] * n_in, out_specs=[ANY] * len(out_shape), out_shape=out_shape,
               scratch_shapes=scratch)


GATHERED = BIG + SMALL_COL
HBM_SPEC = pl.BlockSpec(memory_space=pltpu.HBM)
SEM_SPEC = pl.BlockSpec(memory_space=pltpu.SEMAPHORE)
TOKEN_SHAPE = (8, 128)


def place_block(w, layer, pos_idx, *, by_rows, dtype, name):
    _, r, c = w.shape
    tr = _pick(r, (512, 704, 256))

    def body(pos_ref, w_ref, o_ref):
        if by_rows:
            o_ref[0] = w_ref[0].astype(dtype)
        else:
            o_ref[...] = w_ref[0].astype(dtype)

    if by_rows:
        out_spec, shape = pl.BlockSpec((1, tr, c), lambda i, pos_ref: (pos_ref[0], i, 0)), (N_CHIPS, r, c)
    else:
        out_spec, shape = pl.BlockSpec((tr, c), lambda i, pos_ref: (i, pos_ref[0])), (r, N_CHIPS * c)
    gs = pltpu.PrefetchScalarGridSpec(num_scalar_prefetch=1, grid=(r // tr,),
                                      in_specs=[pl.BlockSpec((1, tr, c), lambda i, pos_ref: (layer, i, 0))],
                                      out_specs=out_spec)
    return _pc(body, name=name, grid_spec=gs, out_shape=_sds(shape, dtype), compiler_params=_cp("parallel"))(pos_idx, w)


def _gather_windows(bufs):
    names, arrs = list(bufs), list(bufs.values())

    def dwin(refs, i, k, h):
        if names[i] in BIG_ROW:
            _, r, _ = arrs[i].shape
            return refs[i].at[k] if h is None else refs[i].at[k, pl.ds(h * (r // 2), r // 2), :]
        r, cs = arrs[i].shape[0], arrs[i].shape[1] // N_CHIPS
        cols = pl.ds(pl.multiple_of(k * cs, 128), cs)
        return refs[i].at[:, cols] if h is None else refs[i].at[pl.ds(h * (r // 2), r // 2), cols]

    def swin(refs, i, h):
        x, y, _, _ = _mesh_pos()
        return dwin(refs, i, 2 * x + y, h)

    return names, dwin, swin


def _gather_send(names, ins, outs, ssem, rsem, dwin, swin, stride):
    x, y, c, others = _mesh_pos()
    sends = []
    for i, n in enumerate(names):
        h = c if n in BIG else None
        for j, chip in enumerate(others):
            cp = _rcopy(swin(ins, i, h), swin(outs, i, h), ssem, rsem, stride * i + j, (*chip, c))
            cp.start()
            sends.append(cp)
    return sends


def _gather_pass_on(names, outs, ssem, rsem, dwin, stride, first_off, pass_off):
    x, y, c, others = _mesh_pos()
    sib = (x, y, 1 - c)
    sends = []
    for j, chip in enumerate(others):
        kk = 2 * chip[0] + chip[1]
        for i, n in enumerate(names):
            got = dwin(outs, i, kk, c if n in BIG else None)
            if first_off is not None:
                _rcopy(got, got, ssem, rsem, stride * i + first_off + j, (*chip, c)).wait_recv()
            if n in BIG:
                fwd = _rcopy(got, got, ssem, rsem, stride * i + pass_off + j, sib)
                fwd.start()
                sends.append(fwd)
    for j, chip in enumerate(others):
        kk = 2 * chip[0] + chip[1]
        for i, n in enumerate(names):
            if n in BIG:
                got = dwin(outs, i, kk, 1 - c)
                _rcopy(got, got, ssem, rsem, stride * i + pass_off + j, sib).wait_recv()
    return sends


def _as_weights(bufs):
    return {n: (o.reshape(o.shape[0] * o.shape[1], o.shape[2]) if n in BIG_ROW else o) for n, o in bufs.items()}


def _comm_in_place(body, name, bufs, n_sems):
    nn = len(bufs)
    arrs = list(bufs.values())
    scratch = [pltpu.SemaphoreType.DMA((n_sems,)), pltpu.SemaphoreType.DMA((n_sems,))]
    outs = _pc(body, name=name, in_specs=[ANY] * nn, out_specs=[ANY] * nn, out_shape=[_sds(a.shape, a.dtype) for a in arrs],
               scratch_shapes=scratch, input_output_aliases={i: i for i in range(nn)})(*arrs)
    return dict(zip(bufs, outs))


def allgather_weights(bufs, *, name):
    nn = len(bufs)
    names, dwin, swin = _gather_windows(bufs)

    def body(*refs):
        ins, outs = refs[:nn], refs[nn:2 * nn]
        ssem, rsem = refs[2 * nn:]
        sends = _gather_send(names, ins, outs, ssem, rsem, dwin, swin, 6)
        sends += _gather_pass_on(names, outs, ssem, rsem, dwin, 6, 0, 3)
        for cp in sends:
            cp.wait_send()

    return _comm_in_place(body, name, bufs, 6 * nn)


def _split_start(body, name, bufs, extra_in, n_sems):
    nn = len(bufs)
    arrs = [pltpu.with_memory_space_constraint(a, pltpu.HBM) for a in bufs.values()]
    out_shape = ([pltpu.SemaphoreType.DMA((n_sems,)), pltpu.SemaphoreType.DMA((n_sems,))]
                 + [pltpu.HBM(a.shape, a.dtype) for a in arrs] + [_sds(TOKEN_SHAPE, F32)])
    outs = _pc(body, name=name, in_specs=[HBM_SPEC] * nn + [ANY] * len(extra_in),
               out_specs=[SEM_SPEC, SEM_SPEC] + [HBM_SPEC] * nn + [pl.BlockSpec(memory_space=pltpu.VMEM)],
               out_shape=out_shape, input_output_aliases={i: 2 + i for i in range(nn)},
               compiler_params=pltpu.CompilerParams(has_side_effects=pltpu.SideEffectType.DATAFLOW_SIDE_EFFECTING),
               )(*arrs, *extra_in)
    return dict(ssem=outs[0], rsem=outs[1], bufs=dict(zip(bufs, outs[2:2 + nn])), token=outs[-1])


def _split_wait(body, name, handle, after):
    bufs = handle['bufs']
    nn = len(bufs)
    arrs = list(bufs.values())
    outs = _pc(body, name=name, in_specs=[HBM_SPEC] * nn + [SEM_SPEC, SEM_SPEC] + [ANY] * len(after),
               out_specs=[HBM_SPEC] * nn, out_shape=[pltpu.HBM(a.shape, a.dtype) for a in arrs],
               input_output_aliases={i: i for i in range(nn)},
               compiler_params=pltpu.CompilerParams(has_side_effects=pltpu.SideEffectType.DATAFLOW_SIDE_EFFECTING),
               )(*arrs, handle['ssem'], handle['rsem'], *after)
    return dict(zip(bufs, outs))


def gather_start(bufs, after, *, name):
    nn = len(bufs)
    names, dwin, swin = _gather_windows(bufs)

    def body(*refs):
        ins = refs[:nn]
        ssem, rsem = refs[nn + len(after)], refs[nn + len(after) + 1]
        _gather_send(names, ins, ins, ssem, rsem, dwin, swin, 3)
        refs[-1][...] = jnp.zeros(TOKEN_SHAPE, F32)

    return _split_start(body, name, bufs, after, 3 * nn)


def gather_wait(handle, after, *, name):
    nn = len(handle['bufs'])
    names, dwin, swin = _gather_windows(handle['bufs'])

    def body(*refs):
        ins = refs[:nn]
        ssem, rsem = refs[nn], refs[nn + 1]
        x, y, c, others = _mesh_pos()
        for i, n in enumerate(names):
            h = c if n in BIG else None
            for j, chip in enumerate(others):
                kk = 2 * chip[0] + chip[1]
                cp = _rcopy(swin(ins, i, h), dwin(ins, i, kk, h), ssem, rsem, 3 * i + j, (*chip, c))
                cp.wait_send()
                cp.wait_recv()

    return _split_wait(body, name, handle, after)


def gather_finish(bufs, *, name):
    nn = len(bufs)
    names, dwin, _ = _gather_windows(bufs)

    def body(*refs):
        outs = refs[nn:2 * nn]
        ssem, rsem = refs[2 * nn:]
        for cp in _gather_pass_on(names, outs, ssem, rsem, dwin, 3, None, 0):
            cp.wait_send()

    return _comm_in_place(body, name, bufs, 3 * nn)


def _grad_view(n, g):
    return g.reshape(N_CHIPS, g.shape[0] // N_CHIPS, g.shape[1]) if n in BIG_ROW else g.reshape((1,) + g.shape)


def exchange_halves(gv, *, name):
    nn = len(gv)
    arrs = list(gv.values())

    def body(*refs):
        ins, outs = refs[:nn], refs[nn:2 * nn]
        _, ssem, rsem = refs[2 * nn:]
        x, y, c, _ = _mesh_pos()
        cps = []
        for i in range(nn):
            rh = arrs[i].shape[1] // 2
            cp = _rcopy(ins[i].at[:, pl.ds((1 - c) * rh, rh), :], outs[i], ssem, rsem, i, (x, y, 1 - c))
            cp.start()
            cps.append(cp)
        for cp in cps:
            cp.wait()

    out_shape = [_sds((a.shape[0], a.shape[1] // 2, a.shape[2]), a.dtype) for a in arrs]
    return dict(zip(gv, _comm_call(body, name, nn, out_shape, 0, nn)(*arrs)))


def _exchange_copies(gv, ins, lands, ssem, rsem):
    x, y, c, _ = _mesh_pos()
    cps = []
    for i, a in enumerate(gv.values()):
        rh = a.shape[1] // 2
        cps.append(_rcopy(ins[i].at[:, pl.ds((1 - c) * rh, rh), :], lands[i], ssem, rsem, i, (x, y, 1 - c)))
    return cps


def exchange_start(gv, *, name):
    nn = len(gv)

    def body(*refs):
        for cp in _exchange_copies(gv, refs[:nn], refs[nn:2 * nn], refs[2 * nn], refs[2 * nn + 1]):
            cp.start()
        refs[-1][...] = jnp.zeros(TOKEN_SHAPE, F32)

    lands = {n + _RECV: lax.empty((a.shape[0], a.shape[1] // 2, a.shape[2]), a.dtype) for n, a in gv.items()}
    return _split_start(body, name, {**gv, **lands}, (), nn)


def exchange_wait(handle, after, *, name):
    nn = len(handle['bufs']) // 2
    gv = dict(list(handle['bufs'].items())[:nn])

    def body(*refs):
        for cp in _exchange_copies(gv, refs[:nn], refs[nn:2 * nn], refs[2 * nn], refs[2 * nn + 1]):
            cp.wait_send()
            cp.wait_recv()

    outs = _split_wait(body, name, handle, after)
    return {n: outs[n] for n in gv}, {n: outs[n + _RECV] for n in gv}


def _shard_shape(n, p):
    _, rh, w = p.shape
    return (rh, w) if n in BIG_ROW else (rh, w // N_CHIPS)


def _scatter_copies(pv, ins, outs, ssem, rsem):
    x, y, c, others = _mesh_pos()
    cps = []
    for i, (n, p) in enumerate(pv.items()):
        _, ws = _shard_shape(n, p)
        for j, chip in enumerate(others):
            kk = 2 * chip[0] + chip[1]
            if n in BIG_ROW:
                src = ins[i].at[kk]
            else:
                src = ins[i].at[0, :, pl.ds(pl.multiple_of(kk * ws, 128), ws)]
            cps.append(_rcopy(src, outs[i].at[j], ssem, rsem, 3 * i + j, (*chip, c)))
    return cps


def _recv_shapes(pv):
    return [(N_CHIPS - 1,) + _shard_shape(n, p) for n, p in pv.items()]


def scatter_partials(pv, *, name):
    nn = len(pv)

    def body(*refs):
        ins, outs = refs[:nn], refs[nn:2 * nn]
        _, ssem, rsem = refs[2 * nn:]
        cps = _scatter_copies(pv, ins, outs, ssem, rsem)
        for cp in cps:
            cp.start()
        for cp in cps:
            cp.wait()

    out_shape = [_sds(s, p.dtype) for s, p in zip(_recv_shapes(pv), pv.values())]
    return pv, dict(zip(pv, _comm_call(body, name, nn, out_shape, 0, 3 * nn)(*pv.values())))


_RECV = "/recv"


def scatter_start(pv, *, name):
    nn = len(pv)

    def body(*refs):
        ins, lands = refs[:nn], refs[nn:2 * nn]
        ssem, rsem = refs[2 * nn], refs[2 * nn + 1]
        for cp in _scatter_copies(pv, ins, lands, ssem, rsem):
            cp.start()
        refs[-1][...] = jnp.zeros(TOKEN_SHAPE, F32)

    lands = {n + _RECV: lax.empty(s, p.dtype) for (n, p), s in zip(pv.items(), _recv_shapes(pv))}
    return _split_start(body, name, {**pv, **lands}, (), 3 * nn)


def scatter_wait(handle, after, *, name):
    nn = len(handle['bufs']) // 2
    pv = dict(list(handle['bufs'].items())[:nn])

    def body(*refs):
        ins, zones = refs[:nn], refs[nn:2 * nn]
        ssem, rsem = refs[2 * nn], refs[2 * nn + 1]
        for cp in _scatter_copies(pv, ins, zones, ssem, rsem):
            cp.wait_send()
            cp.wait_recv()

    outs = _split_wait(body, name, handle, after)
    return {n: outs[n] for n in pv}, {n: outs[n + _RECV] for n in pv}


def join_halves(rv, layer, *, name):
    nn = len(rv)
    arrs = list(rv.values())

    def body(*refs):
        ins, outs = refs[:nn], refs[nn:2 * nn]
        ssem, rsem = refs[2 * nn:]
        x, y, c, _ = _mesh_pos()
        cps = []
        for i in range(nn):
            rh = arrs[i].shape[1] // 2
            rows = pl.ds(c * rh, rh)
            cp = _rcopy(ins[i].at[layer, rows, :], outs[i].at[layer, rows, :], ssem, rsem, i, (x, y, 1 - c))
            cp.start()
            cps.append(cp)
        for i, cp in enumerate(cps):
            cp.wait_send()
            rh = arrs[i].shape[1] // 2
            got = outs[i].at[layer, pl.ds((1 - c) * rh, rh), :]
            _rcopy(got, got, ssem, rsem, i, (x, y, 1 - c)).wait_recv()

    return _comm_in_place(body, name, rv, nn)


def chip_partials(grads, names, c_idx):
    gv = {n: _grad_view(n, grads[n]) for n in names}
    recv = exchange_halves(gv, name="rs_exchange_halves")
    return {n: add_half(gv[n], recv[n], c_idx, name="rs_add_" + n) for n in names}


def reduce_shards(pv, got, pos_idx, stacked, layer):
    rv = {n: sum_shard(pv[n], got[n], pos_idx, stacked[n], layer, by_rows=n in BIG_ROW, name="rs_sum_" + n) for n in pv}
    return join_halves(rv, layer, name="rs_join_halves")


def _slab_first(ref, ssem, rsem):
    x, y, c, others = _mesh_pos()
    mine = ref.at[4 * x + 2 * y + c]
    peers = [(x, y, 1 - c)] + [(*chip, c) for chip in others]
    out = []
    for k, p in enumerate(peers):
        got = ref.at[4 * p[0] + 2 * p[1] + p[2]]
        out.append((_rcopy(mine, mine, ssem, rsem, k, p), _rcopy(got, got, ssem, rsem, k, p)))
    return out


def slab_start(slab, *, name):
    def body(ref, ssem, rsem, thru, token):
        for cp, _ in _slab_first(ref, ssem, rsem):
            cp.start()
        token[...] = jnp.zeros(TOKEN_SHAPE, F32)

    return _split_start(body, name, {'slab': slab}, (), 4)


def slab_wait(handle, after, *, name):
    def body(ref, ssem, rsem, *rest):
        for sent, landed in _slab_first(ref, ssem, rsem):
            sent.wait_send()
            landed.wait_recv()

    return _split_wait(body, name, handle, after)['slab']


def slab_finish(slab, *, name):
    def body(in_ref, out_ref, ssem, rsem):
        x, y, c, others = _mesh_pos()
        sib = (x, y, 1 - c)
        sends = []
        for j, chip in enumerate(others):
            got = out_ref.at[4 * chip[0] + 2 * chip[1] + c]
            sends.append(_rcopy(got, got, ssem, rsem, j, sib))
            sends[-1].start()
        for j, chip in enumerate(others):
            got = out_ref.at[4 * chip[0] + 2 * chip[1] + 1 - c]
            _rcopy(got, got, ssem, rsem, j, sib).wait_recv()
        for cp in sends:
            cp.wait_send()

    return _comm_in_place(body, name, {'slab': slab}, 3)['slab']


def _pad128(n):
    return -(-n // 128) * 128


def _pack_small(grads, shapes):
    parts = []
    for g in grads:
        for n in SMALL:
            v = g[n].astype(F32).reshape(-1)
            parts.append(jnp.pad(v, (0, _pad128(v.shape[0]) - v.shape[0])))
    flat = jnp.concatenate(parts)
    rows = -(-flat.shape[0] // (128 * 512)) * 512
    return jnp.pad(flat, (0, rows * 128 - flat.shape[0])).reshape(rows, 128)


def _unpack_small(slab, shapes, depth):
    flat = slab.reshape(-1)
    out = {n: [] for n in SMALL}
    off = 0
    for _ in range(depth):
        for n in SMALL:
            size = math.prod(shapes[n])
            out[n].append(flat[off:off + size].reshape(shapes[n]))
            off += _pad128(size)
    return {n: jnp.stack(v) for n, v in out.items()}


def _adamw_nd(w, g, m, v, name):
    shp = w.shape
    two = lambda a: a.reshape(-1, shp[-1])
    return tuple(o.reshape(shp) for o in adamw(two(w), two(g), two(m), two(v), name=name))


def kernel(x, ln1_g, w_in, b_gate, sg_ln_g, sg_ln_b, sg_w, sg_b, w_a_out, cv_w, cv_b, cv_ln_g, cv_ln_b, w_b_out, q_norm_g, k_norm_g, w_c_out, w_out, ln2_g, w_up, ffn_conv_w, ffn_conv_b, w_down, loss_target, m_ln1_g, m_w_in, m_b_gate, m_sg_ln_g, m_sg_ln_b, m_sg_w, m_sg_b, m_w_a_out, m_cv_w, m_cv_b, m_cv_ln_g, m_cv_ln_b, m_w_b_out, m_q_norm_g, m_k_norm_g, m_w_c_out, m_w_out, m_ln2_g, m_w_up, m_ffn_conv_w, m_ffn_conv_b, m_w_down, v_ln1_g, v_w_in, v_b_gate, v_sg_ln_g, v_sg_ln_b, v_sg_w, v_sg_b, v_w_a_out, v_cv_w, v_cv_b, v_cv_ln_g, v_cv_ln_b, v_w_b_out, v_q_norm_g, v_k_norm_g, v_w_c_out, v_w_out, v_ln2_g, v_w_up, v_ffn_conv_w, v_ffn_conv_b, v_w_down):
    w = dict(ln1_g=ln1_g, w_in=w_in, b_gate=b_gate, sg_ln_g=sg_ln_g, sg_ln_b=sg_ln_b, sg_w=sg_w, sg_b=sg_b,
             w_a_out=w_a_out, cv_w=cv_w, cv_b=cv_b, cv_ln_g=cv_ln_g, cv_ln_b=cv_ln_b, w_b_out=w_b_out,
             q_norm_g=q_norm_g, k_norm_g=k_norm_g, w_c_out=w_c_out, w_out=w_out, ln2_g=ln2_g, w_up=w_up,
             ffn_conv_w=ffn_conv_w, ffn_conv_b=ffn_conv_b, w_down=w_down)
    m = dict(ln1_g=m_ln1_g, w_in=m_w_in, b_gate=m_b_gate, sg_ln_g=m_sg_ln_g, sg_ln_b=m_sg_ln_b, sg_w=m_sg_w,
             sg_b=m_sg_b, w_a_out=m_w_a_out, cv_w=m_cv_w, cv_b=m_cv_b, cv_ln_g=m_cv_ln_g, cv_ln_b=m_cv_ln_b,
             w_b_out=m_w_b_out, q_norm_g=m_q_norm_g, k_norm_g=m_k_norm_g, w_c_out=m_w_c_out, w_out=m_w_out,
             ln2_g=m_ln2_g, w_up=m_w_up, ffn_conv_w=m_ffn_conv_w, ffn_conv_b=m_ffn_conv_b, w_down=m_w_down)
    v = dict(ln1_g=v_ln1_g, w_in=v_w_in, b_gate=v_b_gate, sg_ln_g=v_sg_ln_g, sg_ln_b=v_sg_ln_b, sg_w=v_sg_w,
             sg_b=v_sg_b, w_a_out=v_w_a_out, cv_w=v_cv_w, cv_b=v_cv_b, cv_ln_g=v_cv_ln_g, cv_ln_b=v_cv_ln_b,
             w_b_out=v_w_b_out, q_norm_g=v_q_norm_g, k_norm_g=v_k_norm_g, w_c_out=v_w_c_out, w_out=v_w_out,
             ln2_g=v_ln2_g, w_up=v_w_up, ffn_conv_w=v_ffn_conv_w, ffn_conv_b=v_ffn_conv_b, w_down=v_w_down)
    depth = ln1_g.shape[0]
    cx, cy, cc = lax.axis_index("x"), lax.axis_index("y"), lax.axis_index("c")
    me = 2 * cx + cy
    pos_idx = jnp.stack([me, cc]).astype(jnp.int32)
    c_idx = jnp.reshape(cc, (1,)).astype(jnp.int32)

    padded = {n: jnp.pad(w[n], ((0, 0), (0, -w[n].shape[1] % 8), (0, 0))) for n in SMALL_COL}
    first, later = ['w_in'] + SMALL_COL, [n for n in BIG if n != 'w_in']

    def blocks(names, l):
        return {n: (place_block(w[n], l, pos_idx, by_rows=n in BIG_ROW, dtype=BF16, name="place_" + n) if n in BIG else
                    place_block(padded[n], l, pos_idx, by_rows=False, dtype=F32, name="place_" + n)) for n in names}

    full0 = allgather_weights(blocks(first, 0), name="allgather_weights")
    gathers = [gather_start(blocks(later, 0), [full0['w_in']], name="gather_start_0")]
    for l in range(1, depth):
        gathers.append(gather_start(blocks(GATHERED, l), [gathers[-1]['token']], name="gather_start_%d" % l))

    def arrived(l, after):
        bufs = gather_wait(gathers[l], after, name="gather_wait_%d" % l)
        return _as_weights(gather_finish(bufs, name="gather_finish_%d" % min(l, 1)))

    def get_layer(l, x_in):
        if l == 0:
            p, after, rest = _as_weights(full0), tuple(h['token'] for h in gathers), functools.partial(arrived, 0)
        else:
            p, after, rest = arrived(l, [x_in]), (), None
        for n in SMALL:
            p[n] = p[n][:w[n].shape[1]] if n in SMALL_COL else w[n][l]
        return p, after, rest

    grads, scatters = [None] * depth, []

    pending = []

    def on_grads(l, g):
        grads[l] = g
        if l == 0:
            scatters.append((0, scatter_partials(chip_partials(g, GRAD_GROUPS[-1], c_idx), name="rs_scatter_partials")))
            return ()
        pending.append((l, exchange_start({n: _grad_view(n, g[n]) for n in BIG}, name="exchange_start_%d" % l)))
        return (pending[-1][1]['token'],)

    def emit_for(l):
        def emit(names, g):
            tokens = ()
            while names == GRAD_GROUPS[0] and pending:
                above, handle = pending.pop()
                gv, recv = exchange_wait(handle, [g[names[-1]]], name="exchange_wait_%d" % above)
                pv = {n: add_half(gv[n], recv[n], c_idx, name="rs_add_" + n) for n in gv}
                scatters.append((above, scatter_start(pv, name="scatter_start_%d" % above)))
                tokens += (scatters[-1][1]['token'],)
            if l == 0:
                scatters.append((0, scatter_start(chip_partials(g, names, c_idx), name="scatter_start_0_" + names[0])))
                tokens += (scatters[-1][1]['token'],)
            return tokens
        return emit

    loss, dx = local_step(x[0], loss_target[0], depth, get_layer, on_grads, emit_for)
    loss = lax.psum(loss, ("x", "y", "c"))
    full_shapes = {n: (w[n].shape[1], N_CHIPS * w[n].shape[2]) if n in SMALL_COL else w[n].shape[1:] for n in SMALL}
    mine = _pack_small(grads, full_shapes)
    slots = lax.dynamic_update_slice(lax.empty((N_DEV,) + mine.shape, F32), mine[None], (4 * cx + 2 * cy + cc, 0, 0))
    gathering = slab_start(slots, name="small_grads_start")
    grad = {n: lax.empty(w[n].shape, F32) for n in BIG}
    for i, (l, sc) in enumerate(scatters):
        pv, got = sc if isinstance(sc, tuple) else scatter_wait(sc, [dx, gathering['token']], name="scatter_wait_%d" % i)
        grad.update(reduce_shards(pv, got, pos_idx, grad, l))

    delta, new_m, new_v = {}, {}, {}
    for n in BIG:
        delta[n], new_m[n], new_v[n] = _adamw_nd(w[n], grad[n], m[n], v[n], "adamw_" + n)
    slots = slab_finish(slab_wait(gathering, [delta[n] for n in BIG], name="small_grads_wait"), name="small_grads_finish")
    small = _unpack_small(sum_slots(slots, name="sum_small_grads"), full_shapes, depth)
    for n in SMALL:
        if n in SMALL_COL:
            cs = w[n].shape[-1]
            grad[n] = lax.dynamic_slice_in_dim(small[n], me * cs, cs, axis=small[n].ndim - 1)
        else:
            grad[n] = small[n]
        delta[n], new_m[n], new_v[n] = _adamw_nd(w[n], grad[n], m[n], v[n], "adamw_" + n)
    return (loss, dx[None], *[grad[n] for n in WEIGHTS], *[delta[n] for n in WEIGHTS],
            *[new_m[n] for n in WEIGHTS], *[new_v[n] for n in WEIGHTS])
```
